```python
import jax, jax.numpy as jnp
from jax import lax
import numpy as np

D_MODEL = 2048
BATCH = 4
SEQ = 2048
DEPTH = 1
DEC_BATCH = 32
DEC_SEQ = 16
PAST_LEN = 4096

CHUNK = 64
Q_BLOCK = 128
EPS = 1e-6

MLA_HEADS = 8
Q_LORA = 512
KV_LORA = 512
NOPE_DIM = 128
ROPE_DIM = 64
V_DIM = 128
ROPE_THETA = 10000.0
MLA_SCALE = (NOPE_DIM + ROPE_DIM) ** -0.5

GLA_HEADS = 4
GLA_DK = 128
GLA_DV = 256
GATE_RANK = 16
GATE_TAU = 16.0

MLA_WIDTH = MLA_HEADS * V_DIM
GLA_WIDTH = GLA_HEADS * GLA_DV
MIX_WIDTH = MLA_WIDTH + GLA_WIDTH

IN_SPLITS = (Q_LORA, KV_LORA, ROPE_DIM, GLA_HEADS * GLA_DK, GLA_HEADS * GLA_DK, GLA_WIDTH, GATE_RANK, GLA_WIDTH)
D_IN = Q_LORA + KV_LORA + ROPE_DIM + 2 * GLA_HEADS * GLA_DK + GLA_WIDTH + GATE_RANK + GLA_WIDTH

D_FF = -(-8 * D_MODEL // (3 * 256)) * 256
PLE_DIM = 256

kernel_name = "hybrid_mla_gla_streaming_step"


def rms_norm(x, g):
    xf = x.astype(jnp.float32)
    y = xf * lax.rsqrt(jnp.mean(xf * xf, axis=-1, keepdims=True) + EPS)
    return (y * g.astype(jnp.float32)).astype(x.dtype)


def rope_angles(pos):
    half = ROPE_DIM // 2
    inv = 1.0 / (ROPE_THETA ** (jnp.arange(half, dtype=jnp.float32) / half))
    ang = pos.astype(jnp.float32)[:, None] * inv[None, :]
    return jnp.cos(ang), jnp.sin(ang)


def apply_rope(x, cos, sin):
    xf = x.astype(jnp.float32)
    x1, x2 = xf[..., :ROPE_DIM // 2], xf[..., ROPE_DIM // 2:]
    return jnp.concatenate([x1 * cos - x2 * sin, x2 * cos + x1 * sin], axis=-1).astype(x.dtype)


def mixer_inputs(h, pos, g_pre_mix, w_in, g_q, w_uq, w_uk, g_kv, w_ga, b_ga):
    B, S, _ = h.shape
    a = rms_norm(h, g_pre_mix)
    z = a @ w_in
    idx = np.cumsum(IN_SPLITS)[:-1].tolist()
    c_q, c_kv, k_r, q_g, k_g, v_g, g_lr, r_g = jnp.split(z, idx, axis=-1)
    cos, sin = rope_angles(pos)
    c_q = rms_norm(c_q, g_q)
    q = (c_q @ w_uq).reshape(B, S, MLA_HEADS, NOPE_DIM + ROPE_DIM)
    q_nope, q_rope = q[..., :NOPE_DIM], q[..., NOPE_DIM:]
    q_rope = apply_rope(q_rope, cos[:, None, :], sin[:, None, :])
    q_lat = jnp.einsum('bshd,chd->bshc', q_nope, w_uk)
    c_kv = rms_norm(c_kv, g_kv)
    k_r = apply_rope(k_r, cos, sin)
    q_g = q_g.reshape(B, S, GLA_HEADS, GLA_DK) * (GLA_DK ** -0.5)
    k_g = k_g.reshape(B, S, GLA_HEADS, GLA_DK)
    v_g = v_g.reshape(B, S, GLA_HEADS, GLA_DV)
    log_a = jax.nn.log_sigmoid((g_lr @ w_ga + b_ga).astype(jnp.float32)) / GATE_TAU
    log_a = log_a.reshape(B, S, GLA_HEADS, GLA_DK)
    return q_lat, q_rope, c_kv, k_r, q_g, k_g, v_g, log_a, r_g


def mla_attend(q_lat, q_rope, ckv, kr, mask):
    s = jnp.einsum('bqhc,bkc->bhqk', q_lat, ckv) + jnp.einsum('bqhr,bkr->bhqk', q_rope, kr)
    s = s.astype(jnp.float32) * MLA_SCALE
    if mask is not None:
        s = jnp.where(mask[None, None], s, -jnp.inf)
    p = jax.nn.softmax(s, axis=-1).astype(ckv.dtype)
    return jnp.einsum('bhqk,bkc->bqhc', p, ckv)


def mla_prompt(q_lat, q_rope, ckv, kr):
    B, S, H, C = q_lat.shape
    nb = S // Q_BLOCK
    ql = q_lat.reshape(B, nb, Q_BLOCK, H, C).swapaxes(0, 1)
    qr = q_rope.reshape(B, nb, Q_BLOCK, H, ROPE_DIM).swapaxes(0, 1)
    key_chunk = jnp.arange(S) // CHUNK

    def block(args):
        j, qlb, qrb = args
        q_chunk = (j * Q_BLOCK + jnp.arange(Q_BLOCK)) // CHUNK
        mask = key_chunk[None, :] <= q_chunk[:, None]
        return mla_attend(qlb, qrb, ckv, kr, mask)

    o = lax.map(block, (jnp.arange(nb), ql, qr))
    return o.swapaxes(0, 1).reshape(B, S, H, C)


def gla_chunk(S0, q, k, v, log_a):
    L = q.shape[1]
    qf, kf, vf = q.astype(jnp.float32), k.astype(jnp.float32), v.astype(jnp.float32)
    b = jnp.cumsum(log_a, axis=1)
    causal = jnp.tril(jnp.ones((L, L), dtype=bool))
    diff = b[:, :, None] - b[:, None, :]
    decay = jnp.exp(jnp.where(causal[None, :, :, None, None], diff, -jnp.inf))
    attn = jnp.einsum('bthd,btshd,bshd->bhts', qf, decay, kf)
    o = jnp.einsum('bhts,bshv->bthv', attn, vf) + jnp.einsum('bthd,bhdv->bthv', qf * jnp.exp(b), S0)
    b_last = b[:, -1]
    S1 = jnp.exp(b_last)[..., None] * S0 + jnp.einsum('bshd,bshv->bhdv', kf * jnp.exp(b_last[:, None] - b), vf)
    return S1, o


def gla_prompt(q, k, v, log_a):
    B, S = q.shape[:2]
    n = S // CHUNK

    def to_blocks(t):
        return t.reshape((B, n, CHUNK) + t.shape[2:]).swapaxes(0, 1)

    S0 = jnp.zeros((B, GLA_HEADS, GLA_DK, GLA_DV), jnp.float32)
    S_fin, o = lax.scan(lambda Sc, xs: gla_chunk(Sc, *xs), S0,
                        (to_blocks(q), to_blocks(k), to_blocks(v), to_blocks(log_a)))
    return o.swapaxes(0, 1).reshape(B, S, GLA_HEADS, GLA_DV), S_fin


def finish_layer(h, p, o_lat, o_gla, r_g, w_uv, g_gla, w_out, g_post_mix, g_pre_ffn,
                 w_gate, w_up, w_down, g_post_ffn, w_ple, w_ple_gate):
    B, S, _ = h.shape
    o_mla = jnp.einsum('bshc,chd->bshd', o_lat, w_uv).reshape(B, S, MLA_WIDTH)
    o_g = rms_norm(o_gla, g_gla) * jax.nn.silu(r_g.astype(jnp.float32)).reshape(B, S, GLA_HEADS, GLA_DV)
    o_g = o_g.reshape(B, S, GLA_WIDTH).astype(h.dtype)
    mix = jnp.concatenate([o_mla.astype(h.dtype), o_g], axis=-1) @ w_out
    h = h + rms_norm(mix, g_post_mix)
    f = rms_norm(h, g_pre_ffn)
    f = (jax.nn.silu(f @ w_gate) * (f @ w_up)) @ w_down
    h = h + rms_norm(f, g_post_ffn)
    return h + jax.nn.sigmoid(h @ w_ple_gate) * (p @ w_ple)


def setup_inputs(seed: int = 0) -> dict:
    key = jax.random.key(seed)
    ks = jax.random.split(key, 32)

    def nrm(k, shape, scale):
        return jax.random.normal(k, shape, jnp.float32) * scale

    def gain(k, n):
        return 1.0 + 0.05 * jax.random.normal(k, (DEPTH, n), jnp.float32)

    return {
        "x_prompt": nrm(ks[0], (BATCH, SEQ, D_MODEL), 1.0),
        "x_sample": nrm(ks[1], (DEC_BATCH, DEC_SEQ, D_MODEL), 1.0),
        "cache_ckv": nrm(ks[2], (DEPTH, DEC_BATCH, PAST_LEN, KV_LORA), 1.0),
        "cache_krope": nrm(ks[3], (DEPTH, DEC_BATCH, PAST_LEN, ROPE_DIM), 1.0),
        "state_gla": nrm(ks[4], (DEPTH, DEC_BATCH, GLA_HEADS, GLA_DK, GLA_DV), 2.0),
        "p_prompt": nrm(ks[5], (DEPTH, BATCH, SEQ, PLE_DIM), 1.0),
        "p_sample": nrm(ks[6], (DEPTH, DEC_BATCH, DEC_SEQ, PLE_DIM), 1.0),
        "g_pre_mix": gain(ks[7], D_MODEL),
        "w_in": nrm(ks[8], (DEPTH, D_MODEL, D_IN), D_MODEL ** -0.5),
        "g_q": gain(ks[9], Q_LORA),
        "w_uq": nrm(ks[10], (DEPTH, Q_LORA, MLA_HEADS * (NOPE_DIM + ROPE_DIM)), Q_LORA ** -0.5),
        "w_uk": nrm(ks[11], (DEPTH, KV_LORA, MLA_HEADS, NOPE_DIM), KV_LORA ** -0.5),
        "g_kv": gain(ks[12], KV_LORA),
        "w_ga": nrm(ks[13], (DEPTH, GATE_RANK, GLA_HEADS * GLA_DK), GATE_RANK ** -0.5),
        "b_ga": nrm(ks[14], (DEPTH, GLA_HEADS * GLA_DK), 0.1),
        "w_uv": nrm(ks[15], (DEPTH, KV_LORA, MLA_HEADS, V_DIM), KV_LORA ** -0.5),
        "g_gla": gain(ks[16], GLA_DV),
        "w_out": nrm(ks[17], (DEPTH, MIX_WIDTH, D_MODEL), MIX_WIDTH ** -0.5),
        "g_post_mix": gain(ks[18], D_MODEL),
        "g_pre_ffn": gain(ks[19], D_MODEL),
        "w_gate": nrm(ks[20], (DEPTH, D_MODEL, D_FF), D_MODEL ** -0.5),
        "w_up": nrm(ks[21], (DEPTH, D_MODEL, D_FF), D_MODEL ** -0.5),
        "w_down": nrm(ks[22], (DEPTH, D_FF, D_MODEL), D_FF ** -0.5),
        "g_post_ffn": gain(ks[23], D_MODEL),
        "w_ple": nrm(ks[24], (DEPTH, PLE_DIM, D_MODEL), PLE_DIM ** -0.5),
        "w_ple_gate": nrm(ks[25], (DEPTH, D_MODEL, D_MODEL), D_MODEL ** -0.5),
    }


def reference(x_prompt, x_sample, cache_ckv, cache_krope, state_gla, p_prompt, p_sample,
              g_pre_mix, w_in, g_q, w_uq, w_uk, g_kv, w_ga, b_ga, w_uv, g_gla, w_out,
              g_post_mix, g_pre_ffn, w_gate, w_up, w_down, g_post_ffn, w_ple, w_ple_gate):
    pos_p = jnp.arange(x_prompt.shape[1])
    pos_s = PAST_LEN + jnp.arange(x_sample.shape[1])
    h_p, h_s = x_prompt, x_sample
    ckv_p, kr_p, st_p, ckv_s, kr_s, st_s = [], [], [], [], [], []
    for i in range(DEPTH):
        q_lat, q_rope, c_kv, k_r, q_g, k_g, v_g, log_a, r_g = mixer_inputs(
            h_p, pos_p, g_pre_mix[i], w_in[i], g_q[i], w_uq[i], w_uk[i], g_kv[i], w_ga[i], b_ga[i])
        o_lat = mla_prompt(q_lat, q_rope, c_kv, k_r)
        o_gla, S_fin = gla_prompt(q_g, k_g, v_g, log_a)
        h_p = finish_layer(h_p, p_prompt[i], o_lat, o_gla, r_g, w_uv[i], g_gla[i], w_out[i], g_post_mix[i],
                           g_pre_ffn[i], w_gate[i], w_up[i], w_down[i], g_post_ffn[i], w_ple[i], w_ple_gate[i])
        ckv_p.append(c_kv)
        kr_p.append(k_r)
        st_p.append(S_fin.astype(x_prompt.dtype))
        q_lat, q_rope, c_kv, k_r, q_g, k_g, v_g, log_a, r_g = mixer_inputs(
            h_s, pos_s, g_pre_mix[i], w_in[i], g_q[i], w_uq[i], w_uk[i], g_kv[i], w_ga[i], b_ga[i])
        ckv_all = jnp.concatenate([cache_ckv[i].astype(c_kv.dtype), c_kv], axis=1)
        kr_all = jnp.concatenate([cache_krope[i].astype(k_r.dtype), k_r], axis=1)
        o_lat = mla_attend(q_lat, q_rope, ckv_all, kr_all, None)
        S_new, o_gla = gla_chunk(state_gla[i].astype(jnp.float32), q_g, k_g, v_g, log_a)
        h_s = finish_layer(h_s, p_sample[i], o_lat, o_gla, r_g, w_uv[i], g_gla[i], w_out[i], g_post_mix[i],
                           g_pre_ffn[i], w_gate[i], w_up[i], w_down[i], g_post_ffn[i], w_ple[i], w_ple_gate[i])
        ckv_s.append(c_kv)
        kr_s.append(k_r)
        st_s.append(S_new.astype(x_sample.dtype))
    return (h_p, h_s, jnp.stack(ckv_p), jnp.stack(kr_p), jnp.stack(st_p),
            jnp.stack(ckv_s), jnp.stack(kr_s), jnp.stack(st_s))
```

```python
import functools

import numpy as np
import jax
import jax.numpy as jnp
from jax import lax
from jax.experimental import pallas as pl
from jax.experimental.pallas import tpu as pltpu

F32 = jnp.float32
BF16 = jnp.bfloat16

D_MODEL = 2048
PAST_LEN = 4096
CHUNK = 64
EPS = 1e-6

MLA_HEADS = 8
Q_LORA = 512
KV_LORA = 512
NOPE_DIM = 128
ROPE_DIM = 64
V_DIM = 128
ROPE_THETA = 10000.0
MLA_SCALE = (NOPE_DIM + ROPE_DIM) ** -0.5
QK_PAD = 256

GLA_HEADS = 4
GLA_DK = 128
GLA_DV = 256
GATE_RANK = 16
GATE_TAU = 16.0
GLA_WIDTH = GLA_HEADS * GLA_DV
MLA_WIDTH = MLA_HEADS * V_DIM

IN_SPLITS = (Q_LORA, KV_LORA, ROPE_DIM, GLA_HEADS * GLA_DK, GLA_HEADS * GLA_DK, GLA_WIDTH, GATE_RANK, GLA_WIDTH)
Z_MAIN = 4096
Z_WIDTH = Z_MAIN + 128
GLR_LO = ROPE_DIM

D_FF = 5632
PLE_DIM = 256

VMEM_LIMIT = 56 * 1024 * 1024
NEG_BIG = -1e30


def _dot(a, b):
    return jnp.dot(a, b, preferred_element_type=F32)


def _dot_nt(a, b):
    return lax.dot_general(a, b, (((1,), (1,)), ((), ())), preferred_element_type=F32)


def _dot_tn(a, b):
    return lax.dot_general(a, b, (((0,), (0,)), ((), ())), preferred_element_type=F32)


def _rms(x, g):
    ms = jnp.mean(x * x, axis=-1, keepdims=True)
    return x * lax.rsqrt(ms + EPS) * g


def _rope(x, cos_t, sin_t):
    lane = lax.broadcasted_iota(jnp.int32, x.shape, 1)
    rot = jnp.where((lane & 32) == 0, pltpu.roll(x, 96, 1), pltpu.roll(x, 32, 1))
    return x * cos_t + rot * sin_t


def _params(*sem):
    return pltpu.CompilerParams(dimension_semantics=sem, vmem_limit_bytes=VMEM_LIMIT)


def _inproj_kernel(x_ref, g_ref, w_ref, o_ref, xn_ref):
    @pl.when(pl.program_id(1) == 0)
    def _():
        xn_ref[...] = _rms(x_ref[...], g_ref[...]).astype(BF16)

    o_ref[...] = _dot(xn_ref[...], w_ref[...])


def _inproj(x, g, w):
    T = x.shape[0]
    tm, tn = 512, 1408
    return pl.pallas_call(
        _inproj_kernel,
        grid=(T // tm, Z_WIDTH // tn),
        in_specs=[pl.BlockSpec((tm, D_MODEL), lambda i, j: (i, 0)),
                  pl.BlockSpec((1, D_MODEL), lambda i, j: (0, 0)),
                  pl.BlockSpec((D_MODEL, tn), lambda i, j: (0, j))],
        out_specs=pl.BlockSpec((tm, tn), lambda i, j: (i, j)),
        out_shape=jax.ShapeDtypeStruct((T, Z_WIDTH), F32),
        scratch_shapes=[pltpu.VMEM((tm, D_MODEL), BF16)],
        compiler_params=_params("parallel", "arbitrary"),
        name="inproj",
    )(x, g, w)


def _mla_q(cq_ref, gq_ref, wuq_ref, cos, sin):
    cqn = _rms(cq_ref[...], gq_ref[...]).astype(BF16)
    q = _dot(cqn, wuq_ref[...])
    out = []
    for h in range(MLA_HEADS):
        nope = q[:, h * QK_PAD:h * QK_PAD + NOPE_DIM]
        rp = _rope(q[:, h * QK_PAD + NOPE_DIM:(h + 1) * QK_PAD], cos, sin)
        out.append((nope, rp))
    return out


def _mla_prep_prompt_kernel(cq_ref, ckv_ref, sm_ref, gq_ref, gkv_ref, wuq_ref, wukv_ref, cos_ref, sin_ref,
                            ckvn_ref, kr_ref, qcat_ref, kcat_ref, v_ref):
    cos, sin = cos_ref[...], sin_ref[...]
    for h, (nope, rp) in enumerate(_mla_q(cq_ref, gq_ref, wuq_ref, cos, sin)):
        qcat_ref[:, h * QK_PAD:h * QK_PAD + NOPE_DIM] = nope.astype(BF16)
        qcat_ref[:, h * QK_PAD + NOPE_DIM:(h + 1) * QK_PAD] = rp.astype(BF16)
    ckvn = _rms(ckv_ref[...], gkv_ref[...])
    ckvn_ref[...] = ckvn
    kr = _rope(sm_ref[...], cos, sin)
    kr_ref[...] = kr[:, :ROPE_DIM]
    kv = _dot(ckvn.astype(BF16), wukv_ref[...])
    krb = kr.astype(BF16)
    for h in range(MLA_HEADS):
        kcat_ref[:, h * QK_PAD:h * QK_PAD + NOPE_DIM] = kv[:, h * NOPE_DIM:(h + 1) * NOPE_DIM].astype(BF16)
        kcat_ref[:, h * QK_PAD + NOPE_DIM:(h + 1) * QK_PAD] = krb
    v_ref[...] = kv[:, MLA_WIDTH:].astype(BF16)


def _mla_prep_prompt(z, g_q, g_kv, w_uq, w_ukv, cos_t, sin_t, seq):
    T = z.shape[0]
    tm = 512
    nseq = seq // tm
    row = lambda i: (i, 0)
    const = lambda i: (0, 0)
    return pl.pallas_call(
        _mla_prep_prompt_kernel,
        grid=(T // tm,),
        in_specs=[pl.BlockSpec((tm, Q_LORA), lambda i: (i, 0)),
                  pl.BlockSpec((tm, KV_LORA), lambda i: (i, 1)),
                  pl.BlockSpec((tm, 128), lambda i: (i, Z_MAIN // 128)),
                  pl.BlockSpec((1, Q_LORA), const),
                  pl.BlockSpec((1, KV_LORA), const),
                  pl.BlockSpec((Q_LORA, MLA_HEADS * QK_PAD), const),
                  pl.BlockSpec((KV_LORA, 2 * MLA_WIDTH), const),
                  pl.BlockSpec((tm, 128), lambda i: (i % nseq, 0)),
                  pl.BlockSpec((tm, 128), lambda i: (i % nseq, 0))],
        out_specs=[pl.BlockSpec((tm, KV_LORA), row),
                   pl.BlockSpec((tm, ROPE_DIM), row),
                   pl.BlockSpec((tm, MLA_HEADS * QK_PAD), row),
                   pl.BlockSpec((tm, MLA_HEADS * QK_PAD), row),
                   pl.BlockSpec((tm, MLA_WIDTH), row)],
        out_shape=[jax.ShapeDtypeStruct((T, KV_LORA), F32),
                   jax.ShapeDtypeStruct((T, ROPE_DIM), F32),
                   jax.ShapeDtypeStruct((T, MLA_HEADS * QK_PAD), BF16),
                   jax.ShapeDtypeStruct((T, MLA_HEADS * QK_PAD), BF16),
                   jax.ShapeDtypeStruct((T, MLA_WIDTH), BF16)],
        compiler_params=_params("parallel"),
        name="mla_prep_prompt",
    )(z, z, z, g_q, g_kv, w_uq, w_ukv, cos_t, sin_t)


def _mla_prep_sample_kernel(cq_ref, ckv_ref, sm_ref, gq_ref, gkv_ref, wuq_ref, wukt_ref, cos_ref, sin_ref,
                            ckvn_ref, kr_ref, qlat_ref, qr_ref):
    cos, sin = cos_ref[...], sin_ref[...]
    for h, (nope, rp) in enumerate(_mla_q(cq_ref, gq_ref, wuq_ref, cos, sin)):
        qlat_ref[h] = _dot(nope.astype(BF16), wukt_ref[h]).astype(BF16)
        qr_ref[h] = rp[:, :ROPE_DIM].astype(BF16)
    ckvn_ref[...] = _rms(ckv_ref[...], gkv_ref[...])
    kr_ref[...] = _rope(sm_ref[...], cos, sin)[:, :ROPE_DIM]


def _mla_prep_sample(z, g_q, g_kv, w_uq, w_ukt, cos_t, sin_t):
    T = z.shape[0]
    tm = T
    row = lambda i: (i, 0)
    const = lambda i: (0, 0)
    const3 = lambda i: (0, 0, 0)
    return pl.pallas_call(
        _mla_prep_sample_kernel,
        grid=(T // tm,),
        in_specs=[pl.BlockSpec((tm, Q_LORA), lambda i: (i, 0)),
                  pl.BlockSpec((tm, KV_LORA), lambda i: (i, 1)),
                  pl.BlockSpec((tm, 128), lambda i: (i, Z_MAIN // 128)),
                  pl.BlockSpec((1, Q_LORA), const),
                  pl.BlockSpec((1, KV_LORA), const),
                  pl.BlockSpec((Q_LORA, MLA_HEADS * QK_PAD), const),
                  pl.BlockSpec((MLA_HEADS, NOPE_DIM, KV_LORA), const3),
                  pl.BlockSpec((tm, 128), row),
                  pl.BlockSpec((tm, 128), row)],
        out_specs=[pl.BlockSpec((tm, KV_LORA), row),
                   pl.BlockSpec((tm, ROPE_DIM), row),
                   pl.BlockSpec((MLA_HEADS, tm, KV_LORA), lambda i: (0, i, 0)),
                   pl.BlockSpec((MLA_HEADS, tm, ROPE_DIM), lambda i: (0, i, 0))],
        out_shape=[jax.ShapeDtypeStruct((T, KV_LORA), F32),
                   jax.ShapeDtypeStruct((T, ROPE_DIM), F32),
                   jax.ShapeDtypeStruct((MLA_HEADS, T, KV_LORA), BF16),
                   jax.ShapeDtypeStruct((MLA_HEADS, T, ROPE_DIM), BF16)],
        compiler_params=_params("parallel"),
        name="mla_prep_sample",
    )(z, z, z, g_q, g_kv, w_uq, w_ukt, cos_t, sin_t)


ATT_TQ = 256
ATT_TK = 256


def _softmax_step(s, v, m, l, acc):
    m_new = jnp.maximum(m, jnp.max(s, axis=-1, keepdims=True))
    p = jnp.exp(s - m_new)
    alpha = jnp.exp(m - m_new)
    l = alpha * l + jnp.sum(p, axis=-1, keepdims=True)
    acc = alpha * acc + _dot(p.astype(BF16), v)
    return m_new, l, acc


def _attn_prompt_kernel(q_ref, k_ref, v_ref, o_ref):
    i = pl.program_id(2)
    q = q_ref[...]

    def body(kb, carry):
        start = pl.multiple_of(kb * ATT_TK, ATT_TK)
        s = _dot_nt(q, k_ref[pl.ds(start, ATT_TK), :]) * MLA_SCALE
        return _softmax_step(s, v_ref[pl.ds(start, ATT_TK), :], *carry)

    init = (jnp.full((ATT_TQ, 1), NEG_BIG, F32), jnp.zeros((ATT_TQ, 1), F32), jnp.zeros((ATT_TQ, V_DIM), F32))
    carry = lax.fori_loop(0, i, body, init)
    start = pl.multiple_of(i * ATT_TK, ATT_TK)
    s = _dot_nt(q, k_ref[pl.ds(start, ATT_TK), :]) * MLA_SCALE
    qc = lax.broadcasted_iota(jnp.int32, s.shape, 0) // CHUNK
    kc = lax.broadcasted_iota(jnp.int32, s.shape, 1) // CHUNK
    s = jnp.where(kc <= qc, s, NEG_BIG)
    _, l, acc = _softmax_step(s, v_ref[pl.ds(start, ATT_TK), :], *carry)
    o_ref[...] = (acc / l).astype(BF16)


def _attn_prompt(qcat, kcat, v, batch, seq):
    nq = seq // ATT_TQ
    return pl.pallas_call(
        _attn_prompt_kernel,
        grid=(batch, MLA_HEADS, nq),
        in_specs=[pl.BlockSpec((ATT_TQ, QK_PAD), lambda b, h, i: (b * nq + i, h)),
                  pl.BlockSpec((seq, QK_PAD), lambda b, h, i: (b, h)),
                  pl.BlockSpec((seq, V_DIM), lambda b, h, i: (b, h))],
        out_specs=pl.BlockSpec((ATT_TQ, V_DIM), lambda b, h, i: (b * nq + i, h)),
        out_shape=jax.ShapeDtypeStruct((batch * seq, MLA_WIDTH), BF16),
        compiler_params=_params("parallel", "parallel", "arbitrary"),
        name="attn_prompt",
    )(qcat, kcat, v)


DEC_TK = 1024


def _attn_sample_kernel(ql_ref, qr_ref, ckv_ref, kr_ref, nckv_ref, nkr_ref, wuv_ref, o_ref, m_ref, l_ref, acc_ref):
    j = pl.program_id(1)
    dec = ql_ref.shape[1]
    rows = MLA_HEADS * dec
    ql = ql_ref[...].reshape(rows, KV_LORA)
    qr = qr_ref[...].reshape(rows, ROPE_DIM)

    @pl.when(j == 0)
    def _():
        m_ref[...] = jnp.full(m_ref.shape, NEG_BIG, F32)
        l_ref[...] = jnp.zeros(l_ref.shape, F32)
        acc_ref[...] = jnp.zeros(acc_ref.shape, F32)

    def step(ckv, kr):
        ckv = ckv.astype(BF16)
        s = (_dot_nt(ql, ckv) + _dot_nt(qr, kr.astype(BF16))) * MLA_SCALE
        m, l, acc = _softmax_step(s, ckv, m_ref[...], l_ref[...], acc_ref[...])
        m_ref[...] = m
        l_ref[...] = l
        acc_ref[...] = acc

    step(ckv_ref[...], kr_ref[...])

    @pl.when(j == pl.num_programs(1) - 1)
    def _():
        step(nckv_ref[...], nkr_ref[...])
        o = (acc_ref[...] / l_ref[...]).astype(BF16)
        for h in range(MLA_HEADS):
            o_ref[:, h * V_DIM:(h + 1) * V_DIM] = _dot(o[h * dec:(h + 1) * dec], wuv_ref[h]).astype(BF16)


def _attn_sample(q_lat, q_rope, cache_ckv, cache_kr, ckvn, kr, w_uvh, dec):
    batch, past, _ = cache_ckv.shape
    rows = MLA_HEADS * dec
    return pl.pallas_call(
        _attn_sample_kernel,
        grid=(batch, past // DEC_TK),
        in_specs=[pl.BlockSpec((MLA_HEADS, dec, KV_LORA), lambda b, j: (0, b, 0)),
                  pl.BlockSpec((MLA_HEADS, dec, ROPE_DIM), lambda b, j: (0, b, 0)),
                  pl.BlockSpec((None, DEC_TK, KV_LORA), lambda b, j: (b, j, 0)),
                  pl.BlockSpec((None, DEC_TK, ROPE_DIM), lambda b, j: (b, j, 0)),
                  pl.BlockSpec((dec, KV_LORA), lambda b, j: (b, 0)),
                  pl.BlockSpec((dec, ROPE_DIM), lambda b, j: (b, 0)),
                  pl.BlockSpec((MLA_HEADS, KV_LORA, V_DIM), lambda b, j: (0, 0, 0))],
        out_specs=pl.BlockSpec((dec, MLA_WIDTH), lambda b, j: (b, 0)),
        out_shape=jax.ShapeDtypeStruct((batch * dec, MLA_WIDTH), BF16),
        scratch_shapes=[pltpu.VMEM((rows, 1), F32), pltpu.VMEM((rows, 1), F32), pltpu.VMEM((rows, KV_LORA), F32)],
        compiler_params=_params("parallel", "arbitrary"),
        name="attn_sample",
    )(q_lat, q_rope, cache_ckv, cache_kr, ckvn, kr, w_uvh)


def _gla_tables(c):
    nlev = int(np.log2(c))
    t = np.arange(c)[:, None]
    u = np.arange(c)[None, :]
    blocks = []
    lvl = np.full((c, c), -1, np.int32)
    for l in range(nlev):
        width = c >> l
        half = width // 2
        m = (t // width) * width + half - 1
        upper = t > m
        blocks.append(np.where(upper, (u > m) & (u <= t), (u > t) & (u <= m)))
        same = (t // width) == (u // width)
        lvl[same & ((t % width) >= half) & ((u % width) < half)] = l
    blocks.append(u > t)
    blocks.append(u <= t)
    lvl[np.arange(c), np.arange(c)] = nlev
    return np.concatenate(blocks, 0).astype(np.float32), lvl, nlev


def _gla_kernel(*refs, c, nsub, nlev, has_init):
    if has_init:
        (q_ref, k_ref, v_ref, r_ref, sm_ref, wga_ref, bga_ref, gg_ref, p_ref, lvl_ref, s0_ref,
         o_ref, sfin_ref, s_scr) = refs
    else:
        (q_ref, k_ref, v_ref, r_ref, sm_ref, wga_ref, bga_ref, gg_ref, p_ref, lvl_ref,
         o_ref, sfin_ref, s_scr) = refs
    j = pl.program_id(1)

    @pl.when(j == 0)
    def _():
        if has_init:
            s_scr[...] = s0_ref[0]
        else:
            s_scr[...] = jnp.zeros(s_scr.shape, F32)

    p_mat = p_ref[...]
    lvl = lvl_ref[...]
    ones = jnp.ones((c, GLA_DK), BF16)
    for sub in range(nsub):
        rows = pl.ds(sub * c, c)
        sm = sm_ref[rows, :].astype(BF16)
        for h in range(GLA_HEADS):
            dk = slice(h * GLA_DK, (h + 1) * GLA_DK)
            dv = slice(h * GLA_DV, (h + 1) * GLA_DV)
            x = _dot(sm, wga_ref[:, dk]) + bga_ref[:, dk]
            la = (jnp.minimum(x, 0.0) - jnp.log1p(jnp.exp(-jnp.abs(x)))) * (1.0 / GATE_TAU)
            hi = la.astype(BF16)
            r1 = la - hi.astype(F32)
            mid = r1.astype(BF16)
            lo = (r1 - mid.astype(F32)).astype(BF16)
            e_all = jnp.exp(_dot(p_mat, hi) + _dot(p_mat, mid) + _dot(p_mat, lo))
            d_state = jnp.exp(_dot_tn(hi, ones) + _dot_tn(mid, ones) + _dot_tn(lo, ones))
            q = q_ref[rows, dk] * (GLA_DK ** -0.5)
            k = k_ref[rows, dk]
            v = v_ref[rows, dv].astype(BF16)
            attn = jnp.zeros((c, c), F32)
            for l in range(nlev):
                e_l = e_all[l * c:(l + 1) * c]
                attn = jnp.where(lvl == l, _dot_nt((q * e_l).astype(BF16), (k * e_l).astype(BF16)), attn)
            attn = jnp.where(lvl == nlev, _dot_nt(q.astype(BF16), k.astype(BF16)), attn)
            e_end = e_all[nlev * c:(nlev + 1) * c]
            e_beg = e_all[(nlev + 1) * c:(nlev + 2) * c]
            state = s_scr[h]
            o = _dot(attn.astype(BF16), v) + _dot((q * e_beg).astype(BF16), state.astype(BF16))
            s_scr[h] = (jnp.concatenate([d_state, d_state], axis=1) * state
                        + _dot_tn((k * e_end).astype(BF16), v))
            og = _rms(o, gg_ref[...]) * (lambda r: r * jax.nn.sigmoid(r))(r_ref[rows, dv])
            o_ref[rows, dv] = og.astype(BF16)

    @pl.when(j == pl.num_programs(1) - 1)
    def _():
        sfin_ref[0] = s_scr[...]


def _gla(z, w_ga, b_ga, g_gla, batch, seq, c, rows_per_step, s0=None):
    p_np, lvl_np, nlev = _gla_tables(c)
    p_mat = jnp.asarray(p_np, BF16)
    lvl = jnp.asarray(lvl_np)
    nstep = seq // rows_per_step
    r = rows_per_step
    row = lambda b, j: (b * nstep + j, 0)
    const = lambda b, j: (0, 0)
    in_specs = [pl.BlockSpec((r, 512), lambda b, j: (b * nstep + j, 2)),
                pl.BlockSpec((r, 512), lambda b, j: (b * nstep + j, 3)),
                pl.BlockSpec((r, GLA_WIDTH), lambda b, j: (b * nstep + j, 2)),
                pl.BlockSpec((r, GLA_WIDTH), lambda b, j: (b * nstep + j, 3)),
                pl.BlockSpec((r, 128), lambda b, j: (b * nstep + j, Z_MAIN // 128)),
                pl.BlockSpec((128, GLA_HEADS * GLA_DK), const),
                pl.BlockSpec((1, GLA_HEADS * GLA_DK), const),
                pl.BlockSpec((1, GLA_DV), const),
                pl.BlockSpec(p_np.shape, const),
                pl.BlockSpec((c, c), const)]
    args = [z, z, z, z, z, w_ga, b_ga, g_gla, p_mat, lvl]
    state_spec = pl.BlockSpec((1, GLA_HEADS, GLA_DK, GLA_DV), lambda b, j: (b, 0, 0, 0))
    if s0 is not None:
        in_specs.append(state_spec)
        args.append(s0)
    return pl.pallas_call(
        functools.partial(_gla_kernel, c=c, nsub=r // c, nlev=nlev, has_init=s0 is not None),
        grid=(batch, nstep),
        in_specs=in_specs,
        out_specs=[pl.BlockSpec((r, GLA_WIDTH), row), state_spec],
        out_shape=[jax.ShapeDtypeStruct((batch * seq, GLA_WIDTH), BF16),
                   jax.ShapeDtypeStruct((batch, GLA_HEADS, GLA_DK, GLA_DV), F32)],
        scratch_shapes=[pltpu.VMEM((GLA_HEADS, GLA_DK, GLA_DV), F32)],
        compiler_params=_params("parallel", "arbitrary"),
        name="gla_init" if s0 is not None else "gla",
    )(*args)


def _mix_kernel(om_ref, og_ref, w1_ref, w2_ref, h_ref, g1_ref, g2_ref, h1_ref, f_ref):
    mix = _dot(om_ref[...], w1_ref[...]) + _dot(og_ref[...], w2_ref[...])
    h1 = h_ref[...] + _rms(mix, g1_ref[...])
    h1_ref[...] = h1
    f_ref[...] = _rms(h1, g2_ref[...]).astype(BF16)


def _mix(o_mla, o_gla, w_out, h, g_post_mix, g_pre_ffn):
    T = h.shape[0]
    tm = 256
    row = lambda i: (i, 0)
    const = lambda i: (0, 0)
    return pl.pallas_call(
        _mix_kernel,
        grid=(T // tm,),
        in_specs=[pl.BlockSpec((tm, MLA_WIDTH), row),
                  pl.BlockSpec((tm, GLA_WIDTH), row),
                  pl.BlockSpec((MLA_WIDTH, D_MODEL), lambda i: (0, 0)),
                  pl.BlockSpec((GLA_WIDTH, D_MODEL), lambda i: (1, 0)),
                  pl.BlockSpec((tm, D_MODEL), row),
                  pl.BlockSpec((1, D_MODEL), const),
                  pl.BlockSpec((1, D_MODEL), const)],
        out_specs=[pl.BlockSpec((tm, D_MODEL), row), pl.BlockSpec((tm, D_MODEL), row)],
        out_shape=[jax.ShapeDtypeStruct((T, D_MODEL), F32), jax.ShapeDtypeStruct((T, D_MODEL), BF16)],
        compiler_params=_params("parallel"),
        name="mix",
    )(o_mla, o_gla, w_out, w_out, h, g_post_mix, g_pre_ffn)


def _ffn_kernel(x_ref, wg_ref, wu_ref, wd_ref, o_ref):
    j = pl.program_id(1)
    x = x_ref[...]
    g = _dot(x, wg_ref[...])
    u = _dot(x, wu_ref[...])
    d = _dot((g * jax.nn.sigmoid(g) * u).astype(BF16), wd_ref[...])

    @pl.when(j == 0)
    def _():
        o_ref[...] = d

    @pl.when(j > 0)
    def _():
        o_ref[...] += d


def _ffn(x, w_gate, w_up, w_down):
    T = x.shape[0]
    tm, tf = 512, 512
    return pl.pallas_call(
        _ffn_kernel,
        grid=(T // tm, D_FF // tf),
        in_specs=[pl.BlockSpec((tm, D_MODEL), lambda i, j: (i, 0)),
                  pl.BlockSpec((D_MODEL, tf), lambda i, j: (0, j)),
                  pl.BlockSpec((D_MODEL, tf), lambda i, j: (0, j)),
                  pl.BlockSpec((tf, D_MODEL), lambda i, j: (j, 0))],
        out_specs=pl.BlockSpec((tm, D_MODEL), lambda i, j: (i, 0)),
        out_shape=jax.ShapeDtypeStruct((T, D_MODEL), F32),
        compiler_params=_params("parallel", "arbitrary"),
        name="ffn",
    )(x, w_gate, w_up, w_down)


def _final_kernel(f_ref, h1_ref, p_ref, g_ref, wpg_ref, wp_ref, o_ref):
    h2 = h1_ref[...] + _rms(f_ref[...], g_ref[...])
    gate = jax.nn.sigmoid(_dot(h2.astype(BF16), wpg_ref[...]))
    o_ref[...] = h2 + gate * _dot(p_ref[...].astype(BF16), wp_ref[...])


def _final(f, h1, p, g_post_ffn, w_ple_gate, w_ple):
    T = f.shape[0]
    tm = 256
    row = lambda i: (i, 0)
    const = lambda i: (0, 0)
    return pl.pallas_call(
        _final_kernel,
        grid=(T // tm,),
        in_specs=[pl.BlockSpec((tm, D_MODEL), row),
                  pl.BlockSpec((tm, D_MODEL), row),
                  pl.BlockSpec((tm, PLE_DIM), row),
                  pl.BlockSpec((1, D_MODEL), const),
                  pl.BlockSpec((D_MODEL, D_MODEL), const),
                  pl.BlockSpec((PLE_DIM, D_MODEL), const)],
        out_specs=pl.BlockSpec((tm, D_MODEL), row),
        out_shape=jax.ShapeDtypeStruct((T, D_MODEL), F32),
        compiler_params=_params("parallel"),
        name="final",
    )(f, h1, p, g_post_ffn, w_ple_gate, w_ple)


def _rope_tables(pos):
    half = ROPE_DIM // 2
    inv = 1.0 / (ROPE_THETA ** (jnp.arange(half, dtype=F32) / half))
    ang = pos.astype(F32)[:, None] * inv[None, :]
    cos, sin = jnp.cos(ang), jnp.sin(ang)
    zero = jnp.zeros((pos.shape[0], 128 - ROPE_DIM), F32)
    return jnp.concatenate([cos, cos, zero], axis=1), jnp.concatenate([-sin, sin, zero], axis=1)


def _layer_weights(i, g_pre_mix, w_in, g_q, w_uq, w_uk, g_kv, w_ga, b_ga, w_uv, g_gla, w_out, g_post_mix,
                   g_pre_ffn, w_gate, w_up, w_down, g_post_ffn, w_ple, w_ple_gate):
    idx = np.cumsum(IN_SPLITS)[:-1].tolist()
    c_q, c_kv, k_r, q_g, k_g, v_g, g_lr, r_g = jnp.split(w_in[i], idx, axis=1)
    pad = jnp.zeros((D_MODEL, 128 - ROPE_DIM - GATE_RANK), F32)
    w_in_p = jnp.concatenate([c_q, c_kv, q_g, k_g, v_g, r_g, k_r, g_lr, pad], axis=1).astype(BF16)
    w_uq_p = jnp.pad(w_uq[i].reshape(Q_LORA, MLA_HEADS, NOPE_DIM + ROPE_DIM),
                     ((0, 0), (0, 0), (0, QK_PAD - NOPE_DIM - ROPE_DIM))).reshape(Q_LORA, MLA_HEADS * QK_PAD)
    w_ga_p = jnp.zeros((128, GLA_HEADS * GLA_DK), F32).at[GLR_LO:GLR_LO + GATE_RANK].set(w_ga[i])
    vec = lambda g: g[i].reshape(1, -1)
    return dict(
        g_pre_mix=vec(g_pre_mix), w_in=w_in_p, g_q=vec(g_q), g_kv=vec(g_kv),
        w_uq=w_uq_p.astype(BF16),
        w_ukv=jnp.concatenate([w_uk[i].reshape(KV_LORA, MLA_WIDTH), w_uv[i].reshape(KV_LORA, MLA_WIDTH)],
                              axis=1).astype(BF16),
        w_ukt=jnp.transpose(w_uk[i], (1, 2, 0)).astype(BF16),
        w_uvh=jnp.transpose(w_uv[i], (1, 0, 2)).astype(BF16),
        w_ga=w_ga_p.astype(BF16), b_ga=vec(b_ga), g_gla=vec(g_gla),
        w_out=w_out[i].astype(BF16), g_post_mix=vec(g_post_mix), g_pre_ffn=vec(g_pre_ffn),
        w_gate=w_gate[i].astype(BF16), w_up=w_up[i].astype(BF16), w_down=w_down[i].astype(BF16),
        g_post_ffn=vec(g_post_ffn), w_ple=w_ple[i].astype(BF16), w_ple_gate=w_ple_gate[i].astype(BF16))


def _finish(w, h, p, o_mla, o_gla):
    h1, f_in = _mix(o_mla, o_gla, w["w_out"], h, w["g_post_mix"], w["g_pre_ffn"])
    f = _ffn(f_in, w["w_gate"], w["w_up"], w["w_down"])
    return _final(f, h1, p, w["g_post_ffn"], w["w_ple_gate"], w["w_ple"])


def kernel(x_prompt, x_sample, cache_ckv, cache_krope, state_gla, p_prompt, p_sample, g_pre_mix, w_in, g_q, w_uq,
           w_uk, g_kv, w_ga, b_ga, w_uv, g_gla, w_out, g_post_mix, g_pre_ffn, w_gate, w_up, w_down, g_post_ffn,
           w_ple, w_ple_gate):
    batch, seq, _ = x_prompt.shape
    dbatch, dseq, _ = x_sample.shape
    depth = w_in.shape[0]
    cos_p, sin_p = _rope_tables(jnp.arange(seq))
    cos_s, sin_s = _rope_tables(PAST_LEN + jnp.arange(dseq))
    cos_s, sin_s = jnp.tile(cos_s, (dbatch, 1)), jnp.tile(sin_s, (dbatch, 1))
    h_p = x_prompt.reshape(batch * seq, D_MODEL)
    h_s = x_sample.reshape(dbatch * dseq, D_MODEL)
    outs = [[] for _ in range(6)]
    for i in range(depth):
        w = _layer_weights(i, g_pre_mix, w_in, g_q, w_uq, w_uk, g_kv, w_ga, b_ga, w_uv, g_gla, w_out, g_post_mix,
                           g_pre_ffn, w_gate, w_up, w_down, g_post_ffn, w_ple, w_ple_gate)
        z = _inproj(h_p, w["g_pre_mix"], w["w_in"])
        ckvn, kr, qcat, kcat, v = _mla_prep_prompt(z, w["g_q"], w["g_kv"], w["w_uq"], w["w_ukv"], cos_p, sin_p, seq)
        o_mla = _attn_prompt(qcat, kcat, v, batch, seq)
        o_gla, s_fin = _gla(z, w["w_ga"], w["b_ga"], w["g_gla"], batch, seq, CHUNK, 256)
        h_p = _finish(w, h_p, p_prompt[i].reshape(batch * seq, PLE_DIM), o_mla, o_gla)
        outs[0].append(ckvn.reshape(batch, seq, KV_LORA))
        outs[1].append(kr.reshape(batch, seq, ROPE_DIM))
        outs[2].append(s_fin)
        z = _inproj(h_s, w["g_pre_mix"], w["w_in"])
        ckvn, kr, q_lat, q_rope = _mla_prep_sample(z, w["g_q"], w["g_kv"], w["w_uq"], w["w_ukt"], cos_s, sin_s)
        o_mla = _attn_sample(q_lat, q_rope, cache_ckv[i], cache_krope[i], ckvn, kr, w["w_uvh"], dseq)
        o_gla, s_new = _gla(z, w["w_ga"], w["b_ga"], w["g_gla"], dbatch, dseq, dseq, dseq, s0=state_gla[i])
        h_s = _finish(w, h_s, p_sample[i].reshape(dbatch * dseq, PLE_DIM), o_mla, o_gla)
        outs[3].append(ckvn.reshape(dbatch, dseq, KV_LORA))
        outs[4].append(kr.reshape(dbatch, dseq, ROPE_DIM))
        outs[5].append(s_new)
    return (h_p.reshape(batch, seq, D_MODEL), h_s.reshape(dbatch, dseq, D_MODEL),
            jnp.stack(outs[0]), jnp.stack(outs[1]), jnp.stack(outs[2]),
            jnp.stack(outs[3]), jnp.stack(outs[4]), jnp.stack(outs[5]))
```

```python
import functools

import numpy as np
import jax
import jax.numpy as jnp
from jax import lax
from jax.experimental import pallas as pl
from jax.experimental.pallas import tpu as pltpu

F32 = jnp.float32
BF16 = jnp.bfloat16

D_MODEL = 2048
PAST_LEN = 4096
CHUNK = 64
EPS = 1e-6

MLA_HEADS = 8
Q_LORA = 512
KV_LORA = 512
NOPE_DIM = 128
ROPE_DIM = 64
V_DIM = 128
ROPE_THETA = 10000.0
MLA_SCALE = (NOPE_DIM + ROPE_DIM) ** -0.5
QK_PAD = 256

GLA_HEADS = 4
GLA_DK = 128
GLA_DV = 256
GATE_RANK = 16
GATE_TAU = 16.0
GLA_WIDTH = GLA_HEADS * GLA_DV
MLA_WIDTH = MLA_HEADS * V_DIM

IN_SPLITS = (Q_LORA, KV_LORA, ROPE_DIM, GLA_HEADS * GLA_DK, GLA_HEADS * GLA_DK, GLA_WIDTH, GATE_RANK, GLA_WIDTH)
Z_MAIN = 4096
Z_WIDTH = Z_MAIN + 128
GLR_LO = ROPE_DIM

D_FF = 5632
PLE_DIM = 256

VMEM_LIMIT = 56 * 1024 * 1024
NEG_BIG = -1e30
LOG2E = 1.4426950408889634


def _dot(a, b):
    return jnp.dot(a, b, preferred_element_type=F32)


def _dot_nt(a, b):
    return lax.dot_general(a, b, (((1,), (1,)), ((), ())), preferred_element_type=F32)


def _dot_tn(a, b):
    return lax.dot_general(a, b, (((0,), (0,)), ((), ())), preferred_element_type=F32)


def _rms(x, g):
    ms = jnp.mean(x * x, axis=-1, keepdims=True)
    return x * lax.rsqrt(ms + EPS) * g


def _rope(x, cos_t, sin_t):
    lane = lax.broadcasted_iota(jnp.int32, x.shape, 1)
    rot = jnp.where((lane & 32) == 0, pltpu.roll(x, 96, 1), pltpu.roll(x, 32, 1))
    return x * cos_t + rot * sin_t


def _params(*sem):
    return pltpu.CompilerParams(dimension_semantics=sem, vmem_limit_bytes=VMEM_LIMIT)


def _inproj_kernel(x_ref, g_ref, w_ref, o_ref):
    o_ref[...] = _dot(_rms(x_ref[...], g_ref[...]).astype(BF16), w_ref[...])


def _inproj(x, g, w):
    T = x.shape[0]
    tm = 512
    return pl.pallas_call(
        _inproj_kernel,
        grid=(T // tm,),
        in_specs=[pl.BlockSpec((tm, D_MODEL), lambda i: (i, 0)),
                  pl.BlockSpec((1, D_MODEL), lambda i: (0, 0)),
                  pl.BlockSpec((D_MODEL, Z_WIDTH), lambda i: (0, 0), pipeline_mode=pl.Buffered(1))],
        out_specs=pl.BlockSpec((tm, Z_WIDTH), lambda i: (i, 0)),
        out_shape=jax.ShapeDtypeStruct((T, Z_WIDTH), F32),
        compiler_params=_params("parallel"),
        name="inproj",
    )(x, g, w)


def _mla_q(cq_ref, gq_ref, wuq_ref, cos, sin):
    cqn = _rms(cq_ref[...], gq_ref[...]).astype(BF16)
    q = _dot(cqn, wuq_ref[...])
    out = []
    for h in range(MLA_HEADS):
        nope = q[:, h * QK_PAD:h * QK_PAD + NOPE_DIM]
        rp = _rope(q[:, h * QK_PAD + NOPE_DIM:(h + 1) * QK_PAD], cos, sin)
        out.append((nope, rp))
    return out


def _mla_prep_prompt_kernel(cq_ref, ckv_ref, sm_ref, gq_ref, gkv_ref, wuq_ref, wukv_ref, cos_ref, sin_ref,
                            ckvn_ref, kr_ref, qcat_ref, kcat_ref, v_ref):
    cos, sin = cos_ref[...], sin_ref[...]
    for h, (nope, rp) in enumerate(_mla_q(cq_ref, gq_ref, wuq_ref, cos, sin)):
        qcat_ref[:, h * QK_PAD:h * QK_PAD + NOPE_DIM] = nope.astype(BF16)
        qcat_ref[:, h * QK_PAD + NOPE_DIM:(h + 1) * QK_PAD] = rp.astype(BF16)
    ckvn = _rms(ckv_ref[...], gkv_ref[...])
    ckvn_ref[...] = ckvn
    kr = _rope(sm_ref[...], cos, sin)
    kr_ref[...] = kr[:, :ROPE_DIM]
    kv = _dot(ckvn.astype(BF16), wukv_ref[...])
    krb = kr.astype(BF16)
    for h in range(MLA_HEADS):
        kcat_ref[:, h * QK_PAD:h * QK_PAD + NOPE_DIM] = kv[:, h * NOPE_DIM:(h + 1) * NOPE_DIM].astype(BF16)
        kcat_ref[:, h * QK_PAD + NOPE_DIM:(h + 1) * QK_PAD] = krb
    v_ref[...] = kv[:, MLA_WIDTH:].astype(BF16)


def _mla_prep_prompt(z, g_q, g_kv, w_uq, w_ukv, cos_t, sin_t, seq):
    T = z.shape[0]
    tm = 512
    nseq = seq // tm
    row = lambda i: (i, 0)
    const = lambda i: (0, 0)
    return pl.pallas_call(
        _mla_prep_prompt_kernel,
        grid=(T // tm,),
        in_specs=[pl.BlockSpec((tm, Q_LORA), lambda i: (i, 0)),
                  pl.BlockSpec((tm, KV_LORA), lambda i: (i, 1)),
                  pl.BlockSpec((tm, 128), lambda i: (i, Z_MAIN // 128)),
                  pl.BlockSpec((1, Q_LORA), const),
                  pl.BlockSpec((1, KV_LORA), const),
                  pl.BlockSpec((Q_LORA, MLA_HEADS * QK_PAD), const),
                  pl.BlockSpec((KV_LORA, 2 * MLA_WIDTH), const),
                  pl.BlockSpec((tm, 128), lambda i: (i % nseq, 0)),
                  pl.BlockSpec((tm, 128), lambda i: (i % nseq, 0))],
        out_specs=[pl.BlockSpec((tm, KV_LORA), row),
                   pl.BlockSpec((tm, ROPE_DIM), row),
                   pl.BlockSpec((tm, MLA_HEADS * QK_PAD), row),
                   pl.BlockSpec((tm, MLA_HEADS * QK_PAD), row),
                   pl.BlockSpec((tm, MLA_WIDTH), row)],
        out_shape=[jax.ShapeDtypeStruct((T, KV_LORA), F32),
                   jax.ShapeDtypeStruct((T, ROPE_DIM), F32),
                   jax.ShapeDtypeStruct((T, MLA_HEADS * QK_PAD), BF16),
                   jax.ShapeDtypeStruct((T, MLA_HEADS * QK_PAD), BF16),
                   jax.ShapeDtypeStruct((T, MLA_WIDTH), BF16)],
        compiler_params=_params("parallel"),
        name="mla_prep_prompt",
    )(z, z, z, g_q, g_kv, w_uq, w_ukv, cos_t, sin_t)


def _mla_prep_sample_kernel(cq_ref, ckv_ref, sm_ref, gq_ref, gkv_ref, wuq_ref, wukt_ref, cos_ref, sin_ref,
                            ckvn_ref, kr_ref, qlat_ref, qr_ref):
    cos, sin = cos_ref[...], sin_ref[...]
    for h, (nope, rp) in enumerate(_mla_q(cq_ref, gq_ref, wuq_ref, cos, sin)):
        qlat_ref[h] = _dot(nope.astype(BF16), wukt_ref[h]).astype(BF16)
        qr_ref[h] = rp[:, :ROPE_DIM].astype(BF16)
    ckvn_ref[...] = _rms(ckv_ref[...], gkv_ref[...])
    kr_ref[...] = _rope(sm_ref[...], cos, sin)[:, :ROPE_DIM]


def _mla_prep_sample(z, g_q, g_kv, w_uq, w_ukt, cos_t, sin_t):
    T = z.shape[0]
    tm = T
    row = lambda i: (i, 0)
    const = lambda i: (0, 0)
    const3 = lambda i: (0, 0, 0)
    return pl.pallas_call(
        _mla_prep_sample_kernel,
        grid=(T // tm,),
        in_specs=[pl.BlockSpec((tm, Q_LORA), lambda i: (i, 0)),
                  pl.BlockSpec((tm, KV_LORA), lambda i: (i, 1)),
                  pl.BlockSpec((tm, 128), lambda i: (i, Z_MAIN // 128)),
                  pl.BlockSpec((1, Q_LORA), const),
                  pl.BlockSpec((1, KV_LORA), const),
                  pl.BlockSpec((Q_LORA, MLA_HEADS * QK_PAD), const),
                  pl.BlockSpec((MLA_HEADS, NOPE_DIM, KV_LORA), const3),
                  pl.BlockSpec((tm, 128), row),
                  pl.BlockSpec((tm, 128), row)],
        out_specs=[pl.BlockSpec((tm, KV_LORA), row),
                   pl.BlockSpec((tm, ROPE_DIM), row),
                   pl.BlockSpec((MLA_HEADS, tm, KV_LORA), lambda i: (0, i, 0)),
                   pl.BlockSpec((MLA_HEADS, tm, ROPE_DIM), lambda i: (0, i, 0))],
        out_shape=[jax.ShapeDtypeStruct((T, KV_LORA), F32),
                   jax.ShapeDtypeStruct((T, ROPE_DIM), F32),
                   jax.ShapeDtypeStruct((MLA_HEADS, T, KV_LORA), BF16),
                   jax.ShapeDtypeStruct((MLA_HEADS, T, ROPE_DIM), BF16)],
        compiler_params=_params("parallel"),
        name="mla_prep_sample",
    )(z, z, z, g_q, g_kv, w_uq, w_ukt, cos_t, sin_t)


ATT_TQ = 256
ATT_TK = 256


def _softmax_step(s, v, m, l, acc):
    m_new = jnp.maximum(m, jnp.max(s, axis=-1, keepdims=True))
    p = jnp.exp(s - m_new)
    alpha = jnp.exp(m - m_new)
    l = alpha * l + jnp.sum(p, axis=-1, keepdims=True)
    acc = alpha * acc + _dot(p.astype(BF16), v)
    return m_new, l, acc


def _attn_prompt_kernel(q_ref, k_ref, v_ref, o_ref):
    seq = q_ref.shape[0]
    qc = lax.broadcasted_iota(jnp.int32, (ATT_TQ, ATT_TQ), 0) // CHUNK
    kc = lax.broadcasted_iota(jnp.int32, (ATT_TQ, ATT_TQ), 1) // CHUNK
    visible = kc <= qc
    c = MLA_SCALE * LOG2E
    for i in range(seq // ATT_TQ):
        lo, hi = i * ATT_TQ, (i + 1) * ATT_TQ
        s = _dot_nt(q_ref[lo:hi, :], k_ref[0:hi, :])
        s_diag = jnp.where(visible, s[:, lo:hi], NEG_BIG)
        s = jnp.concatenate([s[:, :lo], s_diag], axis=1) if i else s_diag
        m = jnp.max(s, axis=-1, keepdims=True)
        p = jnp.exp2((s - m) * c)
        l = jnp.sum(p, axis=-1, keepdims=True)
        o_ref[lo:hi, :] = (_dot(p.astype(BF16), v_ref[0:hi, :]) / l).astype(BF16)


def _attn_prompt(qcat, kcat, v, batch, seq):
    return pl.pallas_call(
        _attn_prompt_kernel,
        grid=(batch, MLA_HEADS),
        in_specs=[pl.BlockSpec((seq, QK_PAD), lambda b, h: (b, h)),
                  pl.BlockSpec((seq, QK_PAD), lambda b, h: (b, h)),
                  pl.BlockSpec((seq, V_DIM), lambda b, h: (b, h))],
        out_specs=pl.BlockSpec((seq, V_DIM), lambda b, h: (b, h)),
        out_shape=jax.ShapeDtypeStruct((batch * seq, MLA_WIDTH), BF16),
        compiler_params=_params("parallel", "parallel"),
        name="attn_prompt",
    )(qcat, kcat, v)


DEC_TK = 1024


def _attn_sample_kernel(ql_ref, qr_ref, ckv_ref, kr_ref, nckv_ref, nkr_ref, wuv_ref, o_ref, m_ref, l_ref, acc_ref):
    j = pl.program_id(1)
    dec = ql_ref.shape[1]
    rows = MLA_HEADS * dec
    ql = ql_ref[...].reshape(rows, KV_LORA)
    qr = qr_ref[...].reshape(rows, ROPE_DIM)

    @pl.when(j == 0)
    def _():
        m_ref[...] = jnp.full(m_ref.shape, NEG_BIG, F32)
        l_ref[...] = jnp.zeros(l_ref.shape, F32)
        acc_ref[...] = jnp.zeros(acc_ref.shape, F32)

    def step(ckv, kr):
        ckv = ckv.astype(BF16)
        s = (_dot_nt(ql, ckv) + _dot_nt(qr, kr.astype(BF16))) * MLA_SCALE
        m, l, acc = _softmax_step(s, ckv, m_ref[...], l_ref[...], acc_ref[...])
        m_ref[...] = m
        l_ref[...] = l
        acc_ref[...] = acc

    step(ckv_ref[...], kr_ref[...])

    @pl.when(j == pl.num_programs(1) - 1)
    def _():
        step(nckv_ref[...], nkr_ref[...])
        o = (acc_ref[...] / l_ref[...]).astype(BF16)
        for h in range(MLA_HEADS):
            o_ref[:, h * V_DIM:(h + 1) * V_DIM] = _dot(o[h * dec:(h + 1) * dec], wuv_ref[h]).astype(BF16)


def _attn_sample(q_lat, q_rope, cache_ckv, cache_kr, ckvn, kr, w_uvh, dec):
    batch, past, _ = cache_ckv.shape
    rows = MLA_HEADS * dec
    return pl.pallas_call(
        _attn_sample_kernel,
        grid=(batch, past // DEC_TK),
        in_specs=[pl.BlockSpec((MLA_HEADS, dec, KV_LORA), lambda b, j: (0, b, 0)),
                  pl.BlockSpec((MLA_HEADS, dec, ROPE_DIM), lambda b, j: (0, b, 0)),
                  pl.BlockSpec((None, DEC_TK, KV_LORA), lambda b, j: (b, j, 0)),
                  pl.BlockSpec((None, DEC_TK, ROPE_DIM), lambda b, j: (b, j, 0)),
                  pl.BlockSpec((dec, KV_LORA), lambda b, j: (b, 0)),
                  pl.BlockSpec((dec, ROPE_DIM), lambda b, j: (b, 0)),
                  pl.BlockSpec((MLA_HEADS, KV_LORA, V_DIM), lambda b, j: (0, 0, 0))],
        out_specs=pl.BlockSpec((dec, MLA_WIDTH), lambda b, j: (b, 0)),
        out_shape=jax.ShapeDtypeStruct((batch * dec, MLA_WIDTH), BF16),
        scratch_shapes=[pltpu.VMEM((rows, 1), F32), pltpu.VMEM((rows, 1), F32), pltpu.VMEM((rows, KV_LORA), F32)],
        compiler_params=_params("parallel", "arbitrary"),
        name="attn_sample",
    )(q_lat, q_rope, cache_ckv, cache_kr, ckvn, kr, w_uvh)


GLA_CHUNK = 256


def _gla_tables(c):
    nlev = int(np.log2(c))
    t = np.arange(c)[:, None]
    u = np.arange(c)[None, :]
    blocks = []
    lvl = np.full((c, c), -1, np.int32)
    for l in range(nlev):
        width = c >> l
        half = width // 2
        m = (t // width) * width + half - 1
        upper = t > m
        blocks.append(np.where(upper, (u > m) & (u <= t), (u > t) & (u <= m)))
        same = (t // width) == (u // width)
        lvl[same & ((t % width) >= half) & ((u % width) < half)] = l
    blocks.append(u > t)
    blocks.append(u <= t)
    lvl[np.arange(c), np.arange(c)] = nlev
    return np.concatenate(blocks, 0).astype(np.float32), lvl, nlev


def _gla_kernel(*refs, c, nlev, has_init):
    if has_init:
        (q_ref, k_ref, v_ref, r_ref, sm_ref, wga_ref, bga_ref, gg_ref, p_ref, lvl_ref, s0_ref,
         o_ref, sfin_ref, s_scr) = refs
    else:
        (q_ref, k_ref, v_ref, r_ref, sm_ref, wga_ref, bga_ref, gg_ref, p_ref, lvl_ref,
         o_ref, sfin_ref, s_scr) = refs
    j = pl.program_id(1)

    @pl.when(j == 0)
    def _():
        if has_init:
            s_scr[...] = s0_ref[0]
        else:
            s_scr[...] = jnp.zeros(s_scr.shape, F32)

    x = _dot(sm_ref[...].astype(BF16), wga_ref[...]) + bga_ref[...]
    la = (jnp.minimum(x, 0.0) - jnp.log1p(jnp.exp(-jnp.abs(x)))) * (1.0 / GATE_TAU)
    hi = la.astype(BF16)
    mid = (la - hi.astype(F32)).astype(BF16)
    p_mat = p_ref[...]
    e_all = jnp.exp(_dot(p_mat, hi) + _dot(p_mat, mid))
    ones = jnp.ones((c, 128), BF16)
    d_state = jnp.exp(_dot_tn(hi, ones) + _dot_tn(mid, ones))
    q = q_ref[...] * (GLA_DK ** -0.5)
    k = k_ref[...]
    qe = [(q * e_all[l * c:(l + 1) * c]).astype(BF16) for l in range(nlev)] + [q.astype(BF16)]
    ke = [(k * e_all[l * c:(l + 1) * c]).astype(BF16) for l in range(nlev)] + [k.astype(BF16)]
    k_end = (k * e_all[nlev * c:(nlev + 1) * c]).astype(BF16)
    q_beg = (q * e_all[(nlev + 1) * c:(nlev + 2) * c]).astype(BF16)
    lvl = lvl_ref[...]
    for h in range(GLA_HEADS):
        dk = slice(h * GLA_DK, (h + 1) * GLA_DK)
        dv = slice(h * GLA_DV, (h + 1) * GLA_DV)
        attn = jnp.zeros((c, c), F32)
        for l in range(nlev + 1):
            attn = jnp.where(lvl == l, _dot_nt(qe[l][:, dk], ke[l][:, dk]), attn)
        v = v_ref[:, dv].astype(BF16)
        state = s_scr[h]
        o = _dot(attn.astype(BF16), v) + _dot(q_beg[:, dk], state.astype(BF16))
        decay = d_state[dk]
        s_scr[h] = jnp.concatenate([decay, decay], axis=1) * state + _dot_tn(k_end[:, dk], v)
        r = r_ref[:, dv]
        o_ref[:, dv] = (_rms(o, gg_ref[...]) * (r * jax.nn.sigmoid(r))).astype(BF16)

    @pl.when(j == pl.num_programs(1) - 1)
    def _():
        sfin_ref[0] = s_scr[...]


def _gla(z, w_ga, b_ga, g_gla, batch, seq, c, s0=None):
    p_np, lvl_np, nlev = _gla_tables(c)
    p_mat = jnp.asarray(p_np, BF16)
    lvl = jnp.asarray(lvl_np)
    nstep = seq // c
    r = c
    row = lambda b, j: (b * nstep + j, 0)
    const = lambda b, j: (0, 0)
    in_specs = [pl.BlockSpec((r, 512), lambda b, j: (b * nstep + j, 2)),
                pl.BlockSpec((r, 512), lambda b, j: (b * nstep + j, 3)),
                pl.BlockSpec((r, GLA_WIDTH), lambda b, j: (b * nstep + j, 2)),
                pl.BlockSpec((r, GLA_WIDTH), lambda b, j: (b * nstep + j, 3)),
                pl.BlockSpec((r, 128), lambda b, j: (b * nstep + j, Z_MAIN // 128)),
                pl.BlockSpec((128, GLA_HEADS * GLA_DK), const),
                pl.BlockSpec((1, GLA_HEADS * GLA_DK), const),
                pl.BlockSpec((1, GLA_DV), const),
                pl.BlockSpec(p_np.shape, const),
                pl.BlockSpec((c, c), const)]
    args = [z, z, z, z, z, w_ga, b_ga, g_gla, p_mat, lvl]
    state_spec = pl.BlockSpec((1, GLA_HEADS, GLA_DK, GLA_DV), lambda b, j: (b, 0, 0, 0))
    if s0 is not None:
        in_specs.append(state_spec)
        args.append(s0)
    return pl.pallas_call(
        functools.partial(_gla_kernel, c=c, nlev=nlev, has_init=s0 is not None),
        grid=(batch, nstep),
        in_specs=in_specs,
        out_specs=[pl.BlockSpec((r, GLA_WIDTH), row), state_spec],
        out_shape=[jax.ShapeDtypeStruct((batch * seq, GLA_WIDTH), BF16),
                   jax.ShapeDtypeStruct((batch, GLA_HEADS, GLA_DK, GLA_DV), F32)],
        scratch_shapes=[pltpu.VMEM((GLA_HEADS, GLA_DK, GLA_DV), F32)],
        compiler_params=_params("parallel", "arbitrary"),
        name="gla_init" if s0 is not None else "gla",
    )(*args)


def _mix_kernel(om_ref, og_ref, w1_ref, w2_ref, h_ref, g1_ref, g2_ref, h1_ref, f_ref):
    mix = _dot(om_ref[...], w1_ref[...]) + _dot(og_ref[...], w2_ref[...])
    h1 = h_ref[...] + _rms(mix, g1_ref[...])
    h1_ref[...] = h1
    f_ref[...] = _rms(h1, g2_ref[...]).astype(BF16)


def _mix(o_mla, o_gla, w_out, h, g_post_mix, g_pre_ffn):
    T = h.shape[0]
    tm = 512
    row = lambda i: (i, 0)
    const = lambda i: (0, 0)
    return pl.pallas_call(
        _mix_kernel,
        grid=(T // tm,),
        in_specs=[pl.BlockSpec((tm, MLA_WIDTH), row),
                  pl.BlockSpec((tm, GLA_WIDTH), row),
                  pl.BlockSpec((MLA_WIDTH, D_MODEL), lambda i: (0, 0), pipeline_mode=pl.Buffered(1)),
                  pl.BlockSpec((GLA_WIDTH, D_MODEL), lambda i: (1, 0), pipeline_mode=pl.Buffered(1)),
                  pl.BlockSpec((tm, D_MODEL), row),
                  pl.BlockSpec((1, D_MODEL), const),
                  pl.BlockSpec((1, D_MODEL), const)],
        out_specs=[pl.BlockSpec((tm, D_MODEL), row), pl.BlockSpec((tm, D_MODEL), row)],
        out_shape=[jax.ShapeDtypeStruct((T, D_MODEL), F32), jax.ShapeDtypeStruct((T, D_MODEL), BF16)],
        compiler_params=_params("parallel"),
        name="mix",
    )(o_mla, o_gla, w_out, w_out, h, g_post_mix, g_pre_ffn)


def _ffn_kernel(x_ref, wg_ref, wu_ref, wd_ref, o_ref):
    @pl.when(pl.program_id(1) == 0)
    def _():
        o_ref[...] = jnp.zeros(o_ref.shape, F32)

    x = x_ref[...]
    g = _dot(x, wg_ref[...])
    u = _dot(x, wu_ref[...])
    o_ref[...] += _dot((g * jax.nn.sigmoid(g) * u).astype(BF16), wd_ref[...])


def _ffn(x, w_gate, w_up, w_down):
    T = x.shape[0]
    tm, tf = min(T, 1024), 512
    return pl.pallas_call(
        _ffn_kernel,
        grid=(T // tm, D_FF // tf),
        in_specs=[pl.BlockSpec((tm, D_MODEL), lambda i, j: (i, 0)),
                  pl.BlockSpec((D_MODEL, tf), lambda i, j: (0, j)),
                  pl.BlockSpec((D_MODEL, tf), lambda i, j: (0, j)),
                  pl.BlockSpec((tf, D_MODEL), lambda i, j: (j, 0))],
        out_specs=pl.BlockSpec((tm, D_MODEL), lambda i, j: (i, 0)),
        out_shape=jax.ShapeDtypeStruct((T, D_MODEL), F32),
        compiler_params=_params("parallel", "arbitrary"),
        name="ffn",
    )(x, w_gate, w_up, w_down)


def _final_kernel(f_ref, h1_ref, p_ref, g_ref, wpg_ref, wp_ref, o_ref):
    h2 = h1_ref[...] + _rms(f_ref[...], g_ref[...])
    gate = jax.nn.sigmoid(_dot(h2.astype(BF16), wpg_ref[...]))
    o_ref[...] = h2 + gate * _dot(p_ref[...].astype(BF16), wp_ref[...])


def _final(f, h1, p, g_post_ffn, w_ple_gate, w_ple):
    T = f.shape[0]
    tm = 512
    row = lambda i: (i, 0)
    const = lambda i: (0, 0)
    return pl.pallas_call(
        _final_kernel,
        grid=(T // tm,),
        in_specs=[pl.BlockSpec((tm, D_MODEL), row),
                  pl.BlockSpec((tm, D_MODEL), row),
                  pl.BlockSpec((tm, PLE_DIM), row),
                  pl.BlockSpec((1, D_MODEL), const),
                  pl.BlockSpec((D_MODEL, D_MODEL), const, pipeline_mode=pl.Buffered(1)),
                  pl.BlockSpec((PLE_DIM, D_MODEL), const)],
        out_specs=pl.BlockSpec((tm, D_MODEL), row),
        out_shape=jax.ShapeDtypeStruct((T, D_MODEL), F32),
        compiler_params=_params("parallel"),
        name="final",
    )(f, h1, p, g_post_ffn, w_ple_gate, w_ple)


def _rope_tables(pos):
    half = ROPE_DIM // 2
    inv = 1.0 / (ROPE_THETA ** (jnp.arange(half, dtype=F32) / half))
    ang = pos.astype(F32)[:, None] * inv[None, :]
    cos, sin = jnp.cos(ang), jnp.sin(ang)
    zero = jnp.zeros((pos.shape[0], 128 - ROPE_DIM), F32)
    return jnp.concatenate([cos, cos, zero], axis=1), jnp.concatenate([-sin, sin, zero], axis=1)


def _layer_weights(i, g_pre_mix, w_in, g_q, w_uq, w_uk, g_kv, w_ga, b_ga, w_uv, g_gla, w_out, g_post_mix,
                   g_pre_ffn, w_gate, w_up, w_down, g_post_ffn, w_ple, w_ple_gate):
    idx = np.cumsum(IN_SPLITS)[:-1].tolist()
    c_q, c_kv, k_r, q_g, k_g, v_g, g_lr, r_g = jnp.split(w_in[i], idx, axis=1)
    pad = jnp.zeros((D_MODEL, 128 - ROPE_DIM - GATE_RANK), F32)
    w_in_p = jnp.concatenate([c_q, c_kv, q_g, k_g, v_g, r_g, k_r, g_lr, pad], axis=1).astype(BF16)
    w_uq_p = jnp.pad(w_uq[i].reshape(Q_LORA, MLA_HEADS, NOPE_DIM + ROPE_DIM),
                     ((0, 0), (0, 0), (0, QK_PAD - NOPE_DIM - ROPE_DIM))).reshape(Q_LORA, MLA_HEADS * QK_PAD)
    w_ga_p = jnp.zeros((128, GLA_HEADS * GLA_DK), F32).at[GLR_LO:GLR_LO + GATE_RANK].set(w_ga[i])
    vec = lambda g: g[i].reshape(1, -1)
    return dict(
        g_pre_mix=vec(g_pre_mix), w_in=w_in_p, g_q=vec(g_q), g_kv=vec(g_kv),
        w_uq=w_uq_p.astype(BF16),
        w_ukv=jnp.concatenate([w_uk[i].reshape(KV_LORA, MLA_WIDTH), w_uv[i].reshape(KV_LORA, MLA_WIDTH)],
                              axis=1).astype(BF16),
        w_ukt=jnp.transpose(w_uk[i], (1, 2, 0)).astype(BF16),
        w_uvh=jnp.transpose(w_uv[i], (1, 0, 2)).astype(BF16),
        w_ga=w_ga_p.astype(BF16), b_ga=vec(b_ga), g_gla=vec(g_gla),
        w_out=w_out[i].astype(BF16), g_post_mix=vec(g_post_mix), g_pre_ffn=vec(g_pre_ffn),
        w_gate=w_gate[i].astype(BF16), w_up=w_up[i].astype(BF16), w_down=w_down[i].astype(BF16),
        g_post_ffn=vec(g_post_ffn), w_ple=w_ple[i].astype(BF16), w_ple_gate=w_ple_gate[i].astype(BF16))


def _finish(w, h, p, o_mla, o_gla):
    h1, f_in = _mix(o_mla, o_gla, w["w_out"], h, w["g_post_mix"], w["g_pre_ffn"])
    f = _ffn(f_in, w["w_gate"], w["w_up"], w["w_down"])
    return _final(f, h1, p, w["g_post_ffn"], w["w_ple_gate"], w["w_ple"])


def kernel(x_prompt, x_sample, cache_ckv, cache_krope, state_gla, p_prompt, p_sample, g_pre_mix, w_in, g_q, w_uq,
           w_uk, g_kv, w_ga, b_ga, w_uv, g_gla, w_out, g_post_mix, g_pre_ffn, w_gate, w_up, w_down, g_post_ffn,
           w_ple, w_ple_gate):
    batch, seq, _ = x_prompt.shape
    dbatch, dseq, _ = x_sample.shape
    depth = w_in.shape[0]
    cos_p, sin_p = _rope_tables(jnp.arange(seq))
    cos_s, sin_s = _rope_tables(PAST_LEN + jnp.arange(dseq))
    cos_s, sin_s = jnp.tile(cos_s, (dbatch, 1)), jnp.tile(sin_s, (dbatch, 1))
    h_p = x_prompt.reshape(batch * seq, D_MODEL)
    h_s = x_sample.reshape(dbatch * dseq, D_MODEL)
    outs = [[] for _ in range(6)]
    for i in range(depth):
        w = _layer_weights(i, g_pre_mix, w_in, g_q, w_uq, w_uk, g_kv, w_ga, b_ga, w_uv, g_gla, w_out, g_post_mix,
                           g_pre_ffn, w_gate, w_up, w_down, g_post_ffn, w_ple, w_ple_gate)
        z = _inproj(h_p, w["g_pre_mix"], w["w_in"])
        ckvn, kr, qcat, kcat, v = _mla_prep_prompt(z, w["g_q"], w["g_kv"], w["w_uq"], w["w_ukv"], cos_p, sin_p, seq)
        o_mla = _attn_prompt(qcat, kcat, v, batch, seq)
        o_gla, s_fin = _gla(z, w["w_ga"], w["b_ga"], w["g_gla"], batch, seq, GLA_CHUNK)
        h_p = _finish(w, h_p, p_prompt[i].reshape(batch * seq, PLE_DIM), o_mla, o_gla)
        outs[0].append(ckvn.reshape(batch, seq, KV_LORA))
        outs[1].append(kr.reshape(batch, seq, ROPE_DIM))
        outs[2].append(s_fin)
        z = _inproj(h_s, w["g_pre_mix"], w["w_in"])
        ckvn, kr, q_lat, q_rope = _mla_prep_sample(z, w["g_q"], w["g_kv"], w["w_uq"], w["w_ukt"], cos_s, sin_s)
        o_mla = _attn_sample(q_lat, q_rope, cache_ckv[i], cache_krope[i], ckvn, kr, w["w_uvh"], dseq)
        o_gla, s_new = _gla(z, w["w_ga"], w["b_ga"], w["g_gla"], dbatch, dseq, dseq, s0=state_gla[i])
        h_s = _finish(w, h_s, p_sample[i].reshape(dbatch * dseq, PLE_DIM), o_mla, o_gla)
        outs[3].append(ckvn.reshape(dbatch, dseq, KV_LORA))
        outs[4].append(kr.reshape(dbatch, dseq, ROPE_DIM))
        outs[5].append(s_new)
    return (h_p.reshape(batch, seq, D_MODEL), h_s.reshape(dbatch, dseq, D_MODEL),
            jnp.stack(outs[0]), jnp.stack(outs[1]), jnp.stack(outs[2]),
            jnp.stack(outs[3]), jnp.stack(outs[4]), jnp.stack(outs[5]))
```

```python
import functools

import numpy as np
import jax
import jax.numpy as jnp
from jax import lax
from jax.experimental import pallas as pl
from jax.experimental.pallas import tpu as pltpu

F32 = jnp.float32
BF16 = jnp.bfloat16

D_MODEL = 2048
PAST_LEN = 4096
CHUNK = 64
EPS = 1e-6

MLA_HEADS = 8
Q_LORA = 512
KV_LORA = 512
NOPE_DIM = 128
ROPE_DIM = 64
V_DIM = 128
ROPE_THETA = 10000.0
MLA_SCALE = (NOPE_DIM + ROPE_DIM) ** -0.5
QK_PAD = 256

GLA_HEADS = 4
GLA_DK = 128
GLA_DV = 256
GATE_RANK = 16
GATE_TAU = 16.0
GLA_WIDTH = GLA_HEADS * GLA_DV
MLA_WIDTH = MLA_HEADS * V_DIM

IN_SPLITS = (Q_LORA, KV_LORA, ROPE_DIM, GLA_HEADS * GLA_DK, GLA_HEADS * GLA_DK, GLA_WIDTH, GATE_RANK, GLA_WIDTH)
Z_MAIN = 4096
GLR_LO = ROPE_DIM

D_FF = 5632
PLE_DIM = 256

VMEM_LIMIT = 56 * 1024 * 1024
NEG_BIG = -1e30
LOG2E = 1.4426950408889634


def _dot(a, b):
    return jnp.dot(a, b, preferred_element_type=F32)


def _dot_nt(a, b):
    return lax.dot_general(a, b, (((1,), (1,)), ((), ())), preferred_element_type=F32)


def _dot_tn(a, b):
    return lax.dot_general(a, b, (((0,), (0,)), ((), ())), preferred_element_type=F32)


def _rms(x, g):
    ms = jnp.mean(x * x, axis=-1, keepdims=True)
    return x * lax.rsqrt(ms + EPS) * g


def _rope(x, cos_t, sin_t):
    lane = lax.broadcasted_iota(jnp.int32, x.shape, 1)
    rot = jnp.where((lane & 32) == 0, pltpu.roll(x, 96, 1), pltpu.roll(x, 32, 1))
    return x * cos_t + rot * sin_t


def _params(*sem):
    return pltpu.CompilerParams(dimension_semantics=sem, vmem_limit_bytes=VMEM_LIMIT)


_IN_OFF = np.concatenate([[0], np.cumsum(IN_SPLITS)]).tolist()


_WIN_TILE = 512


def _win_prep_kernel(w_ref, kr_ref, glr_ref, o_ref, small_ref):
    o_ref[...] = w_ref[...].T.astype(BF16)

    @pl.when(pl.program_id(0) == 0)
    def _():
        pad = jnp.zeros((128 - GLR_LO - GATE_RANK, D_MODEL), F32)
        small_ref[...] = jnp.concatenate([kr_ref[...], glr_ref[...], pad], axis=0).T.astype(BF16)


def _win_prep(w_t):
    c_q, c_kv, k_r, q_g, k_g, v_g, g_lr, r_g, end = _IN_OFF
    n_head, n_mid = k_r // _WIN_TILE, (g_lr - q_g) // _WIN_TILE
    assert k_r % _WIN_TILE == 0 and (g_lr - q_g) % _WIN_TILE == 0 and (end - r_g) % _WIN_TILE == 0
    assert (n_head + n_mid) * _WIN_TILE + end - r_g == Z_MAIN and GLR_LO == ROPE_DIM

    def src_row(j):
        skip_mid, skip_tail = (q_g - k_r) // 8, (r_g - g_lr + q_g - k_r) // 8
        return 8 * (j * (_WIN_TILE // 8) + jnp.where(j < n_head, 0, jnp.where(j < n_head + n_mid, skip_mid, skip_tail)))

    return pl.pallas_call(
        _win_prep_kernel,
        grid=(Z_MAIN // _WIN_TILE,),
        in_specs=[pl.BlockSpec((pl.Element(_WIN_TILE), pl.Element(D_MODEL)), lambda j: (src_row(j), 0)),
                  pl.BlockSpec((pl.Element(ROPE_DIM), pl.Element(D_MODEL)), lambda j: (k_r, 0)),
                  pl.BlockSpec((pl.Element(GATE_RANK), pl.Element(D_MODEL)), lambda j: (g_lr, 0))],
        out_specs=[pl.BlockSpec((D_MODEL, _WIN_TILE), lambda j: (0, j)),
                   pl.BlockSpec((D_MODEL, 128), lambda j: (0, 0))],
        out_shape=[jax.ShapeDtypeStruct((D_MODEL, Z_MAIN), BF16), jax.ShapeDtypeStruct((D_MODEL, 128), BF16)],
        compiler_params=_params("arbitrary"),
        name="win_prep",
    )(w_t, w_t, w_t)


def _inproj_kernel(x_ref, g_ref, w_ref, ws_ref, o_ref, os_ref):
    xn = _rms(x_ref[...], g_ref[...]).astype(BF16)
    o_ref[...] = _dot(xn, w_ref[...])
    os_ref[...] = _dot(xn, ws_ref[...])


def _inproj(x, g, w_main, w_small):
    T = x.shape[0]
    tm = 512
    return pl.pallas_call(
        _inproj_kernel,
        grid=(T // tm,),
        in_specs=[pl.BlockSpec((tm, D_MODEL), lambda i: (i, 0)),
                  pl.BlockSpec((1, D_MODEL), lambda i: (0, 0)),
                  pl.BlockSpec((D_MODEL, Z_MAIN), lambda i: (0, 0), pipeline_mode=pl.Buffered(1)),
                  pl.BlockSpec((D_MODEL, 128), lambda i: (0, 0), pipeline_mode=pl.Buffered(1))],
        out_specs=[pl.BlockSpec((tm, Z_MAIN), lambda i: (i, 0)), pl.BlockSpec((tm, 128), lambda i: (i, 0))],
        out_shape=[jax.ShapeDtypeStruct((T, Z_MAIN), F32), jax.ShapeDtypeStruct((T, 128), F32)],
        compiler_params=_params("parallel"),
        name="inproj",
    )(x, g, w_main, w_small)


def _mla_q(cq_ref, gq_ref, wuq_ref, cos, sin):
    cqn = _rms(cq_ref[...], gq_ref[...]).astype(BF16)
    q = _dot(cqn, wuq_ref[...])
    out = []
    for h in range(MLA_HEADS):
        nope = q[:, h * QK_PAD:h * QK_PAD + NOPE_DIM]
        rp = _rope(q[:, h * QK_PAD + NOPE_DIM:(h + 1) * QK_PAD], cos, sin)
        out.append((nope, rp))
    return out


def _mla_prep_prompt_kernel(cq_ref, ckv_ref, sm_ref, gq_ref, gkv_ref, wuq_ref, wukv_ref, cos_ref, sin_ref,
                            ckvn_ref, kr_ref, qcat_ref, kcat_ref, v_ref):
    cos, sin = cos_ref[...], sin_ref[...]
    for h, (nope, rp) in enumerate(_mla_q(cq_ref, gq_ref, wuq_ref, cos, sin)):
        qcat_ref[:, h * QK_PAD:h * QK_PAD + NOPE_DIM] = nope.astype(BF16)
        qcat_ref[:, h * QK_PAD + NOPE_DIM:(h + 1) * QK_PAD] = rp.astype(BF16)
    ckvn = _rms(ckv_ref[...], gkv_ref[...])
    ckvn_ref[...] = ckvn
    kr = _rope(sm_ref[...], cos, sin)
    kr_ref[...] = kr[:, :ROPE_DIM]
    kv = _dot(ckvn.astype(BF16), wukv_ref[...])
    krb = kr.astype(BF16)
    for h in range(MLA_HEADS):
        kcat_ref[:, h * QK_PAD:h * QK_PAD + NOPE_DIM] = kv[:, h * NOPE_DIM:(h + 1) * NOPE_DIM].astype(BF16)
        kcat_ref[:, h * QK_PAD + NOPE_DIM:(h + 1) * QK_PAD] = krb
    v_ref[...] = kv[:, MLA_WIDTH:].astype(BF16)


def _mla_prep_prompt(z, zs, g_q, g_kv, w_uq, w_ukv, cos_t, sin_t, seq):
    T = z.shape[0]
    tm = 512
    nseq = seq // tm
    row = lambda i: (i, 0)
    const = lambda i: (0, 0)
    return pl.pallas_call(
        _mla_prep_prompt_kernel,
        grid=(T // tm,),
        in_specs=[pl.BlockSpec((tm, Q_LORA), lambda i: (i, 0)),
                  pl.BlockSpec((tm, KV_LORA), lambda i: (i, 1)),
                  pl.BlockSpec((tm, 128), lambda i: (i, 0)),
                  pl.BlockSpec((1, Q_LORA), const),
                  pl.BlockSpec((1, KV_LORA), const),
                  pl.BlockSpec((Q_LORA, MLA_HEADS * QK_PAD), const),
                  pl.BlockSpec((KV_LORA, 2 * MLA_WIDTH), const),
                  pl.BlockSpec((tm, 128), lambda i: (i % nseq, 0)),
                  pl.BlockSpec((tm, 128), lambda i: (i % nseq, 0))],
        out_specs=[pl.BlockSpec((tm, KV_LORA), row),
                   pl.BlockSpec((tm, ROPE_DIM), row),
                   pl.BlockSpec((tm, MLA_HEADS * QK_PAD), row),
                   pl.BlockSpec((tm, MLA_HEADS * QK_PAD), row),
                   pl.BlockSpec((tm, MLA_WIDTH), row)],
        out_shape=[jax.ShapeDtypeStruct((T, KV_LORA), F32),
                   jax.ShapeDtypeStruct((T, ROPE_DIM), F32),
                   jax.ShapeDtypeStruct((T, MLA_HEADS * QK_PAD), BF16),
                   jax.ShapeDtypeStruct((T, MLA_HEADS * QK_PAD), BF16),
                   jax.ShapeDtypeStruct((T, MLA_WIDTH), BF16)],
        compiler_params=_params("parallel"),
        name="mla_prep_prompt",
    )(z, z, zs, g_q, g_kv, w_uq, w_ukv, cos_t, sin_t)


def _mla_prep_sample_kernel(cq_ref, ckv_ref, sm_ref, gq_ref, gkv_ref, wuq_ref, wukt_ref, cos_ref, sin_ref,
                            ckvn_ref, kr_ref, qlat_ref, qr_ref):
    cos, sin = cos_ref[...], sin_ref[...]
    for h, (nope, rp) in enumerate(_mla_q(cq_ref, gq_ref, wuq_ref, cos, sin)):
        qlat_ref[h] = _dot(nope.astype(BF16), wukt_ref[h]).astype(BF16)
        qr_ref[h] = rp[:, :ROPE_DIM].astype(BF16)
    ckvn_ref[...] = _rms(ckv_ref[...], gkv_ref[...])
    kr_ref[...] = _rope(sm_ref[...], cos, sin)[:, :ROPE_DIM]


def _mla_prep_sample(z, zs, g_q, g_kv, w_uq, w_ukt, cos_t, sin_t):
    T = z.shape[0]
    tm = T
    row = lambda i: (i, 0)
    const = lambda i: (0, 0)
    const3 = lambda i: (0, 0, 0)
    return pl.pallas_call(
        _mla_prep_sample_kernel,
        grid=(T // tm,),
        in_specs=[pl.BlockSpec((tm, Q_LORA), lambda i: (i, 0)),
                  pl.BlockSpec((tm, KV_LORA), lambda i: (i, 1)),
                  pl.BlockSpec((tm, 128), lambda i: (i, 0)),
                  pl.BlockSpec((1, Q_LORA), const),
                  pl.BlockSpec((1, KV_LORA), const),
                  pl.BlockSpec((Q_LORA, MLA_HEADS * QK_PAD), const),
                  pl.BlockSpec((MLA_HEADS, NOPE_DIM, KV_LORA), const3),
                  pl.BlockSpec((tm, 128), row),
                  pl.BlockSpec((tm, 128), row)],
        out_specs=[pl.BlockSpec((tm, KV_LORA), row),
                   pl.BlockSpec((tm, ROPE_DIM), row),
                   pl.BlockSpec((MLA_HEADS, tm, KV_LORA), lambda i: (0, i, 0)),
                   pl.BlockSpec((MLA_HEADS, tm, ROPE_DIM), lambda i: (0, i, 0))],
        out_shape=[jax.ShapeDtypeStruct((T, KV_LORA), F32),
                   jax.ShapeDtypeStruct((T, ROPE_DIM), F32),
                   jax.ShapeDtypeStruct((MLA_HEADS, T, KV_LORA), BF16),
                   jax.ShapeDtypeStruct((MLA_HEADS, T, ROPE_DIM), BF16)],
        compiler_params=_params("parallel"),
        name="mla_prep_sample",
    )(z, z, zs, g_q, g_kv, w_uq, w_ukt, cos_t, sin_t)


ATT_TQ = 256


def _attn_prompt_kernel(q_ref, k_ref, v_ref, o_ref):
    seq = q_ref.shape[0]
    qc = lax.broadcasted_iota(jnp.int32, (ATT_TQ, ATT_TQ), 0) // CHUNK
    kc = lax.broadcasted_iota(jnp.int32, (ATT_TQ, ATT_TQ), 1) // CHUNK
    visible = kc <= qc
    c = MLA_SCALE * LOG2E
    for i in range(seq // ATT_TQ):
        lo, hi = i * ATT_TQ, (i + 1) * ATT_TQ
        s = _dot_nt(q_ref[lo:hi, :], k_ref[0:hi, :])
        s_diag = jnp.where(visible, s[:, lo:hi], NEG_BIG)
        s = jnp.concatenate([s[:, :lo], s_diag], axis=1) if i else s_diag
        m = jnp.max(s, axis=-1, keepdims=True)
        p = jnp.exp2((s - m) * c)
        l = jnp.sum(p, axis=-1, keepdims=True)
        o_ref[lo:hi, :] = (_dot(p.astype(BF16), v_ref[0:hi, :]) / l).astype(BF16)


def _attn_prompt(qcat, kcat, v, batch, seq):
    return pl.pallas_call(
        _attn_prompt_kernel,
        grid=(batch, MLA_HEADS),
        in_specs=[pl.BlockSpec((seq, QK_PAD), lambda b, h: (b, h)),
                  pl.BlockSpec((seq, QK_PAD), lambda b, h: (b, h)),
                  pl.BlockSpec((seq, V_DIM), lambda b, h: (b, h))],
        out_specs=pl.BlockSpec((seq, V_DIM), lambda b, h: (b, h)),
        out_shape=jax.ShapeDtypeStruct((batch * seq, MLA_WIDTH), BF16),
        compiler_params=_params("parallel", "parallel"),
        name="attn_prompt",
    )(qcat, kcat, v)


def _attn_sample_kernel(ql_ref, qr_ref, ckv_ref, krt_ref, nckv_ref, nkr_ref, wuv_ref, o_ref):
    dec = ql_ref.shape[1]
    rows = MLA_HEADS * dec
    ql = ql_ref[...].reshape(rows, KV_LORA)
    qr = qr_ref[...].reshape(rows, ROPE_DIM)
    ckv = ckv_ref[...].astype(BF16)
    nckv = nckv_ref[...].astype(BF16)
    s = _dot_nt(ql, ckv) + _dot(qr, krt_ref[...].astype(BF16))
    s_new = _dot_nt(ql, nckv) + _dot_nt(qr, nkr_ref[...].astype(BF16))
    m = jnp.maximum(jnp.max(s, axis=-1, keepdims=True), jnp.max(s_new, axis=-1, keepdims=True))
    c = MLA_SCALE * LOG2E
    p = jnp.exp2((s - m) * c)
    p_new = jnp.exp2((s_new - m) * c)
    l = jnp.sum(p, axis=-1, keepdims=True) + jnp.sum(p_new, axis=-1, keepdims=True)
    o = ((_dot(p.astype(BF16), ckv) + _dot(p_new.astype(BF16), nckv)) / l).astype(BF16)
    for h in range(MLA_HEADS):
        o_ref[:, h * V_DIM:(h + 1) * V_DIM] = _dot(o[h * dec:(h + 1) * dec], wuv_ref[h]).astype(BF16)


def _attn_sample(q_lat, q_rope, cache_ckv, cache_krt, ckvn, kr, w_uvh, dec):
    batch, past, _ = cache_ckv.shape
    return pl.pallas_call(
        _attn_sample_kernel,
        grid=(batch,),
        in_specs=[pl.BlockSpec((MLA_HEADS, dec, KV_LORA), lambda b: (0, b, 0)),
                  pl.BlockSpec((MLA_HEADS, dec, ROPE_DIM), lambda b: (0, b, 0)),
                  pl.BlockSpec((None, past, KV_LORA), lambda b: (b, 0, 0)),
                  pl.BlockSpec((None, ROPE_DIM, past), lambda b: (b, 0, 0)),
                  pl.BlockSpec((dec, KV_LORA), lambda b: (b, 0)),
                  pl.BlockSpec((dec, ROPE_DIM), lambda b: (b, 0)),
                  pl.BlockSpec((MLA_HEADS, KV_LORA, V_DIM), lambda b: (0, 0, 0))],
        out_specs=pl.BlockSpec((dec, MLA_WIDTH), lambda b: (b, 0)),
        out_shape=jax.ShapeDtypeStruct((batch * dec, MLA_WIDTH), BF16),
        compiler_params=_params("parallel"),
        name="attn_sample",
    )(q_lat, q_rope, cache_ckv, cache_krt, ckvn, kr, w_uvh)


GLA_CHUNK = 256


def _gla_tables(c):
    nlev = int(np.log2(c))
    t = np.arange(c)[:, None]
    u = np.arange(c)[None, :]
    blocks = []
    lvl = np.full((c, c), -1, np.int32)
    for l in range(nlev):
        width = c >> l
        half = width // 2
        m = (t // width) * width + half - 1
        upper = t > m
        blocks.append(np.where(upper, (u > m) & (u <= t), (u > t) & (u <= m)))
        same = (t // width) == (u // width)
        lvl[same & ((t % width) >= half) & ((u % width) < half)] = l
    blocks.append(u > t)
    blocks.append(u <= t)
    lvl[np.arange(c), np.arange(c)] = nlev
    return np.concatenate(blocks, 0).astype(np.float32), lvl, nlev


def _gla_kernel(*refs, c, nlev, has_init):
    if has_init:
        (q_ref, k_ref, v_ref, r_ref, sm_ref, wga_ref, bga_ref, gg_ref, p_ref, lvl_ref, s0_ref,
         o_ref, sfin_ref, s_scr) = refs
    else:
        (q_ref, k_ref, v_ref, r_ref, sm_ref, wga_ref, bga_ref, gg_ref, p_ref, lvl_ref,
         o_ref, sfin_ref, s_scr) = refs
    j = pl.program_id(1)

    @pl.when(j == 0)
    def _():
        if has_init:
            s_scr[...] = s0_ref[0]
        else:
            s_scr[...] = jnp.zeros(s_scr.shape, F32)

    x = _dot(sm_ref[...].astype(BF16), wga_ref[...]) + bga_ref[...]
    la = (jnp.minimum(x, 0.0) - jnp.log1p(jnp.exp(-jnp.abs(x)))) * (1.0 / GATE_TAU)
    hi = la.astype(BF16)
    mid = (la - hi.astype(F32)).astype(BF16)
    p_mat = p_ref[...]
    e_all = jnp.exp(_dot(p_mat, hi) + _dot(p_mat, mid))
    ones = jnp.ones((c, 128), BF16)
    d_state = jnp.exp(_dot_tn(hi, ones) + _dot_tn(mid, ones))
    q = q_ref[...] * (GLA_DK ** -0.5)
    k = k_ref[...]
    qe = [(q * e_all[l * c:(l + 1) * c]).astype(BF16) for l in range(nlev)] + [q.astype(BF16)]
    ke = [(k * e_all[l * c:(l + 1) * c]).astype(BF16) for l in range(nlev)] + [k.astype(BF16)]
    k_end = (k * e_all[nlev * c:(nlev + 1) * c]).astype(BF16)
    q_beg = (q * e_all[(nlev + 1) * c:(nlev + 2) * c]).astype(BF16)
    lvl = lvl_ref[...]
    for h in range(GLA_HEADS):
        dk = slice(h * GLA_DK, (h + 1) * GLA_DK)
        dv = slice(h * GLA_DV, (h + 1) * GLA_DV)
        attn = jnp.zeros((c, c), F32)
        for l in range(nlev + 1):
            attn = jnp.where(lvl == l, _dot_nt(qe[l][:, dk], ke[l][:, dk]), attn)
        v = v_ref[:, dv].astype(BF16)
        state = s_scr[h]
        o = _dot(attn.astype(BF16), v) + _dot(q_beg[:, dk], state.astype(BF16))
        decay = d_state[dk]
        s_scr[h] = jnp.concatenate([decay, decay], axis=1) * state + _dot_tn(k_end[:, dk], v)
        r = r_ref[:, dv]
        o_ref[:, dv] = (_rms(o, gg_ref[...]) * (r * jax.nn.sigmoid(r))).astype(BF16)

    @pl.when(j == pl.num_programs(1) - 1)
    def _():
        sfin_ref[0] = s_scr[...]


def _gla(z, zs, w_ga, b_ga, g_gla, batch, seq, c, s0=None):
    p_np, lvl_np, nlev = _gla_tables(c)
    p_mat = jnp.asarray(p_np, BF16)
    lvl = jnp.asarray(lvl_np)
    nstep = seq // c
    r = c
    row = lambda b, j: (b * nstep + j, 0)
    const = lambda b, j: (0, 0)
    in_specs = [pl.BlockSpec((r, 512), lambda b, j: (b * nstep + j, 2)),
                pl.BlockSpec((r, 512), lambda b, j: (b * nstep + j, 3)),
                pl.BlockSpec((r, GLA_WIDTH), lambda b, j: (b * nstep + j, 2)),
                pl.BlockSpec((r, GLA_WIDTH), lambda b, j: (b * nstep + j, 3)),
                pl.BlockSpec((r, 128), lambda b, j: (b * nstep + j, 0)),
                pl.BlockSpec((128, GLA_HEADS * GLA_DK), const),
                pl.BlockSpec((1, GLA_HEADS * GLA_DK), const),
                pl.BlockSpec((1, GLA_DV), const),
                pl.BlockSpec(p_np.shape, const),
                pl.BlockSpec((c, c), const)]
    args = [z, z, z, z, zs, w_ga, b_ga, g_gla, p_mat, lvl]
    state_spec = pl.BlockSpec((1, GLA_HEADS, GLA_DK, GLA_DV), lambda b, j: (b, 0, 0, 0))
    if s0 is not None:
        in_specs.append(state_spec)
        args.append(s0)
    return pl.pallas_call(
        functools.partial(_gla_kernel, c=c, nlev=nlev, has_init=s0 is not None),
        grid=(batch, nstep),
        in_specs=in_specs,
        out_specs=[pl.BlockSpec((r, GLA_WIDTH), row), state_spec],
        out_shape=[jax.ShapeDtypeStruct((batch * seq, GLA_WIDTH), BF16),
                   jax.ShapeDtypeStruct((batch, GLA_HEADS, GLA_DK, GLA_DV), F32)],
        scratch_shapes=[pltpu.VMEM((GLA_HEADS, GLA_DK, GLA_DV), F32)],
        compiler_params=_params("parallel", "arbitrary"),
        name="gla_init" if s0 is not None else "gla",
    )(*args)


def _mix_kernel(om_ref, og_ref, w1_ref, w2_ref, h_ref, g1_ref, g2_ref, h1_ref, f_ref):
    mix = _dot(om_ref[...], w1_ref[...]) + _dot(og_ref[...], w2_ref[...])
    h1 = h_ref[...] + _rms(mix, g1_ref[...])
    h1_ref[...] = h1
    f_ref[...] = _rms(h1, g2_ref[...]).astype(BF16)


def _mix(o_mla, o_gla, w_out, h, g_post_mix, g_pre_ffn):
    T = h.shape[0]
    tm = 512
    row = lambda i: (i, 0)
    const = lambda i: (0, 0)
    return pl.pallas_call(
        _mix_kernel,
        grid=(T // tm,),
        in_specs=[pl.BlockSpec((tm, MLA_WIDTH), row),
                  pl.BlockSpec((tm, GLA_WIDTH), row),
                  pl.BlockSpec((MLA_WIDTH, D_MODEL), lambda i: (0, 0), pipeline_mode=pl.Buffered(1)),
                  pl.BlockSpec((GLA_WIDTH, D_MODEL), lambda i: (1, 0), pipeline_mode=pl.Buffered(1)),
                  pl.BlockSpec((tm, D_MODEL), row),
                  pl.BlockSpec((1, D_MODEL), const),
                  pl.BlockSpec((1, D_MODEL), const)],
        out_specs=[pl.BlockSpec((tm, D_MODEL), row), pl.BlockSpec((tm, D_MODEL), row)],
        out_shape=[jax.ShapeDtypeStruct((T, D_MODEL), F32), jax.ShapeDtypeStruct((T, D_MODEL), BF16)],
        compiler_params=_params("parallel"),
        name="mix",
    )(o_mla, o_gla, w_out, w_out, h, g_post_mix, g_pre_ffn)


def _ffn_kernel(x_ref, wg_ref, wu_ref, wd_ref, o_ref):
    @pl.when(pl.program_id(1) == 0)
    def _():
        o_ref[...] = jnp.zeros(o_ref.shape, F32)

    x = x_ref[...]
    g = _dot(x, wg_ref[...])
    u = _dot(x, wu_ref[...])
    o_ref[...] += _dot((g * jax.nn.sigmoid(g) * u).astype(BF16), wd_ref[...])


def _ffn(x, w_gate, w_up, w_down):
    T = x.shape[0]
    tm, tf = min(T, 1024), 512
    return pl.pallas_call(
        _ffn_kernel,
        grid=(T // tm, D_FF // tf),
        in_specs=[pl.BlockSpec((tm, D_MODEL), lambda i, j: (i, 0)),
                  pl.BlockSpec((D_MODEL, tf), lambda i, j: (0, j)),
                  pl.BlockSpec((D_MODEL, tf), lambda i, j: (0, j)),
                  pl.BlockSpec((tf, D_MODEL), lambda i, j: (j, 0))],
        out_specs=pl.BlockSpec((tm, D_MODEL), lambda i, j: (i, 0)),
        out_shape=jax.ShapeDtypeStruct((T, D_MODEL), F32),
        compiler_params=_params("parallel", "arbitrary"),
        name="ffn",
    )(x, w_gate, w_up, w_down)


def _final_kernel(f_ref, h1_ref, p_ref, g_ref, wpg_ref, wp_ref, o_ref):
    h2 = h1_ref[...] + _rms(f_ref[...], g_ref[...])
    gate = jax.nn.sigmoid(_dot(h2.astype(BF16), wpg_ref[...]))
    o_ref[...] = h2 + gate * _dot(p_ref[...].astype(BF16), wp_ref[...])


def _final(f, h1, p, g_post_ffn, w_ple_gate, w_ple):
    T = f.shape[0]
    tm = 512
    row = lambda i: (i, 0)
    const = lambda i: (0, 0)
    return pl.pallas_call(
        _final_kernel,
        grid=(T // tm,),
        in_specs=[pl.BlockSpec((tm, D_MODEL), row),
                  pl.BlockSpec((tm, D_MODEL), row),
                  pl.BlockSpec((tm, PLE_DIM), row),
                  pl.BlockSpec((1, D_MODEL), const),
                  pl.BlockSpec((D_MODEL, D_MODEL), const, pipeline_mode=pl.Buffered(1)),
                  pl.BlockSpec((PLE_DIM, D_MODEL), const)],
        out_specs=pl.BlockSpec((tm, D_MODEL), row),
        out_shape=jax.ShapeDtypeStruct((T, D_MODEL), F32),
        compiler_params=_params("parallel"),
        name="final",
    )(f, h1, p, g_post_ffn, w_ple_gate, w_ple)


def _rope_tables(pos):
    half = ROPE_DIM // 2
    inv = 1.0 / (ROPE_THETA ** (jnp.arange(half, dtype=F32) / half))
    ang = pos.astype(F32)[:, None] * inv[None, :]
    cos, sin = jnp.cos(ang), jnp.sin(ang)
    zero = jnp.zeros((pos.shape[0], 128 - ROPE_DIM), F32)
    return jnp.concatenate([cos, cos, zero], axis=1), jnp.concatenate([-sin, sin, zero], axis=1)


def _layer_weights(i, g_pre_mix, w_in, g_q, w_uq, w_uk, g_kv, w_ga, b_ga, w_uv, g_gla, w_out, g_post_mix,
                   g_pre_ffn, w_gate, w_up, w_down, g_post_ffn, w_ple, w_ple_gate):
    w_uq_p = jnp.pad(w_uq[i].reshape(Q_LORA, MLA_HEADS, NOPE_DIM + ROPE_DIM),
                     ((0, 0), (0, 0), (0, QK_PAD - NOPE_DIM - ROPE_DIM))).reshape(Q_LORA, MLA_HEADS * QK_PAD)
    w_ga_p = jnp.zeros((128, GLA_HEADS * GLA_DK), F32).at[GLR_LO:GLR_LO + GATE_RANK].set(w_ga[i])
    vec = lambda g: g[i].reshape(1, -1)
    return dict(
        g_pre_mix=vec(g_pre_mix), w_in=_win_prep(jnp.swapaxes(w_in[i], 0, 1)), g_q=vec(g_q), g_kv=vec(g_kv),
        w_uq=w_uq_p.astype(BF16),
        w_ukv=jnp.concatenate([w_uk[i].reshape(KV_LORA, MLA_WIDTH), w_uv[i].reshape(KV_LORA, MLA_WIDTH)],
                              axis=1).astype(BF16),
        w_ukt=jnp.transpose(w_uk[i], (1, 2, 0)).astype(BF16),
        w_uvh=jnp.transpose(w_uv[i], (1, 0, 2)).astype(BF16),
        w_ga=w_ga_p.astype(BF16), b_ga=vec(b_ga), g_gla=vec(g_gla),
        w_out=w_out[i].astype(BF16), g_post_mix=vec(g_post_mix), g_pre_ffn=vec(g_pre_ffn),
        w_gate=w_gate[i].astype(BF16), w_up=w_up[i].astype(BF16), w_down=w_down[i].astype(BF16),
        g_post_ffn=vec(g_post_ffn), w_ple=w_ple[i].astype(BF16), w_ple_gate=w_ple_gate[i].astype(BF16))


def _finish(w, h, p, o_mla, o_gla):
    h1, f_in = _mix(o_mla, o_gla, w["w_out"], h, w["g_post_mix"], w["g_pre_ffn"])
    f = _ffn(f_in, w["w_gate"], w["w_up"], w["w_down"])
    return _final(f, h1, p, w["g_post_ffn"], w["w_ple_gate"], w["w_ple"])


def kernel(x_prompt, x_sample, cache_ckv, cache_krope, state_gla, p_prompt, p_sample, g_pre_mix, w_in, g_q, w_uq,
           w_uk, g_kv, w_ga, b_ga, w_uv, g_gla, w_out, g_post_mix, g_pre_ffn, w_gate, w_up, w_down, g_post_ffn,
           w_ple, w_ple_gate):
    batch, seq, _ = x_prompt.shape
    dbatch, dseq, _ = x_sample.shape
    depth = w_in.shape[0]
    cos_p, sin_p = _rope_tables(jnp.arange(seq))
    cos_s, sin_s = _rope_tables(PAST_LEN + jnp.arange(dseq))
    cos_s, sin_s = jnp.tile(cos_s, (dbatch, 1)), jnp.tile(sin_s, (dbatch, 1))
    h_p = x_prompt.reshape(batch * seq, D_MODEL)
    h_s = x_sample.reshape(dbatch * dseq, D_MODEL)
    outs = [[] for _ in range(6)]
    for i in range(depth):
        w = _layer_weights(i, g_pre_mix, w_in, g_q, w_uq, w_uk, g_kv, w_ga, b_ga, w_uv, g_gla, w_out, g_post_mix,
                           g_pre_ffn, w_gate, w_up, w_down, g_post_ffn, w_ple, w_ple_gate)
        z, zs = _inproj(h_p, w["g_pre_mix"], *w["w_in"])
        ckvn, kr, qcat, kcat, v = _mla_prep_prompt(z, zs, w["g_q"], w["g_kv"], w["w_uq"], w["w_ukv"], cos_p, sin_p, seq)
        o_mla = _attn_prompt(qcat, kcat, v, batch, seq)
        o_gla, s_fin = _gla(z, zs, w["w_ga"], w["b_ga"], w["g_gla"], batch, seq, GLA_CHUNK)
        h_p = _finish(w, h_p, p_prompt[i].reshape(batch * seq, PLE_DIM), o_mla, o_gla)
        outs[0].append(ckvn.reshape(batch, seq, KV_LORA))
        outs[1].append(kr.reshape(batch, seq, ROPE_DIM))
        outs[2].append(s_fin)
        z, zs = _inproj(h_s, w["g_pre_mix"], *w["w_in"])
        ckvn, kr, q_lat, q_rope = _mla_prep_sample(z, zs, w["g_q"], w["g_kv"], w["w_uq"], w["w_ukt"], cos_s, sin_s)
        o_mla = _attn_sample(q_lat, q_rope, cache_ckv[i], jnp.swapaxes(cache_krope[i], 1, 2), ckvn, kr,
                             w["w_uvh"], dseq)
        o_gla, s_new = _gla(z, zs, w["w_ga"], w["b_ga"], w["g_gla"], dbatch, dseq, dseq, s0=state_gla[i])
        h_s = _finish(w, h_s, p_sample[i].reshape(dbatch * dseq, PLE_DIM), o_mla, o_gla)
        outs[3].append(ckvn.reshape(dbatch, dseq, KV_LORA))
        outs[4].append(kr.reshape(dbatch, dseq, ROPE_DIM))
        outs[5].append(s_new)
    return (h_p.reshape(batch, seq, D_MODEL), h_s.reshape(dbatch, dseq, D_MODEL),
            jnp.stack(outs[0]), jnp.stack(outs[1]), jnp.stack(outs[2]),
            jnp.stack(outs[3]), jnp.stack(outs[4]), jnp.stack(outs[5]))
```

```python
import functools

import numpy as np
import jax
import jax.numpy as jnp
from jax import lax
from jax.experimental import pallas as pl
from jax.experimental.pallas import tpu as pltpu

F32 = jnp.float32
BF16 = jnp.bfloat16

D_MODEL = 2048
PAST_LEN = 4096
CHUNK = 64
EPS = 1e-6

MLA_HEADS = 8
Q_LORA = 512
KV_LORA = 512
NOPE_DIM = 128
ROPE_DIM = 64
V_DIM = 128
ROPE_THETA = 10000.0
MLA_SCALE = (NOPE_DIM + ROPE_DIM) ** -0.5
QK_PAD = 256

GLA_HEADS = 4
GLA_DK = 128
GLA_DV = 256
GATE_RANK = 16
GATE_TAU = 16.0
GLA_WIDTH = GLA_HEADS * GLA_DV
MLA_WIDTH = MLA_HEADS * V_DIM

IN_SPLITS = (Q_LORA, KV_LORA, ROPE_DIM, GLA_HEADS * GLA_DK, GLA_HEADS * GLA_DK, GLA_WIDTH, GATE_RANK, GLA_WIDTH)
Z_MAIN = 4096
GLR_LO = ROPE_DIM

D_FF = 5632
PLE_DIM = 256

VMEM_LIMIT = 56 * 1024 * 1024
NEG_BIG = -1e30
LOG2E = 1.4426950408889634
QK_SCALE = MLA_SCALE * LOG2E


def _dot(a, b):
    return jnp.dot(a, b, preferred_element_type=F32)


def _dot_nt(a, b):
    return lax.dot_general(a, b, (((1,), (1,)), ((), ())), preferred_element_type=F32)


def _dot_tn(a, b):
    return lax.dot_general(a, b, (((0,), (0,)), ((), ())), preferred_element_type=F32)


def _rms(x, g):
    ms = jnp.mean(x * x, axis=-1, keepdims=True)
    return x * lax.rsqrt(ms + EPS) * g


def _rope(x, cos_t, sin_t):
    lane = lax.broadcasted_iota(jnp.int32, x.shape, 1)
    rot = jnp.where((lane & 32) == 0, pltpu.roll(x, 96, 1), pltpu.roll(x, 32, 1))
    return x * cos_t + rot * sin_t


def _params(*sem):
    return pltpu.CompilerParams(dimension_semantics=sem, vmem_limit_bytes=VMEM_LIMIT)


_IN_OFF = np.concatenate([[0], np.cumsum(IN_SPLITS)]).tolist()


_WIN_TILE = 512


def _win_prep_kernel(w_ref, kr_ref, glr_ref, o_ref, small_ref):
    o_ref[...] = w_ref[...].T.astype(BF16)

    @pl.when(pl.program_id(0) == 0)
    def _():
        pad = jnp.zeros((128 - GLR_LO - GATE_RANK, D_MODEL), F32)
        small_ref[...] = jnp.concatenate([kr_ref[...], glr_ref[...], pad], axis=0).T.astype(BF16)


def _win_prep(w_t):
    c_q, c_kv, k_r, q_g, k_g, v_g, g_lr, r_g, end = _IN_OFF
    n_head, n_mid = k_r // _WIN_TILE, (g_lr - q_g) // _WIN_TILE
    assert k_r % _WIN_TILE == 0 and (g_lr - q_g) % _WIN_TILE == 0 and (end - r_g) % _WIN_TILE == 0
    assert (n_head + n_mid) * _WIN_TILE + end - r_g == Z_MAIN and GLR_LO == ROPE_DIM

    def src_row(j):
        skip_mid, skip_tail = (q_g - k_r) // 8, (r_g - g_lr + q_g - k_r) // 8
        return 8 * (j * (_WIN_TILE // 8) + jnp.where(j < n_head, 0, jnp.where(j < n_head + n_mid, skip_mid, skip_tail)))

    return pl.pallas_call(
        _win_prep_kernel,
        grid=(Z_MAIN // _WIN_TILE,),
        in_specs=[pl.BlockSpec((pl.Element(_WIN_TILE), pl.Element(D_MODEL)), lambda j: (src_row(j), 0)),
                  pl.BlockSpec((pl.Element(ROPE_DIM), pl.Element(D_MODEL)), lambda j: (k_r, 0)),
                  pl.BlockSpec((pl.Element(GATE_RANK), pl.Element(D_MODEL)), lambda j: (g_lr, 0))],
        out_specs=[pl.BlockSpec((D_MODEL, _WIN_TILE), lambda j: (0, j)),
                   pl.BlockSpec((D_MODEL, 128), lambda j: (0, 0))],
        out_shape=[jax.ShapeDtypeStruct((D_MODEL, Z_MAIN), BF16), jax.ShapeDtypeStruct((D_MODEL, 128), BF16)],
        compiler_params=_params("arbitrary"),
        name="win_prep",
    )(w_t, w_t, w_t)


def _inproj_kernel(x_ref, g_ref, w_ref, ws_ref, o_ref, os_ref):
    xn = _rms(x_ref[...], g_ref[...]).astype(BF16)
    o_ref[...] = _dot(xn, w_ref[...])
    os_ref[...] = _dot(xn, ws_ref[...])


def _inproj(x, g, w_main, w_small):
    T = x.shape[0]
    tm = 512
    return pl.pallas_call(
        _inproj_kernel,
        grid=(T // tm,),
        in_specs=[pl.BlockSpec((tm, D_MODEL), lambda i: (i, 0)),
                  pl.BlockSpec((1, D_MODEL), lambda i: (0, 0)),
                  pl.BlockSpec((D_MODEL, Z_MAIN), lambda i: (0, 0), pipeline_mode=pl.Buffered(1)),
                  pl.BlockSpec((D_MODEL, 128), lambda i: (0, 0), pipeline_mode=pl.Buffered(1))],
        out_specs=[pl.BlockSpec((tm, Z_MAIN), lambda i: (i, 0)), pl.BlockSpec((tm, 128), lambda i: (i, 0))],
        out_shape=[jax.ShapeDtypeStruct((T, Z_MAIN), F32), jax.ShapeDtypeStruct((T, 128), F32)],
        compiler_params=_params("parallel"),
        name="inproj",
    )(x, g, w_main, w_small)


def _mla_q(cq_ref, gq_ref, wuq_ref, cos, sin):
    cqn = _rms(cq_ref[...], gq_ref[...]).astype(BF16)
    q = _dot(cqn, wuq_ref[...]) * QK_SCALE
    out = []
    for h in range(MLA_HEADS):
        nope = q[:, h * QK_PAD:h * QK_PAD + NOPE_DIM]
        rp = _rope(q[:, h * QK_PAD + NOPE_DIM:(h + 1) * QK_PAD], cos, sin)
        out.append((nope, rp))
    return out


def _mla_prep_prompt_kernel(cq_ref, ckv_ref, sm_ref, gq_ref, gkv_ref, wuq_ref, wukv_ref, cos_ref, sin_ref,
                            ckvn_ref, kr_ref, qcat_ref, kcat_ref, v_ref):
    cos, sin = cos_ref[...], sin_ref[...]
    for h, (nope, rp) in enumerate(_mla_q(cq_ref, gq_ref, wuq_ref, cos, sin)):
        qcat_ref[:, h * QK_PAD:h * QK_PAD + NOPE_DIM] = nope.astype(BF16)
        qcat_ref[:, h * QK_PAD + NOPE_DIM:(h + 1) * QK_PAD] = rp.astype(BF16)
    ckvn = _rms(ckv_ref[...], gkv_ref[...])
    ckvn_ref[...] = ckvn
    kr = _rope(sm_ref[...], cos, sin)
    kr_ref[...] = kr[:, :ROPE_DIM]
    kv = _dot(ckvn.astype(BF16), wukv_ref[...])
    krb = kr.astype(BF16)
    for h in range(MLA_HEADS):
        kcat_ref[:, h * QK_PAD:h * QK_PAD + NOPE_DIM] = kv[:, h * NOPE_DIM:(h + 1) * NOPE_DIM].astype(BF16)
        kcat_ref[:, h * QK_PAD + NOPE_DIM:(h + 1) * QK_PAD] = krb
    v_ref[...] = kv[:, MLA_WIDTH:].astype(BF16)


def _mla_prep_prompt(z, zs, g_q, g_kv, w_uq, w_ukv, cos_t, sin_t, seq):
    T = z.shape[0]
    tm = 512
    nseq = seq // tm
    row = lambda i: (i, 0)
    const = lambda i: (0, 0)
    return pl.pallas_call(
        _mla_prep_prompt_kernel,
        grid=(T // tm,),
        in_specs=[pl.BlockSpec((tm, Q_LORA), lambda i: (i, 0)),
                  pl.BlockSpec((tm, KV_LORA), lambda i: (i, 1)),
                  pl.BlockSpec((tm, 128), lambda i: (i, 0)),
                  pl.BlockSpec((1, Q_LORA), const),
                  pl.BlockSpec((1, KV_LORA), const),
                  pl.BlockSpec((Q_LORA, MLA_HEADS * QK_PAD), const),
                  pl.BlockSpec((KV_LORA, 2 * MLA_WIDTH), const),
                  pl.BlockSpec((tm, 128), lambda i: (i % nseq, 0)),
                  pl.BlockSpec((tm, 128), lambda i: (i % nseq, 0))],
        out_specs=[pl.BlockSpec((tm, KV_LORA), row),
                   pl.BlockSpec((tm, ROPE_DIM), row),
                   pl.BlockSpec((tm, MLA_HEADS * QK_PAD), row),
                   pl.BlockSpec((tm, MLA_HEADS * QK_PAD), row),
                   pl.BlockSpec((tm, MLA_WIDTH), row)],
        out_shape=[jax.ShapeDtypeStruct((T, KV_LORA), F32),
                   jax.ShapeDtypeStruct((T, ROPE_DIM), F32),
                   jax.ShapeDtypeStruct((T, MLA_HEADS * QK_PAD), BF16),
                   jax.ShapeDtypeStruct((T, MLA_HEADS * QK_PAD), BF16),
                   jax.ShapeDtypeStruct((T, MLA_WIDTH), BF16)],
        compiler_params=_params("parallel"),
        name="mla_prep_prompt",
    )(z, z, zs, g_q, g_kv, w_uq, w_ukv, cos_t, sin_t)


def _mla_prep_sample_kernel(cq_ref, ckv_ref, sm_ref, gq_ref, gkv_ref, wuq_ref, wukt_ref, cos_ref, sin_ref,
                            ckvn_ref, kr_ref, qlat_ref, qr_ref):
    cos, sin = cos_ref[...], sin_ref[...]
    for h, (nope, rp) in enumerate(_mla_q(cq_ref, gq_ref, wuq_ref, cos, sin)):
        qlat_ref[h] = _dot(nope.astype(BF16), wukt_ref[h]).astype(BF16)
        qr_ref[h] = rp[:, :ROPE_DIM].astype(BF16)
    ckvn_ref[...] = _rms(ckv_ref[...], gkv_ref[...])
    kr_ref[...] = _rope(sm_ref[...], cos, sin)[:, :ROPE_DIM]


def _mla_prep_sample(z, zs, g_q, g_kv, w_uq, w_ukt, cos_t, sin_t):
    T = z.shape[0]
    tm = T
    row = lambda i: (i, 0)
    const = lambda i: (0, 0)
    const3 = lambda i: (0, 0, 0)
    return pl.pallas_call(
        _mla_prep_sample_kernel,
        grid=(T // tm,),
        in_specs=[pl.BlockSpec((tm, Q_LORA), lambda i: (i, 0)),
                  pl.BlockSpec((tm, KV_LORA), lambda i: (i, 1)),
                  pl.BlockSpec((tm, 128), lambda i: (i, 0)),
                  pl.BlockSpec((1, Q_LORA), const),
                  pl.BlockSpec((1, KV_LORA), const),
                  pl.BlockSpec((Q_LORA, MLA_HEADS * QK_PAD), const),
                  pl.BlockSpec((MLA_HEADS, NOPE_DIM, KV_LORA), const3),
                  pl.BlockSpec((tm, 128), row),
                  pl.BlockSpec((tm, 128), row)],
        out_specs=[pl.BlockSpec((tm, KV_LORA), row),
                   pl.BlockSpec((tm, ROPE_DIM), row),
                   pl.BlockSpec((MLA_HEADS, tm, KV_LORA), lambda i: (0, i, 0)),
                   pl.BlockSpec((MLA_HEADS, tm, ROPE_DIM), lambda i: (0, i, 0))],
        out_shape=[jax.ShapeDtypeStruct((T, KV_LORA), F32),
                   jax.ShapeDtypeStruct((T, ROPE_DIM), F32),
                   jax.ShapeDtypeStruct((MLA_HEADS, T, KV_LORA), BF16),
                   jax.ShapeDtypeStruct((MLA_HEADS, T, ROPE_DIM), BF16)],
        compiler_params=_params("parallel"),
        name="mla_prep_sample",
    )(z, z, zs, g_q, g_kv, w_uq, w_ukt, cos_t, sin_t)


ATT_TQ = 256
ATT_HEADS = 2


def _attn_prompt_kernel(q_ref, k_ref, v_ref, o_ref):
    seq = q_ref.shape[0]
    qc = lax.broadcasted_iota(jnp.int32, (ATT_TQ, ATT_TQ), 0) // CHUNK
    kc = lax.broadcasted_iota(jnp.int32, (ATT_TQ, ATT_TQ), 1) // CHUNK
    visible = kc <= qc
    for i in reversed(range(seq // ATT_TQ)):
        lo, hi = i * ATT_TQ, (i + 1) * ATT_TQ
        for h in range(ATT_HEADS):
            qk = slice(h * QK_PAD, (h + 1) * QK_PAD)
            dv = slice(h * V_DIM, (h + 1) * V_DIM)
            s = _dot_nt(q_ref[lo:hi, qk], k_ref[0:hi, qk])
            s_diag = jnp.where(visible, s[:, lo:hi], NEG_BIG)
            s = jnp.concatenate([s[:, :lo], s_diag], axis=1) if i else s_diag
            m = jnp.max(s, axis=-1, keepdims=True)
            p = jnp.exp2(s - m).astype(BF16)
            v_one = jnp.concatenate([v_ref[0:hi, dv], jnp.ones((hi, V_DIM), BF16)], axis=1)
            ol = _dot(p, v_one)
            o_ref[lo:hi, dv] = (ol[:, :V_DIM] / ol[:, V_DIM:]).astype(BF16)


def _attn_prompt(qcat, kcat, v, batch, seq):
    return pl.pallas_call(
        _attn_prompt_kernel,
        grid=(batch, MLA_HEADS // ATT_HEADS),
        in_specs=[pl.BlockSpec((seq, ATT_HEADS * QK_PAD), lambda b, h: (b, h)),
                  pl.BlockSpec((seq, ATT_HEADS * QK_PAD), lambda b, h: (b, h)),
                  pl.BlockSpec((seq, ATT_HEADS * V_DIM), lambda b, h: (b, h))],
        out_specs=pl.BlockSpec((seq, ATT_HEADS * V_DIM), lambda b, h: (b, h)),
        out_shape=jax.ShapeDtypeStruct((batch * seq, MLA_WIDTH), BF16),
        compiler_params=_params("parallel", "parallel"),
        name="attn_prompt",
    )(qcat, kcat, v)


def _attn_sample_kernel(ql_ref, qr_ref, ckv_ref, krt_ref, nckv_ref, nkr_ref, wuv_ref, o_ref):
    dec = ql_ref.shape[1]
    rows = MLA_HEADS * dec
    ql = ql_ref[...].reshape(rows, KV_LORA)
    qr = qr_ref[...].reshape(rows, ROPE_DIM)
    ckv = ckv_ref[...].astype(BF16)
    nckv = nckv_ref[...].astype(BF16)
    s = _dot_nt(ql, ckv) + _dot(qr, krt_ref[...].astype(BF16))
    s_new = _dot_nt(ql, nckv) + _dot_nt(qr, nkr_ref[...].astype(BF16))
    m = jnp.maximum(jnp.max(s, axis=-1, keepdims=True), jnp.max(s_new, axis=-1, keepdims=True))
    p = jnp.exp2(s - m)
    p_new = jnp.exp2(s_new - m)
    l = jnp.sum(p, axis=-1, keepdims=True) + jnp.sum(p_new, axis=-1, keepdims=True)
    o = ((_dot(p.astype(BF16), ckv) + _dot(p_new.astype(BF16), nckv)) / l).astype(BF16)
    for h in range(MLA_HEADS):
        o_ref[:, h * V_DIM:(h + 1) * V_DIM] = _dot(o[h * dec:(h + 1) * dec], wuv_ref[h]).astype(BF16)


def _attn_sample(q_lat, q_rope, cache_ckv, cache_krt, ckvn, kr, w_uvh, dec):
    batch, past, _ = cache_ckv.shape
    return pl.pallas_call(
        _attn_sample_kernel,
        grid=(batch,),
        in_specs=[pl.BlockSpec((MLA_HEADS, dec, KV_LORA), lambda b: (0, b, 0)),
                  pl.BlockSpec((MLA_HEADS, dec, ROPE_DIM), lambda b: (0, b, 0)),
                  pl.BlockSpec((None, past, KV_LORA), lambda b: (b, 0, 0)),
                  pl.BlockSpec((None, ROPE_DIM, past), lambda b: (b, 0, 0)),
                  pl.BlockSpec((dec, KV_LORA), lambda b: (b, 0)),
                  pl.BlockSpec((dec, ROPE_DIM), lambda b: (b, 0)),
                  pl.BlockSpec((MLA_HEADS, KV_LORA, V_DIM), lambda b: (0, 0, 0))],
        out_specs=pl.BlockSpec((dec, MLA_WIDTH), lambda b: (b, 0)),
        out_shape=jax.ShapeDtypeStruct((batch * dec, MLA_WIDTH), BF16),
        compiler_params=_params("parallel"),
        name="attn_sample",
    )(q_lat, q_rope, cache_ckv, cache_krt, ckvn, kr, w_uvh)


GLA_CHUNK = 256


def _gla_tables(c):
    nlev = int(np.log2(c))
    t = np.arange(c)[:, None]
    u = np.arange(c)[None, :]
    blocks = []
    lvl = np.full((c, c), -1, np.int32)
    for l in range(nlev):
        width = c >> l
        half = width // 2
        m = (t // width) * width + half - 1
        upper = t > m
        blocks.append(np.where(upper, (u > m) & (u <= t), (u > t) & (u <= m)))
        same = (t // width) == (u // width)
        lvl[same & ((t % width) >= half) & ((u % width) < half)] = l
    blocks.append(u > t)
    blocks.append(u <= t)
    lvl[np.arange(c), np.arange(c)] = nlev
    return np.concatenate(blocks, 0).astype(np.float32), lvl, nlev


def _gla_kernel(*refs, c, nlev, has_init):
    if has_init:
        (q_ref, k_ref, v_ref, r_ref, sm_ref, wga_ref, bga_ref, gg_ref, p_ref, lvl_ref, s0_ref,
         o_ref, sfin_ref, s_scr) = refs
    else:
        (q_ref, k_ref, v_ref, r_ref, sm_ref, wga_ref, bga_ref, gg_ref, p_ref, lvl_ref,
         o_ref, sfin_ref, s_scr) = refs
    j = pl.program_id(1)

    @pl.when(j == 0)
    def _():
        if has_init:
            s_scr[...] = s0_ref[0]
        else:
            s_scr[...] = jnp.zeros(s_scr.shape, F32)

    x = _dot(sm_ref[...].astype(BF16), wga_ref[...]) + bga_ref[...]
    la = (jnp.minimum(x, 0.0) - jnp.log1p(jnp.exp(-jnp.abs(x)))) * (1.0 / GATE_TAU)
    hi = la.astype(BF16)
    mid = (la - hi.astype(F32)).astype(BF16)
    p_mat = p_ref[...]
    e_all = jnp.exp(_dot(p_mat, hi) + _dot(p_mat, mid))
    ones = jnp.ones((c, 128), BF16)
    d_state = jnp.exp(_dot_tn(hi, ones) + _dot_tn(mid, ones))
    q = q_ref[...] * (GLA_DK ** -0.5)
    k = k_ref[...]
    qe = [(q * e_all[l * c:(l + 1) * c]).astype(BF16) for l in range(nlev)] + [q.astype(BF16)]
    ke = [(k * e_all[l * c:(l + 1) * c]).astype(BF16) for l in range(nlev)] + [k.astype(BF16)]
    k_end = (k * e_all[nlev * c:(nlev + 1) * c]).astype(BF16)
    q_beg = (q * e_all[(nlev + 1) * c:(nlev + 2) * c]).astype(BF16)
    lvl = lvl_ref[...]
    for h in range(GLA_HEADS):
        dk = slice(h * GLA_DK, (h + 1) * GLA_DK)
        dv = slice(h * GLA_DV, (h + 1) * GLA_DV)
        attn = jnp.zeros((c, c), F32)
        for l in range(nlev + 1):
            attn = jnp.where(lvl == l, _dot_nt(qe[l][:, dk], ke[l][:, dk]), attn)
        v = v_ref[:, dv].astype(BF16)
        state = s_scr[h]
        o = _dot(attn.astype(BF16), v) + _dot(q_beg[:, dk], state.astype(BF16))
        decay = d_state[dk]
        s_scr[h] = jnp.concatenate([decay, decay], axis=1) * state + _dot_tn(k_end[:, dk], v)
        r = r_ref[:, dv]
        o_ref[:, dv] = (_rms(o, gg_ref[...]) * (r * jax.nn.sigmoid(r))).astype(BF16)

    @pl.when(j == pl.num_programs(1) - 1)
    def _():
        sfin_ref[0] = s_scr[...]


def _gla(z, zs, w_ga, b_ga, g_gla, batch, seq, c, s0=None):
    p_np, lvl_np, nlev = _gla_tables(c)
    p_mat = jnp.asarray(p_np, BF16)
    lvl = jnp.asarray(lvl_np)
    nstep = seq // c
    r = c
    row = lambda b, j: (b * nstep + j, 0)
    const = lambda b, j: (0, 0)
    in_specs = [pl.BlockSpec((r, 512), lambda b, j: (b * nstep + j, 2)),
                pl.BlockSpec((r, 512), lambda b, j: (b * nstep + j, 3)),
                pl.BlockSpec((r, GLA_WIDTH), lambda b, j: (b * nstep + j, 2)),
                pl.BlockSpec((r, GLA_WIDTH), lambda b, j: (b * nstep + j, 3)),
                pl.BlockSpec((r, 128), lambda b, j: (b * nstep + j, 0)),
                pl.BlockSpec((128, GLA_HEADS * GLA_DK), const),
                pl.BlockSpec((1, GLA_HEADS * GLA_DK), const),
                pl.BlockSpec((1, GLA_DV), const),
                pl.BlockSpec(p_np.shape, const),
                pl.BlockSpec((c, c), const)]
    args = [z, z, z, z, zs, w_ga, b_ga, g_gla, p_mat, lvl]
    state_spec = pl.BlockSpec((1, GLA_HEADS, GLA_DK, GLA_DV), lambda b, j: (b, 0, 0, 0))
    if s0 is not None:
        in_specs.append(state_spec)
        args.append(s0)
    return pl.pallas_call(
        functools.partial(_gla_kernel, c=c, nlev=nlev, has_init=s0 is not None),
        grid=(batch, nstep),
        in_specs=in_specs,
        out_specs=[pl.BlockSpec((r, GLA_WIDTH), row), state_spec],
        out_shape=[jax.ShapeDtypeStruct((batch * seq, GLA_WIDTH), BF16),
                   jax.ShapeDtypeStruct((batch, GLA_HEADS, GLA_DK, GLA_DV), F32)],
        scratch_shapes=[pltpu.VMEM((GLA_HEADS, GLA_DK, GLA_DV), F32)],
        compiler_params=_params("parallel", "arbitrary"),
        name="gla_init" if s0 is not None else "gla",
    )(*args)


ROW_SUB = 128


def _mix_kernel(om_ref, og_ref, w1_ref, w2_ref, h_ref, g1_ref, g2_ref, h1_ref, f_ref):
    for r in range(0, h_ref.shape[0], ROW_SUB):
        rows = slice(r, r + ROW_SUB)
        mix = _dot(om_ref[rows, :], w1_ref[...]) + _dot(og_ref[rows, :], w2_ref[...])
        h1 = h_ref[rows, :] + _rms(mix, g1_ref[...])
        h1_ref[rows, :] = h1
        f_ref[rows, :] = _rms(h1, g2_ref[...]).astype(BF16)


def _mix(o_mla, o_gla, w_out, h, g_post_mix, g_pre_ffn):
    T = h.shape[0]
    tm = 512
    row = lambda i: (i, 0)
    const = lambda i: (0, 0)
    return pl.pallas_call(
        _mix_kernel,
        grid=(T // tm,),
        in_specs=[pl.BlockSpec((tm, MLA_WIDTH), row),
                  pl.BlockSpec((tm, GLA_WIDTH), row),
                  pl.BlockSpec((MLA_WIDTH, D_MODEL), lambda i: (0, 0), pipeline_mode=pl.Buffered(1)),
                  pl.BlockSpec((GLA_WIDTH, D_MODEL), lambda i: (1, 0), pipeline_mode=pl.Buffered(1)),
                  pl.BlockSpec((tm, D_MODEL), row),
                  pl.BlockSpec((1, D_MODEL), const),
                  pl.BlockSpec((1, D_MODEL), const)],
        out_specs=[pl.BlockSpec((tm, D_MODEL), row), pl.BlockSpec((tm, D_MODEL), row)],
        out_shape=[jax.ShapeDtypeStruct((T, D_MODEL), F32), jax.ShapeDtypeStruct((T, D_MODEL), BF16)],
        compiler_params=_params("parallel"),
        name="mix",
    )(o_mla, o_gla, w_out, w_out, h, g_post_mix, g_pre_ffn)


def _ffn_kernel(x_ref, wg_ref, wu_ref, wd_ref, o_ref):
    @pl.when(pl.program_id(1) == 0)
    def _():
        o_ref[...] = jnp.zeros(o_ref.shape, F32)

    x = x_ref[...]
    g = _dot(x, wg_ref[...])
    u = _dot(x, wu_ref[...])
    o_ref[...] += _dot((g * jax.nn.sigmoid(g) * u).astype(BF16), wd_ref[...])


def _ffn(x, w_gate, w_up, w_down):
    T = x.shape[0]
    tm, tf = min(T, 1024), 512
    return pl.pallas_call(
        _ffn_kernel,
        grid=(T // tm, D_FF // tf),
        in_specs=[pl.BlockSpec((tm, D_MODEL), lambda i, j: (i, 0)),
                  pl.BlockSpec((D_MODEL, tf), lambda i, j: (0, j)),
                  pl.BlockSpec((D_MODEL, tf), lambda i, j: (0, j)),
                  pl.BlockSpec((tf, D_MODEL), lambda i, j: (j, 0))],
        out_specs=pl.BlockSpec((tm, D_MODEL), lambda i, j: (i, 0)),
        out_shape=jax.ShapeDtypeStruct((T, D_MODEL), F32),
        compiler_params=_params("parallel", "arbitrary"),
        name="ffn",
    )(x, w_gate, w_up, w_down)


def _final_kernel(f_ref, h1_ref, p_ref, g_ref, wpg_ref, wp_ref, o_ref):
    h2 = h1_ref[...] + _rms(f_ref[...], g_ref[...])
    gate = jax.nn.sigmoid(_dot(h2.astype(BF16), wpg_ref[...]))
    o_ref[...] = h2 + gate * _dot(p_ref[...].astype(BF16), wp_ref[...])


def _final(f, h1, p, g_post_ffn, w_ple_gate, w_ple):
    T = f.shape[0]
    tm = 512
    row = lambda i: (i, 0)
    const = lambda i: (0, 0)
    return pl.pallas_call(
        _final_kernel,
        grid=(T // tm,),
        in_specs=[pl.BlockSpec((tm, D_MODEL), row),
                  pl.BlockSpec((tm, D_MODEL), row),
                  pl.BlockSpec((tm, PLE_DIM), row),
                  pl.BlockSpec((1, D_MODEL), const),
                  pl.BlockSpec((D_MODEL, D_MODEL), const, pipeline_mode=pl.Buffered(1)),
                  pl.BlockSpec((PLE_DIM, D_MODEL), const)],
        out_specs=pl.BlockSpec((tm, D_MODEL), row),
        out_shape=jax.ShapeDtypeStruct((T, D_MODEL), F32),
        compiler_params=_params("parallel"),
        name="final",
    )(f, h1, p, g_post_ffn, w_ple_gate, w_ple)


def _rope_tables(pos):
    half = ROPE_DIM // 2
    inv = 1.0 / (ROPE_THETA ** (jnp.arange(half, dtype=F32) / half))
    ang = pos.astype(F32)[:, None] * inv[None, :]
    cos, sin = jnp.cos(ang), jnp.sin(ang)
    zero = jnp.zeros((pos.shape[0], 128 - ROPE_DIM), F32)
    return jnp.concatenate([cos, cos, zero], axis=1), jnp.concatenate([-sin, sin, zero], axis=1)


def _layer_weights(i, g_pre_mix, w_in, g_q, w_uq, w_uk, g_kv, w_ga, b_ga, w_uv, g_gla, w_out, g_post_mix,
                   g_pre_ffn, w_gate, w_up, w_down, g_post_ffn, w_ple, w_ple_gate):
    w_uq_p = jnp.pad(w_uq[i].reshape(Q_LORA, MLA_HEADS, NOPE_DIM + ROPE_DIM),
                     ((0, 0), (0, 0), (0, QK_PAD - NOPE_DIM - ROPE_DIM))).reshape(Q_LORA, MLA_HEADS * QK_PAD)
    w_ga_p = jnp.zeros((128, GLA_HEADS * GLA_DK), F32).at[GLR_LO:GLR_LO + GATE_RANK].set(w_ga[i])
    vec = lambda g: g[i].reshape(1, -1)
    return dict(
        g_pre_mix=vec(g_pre_mix), w_in=_win_prep(jnp.swapaxes(w_in[i], 0, 1)), g_q=vec(g_q), g_kv=vec(g_kv),
        w_uq=w_uq_p.astype(BF16),
        w_ukv=jnp.concatenate([w_uk[i].reshape(KV_LORA, MLA_WIDTH), w_uv[i].reshape(KV_LORA, MLA_WIDTH)],
                              axis=1).astype(BF16),
        w_ukt=jnp.transpose(w_uk[i], (1, 2, 0)).astype(BF16),
        w_uvh=jnp.transpose(w_uv[i], (1, 0, 2)).astype(BF16),
        w_ga=w_ga_p.astype(BF16), b_ga=vec(b_ga), g_gla=vec(g_gla),
        w_out=w_out[i].astype(BF16), g_post_mix=vec(g_post_mix), g_pre_ffn=vec(g_pre_ffn),
        w_gate=w_gate[i].astype(BF16), w_up=w_up[i].astype(BF16), w_down=w_down[i].astype(BF16),
        g_post_ffn=vec(g_post_ffn), w_ple=w_ple[i].astype(BF16), w_ple_gate=w_ple_gate[i].astype(BF16))


def _finish(w, h, p, o_mla, o_gla):
    h1, f_in = _mix(o_mla, o_gla, w["w_out"], h, w["g_post_mix"], w["g_pre_ffn"])
    f = _ffn(f_in, w["w_gate"], w["w_up"], w["w_down"])
    return _final(f, h1, p, w["g_post_ffn"], w["w_ple_gate"], w["w_ple"])


def kernel(x_prompt, x_sample, cache_ckv, cache_krope, state_gla, p_prompt, p_sample, g_pre_mix, w_in, g_q, w_uq,
           w_uk, g_kv, w_ga, b_ga, w_uv, g_gla, w_out, g_post_mix, g_pre_ffn, w_gate, w_up, w_down, g_post_ffn,
           w_ple, w_ple_gate):
    batch, seq, _ = x_prompt.shape
    dbatch, dseq, _ = x_sample.shape
    depth = w_in.shape[0]
    cos_p, sin_p = _rope_tables(jnp.arange(seq))
    cos_s, sin_s = _rope_tables(PAST_LEN + jnp.arange(dseq))
    cos_s, sin_s = jnp.tile(cos_s, (dbatch, 1)), jnp.tile(sin_s, (dbatch, 1))
    h_p = x_prompt.reshape(batch * seq, D_MODEL)
    h_s = x_sample.reshape(dbatch * dseq, D_MODEL)
    outs = [[] for _ in range(6)]
    for i in range(depth):
        w = _layer_weights(i, g_pre_mix, w_in, g_q, w_uq, w_uk, g_kv, w_ga, b_ga, w_uv, g_gla, w_out, g_post_mix,
                           g_pre_ffn, w_gate, w_up, w_down, g_post_ffn, w_ple, w_ple_gate)
        z, zs = _inproj(h_p, w["g_pre_mix"], *w["w_in"])
        ckvn, kr, qcat, kcat, v = _mla_prep_prompt(z, zs, w["g_q"], w["g_kv"], w["w_uq"], w["w_ukv"], cos_p, sin_p, seq)
        o_mla = _attn_prompt(qcat, kcat, v, batch, seq)
        o_gla, s_fin = _gla(z, zs, w["w_ga"], w["b_ga"], w["g_gla"], batch, seq, GLA_CHUNK)
        h_p = _finish(w, h_p, p_prompt[i].reshape(batch * seq, PLE_DIM), o_mla, o_gla)
        outs[0].append(ckvn.reshape(batch, seq, KV_LORA))
        outs[1].append(kr.reshape(batch, seq, ROPE_DIM))
        outs[2].append(s_fin)
        z, zs = _inproj(h_s, w["g_pre_mix"], *w["w_in"])
        ckvn, kr, q_lat, q_rope = _mla_prep_sample(z, zs, w["g_q"], w["g_kv"], w["w_uq"], w["w_ukt"], cos_s, sin_s)
        o_mla = _attn_sample(q_lat, q_rope, cache_ckv[i], jnp.swapaxes(cache_krope[i], 1, 2), ckvn, kr,
                             w["w_uvh"], dseq)
        o_gla, s_new = _gla(z, zs, w["w_ga"], w["b_ga"], w["g_gla"], dbatch, dseq, dseq, s0=state_gla[i])
        h_s = _finish(w, h_s, p_sample[i].reshape(dbatch * dseq, PLE_DIM), o_mla, o_gla)
        outs[3].append(ckvn.reshape(dbatch, dseq, KV_LORA))
        outs[4].append(kr.reshape(dbatch, dseq, ROPE_DIM))
        outs[5].append(s_new)
    return (h_p.reshape(batch, seq, D_MODEL), h_s.reshape(dbatch, dseq, D_MODEL),
            jnp.stack(outs[0]), jnp.stack(outs[1]), jnp.stack(outs[2]),
            jnp.stack(outs[3]), jnp.stack(outs[4]), jnp.stack(outs[5]))
```

```python
import functools

import numpy as np
import jax
import jax.numpy as jnp
from jax import lax
from jax.experimental import pallas as pl
from jax.experimental.pallas import tpu as pltpu

F32 = jnp.float32
BF16 = jnp.bfloat16

D_MODEL = 2048
PAST_LEN = 4096
CHUNK = 64
EPS = 1e-6

MLA_HEADS = 8
Q_LORA = 512
KV_LORA = 512
NOPE_DIM = 128
ROPE_DIM = 64
V_DIM = 128
ROPE_THETA = 10000.0
MLA_SCALE = (NOPE_DIM + ROPE_DIM) ** -0.5
QK_PAD = 256

GLA_HEADS = 4
GLA_DK = 128
GLA_DV = 256
GATE_RANK = 16
GATE_TAU = 16.0
GLA_WIDTH = GLA_HEADS * GLA_DV
MLA_WIDTH = MLA_HEADS * V_DIM

IN_SPLITS = (Q_LORA, KV_LORA, ROPE_DIM, GLA_HEADS * GLA_DK, GLA_HEADS * GLA_DK, GLA_WIDTH, GATE_RANK, GLA_WIDTH)
Z_MAIN = 4096
GLR_LO = ROPE_DIM

D_FF = 5632
PLE_DIM = 256

VMEM_LIMIT = 56 * 1024 * 1024
NEG_BIG = -1e30
LOG2E = 1.4426950408889634
QK_SCALE = MLA_SCALE * LOG2E


def _dot(a, b):
    return jnp.dot(a, b, preferred_element_type=F32)


def _dot_nt(a, b):
    return lax.dot_general(a, b, (((1,), (1,)), ((), ())), preferred_element_type=F32)


def _dot_tn(a, b):
    return lax.dot_general(a, b, (((0,), (0,)), ((), ())), preferred_element_type=F32)


def _rms(x, g):
    ms = jnp.mean(x * x, axis=-1, keepdims=True)
    return x * lax.rsqrt(ms + EPS) * g


def _rope(x, cos_t, sin_t):
    lane = lax.broadcasted_iota(jnp.int32, x.shape, 1)
    rot = jnp.where((lane & 32) == 0, pltpu.roll(x, 96, 1), pltpu.roll(x, 32, 1))
    return x * cos_t + rot * sin_t


def _params(*sem):
    return pltpu.CompilerParams(dimension_semantics=sem, vmem_limit_bytes=VMEM_LIMIT)


_IN_OFF = np.concatenate([[0], np.cumsum(IN_SPLITS)]).tolist()


_WIN_TILE = 512


def _win_prep_kernel(w_ref, kr_ref, glr_ref, o_ref, small_ref):
    o_ref[...] = w_ref[...].T.astype(BF16)

    @pl.when(pl.program_id(0) == 0)
    def _():
        pad = jnp.zeros((128 - GLR_LO - GATE_RANK, D_MODEL), F32)
        small_ref[...] = jnp.concatenate([kr_ref[...], glr_ref[...], pad], axis=0).T.astype(BF16)


def _win_prep(w_t):
    c_q, c_kv, k_r, q_g, k_g, v_g, g_lr, r_g, end = _IN_OFF
    n_head, n_mid = k_r // _WIN_TILE, (g_lr - q_g) // _WIN_TILE
    assert k_r % _WIN_TILE == 0 and (g_lr - q_g) % _WIN_TILE == 0 and (end - r_g) % _WIN_TILE == 0
    assert (n_head + n_mid) * _WIN_TILE + end - r_g == Z_MAIN and GLR_LO == ROPE_DIM

    def src_row(j):
        skip_mid, skip_tail = (q_g - k_r) // 8, (r_g - g_lr + q_g - k_r) // 8
        return 8 * (j * (_WIN_TILE // 8) + jnp.where(j < n_head, 0, jnp.where(j < n_head + n_mid, skip_mid, skip_tail)))

    return pl.pallas_call(
        _win_prep_kernel,
        grid=(Z_MAIN // _WIN_TILE,),
        in_specs=[pl.BlockSpec((pl.Element(_WIN_TILE), pl.Element(D_MODEL)), lambda j: (src_row(j), 0)),
                  pl.BlockSpec((pl.Element(ROPE_DIM), pl.Element(D_MODEL)), lambda j: (k_r, 0)),
                  pl.BlockSpec((pl.Element(GATE_RANK), pl.Element(D_MODEL)), lambda j: (g_lr, 0))],
        out_specs=[pl.BlockSpec((D_MODEL, _WIN_TILE), lambda j: (0, j)),
                   pl.BlockSpec((D_MODEL, 128), lambda j: (0, 0))],
        out_shape=[jax.ShapeDtypeStruct((D_MODEL, Z_MAIN), BF16), jax.ShapeDtypeStruct((D_MODEL, 128), BF16)],
        compiler_params=_params("arbitrary"),
        name="win_prep",
    )(w_t, w_t, w_t)


def _inproj_kernel(x_ref, g_ref, w_ref, ws_ref, o_ref, os_ref):
    xn = _rms(x_ref[...], g_ref[...]).astype(BF16)
    o_ref[...] = _dot(xn, w_ref[...])
    os_ref[...] = _dot(xn, ws_ref[...])


def _inproj(x, g, w_main, w_small):
    T = x.shape[0]
    tm = 512
    return pl.pallas_call(
        _inproj_kernel,
        grid=(T // tm,),
        in_specs=[pl.BlockSpec((tm, D_MODEL), lambda i: (i, 0)),
                  pl.BlockSpec((1, D_MODEL), lambda i: (0, 0)),
                  pl.BlockSpec((D_MODEL, Z_MAIN), lambda i: (0, 0), pipeline_mode=pl.Buffered(1)),
                  pl.BlockSpec((D_MODEL, 128), lambda i: (0, 0), pipeline_mode=pl.Buffered(1))],
        out_specs=[pl.BlockSpec((tm, Z_MAIN), lambda i: (i, 0)), pl.BlockSpec((tm, 128), lambda i: (i, 0))],
        out_shape=[jax.ShapeDtypeStruct((T, Z_MAIN), F32), jax.ShapeDtypeStruct((T, 128), F32)],
        compiler_params=_params("parallel"),
        name="inproj",
    )(x, g, w_main, w_small)


def _mla_q(cq_ref, gq_ref, wuq_ref, cos, sin):
    cqn = _rms(cq_ref[...], gq_ref[...]).astype(BF16)
    q = _dot(cqn, wuq_ref[...]) * QK_SCALE
    out = []
    for h in range(MLA_HEADS):
        nope = q[:, h * QK_PAD:h * QK_PAD + NOPE_DIM]
        rp = _rope(q[:, h * QK_PAD + NOPE_DIM:(h + 1) * QK_PAD], cos, sin)
        out.append((nope, rp))
    return out


def _mla_prep_prompt_kernel(cq_ref, ckv_ref, sm_ref, gq_ref, gkv_ref, wuq_ref, wukv_ref, cos_ref, sin_ref,
                            ckvn_ref, kr_ref, qcat_ref, kcat_ref, v_ref):
    cos, sin = cos_ref[...], sin_ref[...]
    for h, (nope, rp) in enumerate(_mla_q(cq_ref, gq_ref, wuq_ref, cos, sin)):
        qcat_ref[:, h * QK_PAD:h * QK_PAD + NOPE_DIM] = nope.astype(BF16)
        qcat_ref[:, h * QK_PAD + NOPE_DIM:(h + 1) * QK_PAD] = rp.astype(BF16)
    ckvn = _rms(ckv_ref[...], gkv_ref[...])
    ckvn_ref[...] = ckvn
    kr = _rope(sm_ref[...], cos, sin)
    kr_ref[...] = kr[:, :ROPE_DIM]
    kv = _dot(ckvn.astype(BF16), wukv_ref[...])
    krb = kr.astype(BF16)
    for h in range(MLA_HEADS):
        kcat_ref[:, h * QK_PAD:h * QK_PAD + NOPE_DIM] = kv[:, h * NOPE_DIM:(h + 1) * NOPE_DIM].astype(BF16)
        kcat_ref[:, h * QK_PAD + NOPE_DIM:(h + 1) * QK_PAD] = krb
    v_ref[...] = kv[:, MLA_WIDTH:].astype(BF16)


def _mla_prep_prompt(z, zs, g_q, g_kv, w_uq, w_ukv, cos_t, sin_t, seq):
    T = z.shape[0]
    tm = 512
    nseq = seq // tm
    row = lambda i: (i, 0)
    const = lambda i: (0, 0)
    return pl.pallas_call(
        _mla_prep_prompt_kernel,
        grid=(T // tm,),
        in_specs=[pl.BlockSpec((tm, Q_LORA), lambda i: (i, 0)),
                  pl.BlockSpec((tm, KV_LORA), lambda i: (i, 1)),
                  pl.BlockSpec((tm, 128), lambda i: (i, 0)),
                  pl.BlockSpec((1, Q_LORA), const),
                  pl.BlockSpec((1, KV_LORA), const),
                  pl.BlockSpec((Q_LORA, MLA_HEADS * QK_PAD), const),
                  pl.BlockSpec((KV_LORA, 2 * MLA_WIDTH), const),
                  pl.BlockSpec((tm, 128), lambda i: (i % nseq, 0)),
                  pl.BlockSpec((tm, 128), lambda i: (i % nseq, 0))],
        out_specs=[pl.BlockSpec((tm, KV_LORA), row),
                   pl.BlockSpec((tm, ROPE_DIM), row),
                   pl.BlockSpec((tm, MLA_HEADS * QK_PAD), row),
                   pl.BlockSpec((tm, MLA_HEADS * QK_PAD), row),
                   pl.BlockSpec((tm, MLA_WIDTH), row)],
        out_shape=[jax.ShapeDtypeStruct((T, KV_LORA), F32),
                   jax.ShapeDtypeStruct((T, ROPE_DIM), F32),
                   jax.ShapeDtypeStruct((T, MLA_HEADS * QK_PAD), BF16),
                   jax.ShapeDtypeStruct((T, MLA_HEADS * QK_PAD), BF16),
                   jax.ShapeDtypeStruct((T, MLA_WIDTH), BF16)],
        compiler_params=_params("parallel"),
        name="mla_prep_prompt",
    )(z, z, zs, g_q, g_kv, w_uq, w_ukv, cos_t, sin_t)


def _mla_prep_sample_kernel(cq_ref, ckv_ref, sm_ref, gq_ref, gkv_ref, wuq_ref, wukt_ref, cos_ref, sin_ref,
                            ckvn_ref, kr_ref, qlat_ref, qr_ref):
    cos, sin = cos_ref[...], sin_ref[...]
    for h, (nope, rp) in enumerate(_mla_q(cq_ref, gq_ref, wuq_ref, cos, sin)):
        qlat_ref[h] = _dot(nope.astype(BF16), wukt_ref[h]).astype(BF16)
        qr_ref[h] = rp[:, :ROPE_DIM].astype(BF16)
    ckvn_ref[...] = _rms(ckv_ref[...], gkv_ref[...])
    kr_ref[...] = _rope(sm_ref[...], cos, sin)[:, :ROPE_DIM]


def _mla_prep_sample(z, zs, g_q, g_kv, w_uq, w_ukt, cos_t, sin_t):
    T = z.shape[0]
    tm = T
    row = lambda i: (i, 0)
    const = lambda i: (0, 0)
    const3 = lambda i: (0, 0, 0)
    return pl.pallas_call(
        _mla_prep_sample_kernel,
        grid=(T // tm,),
        in_specs=[pl.BlockSpec((tm, Q_LORA), lambda i: (i, 0)),
                  pl.BlockSpec((tm, KV_LORA), lambda i: (i, 1)),
                  pl.BlockSpec((tm, 128), lambda i: (i, 0)),
                  pl.BlockSpec((1, Q_LORA), const),
                  pl.BlockSpec((1, KV_LORA), const),
                  pl.BlockSpec((Q_LORA, MLA_HEADS * QK_PAD), const),
                  pl.BlockSpec((MLA_HEADS, NOPE_DIM, KV_LORA), const3),
                  pl.BlockSpec((tm, 128), row),
                  pl.BlockSpec((tm, 128), row)],
        out_specs=[pl.BlockSpec((tm, KV_LORA), row),
                   pl.BlockSpec((tm, ROPE_DIM), row),
                   pl.BlockSpec((MLA_HEADS, tm, KV_LORA), lambda i: (0, i, 0)),
                   pl.BlockSpec((MLA_HEADS, tm, ROPE_DIM), lambda i: (0, i, 0))],
        out_shape=[jax.ShapeDtypeStruct((T, KV_LORA), F32),
                   jax.ShapeDtypeStruct((T, ROPE_DIM), F32),
                   jax.ShapeDtypeStruct((MLA_HEADS, T, KV_LORA), BF16),
                   jax.ShapeDtypeStruct((MLA_HEADS, T, ROPE_DIM), BF16)],
        compiler_params=_params("parallel"),
        name="mla_prep_sample",
    )(z, z, zs, g_q, g_kv, w_uq, w_ukt, cos_t, sin_t)


ATT_TQ = 256
ATT_HEADS = 2


def _attn_prompt_kernel(q_ref, k_ref, v_ref, o_ref):
    seq = q_ref.shape[0]
    qc = lax.broadcasted_iota(jnp.int32, (ATT_TQ, ATT_TQ), 0) // CHUNK
    kc = lax.broadcasted_iota(jnp.int32, (ATT_TQ, ATT_TQ), 1) // CHUNK
    visible = kc <= qc
    for i in reversed(range(seq // ATT_TQ)):
        lo, hi = i * ATT_TQ, (i + 1) * ATT_TQ
        for h in range(ATT_HEADS):
            qk = slice(h * QK_PAD, (h + 1) * QK_PAD)
            dv = slice(h * V_DIM, (h + 1) * V_DIM)
            s = _dot_nt(q_ref[lo:hi, qk], k_ref[0:hi, qk])
            s_diag = jnp.where(visible, s[:, lo:hi], NEG_BIG)
            s = jnp.concatenate([s[:, :lo], s_diag], axis=1) if i else s_diag
            m = jnp.max(s, axis=-1, keepdims=True)
            p = jnp.exp2(s - m).astype(BF16)
            v_one = jnp.concatenate([v_ref[0:hi, dv], jnp.ones((hi, V_DIM), BF16)], axis=1)
            ol = _dot(p, v_one)
            o_ref[lo:hi, dv] = (ol[:, :V_DIM] / ol[:, V_DIM:]).astype(BF16)


def _attn_prompt(qcat, kcat, v, batch, seq):
    return pl.pallas_call(
        _attn_prompt_kernel,
        grid=(batch, MLA_HEADS // ATT_HEADS),
        in_specs=[pl.BlockSpec((seq, ATT_HEADS * QK_PAD), lambda b, h: (b, h)),
                  pl.BlockSpec((seq, ATT_HEADS * QK_PAD), lambda b, h: (b, h)),
                  pl.BlockSpec((seq, ATT_HEADS * V_DIM), lambda b, h: (b, h))],
        out_specs=pl.BlockSpec((seq, ATT_HEADS * V_DIM), lambda b, h: (b, h)),
        out_shape=jax.ShapeDtypeStruct((batch * seq, MLA_WIDTH), BF16),
        compiler_params=_params("parallel", "parallel"),
        name="attn_prompt",
    )(qcat, kcat, v)


def _attn_sample_kernel(ql_ref, qr_ref, ckv_ref, krt_ref, nckv_ref, nkr_ref, wuv_ref, o_ref):
    dec = ql_ref.shape[1]
    rows = MLA_HEADS * dec
    ql = ql_ref[...].reshape(rows, KV_LORA)
    qr = qr_ref[...].reshape(rows, ROPE_DIM)
    ckv = ckv_ref[...].astype(BF16)
    nckv = nckv_ref[...].astype(BF16)
    s = _dot_nt(ql, ckv) + _dot(qr, krt_ref[...].astype(BF16))
    s_new = _dot_nt(ql, nckv) + _dot_nt(qr, nkr_ref[...].astype(BF16))
    m = jnp.maximum(jnp.max(s, axis=-1, keepdims=True), jnp.max(s_new, axis=-1, keepdims=True))
    p = jnp.exp2(s - m)
    p_new = jnp.exp2(s_new - m)
    l = jnp.sum(p, axis=-1, keepdims=True) + jnp.sum(p_new, axis=-1, keepdims=True)
    o = ((_dot(p.astype(BF16), ckv) + _dot(p_new.astype(BF16), nckv)) / l).astype(BF16)
    for h in range(MLA_HEADS):
        o_ref[:, h * V_DIM:(h + 1) * V_DIM] = _dot(o[h * dec:(h + 1) * dec], wuv_ref[h]).astype(BF16)


def _attn_sample(q_lat, q_rope, cache_ckv, cache_krt, ckvn, kr, w_uvh, dec):
    batch, past, _ = cache_ckv.shape
    return pl.pallas_call(
        _attn_sample_kernel,
        grid=(batch,),
        in_specs=[pl.BlockSpec((MLA_HEADS, dec, KV_LORA), lambda b: (0, b, 0)),
                  pl.BlockSpec((MLA_HEADS, dec, ROPE_DIM), lambda b: (0, b, 0)),
                  pl.BlockSpec((None, past, KV_LORA), lambda b: (b, 0, 0)),
                  pl.BlockSpec((None, ROPE_DIM, past), lambda b: (b, 0, 0)),
                  pl.BlockSpec((dec, KV_LORA), lambda b: (b, 0)),
                  pl.BlockSpec((dec, ROPE_DIM), lambda b: (b, 0)),
                  pl.BlockSpec((MLA_HEADS, KV_LORA, V_DIM), lambda b: (0, 0, 0))],
        out_specs=pl.BlockSpec((dec, MLA_WIDTH), lambda b: (b, 0)),
        out_shape=jax.ShapeDtypeStruct((batch * dec, MLA_WIDTH), BF16),
        compiler_params=_params("parallel"),
        name="attn_sample",
    )(q_lat, q_rope, cache_ckv, cache_krt, ckvn, kr, w_uvh)


GLA_CHUNK = 256


def _gla_tables(c):
    nlev = int(np.log2(c))
    t = np.arange(c)[:, None]
    u = np.arange(c)[None, :]
    blocks = []
    lvl = np.full((c, c), -1, np.int32)
    for l in range(nlev):
        width = c >> l
        half = width // 2
        m = (t // width) * width + half - 1
        upper = t > m
        blocks.append(np.where(upper, (u > m) & (u <= t), (u > t) & (u <= m)))
        same = (t // width) == (u // width)
        lvl[same & ((t % width) >= half) & ((u % width) < half)] = l
    blocks.append(u > t)
    blocks.append(u <= t)
    lvl[np.arange(c), np.arange(c)] = nlev
    return np.concatenate(blocks, 0).astype(np.float32), lvl, nlev


def _gla_kernel(*refs, c, nlev, has_init):
    if has_init:
        (q_ref, k_ref, v_ref, r_ref, sm_ref, wga_ref, bga_ref, gg_ref, p_ref, lvl_ref, s0_ref,
         o_ref, sfin_ref, s_scr) = refs
    else:
        (q_ref, k_ref, v_ref, r_ref, sm_ref, wga_ref, bga_ref, gg_ref, p_ref, lvl_ref,
         o_ref, sfin_ref, s_scr) = refs
    j = pl.program_id(1)

    @pl.when(j == 0)
    def _():
        if has_init:
            s_scr[...] = s0_ref[0]
        else:
            s_scr[...] = jnp.zeros(s_scr.shape, F32)

    x = _dot(sm_ref[...].astype(BF16), wga_ref[...]) + bga_ref[...]
    la = (jnp.minimum(x, 0.0) - jnp.log1p(jnp.exp(-jnp.abs(x)))) * (1.0 / GATE_TAU)
    hi = la.astype(BF16)
    mid = (la - hi.astype(F32)).astype(BF16)
    p_mat = p_ref[...]
    e_all = jnp.exp(_dot(p_mat, hi) + _dot(p_mat, mid))
    ones = jnp.ones((c, 128), BF16)
    d_state = jnp.exp(_dot_tn(hi, ones) + _dot_tn(mid, ones))
    q = q_ref[...] * (GLA_DK ** -0.5)
    k = k_ref[...]
    qe = [(q * e_all[l * c:(l + 1) * c]).astype(BF16) for l in range(nlev)] + [q.astype(BF16)]
    ke = [(k * e_all[l * c:(l + 1) * c]).astype(BF16) for l in range(nlev)] + [k.astype(BF16)]
    k_end = (k * e_all[nlev * c:(nlev + 1) * c]).astype(BF16)
    q_beg = (q * e_all[(nlev + 1) * c:(nlev + 2) * c]).astype(BF16)
    lvl = lvl_ref[...]
    for h in range(GLA_HEADS):
        dk = slice(h * GLA_DK, (h + 1) * GLA_DK)
        dv = slice(h * GLA_DV, (h + 1) * GLA_DV)
        attn = jnp.zeros((c, c), F32)
        for l in range(nlev + 1):
            attn = jnp.where(lvl == l, _dot_nt(qe[l][:, dk], ke[l][:, dk]), attn)
        v = v_ref[:, dv].astype(BF16)
        state = s_scr[h]
        o = _dot(attn.astype(BF16), v) + _dot(q_beg[:, dk], state.astype(BF16))
        decay = d_state[dk]
        s_scr[h] = jnp.concatenate([decay, decay], axis=1) * state + _dot_tn(k_end[:, dk], v)
        r = r_ref[:, dv]
        o_ref[:, dv] = (_rms(o, gg_ref[...]) * (r * jax.nn.sigmoid(r))).astype(BF16)

    @pl.when(j == pl.num_programs(1) - 1)
    def _():
        sfin_ref[0] = s_scr[...]


def _gla(z, zs, w_ga, b_ga, g_gla, batch, seq, c, s0=None):
    p_np, lvl_np, nlev = _gla_tables(c)
    p_mat = jnp.asarray(p_np, BF16)
    lvl = jnp.asarray(lvl_np)
    nstep = seq // c
    r = c
    row = lambda b, j: (b * nstep + j, 0)
    const = lambda b, j: (0, 0)
    in_specs = [pl.BlockSpec((r, 512), lambda b, j: (b * nstep + j, 2)),
                pl.BlockSpec((r, 512), lambda b, j: (b * nstep + j, 3)),
                pl.BlockSpec((r, GLA_WIDTH), lambda b, j: (b * nstep + j, 2)),
                pl.BlockSpec((r, GLA_WIDTH), lambda b, j: (b * nstep + j, 3)),
                pl.BlockSpec((r, 128), lambda b, j: (b * nstep + j, 0)),
                pl.BlockSpec((128, GLA_HEADS * GLA_DK), const),
                pl.BlockSpec((1, GLA_HEADS * GLA_DK), const),
                pl.BlockSpec((1, GLA_DV), const),
                pl.BlockSpec(p_np.shape, const),
                pl.BlockSpec((c, c), const)]
    args = [z, z, z, z, zs, w_ga, b_ga, g_gla, p_mat, lvl]
    state_spec = pl.BlockSpec((1, GLA_HEADS, GLA_DK, GLA_DV), lambda b, j: (b, 0, 0, 0))
    if s0 is not None:
        in_specs.append(state_spec)
        args.append(s0)
    return pl.pallas_call(
        functools.partial(_gla_kernel, c=c, nlev=nlev, has_init=s0 is not None),
        grid=(batch, nstep),
        in_specs=in_specs,
        out_specs=[pl.BlockSpec((r, GLA_WIDTH), row), state_spec],
        out_shape=[jax.ShapeDtypeStruct((batch * seq, GLA_WIDTH), BF16),
                   jax.ShapeDtypeStruct((batch, GLA_HEADS, GLA_DK, GLA_DV), F32)],
        scratch_shapes=[pltpu.VMEM((GLA_HEADS, GLA_DK, GLA_DV), F32)],
        compiler_params=_params("parallel", "arbitrary"),
        name="gla_init" if s0 is not None else "gla",
    )(*args)


ROW_SUB = 128


def _mix_kernel(om_ref, og_ref, w1_ref, w2_ref, h_ref, g1_ref, g2_ref, h1_ref, f_ref):
    for r in range(0, h_ref.shape[0], ROW_SUB):
        rows = slice(r, r + ROW_SUB)
        mix = _dot(om_ref[rows, :], w1_ref[...]) + _dot(og_ref[rows, :], w2_ref[...])
        h1 = h_ref[rows, :] + _rms(mix, g1_ref[...])
        h1_ref[rows, :] = h1
        f_ref[rows, :] = _rms(h1, g2_ref[...]).astype(BF16)


def _mix(o_mla, o_gla, w_out, h, g_post_mix, g_pre_ffn):
    T = h.shape[0]
    tm = 512
    row = lambda i: (i, 0)
    const = lambda i: (0, 0)
    return pl.pallas_call(
        _mix_kernel,
        grid=(T // tm,),
        in_specs=[pl.BlockSpec((tm, MLA_WIDTH), row),
                  pl.BlockSpec((tm, GLA_WIDTH), row),
                  pl.BlockSpec((MLA_WIDTH, D_MODEL), lambda i: (0, 0), pipeline_mode=pl.Buffered(1)),
                  pl.BlockSpec((GLA_WIDTH, D_MODEL), lambda i: (1, 0), pipeline_mode=pl.Buffered(1)),
                  pl.BlockSpec((tm, D_MODEL), row),
                  pl.BlockSpec((1, D_MODEL), const),
                  pl.BlockSpec((1, D_MODEL), const)],
        out_specs=[pl.BlockSpec((tm, D_MODEL), row), pl.BlockSpec((tm, D_MODEL), row)],
        out_shape=[jax.ShapeDtypeStruct((T, D_MODEL), F32), jax.ShapeDtypeStruct((T, D_MODEL), BF16)],
        compiler_params=_params("parallel"),
        name="mix",
    )(o_mla, o_gla, w_out, w_out, h, g_post_mix, g_pre_ffn)


def _ffn_kernel(x_ref, wg_ref, wu_ref, wd_ref, o_ref):
    @pl.when(pl.program_id(1) == 0)
    def _():
        o_ref[...] = jnp.zeros(o_ref.shape, F32)

    x = x_ref[...]
    g = _dot(x, wg_ref[...].astype(BF16))
    u = _dot(x, wu_ref[...].astype(BF16))
    o_ref[...] += _dot((g * jax.nn.sigmoid(g) * u).astype(BF16), wd_ref[...].astype(BF16))


def _ffn(x, w_gate, w_up, w_down):
    T = x.shape[0]
    tm, tf = min(T, 1024), 512
    return pl.pallas_call(
        _ffn_kernel,
        grid=(T // tm, D_FF // tf),
        in_specs=[pl.BlockSpec((tm, D_MODEL), lambda i, j: (i, 0)),
                  pl.BlockSpec((D_MODEL, tf), lambda i, j: (0, j)),
                  pl.BlockSpec((D_MODEL, tf), lambda i, j: (0, j)),
                  pl.BlockSpec((tf, D_MODEL), lambda i, j: (j, 0))],
        out_specs=pl.BlockSpec((tm, D_MODEL), lambda i, j: (i, 0)),
        out_shape=jax.ShapeDtypeStruct((T, D_MODEL), F32),
        compiler_params=_params("parallel", "arbitrary"),
        name="ffn",
    )(x, w_gate, w_up, w_down)


def _final_kernel(f_ref, h1_ref, p_ref, g_ref, wpg_ref, wp_ref, o_ref):
    h2 = h1_ref[...] + _rms(f_ref[...], g_ref[...])
    gate = jax.nn.sigmoid(_dot(h2.astype(BF16), wpg_ref[...]))
    o_ref[...] = h2 + gate * _dot(p_ref[...].astype(BF16), wp_ref[...])


def _final(f, h1, p, g_post_ffn, w_ple_gate, w_ple):
    T = f.shape[0]
    tm = 512
    row = lambda i: (i, 0)
    const = lambda i: (0, 0)
    return pl.pallas_call(
        _final_kernel,
        grid=(T // tm,),
        in_specs=[pl.BlockSpec((tm, D_MODEL), row),
                  pl.BlockSpec((tm, D_MODEL), row),
                  pl.BlockSpec((tm, PLE_DIM), row),
                  pl.BlockSpec((1, D_MODEL), const),
                  pl.BlockSpec((D_MODEL, D_MODEL), const, pipeline_mode=pl.Buffered(1)),
                  pl.BlockSpec((PLE_DIM, D_MODEL), const)],
        out_specs=pl.BlockSpec((tm, D_MODEL), row),
        out_shape=jax.ShapeDtypeStruct((T, D_MODEL), F32),
        compiler_params=_params("parallel"),
        name="final",
    )(f, h1, p, g_post_ffn, w_ple_gate, w_ple)


def _rope_tables(pos):
    half = ROPE_DIM // 2
    inv = 1.0 / (ROPE_THETA ** (jnp.arange(half, dtype=F32) / half))
    ang = pos.astype(F32)[:, None] * inv[None, :]
    cos, sin = jnp.cos(ang), jnp.sin(ang)
    zero = jnp.zeros((pos.shape[0], 128 - ROPE_DIM), F32)
    return jnp.concatenate([cos, cos, zero], axis=1), jnp.concatenate([-sin, sin, zero], axis=1)


def _layer_weights(i, g_pre_mix, w_in, g_q, w_uq, w_uk, g_kv, w_ga, b_ga, w_uv, g_gla, w_out, g_post_mix,
                   g_pre_ffn, w_gate, w_up, w_down, g_post_ffn, w_ple, w_ple_gate):
    w_uq_p = jnp.pad(w_uq[i].reshape(Q_LORA, MLA_HEADS, NOPE_DIM + ROPE_DIM),
                     ((0, 0), (0, 0), (0, QK_PAD - NOPE_DIM - ROPE_DIM))).reshape(Q_LORA, MLA_HEADS * QK_PAD)
    w_ga_p = jnp.zeros((128, GLA_HEADS * GLA_DK), F32).at[GLR_LO:GLR_LO + GATE_RANK].set(w_ga[i])
    vec = lambda g: g[i].reshape(1, -1)
    return dict(
        g_pre_mix=vec(g_pre_mix), w_in=_win_prep(jnp.swapaxes(w_in[i], 0, 1)), g_q=vec(g_q), g_kv=vec(g_kv),
        w_uq=w_uq_p.astype(BF16),
        w_ukv=jnp.concatenate([w_uk[i].reshape(KV_LORA, MLA_WIDTH), w_uv[i].reshape(KV_LORA, MLA_WIDTH)],
                              axis=1).astype(BF16),
        w_ukt=jnp.transpose(w_uk[i], (1, 2, 0)).astype(BF16),
        w_uvh=jnp.transpose(w_uv[i], (1, 0, 2)).astype(BF16),
        w_ga=w_ga_p.astype(BF16), b_ga=vec(b_ga), g_gla=vec(g_gla),
        w_out=w_out[i].astype(BF16), g_post_mix=vec(g_post_mix), g_pre_ffn=vec(g_pre_ffn),
        w_gate=w_gate[i], w_up=w_up[i], w_down=w_down[i],
        g_post_ffn=vec(g_post_ffn), w_ple=w_ple[i].astype(BF16), w_ple_gate=w_ple_gate[i].astype(BF16))


def _finish(w, h, p, o_mla, o_gla):
    h1, f_in = _mix(o_mla, o_gla, w["w_out"], h, w["g_post_mix"], w["g_pre_ffn"])
    f = _ffn(f_in, w["w_gate"], w["w_up"], w["w_down"])
    return _final(f, h1, p, w["g_post_ffn"], w["w_ple_gate"], w["w_ple"])


def kernel(x_prompt, x_sample, cache_ckv, cache_krope, state_gla, p_prompt, p_sample, g_pre_mix, w_in, g_q, w_uq,
           w_uk, g_kv, w_ga, b_ga, w_uv, g_gla, w_out, g_post_mix, g_pre_ffn, w_gate, w_up, w_down, g_post_ffn,
           w_ple, w_ple_gate):
    batch, seq, _ = x_prompt.shape
    dbatch, dseq, _ = x_sample.shape
    depth = w_in.shape[0]
    cos_p, sin_p = _rope_tables(jnp.arange(seq))
    cos_s, sin_s = _rope_tables(PAST_LEN + jnp.arange(dseq))
    cos_s, sin_s = jnp.tile(cos_s, (dbatch, 1)), jnp.tile(sin_s, (dbatch, 1))
    h_p = x_prompt.reshape(batch * seq, D_MODEL)
    h_s = x_sample.reshape(dbatch * dseq, D_MODEL)
    outs = [[] for _ in range(6)]
    for i in range(depth):
        w = _layer_weights(i, g_pre_mix, w_in, g_q, w_uq, w_uk, g_kv, w_ga, b_ga, w_uv, g_gla, w_out, g_post_mix,
                           g_pre_ffn, w_gate, w_up, w_down, g_post_ffn, w_ple, w_ple_gate)
        z, zs = _inproj(h_p, w["g_pre_mix"], *w["w_in"])
        ckvn, kr, qcat, kcat, v = _mla_prep_prompt(z, zs, w["g_q"], w["g_kv"], w["w_uq"], w["w_ukv"], cos_p, sin_p, seq)
        o_mla = _attn_prompt(qcat, kcat, v, batch, seq)
        o_gla, s_fin = _gla(z, zs, w["w_ga"], w["b_ga"], w["g_gla"], batch, seq, GLA_CHUNK)
        h_p = _finish(w, h_p, p_prompt[i].reshape(batch * seq, PLE_DIM), o_mla, o_gla)
        outs[0].append(ckvn.reshape(batch, seq, KV_LORA))
        outs[1].append(kr.reshape(batch, seq, ROPE_DIM))
        outs[2].append(s_fin)
        z, zs = _inproj(h_s, w["g_pre_mix"], *w["w_in"])
        ckvn, kr, q_lat, q_rope = _mla_prep_sample(z, zs, w["g_q"], w["g_kv"], w["w_uq"], w["w_ukt"], cos_s, sin_s)
        o_mla = _attn_sample(q_lat, q_rope, cache_ckv[i], jnp.swapaxes(cache_krope[i], 1, 2), ckvn, kr,
                             w["w_uvh"], dseq)
        o_gla, s_new = _gla(z, zs, w["w_ga"], w["b_ga"], w["g_gla"], dbatch, dseq, dseq, s0=state_gla[i])
        h_s = _finish(w, h_s, p_sample[i].reshape(dbatch * dseq, PLE_DIM), o_mla, o_gla)
        outs[3].append(ckvn.reshape(dbatch, dseq, KV_LORA))
        outs[4].append(kr.reshape(dbatch, dseq, ROPE_DIM))
        outs[5].append(s_new)
    return (h_p.reshape(batch, seq, D_MODEL), h_s.reshape(dbatch, dseq, D_MODEL),
            jnp.stack(outs[0]), jnp.stack(outs[1]), jnp.stack(outs[2]),
            jnp.stack(outs[3]), jnp.stack(outs[4]), jnp.stack(outs[5]))
```

```python
import functools

import numpy as np
import jax
import jax.numpy as jnp
from jax import lax
from jax.experimental import pallas as pl
from jax.experimental.pallas import tpu as pltpu

F32 = jnp.float32
BF16 = jnp.bfloat16

D_MODEL = 2048
PAST_LEN = 4096
CHUNK = 64
EPS = 1e-6

MLA_HEADS = 8
Q_LORA = 512
KV_LORA = 512
NOPE_DIM = 128
ROPE_DIM = 64
V_DIM = 128
ROPE_THETA = 10000.0
MLA_SCALE = (NOPE_DIM + ROPE_DIM) ** -0.5
QK_PAD = 256

GLA_HEADS = 4
GLA_DK = 128
GLA_DV = 256
GATE_RANK = 16
GATE_TAU = 16.0
GLA_WIDTH = GLA_HEADS * GLA_DV
MLA_WIDTH = MLA_HEADS * V_DIM

IN_SPLITS = (Q_LORA, KV_LORA, ROPE_DIM, GLA_HEADS * GLA_DK, GLA_HEADS * GLA_DK, GLA_WIDTH, GATE_RANK, GLA_WIDTH)
Z_MAIN = 4096
GLR_LO = ROPE_DIM

D_FF = 5632
PLE_DIM = 256

VMEM_LIMIT = 56 * 1024 * 1024
NEG_BIG = -1e30
LOG2E = 1.4426950408889634
QK_SCALE = MLA_SCALE * LOG2E


def _dot(a, b):
    return jnp.dot(a, b, preferred_element_type=F32)


def _dot_nt(a, b):
    return lax.dot_general(a, b, (((1,), (1,)), ((), ())), preferred_element_type=F32)


def _dot_tn(a, b):
    return lax.dot_general(a, b, (((0,), (0,)), ((), ())), preferred_element_type=F32)


def _rms(x, g):
    ms = jnp.mean(x * x, axis=-1, keepdims=True)
    return x * lax.rsqrt(ms + EPS) * g


def _rope(x, cos_t, sin_t):
    lane = lax.broadcasted_iota(jnp.int32, x.shape, 1)
    rot = jnp.where((lane & 32) == 0, pltpu.roll(x, 96, 1), pltpu.roll(x, 32, 1))
    return x * cos_t + rot * sin_t


def _params(*sem):
    return pltpu.CompilerParams(dimension_semantics=sem, vmem_limit_bytes=VMEM_LIMIT)


_IN_OFF = np.concatenate([[0], np.cumsum(IN_SPLITS)]).tolist()


_WIN_TILE = 512


def _win_prep_kernel(w_ref, kr_ref, glr_ref, o_ref, small_ref):
    o_ref[...] = w_ref[...].T.astype(BF16)

    @pl.when(pl.program_id(0) == 0)
    def _():
        pad = jnp.zeros((128 - GLR_LO - GATE_RANK, D_MODEL), F32)
        small_ref[...] = jnp.concatenate([kr_ref[...], glr_ref[...], pad], axis=0).T.astype(BF16)


def _win_prep(w_t):
    c_q, c_kv, k_r, q_g, k_g, v_g, g_lr, r_g, end = _IN_OFF
    n_head, n_mid = k_r // _WIN_TILE, (g_lr - q_g) // _WIN_TILE
    assert k_r % _WIN_TILE == 0 and (g_lr - q_g) % _WIN_TILE == 0 and (end - r_g) % _WIN_TILE == 0
    assert (n_head + n_mid) * _WIN_TILE + end - r_g == Z_MAIN and GLR_LO == ROPE_DIM

    def src_row(j):
        skip_mid, skip_tail = (q_g - k_r) // 8, (r_g - g_lr + q_g - k_r) // 8
        return 8 * (j * (_WIN_TILE // 8) + jnp.where(j < n_head, 0, jnp.where(j < n_head + n_mid, skip_mid, skip_tail)))

    return pl.pallas_call(
        _win_prep_kernel,
        grid=(Z_MAIN // _WIN_TILE,),
        in_specs=[pl.BlockSpec((pl.Element(_WIN_TILE), pl.Element(D_MODEL)), lambda j: (src_row(j), 0)),
                  pl.BlockSpec((pl.Element(ROPE_DIM), pl.Element(D_MODEL)), lambda j: (k_r, 0)),
                  pl.BlockSpec((pl.Element(GATE_RANK), pl.Element(D_MODEL)), lambda j: (g_lr, 0))],
        out_specs=[pl.BlockSpec((D_MODEL, _WIN_TILE), lambda j: (0, j)),
                   pl.BlockSpec((D_MODEL, 128), lambda j: (0, 0))],
        out_shape=[jax.ShapeDtypeStruct((D_MODEL, Z_MAIN), BF16), jax.ShapeDtypeStruct((D_MODEL, 128), BF16)],
        compiler_params=_params("arbitrary"),
        name="win_prep",
    )(w_t, w_t, w_t)


def _inproj_kernel(x_ref, g_ref, w_ref, ws_ref, o_ref, os_ref):
    xn = _rms(x_ref[...], g_ref[...]).astype(BF16)
    o_ref[...] = _dot(xn, w_ref[...])
    os_ref[...] = _dot(xn, ws_ref[...])


def _inproj(x, g, w_main, w_small):
    T = x.shape[0]
    tm = 512
    return pl.pallas_call(
        _inproj_kernel,
        grid=(T // tm,),
        in_specs=[pl.BlockSpec((tm, D_MODEL), lambda i: (i, 0)),
                  pl.BlockSpec((1, D_MODEL), lambda i: (0, 0)),
                  pl.BlockSpec((D_MODEL, Z_MAIN), lambda i: (0, 0), pipeline_mode=pl.Buffered(1)),
                  pl.BlockSpec((D_MODEL, 128), lambda i: (0, 0), pipeline_mode=pl.Buffered(1))],
        out_specs=[pl.BlockSpec((tm, Z_MAIN), lambda i: (i, 0)), pl.BlockSpec((tm, 128), lambda i: (i, 0))],
        out_shape=[jax.ShapeDtypeStruct((T, Z_MAIN), F32), jax.ShapeDtypeStruct((T, 128), F32)],
        compiler_params=_params("parallel"),
        name="inproj",
    )(x, g, w_main, w_small)


def _mla_q(cq_ref, gq_ref, wuq_ref, cos, sin):
    cqn = _rms(cq_ref[...], gq_ref[...]).astype(BF16)
    q = _dot(cqn, wuq_ref[...]) * QK_SCALE
    out = []
    for h in range(MLA_HEADS):
        nope = q[:, h * QK_PAD:h * QK_PAD + NOPE_DIM]
        rp = _rope(q[:, h * QK_PAD + NOPE_DIM:(h + 1) * QK_PAD], cos, sin)
        out.append((nope, rp))
    return out


def _mla_prep_prompt_kernel(cq_ref, ckv_ref, sm_ref, gq_ref, gkv_ref, wuq_ref, wukt_ref, wuv_ref, cos_ref, sin_ref,
                            ckvn_ref, kr_ref, qcat_ref, kt_ref, v_ref):
    cos, sin = cos_ref[...], sin_ref[...]
    for h, (nope, rp) in enumerate(_mla_q(cq_ref, gq_ref, wuq_ref, cos, sin)):
        qcat_ref[:, h * QK_PAD:h * QK_PAD + NOPE_DIM] = nope.astype(BF16)
        qcat_ref[:, h * QK_PAD + NOPE_DIM:(h + 1) * QK_PAD] = rp.astype(BF16)
    ckvn = _rms(ckv_ref[...], gkv_ref[...])
    ckvn_ref[...] = ckvn
    kr = _rope(sm_ref[...], cos, sin)
    kr_ref[...] = kr[:, :ROPE_DIM]
    ckvn_b = ckvn.astype(BF16)
    k_t = _dot_nt(wukt_ref[...], ckvn_b)
    kr_t = kr.T.astype(BF16)
    for h in range(MLA_HEADS):
        kt_ref[h * QK_PAD:h * QK_PAD + NOPE_DIM, :] = k_t[h * NOPE_DIM:(h + 1) * NOPE_DIM].astype(BF16)
        kt_ref[h * QK_PAD + NOPE_DIM:(h + 1) * QK_PAD, :] = kr_t
    v_ref[...] = _dot(ckvn_b, wuv_ref[...]).astype(BF16)


def _mla_prep_prompt(z, zs, g_q, g_kv, w_uq, w_ukt, w_uv, cos_t, sin_t, seq):
    T = z.shape[0]
    tm = 512
    nseq = seq // tm
    row = lambda i: (i, 0)
    const = lambda i: (0, 0)
    return pl.pallas_call(
        _mla_prep_prompt_kernel,
        grid=(T // tm,),
        in_specs=[pl.BlockSpec((tm, Q_LORA), lambda i: (i, 0)),
                  pl.BlockSpec((tm, KV_LORA), lambda i: (i, 1)),
                  pl.BlockSpec((tm, 128), lambda i: (i, 0)),
                  pl.BlockSpec((1, Q_LORA), const),
                  pl.BlockSpec((1, KV_LORA), const),
                  pl.BlockSpec((Q_LORA, MLA_HEADS * QK_PAD), const),
                  pl.BlockSpec((MLA_WIDTH, KV_LORA), const),
                  pl.BlockSpec((KV_LORA, MLA_WIDTH), const),
                  pl.BlockSpec((tm, 128), lambda i: (i % nseq, 0)),
                  pl.BlockSpec((tm, 128), lambda i: (i % nseq, 0))],
        out_specs=[pl.BlockSpec((tm, KV_LORA), row),
                   pl.BlockSpec((tm, ROPE_DIM), row),
                   pl.BlockSpec((tm, MLA_HEADS * QK_PAD), row),
                   pl.BlockSpec((MLA_HEADS * QK_PAD, tm), lambda i: (0, i)),
                   pl.BlockSpec((tm, MLA_WIDTH), row)],
        out_shape=[jax.ShapeDtypeStruct((T, KV_LORA), F32),
                   jax.ShapeDtypeStruct((T, ROPE_DIM), F32),
                   jax.ShapeDtypeStruct((T, MLA_HEADS * QK_PAD), BF16),
                   jax.ShapeDtypeStruct((MLA_HEADS * QK_PAD, T), BF16),
                   jax.ShapeDtypeStruct((T, MLA_WIDTH), BF16)],
        compiler_params=_params("parallel"),
        name="mla_prep_prompt",
    )(z, z, zs, g_q, g_kv, w_uq, w_ukt, w_uv, cos_t, sin_t)


def _mla_prep_sample_kernel(cq_ref, ckv_ref, sm_ref, gq_ref, gkv_ref, wuq_ref, wukt_ref, cos_ref, sin_ref,
                            ckvn_ref, kr_ref, qlat_ref, qr_ref):
    cos, sin = cos_ref[...], sin_ref[...]
    for h, (nope, rp) in enumerate(_mla_q(cq_ref, gq_ref, wuq_ref, cos, sin)):
        qlat_ref[h] = _dot(nope.astype(BF16), wukt_ref[h]).astype(BF16)
        qr_ref[h] = rp[:, :ROPE_DIM].astype(BF16)
    ckvn_ref[...] = _rms(ckv_ref[...], gkv_ref[...])
    kr_ref[...] = _rope(sm_ref[...], cos, sin)[:, :ROPE_DIM]


def _mla_prep_sample(z, zs, g_q, g_kv, w_uq, w_ukt, cos_t, sin_t):
    T = z.shape[0]
    tm = T
    row = lambda i: (i, 0)
    const = lambda i: (0, 0)
    const3 = lambda i: (0, 0, 0)
    return pl.pallas_call(
        _mla_prep_sample_kernel,
        grid=(T // tm,),
        in_specs=[pl.BlockSpec((tm, Q_LORA), lambda i: (i, 0)),
                  pl.BlockSpec((tm, KV_LORA), lambda i: (i, 1)),
                  pl.BlockSpec((tm, 128), lambda i: (i, 0)),
                  pl.BlockSpec((1, Q_LORA), const),
                  pl.BlockSpec((1, KV_LORA), const),
                  pl.BlockSpec((Q_LORA, MLA_HEADS * QK_PAD), const),
                  pl.BlockSpec((MLA_HEADS, NOPE_DIM, KV_LORA), const3),
                  pl.BlockSpec((tm, 128), row),
                  pl.BlockSpec((tm, 128), row)],
        out_specs=[pl.BlockSpec((tm, KV_LORA), row),
                   pl.BlockSpec((tm, ROPE_DIM), row),
                   pl.BlockSpec((MLA_HEADS, tm, KV_LORA), lambda i: (0, i, 0)),
                   pl.BlockSpec((MLA_HEADS, tm, ROPE_DIM), lambda i: (0, i, 0))],
        out_shape=[jax.ShapeDtypeStruct((T, KV_LORA), F32),
                   jax.ShapeDtypeStruct((T, ROPE_DIM), F32),
                   jax.ShapeDtypeStruct((MLA_HEADS, T, KV_LORA), BF16),
                   jax.ShapeDtypeStruct((MLA_HEADS, T, ROPE_DIM), BF16)],
        compiler_params=_params("parallel"),
        name="mla_prep_sample",
    )(z, z, zs, g_q, g_kv, w_uq, w_ukt, cos_t, sin_t)


ATT_TQ = 256
ATT_HEADS = 2


def _attn_prompt_kernel(q_ref, kt_ref, v_ref, o_ref):
    seq = q_ref.shape[0]
    qc = lax.broadcasted_iota(jnp.int32, (ATT_TQ, ATT_TQ), 0) // CHUNK
    kc = lax.broadcasted_iota(jnp.int32, (ATT_TQ, ATT_TQ), 1) // CHUNK
    visible = kc <= qc
    for i in reversed(range(seq // ATT_TQ)):
        lo, hi = i * ATT_TQ, (i + 1) * ATT_TQ
        for h in range(ATT_HEADS):
            qk = slice(h * QK_PAD, (h + 1) * QK_PAD)
            dv = slice(h * V_DIM, (h + 1) * V_DIM)
            s = _dot(q_ref[lo:hi, qk], kt_ref[qk, 0:hi])
            s_diag = jnp.where(visible, s[:, lo:hi], NEG_BIG)
            s = jnp.concatenate([s[:, :lo], s_diag], axis=1) if i else s_diag
            m = jnp.max(s, axis=-1, keepdims=True)
            p = jnp.exp2(s - m).astype(BF16)
            v_one = jnp.concatenate([v_ref[0:hi, dv], jnp.ones((hi, V_DIM), BF16)], axis=1)
            ol = _dot(p, v_one)
            o_ref[lo:hi, dv] = (ol[:, :V_DIM] / ol[:, V_DIM:]).astype(BF16)


def _attn_prompt(qcat, kt, v, batch, seq):
    return pl.pallas_call(
        _attn_prompt_kernel,
        grid=(batch, MLA_HEADS // ATT_HEADS),
        in_specs=[pl.BlockSpec((seq, ATT_HEADS * QK_PAD), lambda b, h: (b, h)),
                  pl.BlockSpec((ATT_HEADS * QK_PAD, seq), lambda b, h: (h, b)),
                  pl.BlockSpec((seq, ATT_HEADS * V_DIM), lambda b, h: (b, h))],
        out_specs=pl.BlockSpec((seq, ATT_HEADS * V_DIM), lambda b, h: (b, h)),
        out_shape=jax.ShapeDtypeStruct((batch * seq, MLA_WIDTH), BF16),
        compiler_params=_params("parallel", "parallel"),
        name="attn_prompt",
    )(qcat, kt, v)


def _attn_sample_kernel(ql_ref, qr_ref, ckv_ref, krt_ref, nckv_ref, nkr_ref, wuv_ref, o_ref):
    dec = ql_ref.shape[1]
    rows = MLA_HEADS * dec
    ql = ql_ref[...].reshape(rows, KV_LORA)
    qr = qr_ref[...].reshape(rows, ROPE_DIM)
    ckv = ckv_ref[...].astype(BF16)
    nckv = nckv_ref[...].astype(BF16)
    s = _dot_nt(ql, ckv) + _dot(qr, krt_ref[...].astype(BF16))
    s_new = _dot_nt(ql, nckv) + _dot_nt(qr, nkr_ref[...].astype(BF16))
    m = jnp.maximum(jnp.max(s, axis=-1, keepdims=True), jnp.max(s_new, axis=-1, keepdims=True))
    p = jnp.exp2(s - m)
    p_new = jnp.exp2(s_new - m)
    l = jnp.sum(p, axis=-1, keepdims=True) + jnp.sum(p_new, axis=-1, keepdims=True)
    o = ((_dot(p.astype(BF16), ckv) + _dot(p_new.astype(BF16), nckv)) / l).astype(BF16)
    for h in range(MLA_HEADS):
        o_ref[:, h * V_DIM:(h + 1) * V_DIM] = _dot(o[h * dec:(h + 1) * dec], wuv_ref[h]).astype(BF16)


def _attn_sample(q_lat, q_rope, cache_ckv, cache_krt, ckvn, kr, w_uvh, dec):
    batch, past, _ = cache_ckv.shape
    return pl.pallas_call(
        _attn_sample_kernel,
        grid=(batch,),
        in_specs=[pl.BlockSpec((MLA_HEADS, dec, KV_LORA), lambda b: (0, b, 0)),
                  pl.BlockSpec((MLA_HEADS, dec, ROPE_DIM), lambda b: (0, b, 0)),
                  pl.BlockSpec((None, past, KV_LORA), lambda b: (b, 0, 0)),
                  pl.BlockSpec((None, ROPE_DIM, past), lambda b: (b, 0, 0)),
                  pl.BlockSpec((dec, KV_LORA), lambda b: (b, 0)),
                  pl.BlockSpec((dec, ROPE_DIM), lambda b: (b, 0)),
                  pl.BlockSpec((MLA_HEADS, KV_LORA, V_DIM), lambda b: (0, 0, 0))],
        out_specs=pl.BlockSpec((dec, MLA_WIDTH), lambda b: (b, 0)),
        out_shape=jax.ShapeDtypeStruct((batch * dec, MLA_WIDTH), BF16),
        compiler_params=_params("parallel"),
        name="attn_sample",
    )(q_lat, q_rope, cache_ckv, cache_krt, ckvn, kr, w_uvh)


GLA_CHUNK = 256


def _gla_tables(c):
    nlev = int(np.log2(c))
    t = np.arange(c)[:, None]
    u = np.arange(c)[None, :]
    blocks = []
    lvl = np.full((c, c), -1, np.int32)
    for l in range(nlev):
        width = c >> l
        half = width // 2
        m = (t // width) * width + half - 1
        upper = t > m
        blocks.append(np.where(upper, (u > m) & (u <= t), (u > t) & (u <= m)))
        same = (t // width) == (u // width)
        lvl[same & ((t % width) >= half) & ((u % width) < half)] = l
    blocks.append(u > t)
    blocks.append(u <= t)
    lvl[np.arange(c), np.arange(c)] = nlev
    return np.concatenate(blocks, 0).astype(np.float32), lvl, nlev


def _gla_kernel(*refs, c, nlev, has_init):
    if has_init:
        (q_ref, k_ref, v_ref, r_ref, sm_ref, wga_ref, bga_ref, gg_ref, p_ref, lvl_ref, s0_ref,
         o_ref, sfin_ref, s_scr) = refs
    else:
        (q_ref, k_ref, v_ref, r_ref, sm_ref, wga_ref, bga_ref, gg_ref, p_ref, lvl_ref,
         o_ref, sfin_ref, s_scr) = refs
    j = pl.program_id(1)

    @pl.when(j == 0)
    def _():
        if has_init:
            s_scr[...] = s0_ref[0]
        else:
            s_scr[...] = jnp.zeros(s_scr.shape, F32)

    x = _dot(sm_ref[...].astype(BF16), wga_ref[...]) + bga_ref[...]
    la = (jnp.minimum(x, 0.0) - jnp.log1p(jnp.exp(-jnp.abs(x)))) * (1.0 / GATE_TAU)
    hi = la.astype(BF16)
    mid = (la - hi.astype(F32)).astype(BF16)
    p_mat = p_ref[...]
    e_all = jnp.exp(_dot(p_mat, hi) + _dot(p_mat, mid))
    ones = jnp.ones((c, 128), BF16)
    d_state = jnp.exp(_dot_tn(hi, ones) + _dot_tn(mid, ones))
    q = q_ref[...] * (GLA_DK ** -0.5)
    k = k_ref[...]
    qe = [(q * e_all[l * c:(l + 1) * c]).astype(BF16) for l in range(nlev)] + [q.astype(BF16)]
    ke = [(k * e_all[l * c:(l + 1) * c]).astype(BF16) for l in range(nlev)] + [k.astype(BF16)]
    k_end = (k * e_all[nlev * c:(nlev + 1) * c]).astype(BF16)
    q_beg = (q * e_all[(nlev + 1) * c:(nlev + 2) * c]).astype(BF16)
    lvl = lvl_ref[...]
    for h in range(GLA_HEADS):
        dk = slice(h * GLA_DK, (h + 1) * GLA_DK)
        dv = slice(h * GLA_DV, (h + 1) * GLA_DV)
        attn = jnp.zeros((c, c), F32)
        for l in range(nlev + 1):
            attn = jnp.where(lvl == l, _dot_nt(qe[l][:, dk], ke[l][:, dk]), attn)
        v = v_ref[:, dv].astype(BF16)
        state = s_scr[h]
        o = _dot(attn.astype(BF16), v) + _dot(q_beg[:, dk], state.astype(BF16))
        decay = d_state[dk]
        s_scr[h] = jnp.concatenate([decay, decay], axis=1) * state + _dot_tn(k_end[:, dk], v)
        r = r_ref[:, dv]
        o_ref[:, dv] = (_rms(o, gg_ref[...]) * (r * jax.nn.sigmoid(r))).astype(BF16)

    @pl.when(j == pl.num_programs(1) - 1)
    def _():
        sfin_ref[0] = s_scr[...]


def _gla(z, zs, w_ga, b_ga, g_gla, batch, seq, c, s0=None):
    p_np, lvl_np, nlev = _gla_tables(c)
    p_mat = jnp.asarray(p_np, BF16)
    lvl = jnp.asarray(lvl_np)
    nstep = seq // c
    r = c
    row = lambda b, j: (b * nstep + j, 0)
    const = lambda b, j: (0, 0)
    in_specs = [pl.BlockSpec((r, 512), lambda b, j: (b * nstep + j, 2)),
                pl.BlockSpec((r, 512), lambda b, j: (b * nstep + j, 3)),
                pl.BlockSpec((r, GLA_WIDTH), lambda b, j: (b * nstep + j, 2)),
                pl.BlockSpec((r, GLA_WIDTH), lambda b, j: (b * nstep + j, 3)),
                pl.BlockSpec((r, 128), lambda b, j: (b * nstep + j, 0)),
                pl.BlockSpec((128, GLA_HEADS * GLA_DK), const),
                pl.BlockSpec((1, GLA_HEADS * GLA_DK), const),
                pl.BlockSpec((1, GLA_DV), const),
                pl.BlockSpec(p_np.shape, const),
                pl.BlockSpec((c, c), const)]
    args = [z, z, z, z, zs, w_ga, b_ga, g_gla, p_mat, lvl]
    state_spec = pl.BlockSpec((1, GLA_HEADS, GLA_DK, GLA_DV), lambda b, j: (b, 0, 0, 0))
    if s0 is not None:
        in_specs.append(state_spec)
        args.append(s0)
    return pl.pallas_call(
        functools.partial(_gla_kernel, c=c, nlev=nlev, has_init=s0 is not None),
        grid=(batch, nstep),
        in_specs=in_specs,
        out_specs=[pl.BlockSpec((r, GLA_WIDTH), row), state_spec],
        out_shape=[jax.ShapeDtypeStruct((batch * seq, GLA_WIDTH), BF16),
                   jax.ShapeDtypeStruct((batch, GLA_HEADS, GLA_DK, GLA_DV), F32)],
        scratch_shapes=[pltpu.VMEM((GLA_HEADS, GLA_DK, GLA_DV), F32)],
        compiler_params=_params("parallel", "arbitrary"),
        name="gla_init" if s0 is not None else "gla",
    )(*args)


ROW_SUB = 128


def _mix_kernel(om_ref, og_ref, w1_ref, w2_ref, h_ref, g1_ref, g2_ref, h1_ref, f_ref):
    for r in range(0, h_ref.shape[0], ROW_SUB):
        rows = slice(r, r + ROW_SUB)
        mix = _dot(om_ref[rows, :], w1_ref[...]) + _dot(og_ref[rows, :], w2_ref[...])
        h1 = h_ref[rows, :] + _rms(mix, g1_ref[...])
        h1_ref[rows, :] = h1
        f_ref[rows, :] = _rms(h1, g2_ref[...]).astype(BF16)


def _mix(o_mla, o_gla, w_out, h, g_post_mix, g_pre_ffn):
    T = h.shape[0]
    tm = 512
    row = lambda i: (i, 0)
    const = lambda i: (0, 0)
    return pl.pallas_call(
        _mix_kernel,
        grid=(T // tm,),
        in_specs=[pl.BlockSpec((tm, MLA_WIDTH), row),
                  pl.BlockSpec((tm, GLA_WIDTH), row),
                  pl.BlockSpec((MLA_WIDTH, D_MODEL), lambda i: (0, 0), pipeline_mode=pl.Buffered(1)),
                  pl.BlockSpec((GLA_WIDTH, D_MODEL), lambda i: (1, 0), pipeline_mode=pl.Buffered(1)),
                  pl.BlockSpec((tm, D_MODEL), row),
                  pl.BlockSpec((1, D_MODEL), const),
                  pl.BlockSpec((1, D_MODEL), const)],
        out_specs=[pl.BlockSpec((tm, D_MODEL), row), pl.BlockSpec((tm, D_MODEL), row)],
        out_shape=[jax.ShapeDtypeStruct((T, D_MODEL), F32), jax.ShapeDtypeStruct((T, D_MODEL), BF16)],
        compiler_params=_params("parallel"),
        name="mix",
    )(o_mla, o_gla, w_out, w_out, h, g_post_mix, g_pre_ffn)


def _ffn_kernel(x_ref, wg_ref, wu_ref, wd_ref, o_ref):
    @pl.when(pl.program_id(1) == 0)
    def _():
        o_ref[...] = jnp.zeros(o_ref.shape, F32)

    x = x_ref[...]
    g = _dot(x, wg_ref[...].astype(BF16))
    u = _dot(x, wu_ref[...].astype(BF16))
    o_ref[...] += _dot((g * jax.nn.sigmoid(g) * u).astype(BF16), wd_ref[...].astype(BF16))


def _ffn(x, w_gate, w_up, w_down):
    T = x.shape[0]
    tm, tf = min(T, 1024), 512
    return pl.pallas_call(
        _ffn_kernel,
        grid=(T // tm, D_FF // tf),
        in_specs=[pl.BlockSpec((tm, D_MODEL), lambda i, j: (i, 0)),
                  pl.BlockSpec((D_MODEL, tf), lambda i, j: (0, j)),
                  pl.BlockSpec((D_MODEL, tf), lambda i, j: (0, j)),
                  pl.BlockSpec((tf, D_MODEL), lambda i, j: (j, 0))],
        out_specs=pl.BlockSpec((tm, D_MODEL), lambda i, j: (i, 0)),
        out_shape=jax.ShapeDtypeStruct((T, D_MODEL), F32),
        compiler_params=_params("parallel", "arbitrary"),
        name="ffn",
    )(x, w_gate, w_up, w_down)


def _final_kernel(f_ref, h1_ref, p_ref, g_ref, wpg_ref, wp_ref, o_ref):
    h2 = h1_ref[...] + _rms(f_ref[...], g_ref[...])
    gate = jax.nn.sigmoid(_dot(h2.astype(BF16), wpg_ref[...]))
    o_ref[...] = h2 + gate * _dot(p_ref[...].astype(BF16), wp_ref[...])


def _final(f, h1, p, g_post_ffn, w_ple_gate, w_ple):
    T = f.shape[0]
    tm = 512
    row = lambda i: (i, 0)
    const = lambda i: (0, 0)
    return pl.pallas_call(
        _final_kernel,
        grid=(T // tm,),
        in_specs=[pl.BlockSpec((tm, D_MODEL), row),
                  pl.BlockSpec((tm, D_MODEL), row),
                  pl.BlockSpec((tm, PLE_DIM), row),
                  pl.BlockSpec((1, D_MODEL), const),
                  pl.BlockSpec((D_MODEL, D_MODEL), const, pipeline_mode=pl.Buffered(1)),
                  pl.BlockSpec((PLE_DIM, D_MODEL), const)],
        out_specs=pl.BlockSpec((tm, D_MODEL), row),
        out_shape=jax.ShapeDtypeStruct((T, D_MODEL), F32),
        compiler_params=_params("parallel"),
        name="final",
    )(f, h1, p, g_post_ffn, w_ple_gate, w_ple)


def _rope_tables(pos):
    half = ROPE_DIM // 2
    inv = 1.0 / (ROPE_THETA ** (jnp.arange(half, dtype=F32) / half))
    ang = pos.astype(F32)[:, None] * inv[None, :]
    cos, sin = jnp.cos(ang), jnp.sin(ang)
    zero = jnp.zeros((pos.shape[0], 128 - ROPE_DIM), F32)
    return jnp.concatenate([cos, cos, zero], axis=1), jnp.concatenate([-sin, sin, zero], axis=1)


def _layer_weights(i, g_pre_mix, w_in, g_q, w_uq, w_uk, g_kv, w_ga, b_ga, w_uv, g_gla, w_out, g_post_mix,
                   g_pre_ffn, w_gate, w_up, w_down, g_post_ffn, w_ple, w_ple_gate):
    w_uq_p = jnp.pad(w_uq[i].reshape(Q_LORA, MLA_HEADS, NOPE_DIM + ROPE_DIM),
                     ((0, 0), (0, 0), (0, QK_PAD - NOPE_DIM - ROPE_DIM))).reshape(Q_LORA, MLA_HEADS * QK_PAD)
    w_ga_p = jnp.zeros((128, GLA_HEADS * GLA_DK), F32).at[GLR_LO:GLR_LO + GATE_RANK].set(w_ga[i])
    vec = lambda g: g[i].reshape(1, -1)
    return dict(
        g_pre_mix=vec(g_pre_mix), w_in=_win_prep(jnp.swapaxes(w_in[i], 0, 1)), g_q=vec(g_q), g_kv=vec(g_kv),
        w_uq=w_uq_p.astype(BF16),
        w_uv=w_uv[i].reshape(KV_LORA, MLA_WIDTH).astype(BF16),
        w_ukt=jnp.transpose(w_uk[i], (1, 2, 0)).astype(BF16),
        w_uvh=jnp.transpose(w_uv[i], (1, 0, 2)).astype(BF16),
        w_ga=w_ga_p.astype(BF16), b_ga=vec(b_ga), g_gla=vec(g_gla),
        w_out=w_out[i].astype(BF16), g_post_mix=vec(g_post_mix), g_pre_ffn=vec(g_pre_ffn),
        w_gate=w_gate[i], w_up=w_up[i], w_down=w_down[i],
        g_post_ffn=vec(g_post_ffn), w_ple=w_ple[i].astype(BF16), w_ple_gate=w_ple_gate[i].astype(BF16))


def _finish(w, h, p, o_mla, o_gla):
    h1, f_in = _mix(o_mla, o_gla, w["w_out"], h, w["g_post_mix"], w["g_pre_ffn"])
    f = _ffn(f_in, w["w_gate"], w["w_up"], w["w_down"])
    return _final(f, h1, p, w["g_post_ffn"], w["w_ple_gate"], w["w_ple"])


def kernel(x_prompt, x_sample, cache_ckv, cache_krope, state_gla, p_prompt, p_sample, g_pre_mix, w_in, g_q, w_uq,
           w_uk, g_kv, w_ga, b_ga, w_uv, g_gla, w_out, g_post_mix, g_pre_ffn, w_gate, w_up, w_down, g_post_ffn,
           w_ple, w_ple_gate):
    batch, seq, _ = x_prompt.shape
    dbatch, dseq, _ = x_sample.shape
    depth = w_in.shape[0]
    cos_p, sin_p = _rope_tables(jnp.arange(seq))
    cos_s, sin_s = _rope_tables(PAST_LEN + jnp.arange(dseq))
    cos_s, sin_s = jnp.tile(cos_s, (dbatch, 1)), jnp.tile(sin_s, (dbatch, 1))
    h_p = x_prompt.reshape(batch * seq, D_MODEL)
    h_s = x_sample.reshape(dbatch * dseq, D_MODEL)
    outs = [[] for _ in range(6)]
    for i in range(depth):
        w = _layer_weights(i, g_pre_mix, w_in, g_q, w_uq, w_uk, g_kv, w_ga, b_ga, w_uv, g_gla, w_out, g_post_mix,
                           g_pre_ffn, w_gate, w_up, w_down, g_post_ffn, w_ple, w_ple_gate)
        z, zs = _inproj(h_p, w["g_pre_mix"], *w["w_in"])
        ckvn, kr, qcat, kt, v = _mla_prep_prompt(z, zs, w["g_q"], w["g_kv"], w["w_uq"],
                                                 w["w_ukt"].reshape(MLA_WIDTH, KV_LORA), w["w_uv"], cos_p, sin_p, seq)
        o_mla = _attn_prompt(qcat, kt, v, batch, seq)
        o_gla, s_fin = _gla(z, zs, w["w_ga"], w["b_ga"], w["g_gla"], batch, seq, GLA_CHUNK)
        h_p = _finish(w, h_p, p_prompt[i].reshape(batch * seq, PLE_DIM), o_mla, o_gla)
        outs[0].append(ckvn.reshape(batch, seq, KV_LORA))
        outs[1].append(kr.reshape(batch, seq, ROPE_DIM))
        outs[2].append(s_fin)
        z, zs = _inproj(h_s, w["g_pre_mix"], *w["w_in"])
        ckvn, kr, q_lat, q_rope = _mla_prep_sample(z, zs, w["g_q"], w["g_kv"], w["w_uq"], w["w_ukt"], cos_s, sin_s)
        o_mla = _attn_sample(q_lat, q_rope, cache_ckv[i], jnp.swapaxes(cache_krope[i], 1, 2), ckvn, kr,
                             w["w_uvh"], dseq)
        o_gla, s_new = _gla(z, zs, w["w_ga"], w["b_ga"], w["g_gla"], dbatch, dseq, dseq, s0=state_gla[i])
        h_s = _finish(w, h_s, p_sample[i].reshape(dbatch * dseq, PLE_DIM), o_mla, o_gla)
        outs[3].append(ckvn.reshape(dbatch, dseq, KV_LORA))
        outs[4].append(kr.reshape(dbatch, dseq, ROPE_DIM))
        outs[5].append(s_new)
    return (h_p.reshape(batch, seq, D_MODEL), h_s.reshape(dbatch, dseq, D_MODEL),
            jnp.stack(outs[0]), jnp.stack(outs[1]), jnp.stack(outs[2]),
            jnp.stack(outs[3]), jnp.stack(outs[4]), jnp.stack(outs[5]))
```

```python
import functools

import numpy as np
import jax
import jax.numpy as jnp
from jax import lax
from jax.experimental import pallas as pl
from jax.experimental.pallas import tpu as pltpu

F32 = jnp.float32
BF16 = jnp.bfloat16

D_MODEL = 2048
PAST_LEN = 4096
CHUNK = 64
EPS = 1e-6

MLA_HEADS = 8
Q_LORA = 512
KV_LORA = 512
NOPE_DIM = 128
ROPE_DIM = 64
V_DIM = 128
ROPE_THETA = 10000.0
MLA_SCALE = (NOPE_DIM + ROPE_DIM) ** -0.5
QK_PAD = 256

GLA_HEADS = 4
GLA_DK = 128
GLA_DV = 256
GATE_RANK = 16
GATE_TAU = 16.0
GLA_WIDTH = GLA_HEADS * GLA_DV
MLA_WIDTH = MLA_HEADS * V_DIM

IN_SPLITS = (Q_LORA, KV_LORA, ROPE_DIM, GLA_HEADS * GLA_DK, GLA_HEADS * GLA_DK, GLA_WIDTH, GATE_RANK, GLA_WIDTH)
Z_MAIN = 4096
GLR_LO = ROPE_DIM

D_FF = 5632
PLE_DIM = 256

VMEM_LIMIT = 56 * 1024 * 1024
NEG_BIG = -1e30
LOG2E = 1.4426950408889634
QK_SCALE = MLA_SCALE * LOG2E


def _dot(a, b):
    return jnp.dot(a, b, preferred_element_type=F32)


def _dot_nt(a, b):
    return lax.dot_general(a, b, (((1,), (1,)), ((), ())), preferred_element_type=F32)


def _dot_tn(a, b):
    return lax.dot_general(a, b, (((0,), (0,)), ((), ())), preferred_element_type=F32)


def _rms(x, g):
    ms = jnp.mean(x * x, axis=-1, keepdims=True)
    return x * lax.rsqrt(ms + EPS) * g


def _rope(x, cos_t, sin_t):
    lane = lax.broadcasted_iota(jnp.int32, x.shape, 1)
    rot = jnp.where((lane & 32) == 0, pltpu.roll(x, 96, 1), pltpu.roll(x, 32, 1))
    return x * cos_t + rot * sin_t


def _params(*sem):
    return pltpu.CompilerParams(dimension_semantics=sem, vmem_limit_bytes=VMEM_LIMIT)


_IN_OFF = np.concatenate([[0], np.cumsum(IN_SPLITS)]).tolist()


_WIN_TILE = 512


def _win_prep_kernel(w_ref, kr_ref, glr_ref, o_ref, small_ref):
    o_ref[...] = w_ref[...].T.astype(BF16)

    @pl.when(pl.program_id(0) == 0)
    def _():
        pad = jnp.zeros((128 - GLR_LO - GATE_RANK, D_MODEL), F32)
        small_ref[...] = jnp.concatenate([kr_ref[...], glr_ref[...], pad], axis=0).T.astype(BF16)


def _win_prep(w_t):
    c_q, c_kv, k_r, q_g, k_g, v_g, g_lr, r_g, end = _IN_OFF
    n_head, n_mid = k_r // _WIN_TILE, (g_lr - q_g) // _WIN_TILE
    assert k_r % _WIN_TILE == 0 and (g_lr - q_g) % _WIN_TILE == 0 and (end - r_g) % _WIN_TILE == 0
    assert (n_head + n_mid) * _WIN_TILE + end - r_g == Z_MAIN and GLR_LO == ROPE_DIM

    def src_row(j):
        skip_mid, skip_tail = (q_g - k_r) // 8, (r_g - g_lr + q_g - k_r) // 8
        return 8 * (j * (_WIN_TILE // 8) + jnp.where(j < n_head, 0, jnp.where(j < n_head + n_mid, skip_mid, skip_tail)))

    return pl.pallas_call(
        _win_prep_kernel,
        grid=(Z_MAIN // _WIN_TILE,),
        in_specs=[pl.BlockSpec((pl.Element(_WIN_TILE), pl.Element(D_MODEL)), lambda j: (src_row(j), 0)),
                  pl.BlockSpec((pl.Element(ROPE_DIM), pl.Element(D_MODEL)), lambda j: (k_r, 0)),
                  pl.BlockSpec((pl.Element(GATE_RANK), pl.Element(D_MODEL)), lambda j: (g_lr, 0))],
        out_specs=[pl.BlockSpec((D_MODEL, _WIN_TILE), lambda j: (0, j)),
                   pl.BlockSpec((D_MODEL, 128), lambda j: (0, 0))],
        out_shape=[jax.ShapeDtypeStruct((D_MODEL, Z_MAIN), BF16), jax.ShapeDtypeStruct((D_MODEL, 128), BF16)],
        compiler_params=_params("arbitrary"),
        name="win_prep",
    )(w_t, w_t, w_t)


def _inproj_kernel(x_ref, g_ref, w_ref, ws_ref, o_ref, os_ref):
    xn = _rms(x_ref[...], g_ref[...]).astype(BF16)
    o_ref[...] = _dot(xn, w_ref[...])
    os_ref[...] = _dot(xn, ws_ref[...])


def _inproj(x, g, w_main, w_small):
    T = x.shape[0]
    tm = 512
    return pl.pallas_call(
        _inproj_kernel,
        grid=(T // tm,),
        in_specs=[pl.BlockSpec((tm, D_MODEL), lambda i: (i, 0)),
                  pl.BlockSpec((1, D_MODEL), lambda i: (0, 0)),
                  pl.BlockSpec((D_MODEL, Z_MAIN), lambda i: (0, 0), pipeline_mode=pl.Buffered(1)),
                  pl.BlockSpec((D_MODEL, 128), lambda i: (0, 0), pipeline_mode=pl.Buffered(1))],
        out_specs=[pl.BlockSpec((tm, Z_MAIN), lambda i: (i, 0)), pl.BlockSpec((tm, 128), lambda i: (i, 0))],
        out_shape=[jax.ShapeDtypeStruct((T, Z_MAIN), F32), jax.ShapeDtypeStruct((T, 128), F32)],
        compiler_params=_params("parallel"),
        name="inproj",
    )(x, g, w_main, w_small)


def _mla_q(cq_ref, gq_ref, wuq_ref, cos, sin):
    cqn = _rms(cq_ref[...], gq_ref[...]).astype(BF16)
    q = _dot(cqn, wuq_ref[...]) * QK_SCALE
    out = []
    for h in range(MLA_HEADS):
        nope = q[:, h * QK_PAD:h * QK_PAD + NOPE_DIM]
        rp = _rope(q[:, h * QK_PAD + NOPE_DIM:(h + 1) * QK_PAD], cos, sin)
        out.append((nope, rp))
    return out


def _mla_prep_prompt_kernel(cq_ref, ckv_ref, sm_ref, gq_ref, gkv_ref, wuq_ref, wukt_ref, wuv_ref, cos_ref, sin_ref,
                            ckvn_ref, kr_ref, qcat_ref, kt_ref, v_ref):
    cos, sin = cos_ref[...], sin_ref[...]
    for h, (nope, rp) in enumerate(_mla_q(cq_ref, gq_ref, wuq_ref, cos, sin)):
        qcat_ref[:, h * QK_PAD:h * QK_PAD + NOPE_DIM] = nope.astype(BF16)
        qcat_ref[:, h * QK_PAD + NOPE_DIM:(h + 1) * QK_PAD] = rp.astype(BF16)
    ckvn = _rms(ckv_ref[...], gkv_ref[...])
    ckvn_ref[...] = ckvn
    kr = _rope(sm_ref[...], cos, sin)
    kr_ref[...] = kr[:, :ROPE_DIM]
    ckvn_b = ckvn.astype(BF16)
    k_t = _dot_nt(wukt_ref[...], ckvn_b)
    kr_t = kr.T.astype(BF16)
    for h in range(MLA_HEADS):
        kt_ref[h * QK_PAD:h * QK_PAD + NOPE_DIM, :] = k_t[h * NOPE_DIM:(h + 1) * NOPE_DIM].astype(BF16)
        kt_ref[h * QK_PAD + NOPE_DIM:(h + 1) * QK_PAD, :] = kr_t
    v_ref[...] = _dot(ckvn_b, wuv_ref[...]).astype(BF16)


def _mla_prep_prompt(z, zs, g_q, g_kv, w_uq, w_ukt, w_uv, cos_t, sin_t, seq):
    T = z.shape[0]
    tm = 512
    nseq = seq // tm
    row = lambda i: (i, 0)
    const = lambda i: (0, 0)
    return pl.pallas_call(
        _mla_prep_prompt_kernel,
        grid=(T // tm,),
        in_specs=[pl.BlockSpec((tm, Q_LORA), lambda i: (i, 0)),
                  pl.BlockSpec((tm, KV_LORA), lambda i: (i, 1)),
                  pl.BlockSpec((tm, 128), lambda i: (i, 0)),
                  pl.BlockSpec((1, Q_LORA), const),
                  pl.BlockSpec((1, KV_LORA), const),
                  pl.BlockSpec((Q_LORA, MLA_HEADS * QK_PAD), const),
                  pl.BlockSpec((MLA_WIDTH, KV_LORA), const),
                  pl.BlockSpec((KV_LORA, MLA_WIDTH), const),
                  pl.BlockSpec((tm, 128), lambda i: (i % nseq, 0)),
                  pl.BlockSpec((tm, 128), lambda i: (i % nseq, 0))],
        out_specs=[pl.BlockSpec((tm, KV_LORA), row),
                   pl.BlockSpec((tm, ROPE_DIM), row),
                   pl.BlockSpec((tm, MLA_HEADS * QK_PAD), row),
                   pl.BlockSpec((MLA_HEADS * QK_PAD, tm), lambda i: (0, i)),
                   pl.BlockSpec((tm, MLA_WIDTH), row)],
        out_shape=[jax.ShapeDtypeStruct((T, KV_LORA), F32),
                   jax.ShapeDtypeStruct((T, ROPE_DIM), F32),
                   jax.ShapeDtypeStruct((T, MLA_HEADS * QK_PAD), BF16),
                   jax.ShapeDtypeStruct((MLA_HEADS * QK_PAD, T), BF16),
                   jax.ShapeDtypeStruct((T, MLA_WIDTH), BF16)],
        compiler_params=_params("parallel"),
        name="mla_prep_prompt",
    )(z, z, zs, g_q, g_kv, w_uq, w_ukt, w_uv, cos_t, sin_t)


def _mla_prep_sample_kernel(cq_ref, ckv_ref, sm_ref, gq_ref, gkv_ref, wuq_ref, wukt_ref, cos_ref, sin_ref,
                            ckvn_ref, kr_ref, qlat_ref, qr_ref):
    cos, sin = cos_ref[...], sin_ref[...]
    for h, (nope, rp) in enumerate(_mla_q(cq_ref, gq_ref, wuq_ref, cos, sin)):
        qlat_ref[h] = _dot(nope.astype(BF16), wukt_ref[h]).astype(BF16)
        qr_ref[h] = rp[:, :ROPE_DIM].astype(BF16)
    ckvn_ref[...] = _rms(ckv_ref[...], gkv_ref[...])
    kr_ref[...] = _rope(sm_ref[...], cos, sin)[:, :ROPE_DIM]


def _mla_prep_sample(z, zs, g_q, g_kv, w_uq, w_ukt, cos_t, sin_t):
    T = z.shape[0]
    tm = T
    row = lambda i: (i, 0)
    const = lambda i: (0, 0)
    const3 = lambda i: (0, 0, 0)
    return pl.pallas_call(
        _mla_prep_sample_kernel,
        grid=(T // tm,),
        in_specs=[pl.BlockSpec((tm, Q_LORA), lambda i: (i, 0)),
                  pl.BlockSpec((tm, KV_LORA), lambda i: (i, 1)),
                  pl.BlockSpec((tm, 128), lambda i: (i, 0)),
                  pl.BlockSpec((1, Q_LORA), const),
                  pl.BlockSpec((1, KV_LORA), const),
                  pl.BlockSpec((Q_LORA, MLA_HEADS * QK_PAD), const),
                  pl.BlockSpec((MLA_HEADS, NOPE_DIM, KV_LORA), const3),
                  pl.BlockSpec((tm, 128), row),
                  pl.BlockSpec((tm, 128), row)],
        out_specs=[pl.BlockSpec((tm, KV_LORA), row),
                   pl.BlockSpec((tm, ROPE_DIM), row),
                   pl.BlockSpec((MLA_HEADS, tm, KV_LORA), lambda i: (0, i, 0)),
                   pl.BlockSpec((MLA_HEADS, tm, ROPE_DIM), lambda i: (0, i, 0))],
        out_shape=[jax.ShapeDtypeStruct((T, KV_LORA), F32),
                   jax.ShapeDtypeStruct((T, ROPE_DIM), F32),
                   jax.ShapeDtypeStruct((MLA_HEADS, T, KV_LORA), BF16),
                   jax.ShapeDtypeStruct((MLA_HEADS, T, ROPE_DIM), BF16)],
        compiler_params=_params("parallel"),
        name="mla_prep_sample",
    )(z, z, zs, g_q, g_kv, w_uq, w_ukt, cos_t, sin_t)


ATT_TQ = 256
ATT_HEADS = 2


def _attn_prompt_kernel(q_ref, kt_ref, v_ref, o_ref):
    seq = q_ref.shape[0]
    qc = lax.broadcasted_iota(jnp.int32, (ATT_TQ, ATT_TQ), 0) // CHUNK
    kc = lax.broadcasted_iota(jnp.int32, (ATT_TQ, ATT_TQ), 1) // CHUNK
    visible = kc <= qc
    for i in reversed(range(seq // ATT_TQ)):
        lo, hi = i * ATT_TQ, (i + 1) * ATT_TQ
        for h in range(ATT_HEADS):
            qk = slice(h * QK_PAD, (h + 1) * QK_PAD)
            dv = slice(h * V_DIM, (h + 1) * V_DIM)
            s = _dot(q_ref[lo:hi, qk], kt_ref[qk, 0:hi])
            s_diag = jnp.where(visible, s[:, lo:hi], NEG_BIG)
            s = jnp.concatenate([s[:, :lo], s_diag], axis=1) if i else s_diag
            m = jnp.max(s, axis=-1, keepdims=True)
            p = jnp.exp2(s - m).astype(BF16)
            v_one = jnp.concatenate([v_ref[0:hi, dv], jnp.ones((hi, V_DIM), BF16)], axis=1)
            ol = _dot(p, v_one)
            o_ref[lo:hi, dv] = (ol[:, :V_DIM] / ol[:, V_DIM:]).astype(BF16)


def _attn_prompt(qcat, kt, v, batch, seq):
    return pl.pallas_call(
        _attn_prompt_kernel,
        grid=(batch, MLA_HEADS // ATT_HEADS),
        in_specs=[pl.BlockSpec((seq, ATT_HEADS * QK_PAD), lambda b, h: (b, h)),
                  pl.BlockSpec((ATT_HEADS * QK_PAD, seq), lambda b, h: (h, b)),
                  pl.BlockSpec((seq, ATT_HEADS * V_DIM), lambda b, h: (b, h))],
        out_specs=pl.BlockSpec((seq, ATT_HEADS * V_DIM), lambda b, h: (b, h)),
        out_shape=jax.ShapeDtypeStruct((batch * seq, MLA_WIDTH), BF16),
        compiler_params=_params("parallel", "parallel"),
        name="attn_prompt",
    )(qcat, kt, v)


def _attn_sample_kernel(ql_ref, qr_ref, ckv_ref, krt_ref, nckv_ref, nkr_ref, wuv_ref, o_ref):
    dec = ql_ref.shape[1]
    rows = MLA_HEADS * dec
    ql = ql_ref[...].reshape(rows, KV_LORA)
    qr = qr_ref[...].reshape(rows, ROPE_DIM)
    ckv = ckv_ref[...].astype(BF16)
    nckv = nckv_ref[...].astype(BF16)
    s = _dot_nt(ql, ckv) + _dot(qr, krt_ref[...].astype(BF16))
    s_new = _dot_nt(ql, nckv) + _dot_nt(qr, nkr_ref[...].astype(BF16))
    m = jnp.maximum(jnp.max(s, axis=-1, keepdims=True), jnp.max(s_new, axis=-1, keepdims=True))
    p = jnp.exp2(s - m)
    p_new = jnp.exp2(s_new - m)
    l = jnp.sum(p, axis=-1, keepdims=True) + jnp.sum(p_new, axis=-1, keepdims=True)
    o = ((_dot(p.astype(BF16), ckv) + _dot(p_new.astype(BF16), nckv)) / l).astype(BF16)
    for h in range(MLA_HEADS):
        o_ref[:, h * V_DIM:(h + 1) * V_DIM] = _dot(o[h * dec:(h + 1) * dec], wuv_ref[h]).astype(BF16)


def _attn_sample(q_lat, q_rope, cache_ckv, cache_krt, ckvn, kr, w_uvh, dec):
    batch, past, _ = cache_ckv.shape
    return pl.pallas_call(
        _attn_sample_kernel,
        grid=(batch,),
        in_specs=[pl.BlockSpec((MLA_HEADS, dec, KV_LORA), lambda b: (0, b, 0)),
                  pl.BlockSpec((MLA_HEADS, dec, ROPE_DIM), lambda b: (0, b, 0)),
                  pl.BlockSpec((None, past, KV_LORA), lambda b: (b, 0, 0)),
                  pl.BlockSpec((None, ROPE_DIM, past), lambda b: (b, 0, 0)),
                  pl.BlockSpec((dec, KV_LORA), lambda b: (b, 0)),
                  pl.BlockSpec((dec, ROPE_DIM), lambda b: (b, 0)),
                  pl.BlockSpec((MLA_HEADS, KV_LORA, V_DIM), lambda b: (0, 0, 0))],
        out_specs=pl.BlockSpec((dec, MLA_WIDTH), lambda b: (b, 0)),
        out_shape=jax.ShapeDtypeStruct((batch * dec, MLA_WIDTH), BF16),
        compiler_params=_params("parallel"),
        name="attn_sample",
    )(q_lat, q_rope, cache_ckv, cache_krt, ckvn, kr, w_uvh)


GLA_CHUNK = 256
GLA_DEC_ROWS = 128


def _gla_tables(c, sub):
    nlev = int(np.log2(sub))
    assert 1 << nlev == sub and c % sub == 0 and c // sub <= 128
    t = np.arange(c)[:, None]
    u = np.arange(c)[None, :]
    same_sub = (t // sub) == (u // sub)
    blocks = []
    lvl = np.full((c, c), -1, np.int32)
    for l in range(nlev):
        width = sub >> l
        half = width // 2
        m = (t // width) * width + half - 1
        upper = t > m
        blocks.append(np.where(upper, (u > m) & (u <= t), (u > t) & (u <= m)))
        same = (t // width) == (u // width)
        lvl[same & ((t % width) >= half) & ((u % width) < half)] = l
    blocks.append((u > t) & same_sub)
    blocks.append((u <= t) & same_sub)
    lvl[np.arange(c), np.arange(c)] = nlev
    sel = (np.arange(c)[:, None] // sub) == np.arange(128)[None, :]
    return np.concatenate(blocks, 0).astype(np.float32), lvl, sel.astype(np.float32), nlev


def _gla_kernel(*refs, c, sub, nlev, carry):
    if carry:
        (q_ref, k_ref, v_ref, r_ref, sm_ref, wga_ref, bga_ref, gg_ref, p_ref, lvl_ref, sel_ref,
         o_ref, sfin_ref, s_scr) = refs
        assert sub == c
        j = pl.program_id(1)

        @pl.when(j == 0)
        def _():
            s_scr[...] = jnp.zeros(s_scr.shape, F32)
    else:
        (q_ref, k_ref, v_ref, r_ref, sm_ref, wga_ref, bga_ref, gg_ref, p_ref, lvl_ref, sel_ref, s0_ref,
         o_ref, sfin_ref) = refs

    x = _dot(sm_ref[...].astype(BF16), wga_ref[...]) + bga_ref[...]
    la = (jnp.minimum(x, 0.0) - jnp.log1p(jnp.exp(-jnp.abs(x)))) * (1.0 / GATE_TAU)
    hi = la.astype(BF16)
    mid = (la - hi.astype(F32)).astype(BF16)
    p_mat = p_ref[...]
    e_all = jnp.exp(_dot(p_mat, hi) + _dot(p_mat, mid))
    sel = sel_ref[...]
    d_all = jnp.exp(_dot_tn(hi, sel) + _dot_tn(mid, sel))
    q = q_ref[...] * (GLA_DK ** -0.5)
    k = k_ref[...]
    qe = [(q * e_all[l * c:(l + 1) * c]).astype(BF16) for l in range(nlev)] + [q.astype(BF16)]
    ke = [(k * e_all[l * c:(l + 1) * c]).astype(BF16) for l in range(nlev)] + [k.astype(BF16)]
    k_end = (k * e_all[nlev * c:(nlev + 1) * c]).astype(BF16)
    q_beg = (q * e_all[(nlev + 1) * c:(nlev + 2) * c]).astype(BF16)
    lvl = lvl_ref[...]
    for h in range(GLA_HEADS):
        dk = slice(h * GLA_DK, (h + 1) * GLA_DK)
        dv = slice(h * GLA_DV, (h + 1) * GLA_DV)
        attn = jnp.zeros((c, c), F32)
        for l in range(nlev + 1):
            attn = jnp.where(lvl == l, _dot_nt(qe[l][:, dk], ke[l][:, dk]), attn)
        v = v_ref[:, dv].astype(BF16)
        o_intra = _dot(attn.astype(BF16), v)
        for n in range(c // sub):
            rows = slice(n * sub, (n + 1) * sub)
            state = s_scr[h] if carry else s0_ref[n, h]
            o = o_intra[rows] + _dot(q_beg[rows, dk], state.astype(BF16))
            new_state = d_all[dk, n:n + 1] * state + _dot_tn(k_end[rows, dk], v[rows])
            if carry:
                s_scr[h] = new_state
            else:
                sfin_ref[n, h] = new_state
            r = r_ref[rows, dv]
            o_ref[rows, dv] = (_rms(o, gg_ref[...]) * (r * jax.nn.sigmoid(r))).astype(BF16)

    if carry:
        @pl.when(j == pl.num_programs(1) - 1)
        def _():
            sfin_ref[0] = s_scr[...]


def _gla(z, zs, w_ga, b_ga, g_gla, batch, seq, c, s0=None):
    carry = s0 is None
    sub = c if carry else seq
    p_np, lvl_np, sel_np, nlev = _gla_tables(c, sub)
    nstep = seq // c if carry else 1
    ngroup = batch if carry else batch * seq // c
    nstate = 1 if carry else c // sub
    row = lambda g, j: (g * nstep + j, 0)
    const = lambda g, j: (0, 0)
    in_specs = [pl.BlockSpec((c, 512), lambda g, j: (g * nstep + j, 2)),
                pl.BlockSpec((c, 512), lambda g, j: (g * nstep + j, 3)),
                pl.BlockSpec((c, GLA_WIDTH), lambda g, j: (g * nstep + j, 2)),
                pl.BlockSpec((c, GLA_WIDTH), lambda g, j: (g * nstep + j, 3)),
                pl.BlockSpec((c, 128), row),
                pl.BlockSpec((128, GLA_HEADS * GLA_DK), const),
                pl.BlockSpec((1, GLA_HEADS * GLA_DK), const),
                pl.BlockSpec((1, GLA_DV), const),
                pl.BlockSpec(p_np.shape, const),
                pl.BlockSpec((c, c), const),
                pl.BlockSpec((c, 128), const)]
    args = [z, z, z, z, zs, w_ga, b_ga, g_gla,
            jnp.asarray(p_np, BF16), jnp.asarray(lvl_np), jnp.asarray(sel_np, BF16)]
    state_spec = pl.BlockSpec((nstate, GLA_HEADS, GLA_DK, GLA_DV), lambda g, j: (g, 0, 0, 0))
    if not carry:
        in_specs.append(state_spec)
        args.append(s0)
    return pl.pallas_call(
        functools.partial(_gla_kernel, c=c, sub=sub, nlev=nlev, carry=carry),
        grid=(ngroup, nstep),
        in_specs=in_specs,
        out_specs=[pl.BlockSpec((c, GLA_WIDTH), row), state_spec],
        out_shape=[jax.ShapeDtypeStruct((batch * seq, GLA_WIDTH), BF16),
                   jax.ShapeDtypeStruct((batch, GLA_HEADS, GLA_DK, GLA_DV), F32)],
        scratch_shapes=[pltpu.VMEM((GLA_HEADS, GLA_DK, GLA_DV), F32)] if carry else [],
        compiler_params=_params("parallel", "arbitrary"),
        name="gla" if carry else "gla_init",
    )(*args)


ROW_SUB = 128


def _mix_kernel(om_ref, og_ref, w1_ref, w2_ref, h_ref, g1_ref, g2_ref, h1_ref, f_ref):
    for r in range(0, h_ref.shape[0], ROW_SUB):
        rows = slice(r, r + ROW_SUB)
        mix = _dot(om_ref[rows, :], w1_ref[...]) + _dot(og_ref[rows, :], w2_ref[...])
        h1 = h_ref[rows, :] + _rms(mix, g1_ref[...])
        h1_ref[rows, :] = h1
        f_ref[rows, :] = _rms(h1, g2_ref[...]).astype(BF16)


def _mix(o_mla, o_gla, w_out, h, g_post_mix, g_pre_ffn):
    T = h.shape[0]
    tm = 512
    row = lambda i: (i, 0)
    const = lambda i: (0, 0)
    return pl.pallas_call(
        _mix_kernel,
        grid=(T // tm,),
        in_specs=[pl.BlockSpec((tm, MLA_WIDTH), row),
                  pl.BlockSpec((tm, GLA_WIDTH), row),
                  pl.BlockSpec((MLA_WIDTH, D_MODEL), lambda i: (0, 0), pipeline_mode=pl.Buffered(1)),
                  pl.BlockSpec((GLA_WIDTH, D_MODEL), lambda i: (1, 0), pipeline_mode=pl.Buffered(1)),
                  pl.BlockSpec((tm, D_MODEL), row),
                  pl.BlockSpec((1, D_MODEL), const),
                  pl.BlockSpec((1, D_MODEL), const)],
        out_specs=[pl.BlockSpec((tm, D_MODEL), row), pl.BlockSpec((tm, D_MODEL), row)],
        out_shape=[jax.ShapeDtypeStruct((T, D_MODEL), F32), jax.ShapeDtypeStruct((T, D_MODEL), BF16)],
        compiler_params=_params("parallel"),
        name="mix",
    )(o_mla, o_gla, w_out, w_out, h, g_post_mix, g_pre_ffn)


def _ffn_kernel(x_ref, wg_ref, wu_ref, wd_ref, o_ref):
    @pl.when(pl.program_id(1) == 0)
    def _():
        o_ref[...] = jnp.zeros(o_ref.shape, F32)

    x = x_ref[...]
    g = _dot(x, wg_ref[...].astype(BF16))
    u = _dot(x, wu_ref[...].astype(BF16))
    o_ref[...] += _dot((g * jax.nn.sigmoid(g) * u).astype(BF16), wd_ref[...].astype(BF16))


def _ffn(x, w_gate, w_up, w_down):
    T = x.shape[0]
    tm, tf = min(T, 1024), 512
    return pl.pallas_call(
        _ffn_kernel,
        grid=(T // tm, D_FF // tf),
        in_specs=[pl.BlockSpec((tm, D_MODEL), lambda i, j: (i, 0)),
                  pl.BlockSpec((D_MODEL, tf), lambda i, j: (0, j)),
                  pl.BlockSpec((D_MODEL, tf), lambda i, j: (0, j)),
                  pl.BlockSpec((tf, D_MODEL), lambda i, j: (j, 0))],
        out_specs=pl.BlockSpec((tm, D_MODEL), lambda i, j: (i, 0)),
        out_shape=jax.ShapeDtypeStruct((T, D_MODEL), F32),
        compiler_params=_params("parallel", "arbitrary"),
        name="ffn",
    )(x, w_gate, w_up, w_down)


def _final_kernel(f_ref, h1_ref, p_ref, g_ref, wpg_ref, wp_ref, o_ref):
    h2 = h1_ref[...] + _rms(f_ref[...], g_ref[...])
    gate = jax.nn.sigmoid(_dot(h2.astype(BF16), wpg_ref[...]))
    o_ref[...] = h2 + gate * _dot(p_ref[...].astype(BF16), wp_ref[...])


def _final(f, h1, p, g_post_ffn, w_ple_gate, w_ple):
    T = f.shape[0]
    tm = 512
    row = lambda i: (i, 0)
    const = lambda i: (0, 0)
    return pl.pallas_call(
        _final_kernel,
        grid=(T // tm,),
        in_specs=[pl.BlockSpec((tm, D_MODEL), row),
                  pl.BlockSpec((tm, D_MODEL), row),
                  pl.BlockSpec((tm, PLE_DIM), row),
                  pl.BlockSpec((1, D_MODEL), const),
                  pl.BlockSpec((D_MODEL, D_MODEL), const, pipeline_mode=pl.Buffered(1)),
                  pl.BlockSpec((PLE_DIM, D_MODEL), const)],
        out_specs=pl.BlockSpec((tm, D_MODEL), row),
        out_shape=jax.ShapeDtypeStruct((T, D_MODEL), F32),
        compiler_params=_params("parallel"),
        name="final",
    )(f, h1, p, g_post_ffn, w_ple_gate, w_ple)


def _rope_tables(pos):
    half = ROPE_DIM // 2
    inv = 1.0 / (ROPE_THETA ** (jnp.arange(half, dtype=F32) / half))
    ang = pos.astype(F32)[:, None] * inv[None, :]
    cos, sin = jnp.cos(ang), jnp.sin(ang)
    zero = jnp.zeros((pos.shape[0], 128 - ROPE_DIM), F32)
    return jnp.concatenate([cos, cos, zero], axis=1), jnp.concatenate([-sin, sin, zero], axis=1)


def _layer_weights(i, g_pre_mix, w_in, g_q, w_uq, w_uk, g_kv, w_ga, b_ga, w_uv, g_gla, w_out, g_post_mix,
                   g_pre_ffn, w_gate, w_up, w_down, g_post_ffn, w_ple, w_ple_gate):
    w_uq_p = jnp.pad(w_uq[i].reshape(Q_LORA, MLA_HEADS, NOPE_DIM + ROPE_DIM),
                     ((0, 0), (0, 0), (0, QK_PAD - NOPE_DIM - ROPE_DIM))).reshape(Q_LORA, MLA_HEADS * QK_PAD)
    w_ga_p = jnp.zeros((128, GLA_HEADS * GLA_DK), F32).at[GLR_LO:GLR_LO + GATE_RANK].set(w_ga[i])
    vec = lambda g: g[i].reshape(1, -1)
    return dict(
        g_pre_mix=vec(g_pre_mix), w_in=_win_prep(jnp.swapaxes(w_in[i], 0, 1)), g_q=vec(g_q), g_kv=vec(g_kv),
        w_uq=w_uq_p.astype(BF16),
        w_uv=w_uv[i].reshape(KV_LORA, MLA_WIDTH).astype(BF16),
        w_ukt=jnp.transpose(w_uk[i], (1, 2, 0)).astype(BF16),
        w_uvh=jnp.transpose(w_uv[i], (1, 0, 2)).astype(BF16),
        w_ga=w_ga_p.astype(BF16), b_ga=vec(b_ga), g_gla=vec(g_gla),
        w_out=w_out[i].astype(BF16), g_post_mix=vec(g_post_mix), g_pre_ffn=vec(g_pre_ffn),
        w_gate=w_gate[i], w_up=w_up[i], w_down=w_down[i],
        g_post_ffn=vec(g_post_ffn), w_ple=w_ple[i].astype(BF16), w_ple_gate=w_ple_gate[i].astype(BF16))


def _finish(w, h, p, o_mla, o_gla):
    h1, f_in = _mix(o_mla, o_gla, w["w_out"], h, w["g_post_mix"], w["g_pre_ffn"])
    f = _ffn(f_in, w["w_gate"], w["w_up"], w["w_down"])
    return _final(f, h1, p, w["g_post_ffn"], w["w_ple_gate"], w["w_ple"])


def kernel(x_prompt, x_sample, cache_ckv, cache_krope, state_gla, p_prompt, p_sample, g_pre_mix, w_in, g_q, w_uq,
           w_uk, g_kv, w_ga, b_ga, w_uv, g_gla, w_out, g_post_mix, g_pre_ffn, w_gate, w_up, w_down, g_post_ffn,
           w_ple, w_ple_gate):
    batch, seq, _ = x_prompt.shape
    dbatch, dseq, _ = x_sample.shape
    depth = w_in.shape[0]
    cos_p, sin_p = _rope_tables(jnp.arange(seq))
    cos_s, sin_s = _rope_tables(PAST_LEN + jnp.arange(dseq))
    cos_s, sin_s = jnp.tile(cos_s, (dbatch, 1)), jnp.tile(sin_s, (dbatch, 1))
    h_p = x_prompt.reshape(batch * seq, D_MODEL)
    h_s = x_sample.reshape(dbatch * dseq, D_MODEL)
    outs = [[] for _ in range(6)]
    for i in range(depth):
        w = _layer_weights(i, g_pre_mix, w_in, g_q, w_uq, w_uk, g_kv, w_ga, b_ga, w_uv, g_gla, w_out, g_post_mix,
                           g_pre_ffn, w_gate, w_up, w_down, g_post_ffn, w_ple, w_ple_gate)
        z, zs = _inproj(h_p, w["g_pre_mix"], *w["w_in"])
        ckvn, kr, qcat, kt, v = _mla_prep_prompt(z, zs, w["g_q"], w["g_kv"], w["w_uq"],
                                                 w["w_ukt"].reshape(MLA_WIDTH, KV_LORA), w["w_uv"], cos_p, sin_p, seq)
        o_mla = _attn_prompt(qcat, kt, v, batch, seq)
        o_gla, s_fin = _gla(z, zs, w["w_ga"], w["b_ga"], w["g_gla"], batch, seq, GLA_CHUNK)
        h_p = _finish(w, h_p, p_prompt[i].reshape(batch * seq, PLE_DIM), o_mla, o_gla)
        outs[0].append(ckvn.reshape(batch, seq, KV_LORA))
        outs[1].append(kr.reshape(batch, seq, ROPE_DIM))
        outs[2].append(s_fin)
        z, zs = _inproj(h_s, w["g_pre_mix"], *w["w_in"])
        ckvn, kr, q_lat, q_rope = _mla_prep_sample(z, zs, w["g_q"], w["g_kv"], w["w_uq"], w["w_ukt"], cos_s, sin_s)
        o_mla = _attn_sample(q_lat, q_rope, cache_ckv[i], jnp.swapaxes(cache_krope[i], 1, 2), ckvn, kr,
                             w["w_uvh"], dseq)
        o_gla, s_new = _gla(z, zs, w["w_ga"], w["b_ga"], w["g_gla"], dbatch, dseq, GLA_DEC_ROWS, s0=state_gla[i])
        h_s = _finish(w, h_s, p_sample[i].reshape(dbatch * dseq, PLE_DIM), o_mla, o_gla)
        outs[3].append(ckvn.reshape(dbatch, dseq, KV_LORA))
        outs[4].append(kr.reshape(dbatch, dseq, ROPE_DIM))
        outs[5].append(s_new)
    return (h_p.reshape(batch, seq, D_MODEL), h_s.reshape(dbatch, dseq, D_MODEL),
            jnp.stack(outs[0]), jnp.stack(outs[1]), jnp.stack(outs[2]),
            jnp.stack(outs[3]), jnp.stack(outs[4]), jnp.stack(outs[5]))
```

```python
import functools

import numpy as np
import jax
import jax.numpy as jnp
from jax import lax
from jax.experimental import pallas as pl
from jax.experimental.pallas import tpu as pltpu

F32 = jnp.float32
BF16 = jnp.bfloat16

D_MODEL = 2048
PAST_LEN = 4096
CHUNK = 64
EPS = 1e-6

MLA_HEADS = 8
Q_LORA = 512
KV_LORA = 512
NOPE_DIM = 128
ROPE_DIM = 64
V_DIM = 128
ROPE_THETA = 10000.0
MLA_SCALE = (NOPE_DIM + ROPE_DIM) ** -0.5
QK_PAD = 256

GLA_HEADS = 4
GLA_DK = 128
GLA_DV = 256
GATE_RANK = 16
GATE_TAU = 16.0
GLA_WIDTH = GLA_HEADS * GLA_DV
MLA_WIDTH = MLA_HEADS * V_DIM

IN_SPLITS = (Q_LORA, KV_LORA, ROPE_DIM, GLA_HEADS * GLA_DK, GLA_HEADS * GLA_DK, GLA_WIDTH, GATE_RANK, GLA_WIDTH)
Z_MAIN = 4096
GLR_LO = ROPE_DIM

D_FF = 5632
PLE_DIM = 256

VMEM_LIMIT = 56 * 1024 * 1024
NEG_BIG = -1e30
LOG2E = 1.4426950408889634
QK_SCALE = MLA_SCALE * LOG2E


def _dot(a, b):
    return jnp.dot(a, b, preferred_element_type=F32)


def _dot_nt(a, b):
    return lax.dot_general(a, b, (((1,), (1,)), ((), ())), preferred_element_type=F32)


def _dot_tn(a, b):
    return lax.dot_general(a, b, (((0,), (0,)), ((), ())), preferred_element_type=F32)


def _rms(x, g):
    ms = jnp.mean(x * x, axis=-1, keepdims=True)
    return x * lax.rsqrt(ms + EPS) * g


def _rope(x, cos_t, sin_t):
    lane = lax.broadcasted_iota(jnp.int32, x.shape, 1)
    rot = jnp.where((lane & 32) == 0, pltpu.roll(x, 96, 1), pltpu.roll(x, 32, 1))
    return x * cos_t + rot * sin_t


def _params(*sem):
    return pltpu.CompilerParams(dimension_semantics=sem, vmem_limit_bytes=VMEM_LIMIT)


_IN_OFF = np.concatenate([[0], np.cumsum(IN_SPLITS)]).tolist()


_WIN_TILE = 512


def _win_prep_kernel(w_ref, kr_ref, glr_ref, o_ref, small_ref):
    o_ref[...] = w_ref[...].T.astype(BF16)

    @pl.when(pl.program_id(0) == 0)
    def _():
        pad = jnp.zeros((128 - GLR_LO - GATE_RANK, D_MODEL), F32)
        small_ref[...] = jnp.concatenate([kr_ref[...], glr_ref[...], pad], axis=0).T.astype(BF16)


def _win_prep(w_t):
    c_q, c_kv, k_r, q_g, k_g, v_g, g_lr, r_g, end = _IN_OFF
    n_head, n_mid = k_r // _WIN_TILE, (g_lr - q_g) // _WIN_TILE
    assert k_r % _WIN_TILE == 0 and (g_lr - q_g) % _WIN_TILE == 0 and (end - r_g) % _WIN_TILE == 0
    assert (n_head + n_mid) * _WIN_TILE + end - r_g == Z_MAIN and GLR_LO == ROPE_DIM

    def src_row(j):
        skip_mid, skip_tail = (q_g - k_r) // 8, (r_g - g_lr + q_g - k_r) // 8
        return 8 * (j * (_WIN_TILE // 8) + jnp.where(j < n_head, 0, jnp.where(j < n_head + n_mid, skip_mid, skip_tail)))

    return pl.pallas_call(
        _win_prep_kernel,
        grid=(Z_MAIN // _WIN_TILE,),
        in_specs=[pl.BlockSpec((pl.Element(_WIN_TILE), pl.Element(D_MODEL)), lambda j: (src_row(j), 0)),
                  pl.BlockSpec((pl.Element(ROPE_DIM), pl.Element(D_MODEL)), lambda j: (k_r, 0)),
                  pl.BlockSpec((pl.Element(GATE_RANK), pl.Element(D_MODEL)), lambda j: (g_lr, 0))],
        out_specs=[pl.BlockSpec((D_MODEL, _WIN_TILE), lambda j: (0, j)),
                   pl.BlockSpec((D_MODEL, 128), lambda j: (0, 0))],
        out_shape=[jax.ShapeDtypeStruct((D_MODEL, Z_MAIN), BF16), jax.ShapeDtypeStruct((D_MODEL, 128), BF16)],
        compiler_params=_params("arbitrary"),
        name="win_prep",
    )(w_t, w_t, w_t)


def _inproj_kernel(x_ref, g_ref, w_ref, ws_ref, o_ref, os_ref):
    xn = _rms(x_ref[...], g_ref[...]).astype(BF16)
    o_ref[...] = _dot(xn, w_ref[...])
    os_ref[...] = _dot(xn, ws_ref[...])


def _inproj(x, g, w_main, w_small):
    T = x.shape[0]
    tm = 512
    return pl.pallas_call(
        _inproj_kernel,
        grid=(T // tm,),
        in_specs=[pl.BlockSpec((tm, D_MODEL), lambda i: (i, 0)),
                  pl.BlockSpec((1, D_MODEL), lambda i: (0, 0)),
                  pl.BlockSpec((D_MODEL, Z_MAIN), lambda i: (0, 0), pipeline_mode=pl.Buffered(1)),
                  pl.BlockSpec((D_MODEL, 128), lambda i: (0, 0), pipeline_mode=pl.Buffered(1))],
        out_specs=[pl.BlockSpec((tm, Z_MAIN), lambda i: (i, 0)), pl.BlockSpec((tm, 128), lambda i: (i, 0))],
        out_shape=[jax.ShapeDtypeStruct((T, Z_MAIN), F32), jax.ShapeDtypeStruct((T, 128), F32)],
        compiler_params=_params("parallel"),
        name="inproj",
    )(x, g, w_main, w_small)


def _mla_q(cq_ref, gq_ref, wuq_ref, cos, sin):
    cqn = _rms(cq_ref[...], gq_ref[...]).astype(BF16)
    q = _dot(cqn, wuq_ref[...]) * QK_SCALE
    out = []
    for h in range(MLA_HEADS):
        nope = q[:, h * QK_PAD:h * QK_PAD + NOPE_DIM]
        rp = _rope(q[:, h * QK_PAD + NOPE_DIM:(h + 1) * QK_PAD], cos, sin)
        out.append((nope, rp))
    return out


def _mla_prep_prompt_kernel(cq_ref, ckv_ref, sm_ref, gq_ref, gkv_ref, wuq_ref, wukt_ref, wuv_ref, cos_ref, sin_ref,
                            ckvn_ref, kr_ref, qcat_ref, kt_ref, v_ref):
    cos, sin = cos_ref[...], sin_ref[...]
    for h, (nope, rp) in enumerate(_mla_q(cq_ref, gq_ref, wuq_ref, cos, sin)):
        qcat_ref[:, h * QK_PAD:h * QK_PAD + NOPE_DIM] = nope.astype(BF16)
        qcat_ref[:, h * QK_PAD + NOPE_DIM:(h + 1) * QK_PAD] = rp.astype(BF16)
    ckvn = _rms(ckv_ref[...], gkv_ref[...])
    ckvn_ref[...] = ckvn
    kr = _rope(sm_ref[...], cos, sin)
    kr_ref[...] = kr[:, :ROPE_DIM]
    ckvn_b = ckvn.astype(BF16)
    k_t = _dot_nt(wukt_ref[...], ckvn_b)
    kr_t = kr.T.astype(BF16)
    for h in range(MLA_HEADS):
        kt_ref[h * QK_PAD:h * QK_PAD + NOPE_DIM, :] = k_t[h * NOPE_DIM:(h + 1) * NOPE_DIM].astype(BF16)
        kt_ref[h * QK_PAD + NOPE_DIM:(h + 1) * QK_PAD, :] = kr_t
    v_ref[...] = _dot(ckvn_b, wuv_ref[...]).astype(BF16)


def _mla_prep_prompt(z, zs, g_q, g_kv, w_uq, w_ukt, w_uv, cos_t, sin_t, seq):
    T = z.shape[0]
    tm = 512
    nseq = seq // tm
    row = lambda i: (i, 0)
    const = lambda i: (0, 0)
    return pl.pallas_call(
        _mla_prep_prompt_kernel,
        grid=(T // tm,),
        in_specs=[pl.BlockSpec((tm, Q_LORA), lambda i: (i, 0)),
                  pl.BlockSpec((tm, KV_LORA), lambda i: (i, 1)),
                  pl.BlockSpec((tm, 128), lambda i: (i, 0)),
                  pl.BlockSpec((1, Q_LORA), const),
                  pl.BlockSpec((1, KV_LORA), const),
                  pl.BlockSpec((Q_LORA, MLA_HEADS * QK_PAD), const),
                  pl.BlockSpec((MLA_WIDTH, KV_LORA), const),
                  pl.BlockSpec((KV_LORA, MLA_WIDTH), const),
                  pl.BlockSpec((tm, 128), lambda i: (i % nseq, 0)),
                  pl.BlockSpec((tm, 128), lambda i: (i % nseq, 0))],
        out_specs=[pl.BlockSpec((tm, KV_LORA), row),
                   pl.BlockSpec((tm, ROPE_DIM), row),
                   pl.BlockSpec((tm, MLA_HEADS * QK_PAD), row),
                   pl.BlockSpec((MLA_HEADS * QK_PAD, tm), lambda i: (0, i)),
                   pl.BlockSpec((tm, MLA_WIDTH), row)],
        out_shape=[jax.ShapeDtypeStruct((T, KV_LORA), F32),
                   jax.ShapeDtypeStruct((T, ROPE_DIM), F32),
                   jax.ShapeDtypeStruct((T, MLA_HEADS * QK_PAD), BF16),
                   jax.ShapeDtypeStruct((MLA_HEADS * QK_PAD, T), BF16),
                   jax.ShapeDtypeStruct((T, MLA_WIDTH), BF16)],
        compiler_params=_params("parallel"),
        name="mla_prep_prompt",
    )(z, z, zs, g_q, g_kv, w_uq, w_ukt, w_uv, cos_t, sin_t)


def _mla_prep_sample_kernel(cq_ref, ckv_ref, sm_ref, gq_ref, gkv_ref, wuq_ref, wukt_ref, cos_ref, sin_ref,
                            ckvn_ref, kr_ref, qlat_ref, qr_ref):
    cos, sin = cos_ref[...], sin_ref[...]
    for h, (nope, rp) in enumerate(_mla_q(cq_ref, gq_ref, wuq_ref, cos, sin)):
        qlat_ref[h] = _dot(nope.astype(BF16), wukt_ref[h]).astype(BF16)
        qr_ref[h] = rp[:, :ROPE_DIM].astype(BF16)
    ckvn_ref[...] = _rms(ckv_ref[...], gkv_ref[...])
    kr_ref[...] = _rope(sm_ref[...], cos, sin)[:, :ROPE_DIM]


def _mla_prep_sample(z, zs, g_q, g_kv, w_uq, w_ukt, cos_t, sin_t):
    T = z.shape[0]
    tm = T
    row = lambda i: (i, 0)
    const = lambda i: (0, 0)
    const3 = lambda i: (0, 0, 0)
    return pl.pallas_call(
        _mla_prep_sample_kernel,
        grid=(T // tm,),
        in_specs=[pl.BlockSpec((tm, Q_LORA), lambda i: (i, 0)),
                  pl.BlockSpec((tm, KV_LORA), lambda i: (i, 1)),
                  pl.BlockSpec((tm, 128), lambda i: (i, 0)),
                  pl.BlockSpec((1, Q_LORA), const),
                  pl.BlockSpec((1, KV_LORA), const),
                  pl.BlockSpec((Q_LORA, MLA_HEADS * QK_PAD), const),
                  pl.BlockSpec((MLA_HEADS, NOPE_DIM, KV_LORA), const3),
                  pl.BlockSpec((tm, 128), row),
                  pl.BlockSpec((tm, 128), row)],
        out_specs=[pl.BlockSpec((tm, KV_LORA), row),
                   pl.BlockSpec((tm, ROPE_DIM), row),
                   pl.BlockSpec((MLA_HEADS, tm, KV_LORA), lambda i: (0, i, 0)),
                   pl.BlockSpec((MLA_HEADS, tm, ROPE_DIM), lambda i: (0, i, 0))],
        out_shape=[jax.ShapeDtypeStruct((T, KV_LORA), F32),
                   jax.ShapeDtypeStruct((T, ROPE_DIM), F32),
                   jax.ShapeDtypeStruct((MLA_HEADS, T, KV_LORA), BF16),
                   jax.ShapeDtypeStruct((MLA_HEADS, T, ROPE_DIM), BF16)],
        compiler_params=_params("parallel"),
        name="mla_prep_sample",
    )(z, z, zs, g_q, g_kv, w_uq, w_ukt, cos_t, sin_t)


ATT_TQ = 512
ATT_HEADS = 4


def _attn_prompt_kernel(q_ref, kt_ref, v_ref, o_ref):
    seq = q_ref.shape[0]
    qc = lax.broadcasted_iota(jnp.int32, (ATT_TQ, ATT_TQ), 0) // CHUNK
    kc = lax.broadcasted_iota(jnp.int32, (ATT_TQ, ATT_TQ), 1) // CHUNK
    visible = kc <= qc
    for i in reversed(range(seq // ATT_TQ)):
        lo, hi = i * ATT_TQ, (i + 1) * ATT_TQ
        for h in range(ATT_HEADS):
            qk = slice(h * QK_PAD, (h + 1) * QK_PAD)
            dv = slice(h * V_DIM, (h + 1) * V_DIM)
            s = _dot(q_ref[lo:hi, qk], kt_ref[qk, 0:hi])
            s_diag = jnp.where(visible, s[:, lo:hi], NEG_BIG)
            s = jnp.concatenate([s[:, :lo], s_diag], axis=1) if i else s_diag
            m = jnp.max(s, axis=-1, keepdims=True)
            p = jnp.exp2(s - m).astype(BF16)
            v_one = jnp.concatenate([v_ref[0:hi, dv], jnp.ones((hi, V_DIM), BF16)], axis=1)
            ol = _dot(p, v_one)
            o_ref[lo:hi, dv] = (ol[:, :V_DIM] / ol[:, V_DIM:]).astype(BF16)


def _attn_prompt(qcat, kt, v, batch, seq):
    return pl.pallas_call(
        _attn_prompt_kernel,
        grid=(batch, MLA_HEADS // ATT_HEADS),
        in_specs=[pl.BlockSpec((seq, ATT_HEADS * QK_PAD), lambda b, h: (b, h)),
                  pl.BlockSpec((ATT_HEADS * QK_PAD, seq), lambda b, h: (h, b)),
                  pl.BlockSpec((seq, ATT_HEADS * V_DIM), lambda b, h: (b, h))],
        out_specs=pl.BlockSpec((seq, ATT_HEADS * V_DIM), lambda b, h: (b, h)),
        out_shape=jax.ShapeDtypeStruct((batch * seq, MLA_WIDTH), BF16),
        compiler_params=_params("parallel", "parallel"),
        name="attn_prompt",
    )(qcat, kt, v)


def _attn_sample_kernel(ql_ref, qr_ref, ckv_ref, krt_ref, nckv_ref, nkr_ref, wuv_ref, o_ref):
    dec = ql_ref.shape[1]
    rows = MLA_HEADS * dec
    ql = ql_ref[...].reshape(rows, KV_LORA)
    qr = qr_ref[...].reshape(rows, ROPE_DIM)
    ckv = ckv_ref[...].astype(BF16)
    nckv = nckv_ref[...].astype(BF16)
    s = _dot_nt(ql, ckv) + _dot(qr, krt_ref[...].astype(BF16))
    s_new = _dot_nt(ql, nckv) + _dot_nt(qr, nkr_ref[...].astype(BF16))
    m = jnp.maximum(jnp.max(s, axis=-1, keepdims=True), jnp.max(s_new, axis=-1, keepdims=True))
    p = jnp.exp2(s - m)
    p_new = jnp.exp2(s_new - m)
    l = jnp.sum(p, axis=-1, keepdims=True) + jnp.sum(p_new, axis=-1, keepdims=True)
    o = ((_dot(p.astype(BF16), ckv) + _dot(p_new.astype(BF16), nckv)) / l).astype(BF16)
    for h in range(MLA_HEADS):
        o_ref[:, h * V_DIM:(h + 1) * V_DIM] = _dot(o[h * dec:(h + 1) * dec], wuv_ref[h]).astype(BF16)


def _attn_sample(q_lat, q_rope, cache_ckv, cache_krt, ckvn, kr, w_uvh, dec):
    batch, past, _ = cache_ckv.shape
    return pl.pallas_call(
        _attn_sample_kernel,
        grid=(batch,),
        in_specs=[pl.BlockSpec((MLA_HEADS, dec, KV_LORA), lambda b: (0, b, 0)),
                  pl.BlockSpec((MLA_HEADS, dec, ROPE_DIM), lambda b: (0, b, 0)),
                  pl.BlockSpec((None, past, KV_LORA), lambda b: (b, 0, 0)),
                  pl.BlockSpec((None, ROPE_DIM, past), lambda b: (b, 0, 0)),
                  pl.BlockSpec((dec, KV_LORA), lambda b: (b, 0)),
                  pl.BlockSpec((dec, ROPE_DIM), lambda b: (b, 0)),
                  pl.BlockSpec((MLA_HEADS, KV_LORA, V_DIM), lambda b: (0, 0, 0))],
        out_specs=pl.BlockSpec((dec, MLA_WIDTH), lambda b: (b, 0)),
        out_shape=jax.ShapeDtypeStruct((batch * dec, MLA_WIDTH), BF16),
        compiler_params=_params("parallel"),
        name="attn_sample",
    )(q_lat, q_rope, cache_ckv, cache_krt, ckvn, kr, w_uvh)


GLA_CHUNK = 256
GLA_DEC_ROWS = 128
GLA_SEQS = 2


def _gla_tables(c, sub):
    nlev = int(np.log2(sub))
    assert 1 << nlev == sub and c % sub == 0 and c // sub <= 128
    t = np.arange(c)[:, None]
    u = np.arange(c)[None, :]
    same_sub = (t // sub) == (u // sub)
    blocks = []
    lvl = np.full((c, c), -1, np.int32)
    for l in range(nlev):
        width = sub >> l
        half = width // 2
        m = (t // width) * width + half - 1
        upper = t > m
        blocks.append(np.where(upper, (u > m) & (u <= t), (u > t) & (u <= m)))
        same = (t // width) == (u // width)
        lvl[same & ((t % width) >= half) & ((u % width) < half)] = l
    blocks.append((u > t) & same_sub)
    blocks.append((u <= t) & same_sub)
    lvl[np.arange(c), np.arange(c)] = nlev
    sel = (np.arange(c)[:, None] // sub) == np.arange(128)[None, :]
    return np.concatenate(blocks, 0).astype(np.float32), lvl, sel.astype(np.float32), nlev


def _gla_kernel(*refs, c, sub, nlev, carry):
    if carry:
        (q_ref, k_ref, v_ref, r_ref, sm_ref, wga_ref, bga_ref, gg_ref, p_ref, lvl_ref, sel_ref,
         o_ref, sfin_ref, s_scr) = refs
        assert sub == c
        j = pl.program_id(1)

        @pl.when(j == 0)
        def _():
            s_scr[...] = jnp.zeros(s_scr.shape, F32)
    else:
        (q_ref, k_ref, v_ref, r_ref, sm_ref, wga_ref, bga_ref, gg_ref, p_ref, lvl_ref, sel_ref, s0_ref,
         o_ref, sfin_ref) = refs

    p_mat = p_ref[...]
    sel = sel_ref[...]
    lvl = lvl_ref[...]
    groups = range(q_ref.shape[0])
    split = []
    for g in groups:
        x = _dot(sm_ref[g].astype(BF16), wga_ref[...]) + bga_ref[...]
        la = (jnp.minimum(x, 0.0) - jnp.log1p(jnp.exp(-jnp.abs(x)))) * (1.0 / GATE_TAU)
        hi = la.astype(BF16)
        split.append((hi, (la - hi.astype(F32)).astype(BF16)))
    for g in groups:
        hi, mid = split[g]
        e_all = jnp.exp(_dot(p_mat, hi) + _dot(p_mat, mid))
        d_all = jnp.exp(_dot_tn(hi, sel) + _dot_tn(mid, sel))
        q = q_ref[g] * (GLA_DK ** -0.5)
        k = k_ref[g]
        qe = [(q * e_all[l * c:(l + 1) * c]).astype(BF16) for l in range(nlev)] + [q.astype(BF16)]
        ke = [(k * e_all[l * c:(l + 1) * c]).astype(BF16) for l in range(nlev)] + [k.astype(BF16)]
        k_end = (k * e_all[nlev * c:(nlev + 1) * c]).astype(BF16)
        q_beg = (q * e_all[(nlev + 1) * c:(nlev + 2) * c]).astype(BF16)
        for h in range(GLA_HEADS):
            dk = slice(h * GLA_DK, (h + 1) * GLA_DK)
            dv = slice(h * GLA_DV, (h + 1) * GLA_DV)
            attn = jnp.zeros((c, c), F32)
            for l in range(nlev + 1):
                attn = jnp.where(lvl == l, _dot_nt(qe[l][:, dk], ke[l][:, dk]), attn)
            v = v_ref[g, :, dv].astype(BF16)
            o_intra = _dot(attn.astype(BF16), v)
            for n in range(c // sub):
                rows = slice(n * sub, (n + 1) * sub)
                state = s_scr[g, h] if carry else s0_ref[n, h]
                o = o_intra[rows] + _dot(q_beg[rows, dk], state.astype(BF16))
                new_state = d_all[dk, n:n + 1] * state + _dot_tn(k_end[rows, dk], v[rows])
                if carry:
                    s_scr[g, h] = new_state
                else:
                    sfin_ref[n, h] = new_state
                r = r_ref[g, rows, dv]
                o_ref[g, rows, dv] = (_rms(o, gg_ref[...]) * (r * jax.nn.sigmoid(r))).astype(BF16)

    if carry:
        @pl.when(j == pl.num_programs(1) - 1)
        def _():
            sfin_ref[...] = s_scr[...]


def _gla(z, zs, w_ga, b_ga, g_gla, batch, seq, c, s0=None):
    carry = s0 is None
    sub = c if carry else seq
    p_np, lvl_np, sel_np, nlev = _gla_tables(c, sub)
    if carry:
        nb, lead, grid = GLA_SEQS, batch, (batch // GLA_SEQS, seq // c)
        at = lambda col: (lambda g, j: (g, j, col))
    else:
        nb, lead, grid = 1, 1, (batch * seq // c, 1)
        at = lambda col: (lambda g, j: (0, g, col))
    nstate = nb if carry else c // sub
    z, zs = z.reshape(lead, -1, Z_MAIN), zs.reshape(lead, -1, 128)
    const = lambda g, j: (0, 0)
    in_specs = [pl.BlockSpec((nb, c, 512), at(2)),
                pl.BlockSpec((nb, c, 512), at(3)),
                pl.BlockSpec((nb, c, GLA_WIDTH), at(2)),
                pl.BlockSpec((nb, c, GLA_WIDTH), at(3)),
                pl.BlockSpec((nb, c, 128), at(0)),
                pl.BlockSpec((128, GLA_HEADS * GLA_DK), const),
                pl.BlockSpec((1, GLA_HEADS * GLA_DK), const),
                pl.BlockSpec((1, GLA_DV), const),
                pl.BlockSpec(p_np.shape, const),
                pl.BlockSpec((c, c), const),
                pl.BlockSpec((c, 128), const)]
    args = [z, z, z, z, zs, w_ga, b_ga, g_gla,
            jnp.asarray(p_np, BF16), jnp.asarray(lvl_np), jnp.asarray(sel_np, BF16)]
    state_spec = pl.BlockSpec((nstate, GLA_HEADS, GLA_DK, GLA_DV), lambda g, j: (g, 0, 0, 0))
    if not carry:
        in_specs.append(state_spec)
        args.append(s0)
    o, s_fin = pl.pallas_call(
        functools.partial(_gla_kernel, c=c, sub=sub, nlev=nlev, carry=carry),
        grid=grid,
        in_specs=in_specs,
        out_specs=[pl.BlockSpec((nb, c, GLA_WIDTH), at(0)), state_spec],
        out_shape=[jax.ShapeDtypeStruct((lead, batch * seq // lead, GLA_WIDTH), BF16),
                   jax.ShapeDtypeStruct((batch, GLA_HEADS, GLA_DK, GLA_DV), F32)],
        scratch_shapes=[pltpu.VMEM((nb, GLA_HEADS, GLA_DK, GLA_DV), F32)] if carry else [],
        compiler_params=_params("parallel", "arbitrary"),
        name="gla" if carry else "gla_init",
    )(*args)
    return o.reshape(batch * seq, GLA_WIDTH), s_fin


ROW_SUB = 128


def _mix_kernel(om_ref, og_ref, w1_ref, w2_ref, h_ref, g1_ref, g2_ref, h1_ref, f_ref):
    for r in range(0, h_ref.shape[0], ROW_SUB):
        rows = slice(r, r + ROW_SUB)
        mix = _dot(om_ref[rows, :], w1_ref[...]) + _dot(og_ref[rows, :], w2_ref[...])
        h1 = h_ref[rows, :] + _rms(mix, g1_ref[...])
        h1_ref[rows, :] = h1
        f_ref[rows, :] = _rms(h1, g2_ref[...]).astype(BF16)


def _mix(o_mla, o_gla, w_out, h, g_post_mix, g_pre_ffn):
    T = h.shape[0]
    tm = 512
    row = lambda i: (i, 0)
    const = lambda i: (0, 0)
    return pl.pallas_call(
        _mix_kernel,
        grid=(T // tm,),
        in_specs=[pl.BlockSpec((tm, MLA_WIDTH), row),
                  pl.BlockSpec((tm, GLA_WIDTH), row),
                  pl.BlockSpec((MLA_WIDTH, D_MODEL), lambda i: (0, 0), pipeline_mode=pl.Buffered(1)),
                  pl.BlockSpec((GLA_WIDTH, D_MODEL), lambda i: (1, 0), pipeline_mode=pl.Buffered(1)),
                  pl.BlockSpec((tm, D_MODEL), row),
                  pl.BlockSpec((1, D_MODEL), const),
                  pl.BlockSpec((1, D_MODEL), const)],
        out_specs=[pl.BlockSpec((tm, D_MODEL), row), pl.BlockSpec((tm, D_MODEL), row)],
        out_shape=[jax.ShapeDtypeStruct((T, D_MODEL), F32), jax.ShapeDtypeStruct((T, D_MODEL), BF16)],
        compiler_params=_params("parallel"),
        name="mix",
    )(o_mla, o_gla, w_out, w_out, h, g_post_mix, g_pre_ffn)


def _ffn_kernel(x_ref, wg_ref, wu_ref, wd_ref, o_ref):
    @pl.when(pl.program_id(1) == 0)
    def _():
        o_ref[...] = jnp.zeros(o_ref.shape, F32)

    x = x_ref[...]
    g = _dot(x, wg_ref[...].astype(BF16))
    u = _dot(x, wu_ref[...].astype(BF16))
    o_ref[...] += _dot((g * jax.nn.sigmoid(g) * u).astype(BF16), wd_ref[...].astype(BF16))


def _ffn(x, w_gate, w_up, w_down):
    T = x.shape[0]
    tm, tf = min(T, 1024), 512
    return pl.pallas_call(
        _ffn_kernel,
        grid=(T // tm, D_FF // tf),
        in_specs=[pl.BlockSpec((tm, D_MODEL), lambda i, j: (i, 0)),
                  pl.BlockSpec((D_MODEL, tf), lambda i, j: (0, j)),
                  pl.BlockSpec((D_MODEL, tf), lambda i, j: (0, j)),
                  pl.BlockSpec((tf, D_MODEL), lambda i, j: (j, 0))],
        out_specs=pl.BlockSpec((tm, D_MODEL), lambda i, j: (i, 0)),
        out_shape=jax.ShapeDtypeStruct((T, D_MODEL), F32),
        compiler_params=_params("parallel", "arbitrary"),
        name="ffn",
    )(x, w_gate, w_up, w_down)


def _final_kernel(f_ref, h1_ref, p_ref, g_ref, wpg_ref, wp_ref, o_ref):
    h2 = h1_ref[...] + _rms(f_ref[...], g_ref[...])
    gate = jax.nn.sigmoid(_dot(h2.astype(BF16), wpg_ref[...]))
    o_ref[...] = h2 + gate * _dot(p_ref[...].astype(BF16), wp_ref[...])


def _final(f, h1, p, g_post_ffn, w_ple_gate, w_ple):
    T = f.shape[0]
    tm = 512
    row = lambda i: (i, 0)
    const = lambda i: (0, 0)
    return pl.pallas_call(
        _final_kernel,
        grid=(T // tm,),
        in_specs=[pl.BlockSpec((tm, D_MODEL), row),
                  pl.BlockSpec((tm, D_MODEL), row),
                  pl.BlockSpec((tm, PLE_DIM), row),
                  pl.BlockSpec((1, D_MODEL), const),
                  pl.BlockSpec((D_MODEL, D_MODEL), const, pipeline_mode=pl.Buffered(1)),
                  pl.BlockSpec((PLE_DIM, D_MODEL), const)],
        out_specs=pl.BlockSpec((tm, D_MODEL), row),
        out_shape=jax.ShapeDtypeStruct((T, D_MODEL), F32),
        compiler_params=_params("parallel"),
        name="final",
    )(f, h1, p, g_post_ffn, w_ple_gate, w_ple)


def _rope_tables(pos):
    half = ROPE_DIM // 2
    inv = 1.0 / (ROPE_THETA ** (jnp.arange(half, dtype=F32) / half))
    ang = pos.astype(F32)[:, None] * inv[None, :]
    cos, sin = jnp.cos(ang), jnp.sin(ang)
    zero = jnp.zeros((pos.shape[0], 128 - ROPE_DIM), F32)
    return jnp.concatenate([cos, cos, zero], axis=1), jnp.concatenate([-sin, sin, zero], axis=1)


def _layer_weights(i, g_pre_mix, w_in, g_q, w_uq, w_uk, g_kv, w_ga, b_ga, w_uv, g_gla, w_out, g_post_mix,
                   g_pre_ffn, w_gate, w_up, w_down, g_post_ffn, w_ple, w_ple_gate):
    w_uq_p = jnp.pad(w_uq[i].reshape(Q_LORA, MLA_HEADS, NOPE_DIM + ROPE_DIM),
                     ((0, 0), (0, 0), (0, QK_PAD - NOPE_DIM - ROPE_DIM))).reshape(Q_LORA, MLA_HEADS * QK_PAD)
    w_ga_p = jnp.zeros((128, GLA_HEADS * GLA_DK), F32).at[GLR_LO:GLR_LO + GATE_RANK].set(w_ga[i])
    vec = lambda g: g[i].reshape(1, -1)
    return dict(
        g_pre_mix=vec(g_pre_mix), w_in=_win_prep(jnp.swapaxes(w_in[i], 0, 1)), g_q=vec(g_q), g_kv=vec(g_kv),
        w_uq=w_uq_p.astype(BF16),
        w_uv=w_uv[i].reshape(KV_LORA, MLA_WIDTH).astype(BF16),
        w_ukt=jnp.transpose(w_uk[i], (1, 2, 0)).astype(BF16),
        w_uvh=jnp.transpose(w_uv[i], (1, 0, 2)).astype(BF16),
        w_ga=w_ga_p.astype(BF16), b_ga=vec(b_ga), g_gla=vec(g_gla),
        w_out=w_out[i].astype(BF16), g_post_mix=vec(g_post_mix), g_pre_ffn=vec(g_pre_ffn),
        w_gate=w_gate[i], w_up=w_up[i], w_down=w_down[i],
        g_post_ffn=vec(g_post_ffn), w_ple=w_ple[i].astype(BF16), w_ple_gate=w_ple_gate[i].astype(BF16))


def _finish(w, h, p, o_mla, o_gla):
    h1, f_in = _mix(o_mla, o_gla, w["w_out"], h, w["g_post_mix"], w["g_pre_ffn"])
    f = _ffn(f_in, w["w_gate"], w["w_up"], w["w_down"])
    return _final(f, h1, p, w["g_post_ffn"], w["w_ple_gate"], w["w_ple"])


def kernel(x_prompt, x_sample, cache_ckv, cache_krope, state_gla, p_prompt, p_sample, g_pre_mix, w_in, g_q, w_uq,
           w_uk, g_kv, w_ga, b_ga, w_uv, g_gla, w_out, g_post_mix, g_pre_ffn, w_gate, w_up, w_down, g_post_ffn,
           w_ple, w_ple_gate):
    batch, seq, _ = x_prompt.shape
    dbatch, dseq, _ = x_sample.shape
    depth = w_in.shape[0]
    cos_p, sin_p = _rope_tables(jnp.arange(seq))
    cos_s, sin_s = _rope_tables(PAST_LEN + jnp.arange(dseq))
    cos_s, sin_s = jnp.tile(cos_s, (dbatch, 1)), jnp.tile(sin_s, (dbatch, 1))
    h_p = x_prompt.reshape(batch * seq, D_MODEL)
    h_s = x_sample.reshape(dbatch * dseq, D_MODEL)
    outs = [[] for _ in range(6)]
    for i in range(depth):
        w = _layer_weights(i, g_pre_mix, w_in, g_q, w_uq, w_uk, g_kv, w_ga, b_ga, w_uv, g_gla, w_out, g_post_mix,
                           g_pre_ffn, w_gate, w_up, w_down, g_post_ffn, w_ple, w_ple_gate)
        z, zs = _inproj(h_p, w["g_pre_mix"], *w["w_in"])
        ckvn, kr, qcat, kt, v = _mla_prep_prompt(z, zs, w["g_q"], w["g_kv"], w["w_uq"],
                                                 w["w_ukt"].reshape(MLA_WIDTH, KV_LORA), w["w_uv"], cos_p, sin_p, seq)
        o_mla = _attn_prompt(qcat, kt, v, batch, seq)
        o_gla, s_fin = _gla(z, zs, w["w_ga"], w["b_ga"], w["g_gla"], batch, seq, GLA_CHUNK)
        h_p = _finish(w, h_p, p_prompt[i].reshape(batch * seq, PLE_DIM), o_mla, o_gla)
        outs[0].append(ckvn.reshape(batch, seq, KV_LORA))
        outs[1].append(kr.reshape(batch, seq, ROPE_DIM))
        outs[2].append(s_fin)
        z, zs = _inproj(h_s, w["g_pre_mix"], *w["w_in"])
        ckvn, kr, q_lat, q_rope = _mla_prep_sample(z, zs, w["g_q"], w["g_kv"], w["w_uq"], w["w_ukt"], cos_s, sin_s)
        o_mla = _attn_sample(q_lat, q_rope, cache_ckv[i], jnp.swapaxes(cache_krope[i], 1, 2), ckvn, kr,
                             w["w_uvh"], dseq)
        o_gla, s_new = _gla(z, zs, w["w_ga"], w["b_ga"], w["g_gla"], dbatch, dseq, GLA_DEC_ROWS, s0=state_gla[i])
        h_s = _finish(w, h_s, p_sample[i].reshape(dbatch * dseq, PLE_DIM), o_mla, o_gla)
        outs[3].append(ckvn.reshape(dbatch, dseq, KV_LORA))
        outs[4].append(kr.reshape(dbatch, dseq, ROPE_DIM))
        outs[5].append(s_new)
    return (h_p.reshape(batch, seq, D_MODEL), h_s.reshape(dbatch, dseq, D_MODEL),
            jnp.stack(outs[0]), jnp.stack(outs[1]), jnp.stack(outs[2]),
            jnp.stack(outs[3]), jnp.stack(outs[4]), jnp.stack(outs[5]))
```

```python
import functools

import numpy as np
import jax
import jax.numpy as jnp
from jax import lax
from jax.experimental import pallas as pl
from jax.experimental.pallas import tpu as pltpu

F32 = jnp.float32
BF16 = jnp.bfloat16

D_MODEL = 2048
PAST_LEN = 4096
CHUNK = 64
EPS = 1e-6

MLA_HEADS = 8
Q_LORA = 512
KV_LORA = 512
NOPE_DIM = 128
ROPE_DIM = 64
V_DIM = 128
ROPE_THETA = 10000.0
MLA_SCALE = (NOPE_DIM + ROPE_DIM) ** -0.5
QK_PAD = 256

GLA_HEADS = 4
GLA_DK = 128
GLA_DV = 256
GATE_RANK = 16
GATE_TAU = 16.0
GLA_WIDTH = GLA_HEADS * GLA_DV
MLA_WIDTH = MLA_HEADS * V_DIM

IN_SPLITS = (Q_LORA, KV_LORA, ROPE_DIM, GLA_HEADS * GLA_DK, GLA_HEADS * GLA_DK, GLA_WIDTH, GATE_RANK, GLA_WIDTH)
Z_MAIN = 4096
GLR_LO = ROPE_DIM

D_FF = 5632
PLE_DIM = 256

VMEM_LIMIT = 56 * 1024 * 1024
NEG_BIG = -1e30
LOG2E = 1.4426950408889634
QK_SCALE = MLA_SCALE * LOG2E


def _dot(a, b):
    return jnp.dot(a, b, preferred_element_type=F32)


def _dot_nt(a, b):
    return lax.dot_general(a, b, (((1,), (1,)), ((), ())), preferred_element_type=F32)


def _dot_tn(a, b):
    return lax.dot_general(a, b, (((0,), (0,)), ((), ())), preferred_element_type=F32)


def _rms(x, g):
    ms = jnp.mean(x * x, axis=-1, keepdims=True)
    return x * lax.rsqrt(ms + EPS) * g


def _rope(x, cos_t, sin_t):
    lane = lax.broadcasted_iota(jnp.int32, x.shape, 1)
    rot = jnp.where((lane & 32) == 0, pltpu.roll(x, 96, 1), pltpu.roll(x, 32, 1))
    return x * cos_t + rot * sin_t


def _params(*sem):
    return pltpu.CompilerParams(dimension_semantics=sem, vmem_limit_bytes=VMEM_LIMIT)


_IN_OFF = np.concatenate([[0], np.cumsum(IN_SPLITS)]).tolist()


_WIN_TILE = 512


def _win_prep_kernel(w_ref, kr_ref, glr_ref, o_ref, small_ref):
    o_ref[...] = w_ref[...].T.astype(BF16)

    @pl.when(pl.program_id(0) == 0)
    def _():
        pad = jnp.zeros((128 - GLR_LO - GATE_RANK, D_MODEL), F32)
        small_ref[...] = jnp.concatenate([kr_ref[...], glr_ref[...], pad], axis=0).T.astype(BF16)


def _win_prep(w_t):
    c_q, c_kv, k_r, q_g, k_g, v_g, g_lr, r_g, end = _IN_OFF
    n_head, n_mid = k_r // _WIN_TILE, (g_lr - q_g) // _WIN_TILE
    assert k_r % _WIN_TILE == 0 and (g_lr - q_g) % _WIN_TILE == 0 and (end - r_g) % _WIN_TILE == 0
    assert (n_head + n_mid) * _WIN_TILE + end - r_g == Z_MAIN and GLR_LO == ROPE_DIM

    def src_row(j):
        skip_mid, skip_tail = (q_g - k_r) // 8, (r_g - g_lr + q_g - k_r) // 8
        return 8 * (j * (_WIN_TILE // 8) + jnp.where(j < n_head, 0, jnp.where(j < n_head + n_mid, skip_mid, skip_tail)))

    return pl.pallas_call(
        _win_prep_kernel,
        grid=(Z_MAIN // _WIN_TILE,),
        in_specs=[pl.BlockSpec((pl.Element(_WIN_TILE), pl.Element(D_MODEL)), lambda j: (src_row(j), 0)),
                  pl.BlockSpec((pl.Element(ROPE_DIM), pl.Element(D_MODEL)), lambda j: (k_r, 0)),
                  pl.BlockSpec((pl.Element(GATE_RANK), pl.Element(D_MODEL)), lambda j: (g_lr, 0))],
        out_specs=[pl.BlockSpec((D_MODEL, _WIN_TILE), lambda j: (0, j)),
                   pl.BlockSpec((D_MODEL, 128), lambda j: (0, 0))],
        out_shape=[jax.ShapeDtypeStruct((D_MODEL, Z_MAIN), BF16), jax.ShapeDtypeStruct((D_MODEL, 128), BF16)],
        compiler_params=_params("arbitrary"),
        name="win_prep",
    )(w_t, w_t, w_t)


def _inproj_kernel(x_ref, g_ref, w_ref, ws_ref, o_ref, os_ref):
    xn = _rms(x_ref[...], g_ref[...]).astype(BF16)
    o_ref[...] = _dot(xn, w_ref[...])
    os_ref[...] = _dot(xn, ws_ref[...])


def _inproj(x, g, w_main, w_small):
    T = x.shape[0]
    tm = 512
    return pl.pallas_call(
        _inproj_kernel,
        grid=(T // tm,),
        in_specs=[pl.BlockSpec((tm, D_MODEL), lambda i: (i, 0)),
                  pl.BlockSpec((1, D_MODEL), lambda i: (0, 0)),
                  pl.BlockSpec((D_MODEL, Z_MAIN), lambda i: (0, 0), pipeline_mode=pl.Buffered(1)),
                  pl.BlockSpec((D_MODEL, 128), lambda i: (0, 0), pipeline_mode=pl.Buffered(1))],
        out_specs=[pl.BlockSpec((tm, Z_MAIN), lambda i: (i, 0)), pl.BlockSpec((tm, 128), lambda i: (i, 0))],
        out_shape=[jax.ShapeDtypeStruct((T, Z_MAIN), F32), jax.ShapeDtypeStruct((T, 128), F32)],
        compiler_params=_params("parallel"),
        name="inproj",
    )(x, g, w_main, w_small)


def _mla_q(cq_ref, gq_ref, wuq_ref, cos, sin):
    cqn = _rms(cq_ref[...], gq_ref[...]).astype(BF16)
    q = _dot(cqn, wuq_ref[...]) * QK_SCALE
    out = []
    for h in range(MLA_HEADS):
        nope = q[:, h * QK_PAD:h * QK_PAD + NOPE_DIM]
        rp = _rope(q[:, h * QK_PAD + NOPE_DIM:(h + 1) * QK_PAD], cos, sin)
        out.append((nope, rp))
    return out


def _mla_prep_prompt_kernel(cq_ref, ckv_ref, sm_ref, gq_ref, gkv_ref, wuq_ref, wukt_ref, wuv_ref, cos_ref, sin_ref,
                            ckvn_ref, kr_ref, qcat_ref, kt_ref, v_ref):
    cos, sin = cos_ref[...], sin_ref[...]
    for h, (nope, rp) in enumerate(_mla_q(cq_ref, gq_ref, wuq_ref, cos, sin)):
        qcat_ref[:, h * QK_PAD:h * QK_PAD + NOPE_DIM] = nope.astype(BF16)
        qcat_ref[:, h * QK_PAD + NOPE_DIM:(h + 1) * QK_PAD] = rp.astype(BF16)
    ckvn = _rms(ckv_ref[...], gkv_ref[...])
    ckvn_ref[...] = ckvn
    kr = _rope(sm_ref[...], cos, sin)
    kr_ref[...] = kr[:, :ROPE_DIM]
    ckvn_b = ckvn.astype(BF16)
    k_t = _dot_nt(wukt_ref[...], ckvn_b)
    kr_t = kr.T.astype(BF16)
    for h in range(MLA_HEADS):
        kt_ref[h * QK_PAD:h * QK_PAD + NOPE_DIM, :] = k_t[h * NOPE_DIM:(h + 1) * NOPE_DIM].astype(BF16)
        kt_ref[h * QK_PAD + NOPE_DIM:(h + 1) * QK_PAD, :] = kr_t
    v_ref[...] = _dot(ckvn_b, wuv_ref[...]).astype(BF16)


def _mla_prep_prompt(z, zs, g_q, g_kv, w_uq, w_ukt, w_uv, cos_t, sin_t, seq):
    T = z.shape[0]
    tm = 512
    nseq = seq // tm
    row = lambda i: (i, 0)
    const = lambda i: (0, 0)
    return pl.pallas_call(
        _mla_prep_prompt_kernel,
        grid=(T // tm,),
        in_specs=[pl.BlockSpec((tm, Q_LORA), lambda i: (i, 0)),
                  pl.BlockSpec((tm, KV_LORA), lambda i: (i, 1)),
                  pl.BlockSpec((tm, 128), lambda i: (i, 0)),
                  pl.BlockSpec((1, Q_LORA), const),
                  pl.BlockSpec((1, KV_LORA), const),
                  pl.BlockSpec((Q_LORA, MLA_HEADS * QK_PAD), const),
                  pl.BlockSpec((MLA_WIDTH, KV_LORA), const),
                  pl.BlockSpec((KV_LORA, MLA_WIDTH), const),
                  pl.BlockSpec((tm, 128), lambda i: (i % nseq, 0)),
                  pl.BlockSpec((tm, 128), lambda i: (i % nseq, 0))],
        out_specs=[pl.BlockSpec((tm, KV_LORA), row),
                   pl.BlockSpec((tm, ROPE_DIM), row),
                   pl.BlockSpec((tm, MLA_HEADS * QK_PAD), row),
                   pl.BlockSpec((MLA_HEADS * QK_PAD, tm), lambda i: (0, i)),
                   pl.BlockSpec((tm, MLA_WIDTH), row)],
        out_shape=[jax.ShapeDtypeStruct((T, KV_LORA), F32),
                   jax.ShapeDtypeStruct((T, ROPE_DIM), F32),
                   jax.ShapeDtypeStruct((T, MLA_HEADS * QK_PAD), BF16),
                   jax.ShapeDtypeStruct((MLA_HEADS * QK_PAD, T), BF16),
                   jax.ShapeDtypeStruct((T, MLA_WIDTH), BF16)],
        compiler_params=_params("parallel"),
        name="mla_prep_prompt",
    )(z, z, zs, g_q, g_kv, w_uq, w_ukt, w_uv, cos_t, sin_t)


def _mla_prep_sample_kernel(cq_ref, ckv_ref, sm_ref, gq_ref, gkv_ref, wuq_ref, wukt_ref, cos_ref, sin_ref,
                            ckvn_ref, kr_ref, qlat_ref, qr_ref):
    cos, sin = cos_ref[...], sin_ref[...]
    for h, (nope, rp) in enumerate(_mla_q(cq_ref, gq_ref, wuq_ref, cos, sin)):
        qlat_ref[h] = _dot(nope.astype(BF16), wukt_ref[h]).astype(BF16)
        qr_ref[h] = rp[:, :ROPE_DIM].astype(BF16)
    ckvn_ref[...] = _rms(ckv_ref[...], gkv_ref[...])
    kr_ref[...] = _rope(sm_ref[...], cos, sin)[:, :ROPE_DIM]


def _mla_prep_sample(z, zs, g_q, g_kv, w_uq, w_ukt, cos_t, sin_t):
    T = z.shape[0]
    tm = T
    row = lambda i: (i, 0)
    const = lambda i: (0, 0)
    const3 = lambda i: (0, 0, 0)
    return pl.pallas_call(
        _mla_prep_sample_kernel,
        grid=(T // tm,),
        in_specs=[pl.BlockSpec((tm, Q_LORA), lambda i: (i, 0)),
                  pl.BlockSpec((tm, KV_LORA), lambda i: (i, 1)),
                  pl.BlockSpec((tm, 128), lambda i: (i, 0)),
                  pl.BlockSpec((1, Q_LORA), const),
                  pl.BlockSpec((1, KV_LORA), const),
                  pl.BlockSpec((Q_LORA, MLA_HEADS * QK_PAD), const),
                  pl.BlockSpec((MLA_HEADS, NOPE_DIM, KV_LORA), const3),
                  pl.BlockSpec((tm, 128), row),
                  pl.BlockSpec((tm, 128), row)],
        out_specs=[pl.BlockSpec((tm, KV_LORA), row),
                   pl.BlockSpec((tm, ROPE_DIM), row),
                   pl.BlockSpec((MLA_HEADS, tm, KV_LORA), lambda i: (0, i, 0)),
                   pl.BlockSpec((MLA_HEADS, tm, ROPE_DIM), lambda i: (0, i, 0))],
        out_shape=[jax.ShapeDtypeStruct((T, KV_LORA), F32),
                   jax.ShapeDtypeStruct((T, ROPE_DIM), F32),
                   jax.ShapeDtypeStruct((MLA_HEADS, T, KV_LORA), BF16),
                   jax.ShapeDtypeStruct((MLA_HEADS, T, ROPE_DIM), BF16)],
        compiler_params=_params("parallel"),
        name="mla_prep_sample",
    )(z, z, zs, g_q, g_kv, w_uq, w_ukt, cos_t, sin_t)


ATT_TQ = 512
ATT_HEADS = 4


def _attn_prompt_kernel(q_ref, kt_ref, v_ref, o_ref):
    seq = q_ref.shape[0]
    qc = lax.broadcasted_iota(jnp.int32, (ATT_TQ, ATT_TQ), 0) // CHUNK
    kc = lax.broadcasted_iota(jnp.int32, (ATT_TQ, ATT_TQ), 1) // CHUNK
    visible = kc <= qc
    for i in reversed(range(seq // ATT_TQ)):
        lo, hi = i * ATT_TQ, (i + 1) * ATT_TQ
        for h in range(ATT_HEADS):
            qk = slice(h * QK_PAD, (h + 1) * QK_PAD)
            dv = slice(h * V_DIM, (h + 1) * V_DIM)
            s = _dot(q_ref[lo:hi, qk], kt_ref[qk, 0:hi])
            s_diag = jnp.where(visible, s[:, lo:hi], NEG_BIG)
            s = jnp.concatenate([s[:, :lo], s_diag], axis=1) if i else s_diag
            m = jnp.max(s, axis=-1, keepdims=True)
            p = jnp.exp2(s - m).astype(BF16)
            v_one = jnp.concatenate([v_ref[0:hi, dv], jnp.ones((hi, V_DIM), BF16)], axis=1)
            ol = _dot(p, v_one)
            o_ref[lo:hi, dv] = (ol[:, :V_DIM] / ol[:, V_DIM:]).astype(BF16)


def _attn_prompt(qcat, kt, v, batch, seq):
    return pl.pallas_call(
        _attn_prompt_kernel,
        grid=(batch, MLA_HEADS // ATT_HEADS),
        in_specs=[pl.BlockSpec((seq, ATT_HEADS * QK_PAD), lambda b, h: (b, h)),
                  pl.BlockSpec((ATT_HEADS * QK_PAD, seq), lambda b, h: (h, b)),
                  pl.BlockSpec((seq, ATT_HEADS * V_DIM), lambda b, h: (b, h))],
        out_specs=pl.BlockSpec((seq, ATT_HEADS * V_DIM), lambda b, h: (b, h)),
        out_shape=jax.ShapeDtypeStruct((batch * seq, MLA_WIDTH), BF16),
        compiler_params=_params("parallel", "parallel"),
        name="attn_prompt",
    )(qcat, kt, v)


def _attn_sample_kernel(ql_ref, qr_ref, ckv_ref, krt_ref, nckv_ref, nkr_ref, wuv_ref, o_ref):
    dec = ql_ref.shape[1]
    rows = MLA_HEADS * dec
    ql = ql_ref[...].reshape(rows, KV_LORA)
    qr = qr_ref[...].reshape(rows, ROPE_DIM)
    ckv = ckv_ref[...].astype(BF16)
    nckv = nckv_ref[...].astype(BF16)
    s = _dot_nt(ql, ckv) + _dot(qr, krt_ref[...].astype(BF16))
    s_new = _dot_nt(ql, nckv) + _dot_nt(qr, nkr_ref[...].astype(BF16))
    m = jnp.maximum(jnp.max(s, axis=-1, keepdims=True), jnp.max(s_new, axis=-1, keepdims=True))
    p = jnp.exp2(s - m)
    p_new = jnp.exp2(s_new - m)
    l = jnp.sum(p, axis=-1, keepdims=True) + jnp.sum(p_new, axis=-1, keepdims=True)
    o = ((_dot(p.astype(BF16), ckv) + _dot(p_new.astype(BF16), nckv)) / l).astype(BF16)
    for h in range(MLA_HEADS):
        o_ref[:, h * V_DIM:(h + 1) * V_DIM] = _dot(o[h * dec:(h + 1) * dec], wuv_ref[h]).astype(BF16)


def _attn_sample(q_lat, q_rope, cache_ckv, cache_krt, ckvn, kr, w_uvh, dec):
    batch, past, _ = cache_ckv.shape
    return pl.pallas_call(
        _attn_sample_kernel,
        grid=(batch,),
        in_specs=[pl.BlockSpec((MLA_HEADS, dec, KV_LORA), lambda b: (0, b, 0)),
                  pl.BlockSpec((MLA_HEADS, dec, ROPE_DIM), lambda b: (0, b, 0)),
                  pl.BlockSpec((None, past, KV_LORA), lambda b: (b, 0, 0)),
                  pl.BlockSpec((None, ROPE_DIM, past), lambda b: (b, 0, 0)),
                  pl.BlockSpec((dec, KV_LORA), lambda b: (b, 0)),
                  pl.BlockSpec((dec, ROPE_DIM), lambda b: (b, 0)),
                  pl.BlockSpec((MLA_HEADS, KV_LORA, V_DIM), lambda b: (0, 0, 0))],
        out_specs=pl.BlockSpec((dec, MLA_WIDTH), lambda b: (b, 0)),
        out_shape=jax.ShapeDtypeStruct((batch * dec, MLA_WIDTH), BF16),
        compiler_params=_params("parallel"),
        name="attn_sample",
    )(q_lat, q_rope, cache_ckv, cache_krt, ckvn, kr, w_uvh)


GLA_CHUNK = 256
GLA_DEC_ROWS = 128
GLA_SEQS = 2


def _gla_tables(c, sub):
    nlev = int(np.log2(sub))
    assert 1 << nlev == sub and c % sub == 0 and c // sub <= 128
    t = np.arange(c)[:, None]
    u = np.arange(c)[None, :]
    same_sub = (t // sub) == (u // sub)
    blocks = []
    lvl = np.full((c, c), -1, np.int32)
    for l in range(nlev):
        width = sub >> l
        half = width // 2
        m = (t // width) * width + half - 1
        upper = t > m
        blocks.append(np.where(upper, (u > m) & (u <= t), (u > t) & (u <= m)))
        same = (t // width) == (u // width)
        lvl[same & ((t % width) >= half) & ((u % width) < half)] = l
    blocks.append((u > t) & same_sub)
    blocks.append((u <= t) & same_sub)
    lvl[np.arange(c), np.arange(c)] = nlev
    sel = (np.arange(c)[:, None] // sub) == np.arange(128)[None, :]
    return np.concatenate(blocks, 0).astype(np.float32), lvl, sel.astype(np.float32), nlev


def _gla_kernel(*refs, c, sub, nlev, carry):
    if carry:
        (q_ref, k_ref, v_ref, r_ref, sm_ref, wga_ref, bga_ref, gg_ref, p_ref, lvl_ref, sel_ref,
         o_ref, sfin_ref, s_scr) = refs
        assert sub == c
        j = pl.program_id(1)

        @pl.when(j == 0)
        def _():
            s_scr[...] = jnp.zeros(s_scr.shape, F32)
    else:
        (q_ref, k_ref, v_ref, r_ref, sm_ref, wga_ref, bga_ref, gg_ref, p_ref, lvl_ref, sel_ref, s0_ref,
         o_ref, sfin_ref) = refs

    p_mat = p_ref[...]
    sel = sel_ref[...]
    lvl = lvl_ref[...]
    groups = range(q_ref.shape[0])
    split = []
    for g in groups:
        x = _dot(sm_ref[g].astype(BF16), wga_ref[...]) + bga_ref[...]
        la = (jnp.minimum(x, 0.0) - jnp.log1p(jnp.exp(-jnp.abs(x)))) * (1.0 / GATE_TAU)
        hi = la.astype(BF16)
        split.append((hi, (la - hi.astype(F32)).astype(BF16)))
    for g in groups:
        hi, mid = split[g]
        e_all = jnp.exp(_dot(p_mat, hi) + _dot(p_mat, mid))
        d_all = jnp.exp(_dot_tn(hi, sel) + _dot_tn(mid, sel))
        q = q_ref[g] * (GLA_DK ** -0.5)
        k = k_ref[g]
        qe = [(q * e_all[l * c:(l + 1) * c]).astype(BF16) for l in range(nlev)] + [q.astype(BF16)]
        ke = [(k * e_all[l * c:(l + 1) * c]).astype(BF16) for l in range(nlev)] + [k.astype(BF16)]
        k_end = (k * e_all[nlev * c:(nlev + 1) * c]).astype(BF16)
        q_beg = (q * e_all[(nlev + 1) * c:(nlev + 2) * c]).astype(BF16)
        for h in range(GLA_HEADS):
            dk = slice(h * GLA_DK, (h + 1) * GLA_DK)
            dv = slice(h * GLA_DV, (h + 1) * GLA_DV)
            attn = jnp.zeros((c, c), F32)
            for l in range(nlev + 1):
                attn = jnp.where(lvl == l, _dot_nt(qe[l][:, dk], ke[l][:, dk]), attn)
            v = v_ref[g, :, dv].astype(BF16)
            o_intra = _dot(attn.astype(BF16), v)
            for n in range(c // sub):
                rows = slice(n * sub, (n + 1) * sub)
                state = s_scr[g, h] if carry else s0_ref[n, h]
                o = o_intra[rows] + _dot(q_beg[rows, dk], state.astype(BF16))
                new_state = d_all[dk, n:n + 1] * state + _dot_tn(k_end[rows, dk], v[rows])
                if carry:
                    s_scr[g, h] = new_state
                else:
                    sfin_ref[n, h] = new_state
                r = r_ref[g, rows, dv]
                o_ref[g, rows, dv] = (_rms(o, gg_ref[...]) * (r * jax.nn.sigmoid(r))).astype(BF16)

    if carry:
        @pl.when(j == pl.num_programs(1) - 1)
        def _():
            sfin_ref[...] = s_scr[...]


def _gla(z, zs, w_ga, b_ga, g_gla, batch, seq, c, s0=None):
    carry = s0 is None
    sub = c if carry else seq
    p_np, lvl_np, sel_np, nlev = _gla_tables(c, sub)
    if carry:
        nb, lead, grid = GLA_SEQS, batch, (batch // GLA_SEQS, seq // c)
        at = lambda col: (lambda g, j: (g, j, col))
    else:
        nb, lead, grid = 1, 1, (batch * seq // c, 1)
        at = lambda col: (lambda g, j: (0, g, col))
    nstate = nb if carry else c // sub
    z, zs = z.reshape(lead, -1, Z_MAIN), zs.reshape(lead, -1, 128)
    const = lambda g, j: (0, 0)
    in_specs = [pl.BlockSpec((nb, c, 512), at(2)),
                pl.BlockSpec((nb, c, 512), at(3)),
                pl.BlockSpec((nb, c, GLA_WIDTH), at(2)),
                pl.BlockSpec((nb, c, GLA_WIDTH), at(3)),
                pl.BlockSpec((nb, c, 128), at(0)),
                pl.BlockSpec((128, GLA_HEADS * GLA_DK), const),
                pl.BlockSpec((1, GLA_HEADS * GLA_DK), const),
                pl.BlockSpec((1, GLA_DV), const),
                pl.BlockSpec(p_np.shape, const),
                pl.BlockSpec((c, c), const),
                pl.BlockSpec((c, 128), const)]
    args = [z, z, z, z, zs, w_ga, b_ga, g_gla,
            jnp.asarray(p_np, BF16), jnp.asarray(lvl_np), jnp.asarray(sel_np, BF16)]
    state_spec = pl.BlockSpec((nstate, GLA_HEADS, GLA_DK, GLA_DV), lambda g, j: (g, 0, 0, 0))
    if not carry:
        in_specs.append(state_spec)
        args.append(s0)
    o, s_fin = pl.pallas_call(
        functools.partial(_gla_kernel, c=c, sub=sub, nlev=nlev, carry=carry),
        grid=grid,
        in_specs=in_specs,
        out_specs=[pl.BlockSpec((nb, c, GLA_WIDTH), at(0)), state_spec],
        out_shape=[jax.ShapeDtypeStruct((lead, batch * seq // lead, GLA_WIDTH), BF16),
                   jax.ShapeDtypeStruct((batch, GLA_HEADS, GLA_DK, GLA_DV), F32)],
        scratch_shapes=[pltpu.VMEM((nb, GLA_HEADS, GLA_DK, GLA_DV), F32)] if carry else [],
        compiler_params=_params("parallel", "arbitrary"),
        name="gla" if carry else "gla_init",
    )(*args)
    return o.reshape(batch * seq, GLA_WIDTH), s_fin


ROW_SUB = 128


def _mix_kernel(oma_ref, omb_ref, oga_ref, ogb_ref, w1_ref, w2_ref, ha_ref, hb_ref, g1_ref, g2_ref, h1_ref, f_ref,
                *, na):
    def block(om_ref, og_ref, h_ref):
        for r in range(0, h_ref.shape[0], ROW_SUB):
            rows = slice(r, r + ROW_SUB)
            mix = _dot(om_ref[rows, :], w1_ref[...]) + _dot(og_ref[rows, :], w2_ref[...])
            h1 = h_ref[rows, :] + _rms(mix, g1_ref[...])
            h1_ref[rows, :] = h1
            f_ref[rows, :] = _rms(h1, g2_ref[...]).astype(BF16)

    pl.when(pl.program_id(0) < na)(lambda: block(oma_ref, oga_ref, ha_ref))
    pl.when(pl.program_id(0) >= na)(lambda: block(omb_ref, ogb_ref, hb_ref))


def _mix(o_mla, o_gla, h, w_out, g_post_mix, g_pre_ffn):
    tm = 512
    na, nb = h[0].shape[0] // tm, h[1].shape[0] // tm
    T = (na + nb) * tm
    row = lambda i: (i, 0)
    const = lambda i: (0, 0)
    at_a = lambda i: (jnp.minimum(i, na - 1), 0)
    at_b = lambda i: (jnp.maximum(i - na, 0), 0)
    return pl.pallas_call(
        functools.partial(_mix_kernel, na=na),
        grid=(na + nb,),
        in_specs=[pl.BlockSpec((tm, MLA_WIDTH), at_a), pl.BlockSpec((tm, MLA_WIDTH), at_b),
                  pl.BlockSpec((tm, GLA_WIDTH), at_a), pl.BlockSpec((tm, GLA_WIDTH), at_b),
                  pl.BlockSpec((MLA_WIDTH, D_MODEL), lambda i: (0, 0), pipeline_mode=pl.Buffered(1)),
                  pl.BlockSpec((GLA_WIDTH, D_MODEL), lambda i: (1, 0), pipeline_mode=pl.Buffered(1)),
                  pl.BlockSpec((tm, D_MODEL), at_a), pl.BlockSpec((tm, D_MODEL), at_b),
                  pl.BlockSpec((1, D_MODEL), const),
                  pl.BlockSpec((1, D_MODEL), const)],
        out_specs=[pl.BlockSpec((tm, D_MODEL), row), pl.BlockSpec((tm, D_MODEL), row)],
        out_shape=[jax.ShapeDtypeStruct((T, D_MODEL), F32), jax.ShapeDtypeStruct((T, D_MODEL), BF16)],
        compiler_params=_params("arbitrary"),
        name="mix",
    )(o_mla[0], o_mla[1], o_gla[0], o_gla[1], w_out, w_out, h[0], h[1], g_post_mix, g_pre_ffn)


FFN_MAX_ROWS = 1088


def _ffn_kernel(x_ref, wg_ref, wu_ref, wd_ref, o_ref):
    @pl.when(pl.program_id(1) == 0)
    def _():
        o_ref[...] = jnp.zeros(o_ref.shape, F32)

    x = x_ref[...]
    g = _dot(x, wg_ref[...].astype(BF16))
    u = _dot(x, wu_ref[...].astype(BF16))
    o_ref[...] += _dot((g * jax.nn.sigmoid(g) * u).astype(BF16), wd_ref[...].astype(BF16))


def _ffn(x, w_gate, w_up, w_down):
    T = x.shape[0]
    tf = 512
    tm = next(T // n for n in range(1, T + 1) if T % n == 0 and T // n <= FFN_MAX_ROWS and (T // n) % 16 == 0)
    return pl.pallas_call(
        _ffn_kernel,
        grid=(T // tm, D_FF // tf),
        in_specs=[pl.BlockSpec((tm, D_MODEL), lambda i, j: (i, 0)),
                  pl.BlockSpec((D_MODEL, tf), lambda i, j: (0, j)),
                  pl.BlockSpec((D_MODEL, tf), lambda i, j: (0, j)),
                  pl.BlockSpec((tf, D_MODEL), lambda i, j: (j, 0))],
        out_specs=pl.BlockSpec((tm, D_MODEL), lambda i, j: (i, 0)),
        out_shape=jax.ShapeDtypeStruct((T, D_MODEL), F32),
        compiler_params=_params("parallel", "arbitrary"),
        name="ffn",
    )(x, w_gate, w_up, w_down)


def _final_kernel(f_ref, h1_ref, p_ref, g_ref, wpg_ref, wp_ref, o_ref):
    h2 = h1_ref[...] + _rms(f_ref[...], g_ref[...])
    gate = jax.nn.sigmoid(_dot(h2.astype(BF16), wpg_ref[...]))
    o_ref[...] = h2 + gate * _dot(p_ref[...].astype(BF16), wp_ref[...])


def _final(f, h1, p, row0, g_post_ffn, w_ple_gate, w_ple):
    T = p.shape[0]
    tm = 512
    first = row0 // tm
    row = lambda i: (i, 0)
    row_off = lambda i: (i + first, 0)
    const = lambda i: (0, 0)
    return pl.pallas_call(
        _final_kernel,
        grid=(T // tm,),
        in_specs=[pl.BlockSpec((tm, D_MODEL), row_off),
                  pl.BlockSpec((tm, D_MODEL), row_off),
                  pl.BlockSpec((tm, PLE_DIM), row),
                  pl.BlockSpec((1, D_MODEL), const),
                  pl.BlockSpec((D_MODEL, D_MODEL), const, pipeline_mode=pl.Buffered(1)),
                  pl.BlockSpec((PLE_DIM, D_MODEL), const)],
        out_specs=pl.BlockSpec((tm, D_MODEL), row),
        out_shape=jax.ShapeDtypeStruct((T, D_MODEL), F32),
        compiler_params=_params("parallel"),
        name="final",
    )(f, h1, p, g_post_ffn, w_ple_gate, w_ple)


def _rope_tables(pos):
    half = ROPE_DIM // 2
    inv = 1.0 / (ROPE_THETA ** (jnp.arange(half, dtype=F32) / half))
    ang = pos.astype(F32)[:, None] * inv[None, :]
    cos, sin = jnp.cos(ang), jnp.sin(ang)
    zero = jnp.zeros((pos.shape[0], 128 - ROPE_DIM), F32)
    return jnp.concatenate([cos, cos, zero], axis=1), jnp.concatenate([-sin, sin, zero], axis=1)


def _layer_weights(i, g_pre_mix, w_in, g_q, w_uq, w_uk, g_kv, w_ga, b_ga, w_uv, g_gla, w_out, g_post_mix,
                   g_pre_ffn, w_gate, w_up, w_down, g_post_ffn, w_ple, w_ple_gate):
    w_uq_p = jnp.pad(w_uq[i].reshape(Q_LORA, MLA_HEADS, NOPE_DIM + ROPE_DIM),
                     ((0, 0), (0, 0), (0, QK_PAD - NOPE_DIM - ROPE_DIM))).reshape(Q_LORA, MLA_HEADS * QK_PAD)
    w_ga_p = jnp.zeros((128, GLA_HEADS * GLA_DK), F32).at[GLR_LO:GLR_LO + GATE_RANK].set(w_ga[i])
    vec = lambda g: g[i].reshape(1, -1)
    return dict(
        g_pre_mix=vec(g_pre_mix), w_in=_win_prep(jnp.swapaxes(w_in[i], 0, 1)), g_q=vec(g_q), g_kv=vec(g_kv),
        w_uq=w_uq_p.astype(BF16),
        w_uv=w_uv[i].reshape(KV_LORA, MLA_WIDTH).astype(BF16),
        w_ukt=jnp.transpose(w_uk[i], (1, 2, 0)).astype(BF16),
        w_uvh=jnp.transpose(w_uv[i], (1, 0, 2)).astype(BF16),
        w_ga=w_ga_p.astype(BF16), b_ga=vec(b_ga), g_gla=vec(g_gla),
        w_out=w_out[i].astype(BF16), g_post_mix=vec(g_post_mix), g_pre_ffn=vec(g_pre_ffn),
        w_gate=w_gate[i], w_up=w_up[i], w_down=w_down[i],
        g_post_ffn=vec(g_post_ffn), w_ple=w_ple[i].astype(BF16), w_ple_gate=w_ple_gate[i].astype(BF16))


def _finish(w, h, p, o_mla, o_gla):
    h1, f_in = _mix(o_mla, o_gla, h, w["w_out"], w["g_post_mix"], w["g_pre_ffn"])
    f = _ffn(f_in, w["w_gate"], w["w_up"], w["w_down"])
    n_prompt = h[0].shape[0]
    return tuple(_final(f, h1, p_k, row0, w["g_post_ffn"], w["w_ple_gate"], w["w_ple"])
                 for p_k, row0 in zip(p, (0, n_prompt)))


def kernel(x_prompt, x_sample, cache_ckv, cache_krope, state_gla, p_prompt, p_sample, g_pre_mix, w_in, g_q, w_uq,
           w_uk, g_kv, w_ga, b_ga, w_uv, g_gla, w_out, g_post_mix, g_pre_ffn, w_gate, w_up, w_down, g_post_ffn,
           w_ple, w_ple_gate):
    batch, seq, _ = x_prompt.shape
    dbatch, dseq, _ = x_sample.shape
    depth = w_in.shape[0]
    cos_p, sin_p = _rope_tables(jnp.arange(seq))
    cos_s, sin_s = _rope_tables(PAST_LEN + jnp.arange(dseq))
    cos_s, sin_s = jnp.tile(cos_s, (dbatch, 1)), jnp.tile(sin_s, (dbatch, 1))
    h_p = x_prompt.reshape(batch * seq, D_MODEL)
    h_s = x_sample.reshape(dbatch * dseq, D_MODEL)
    outs = [[] for _ in range(6)]
    for i in range(depth):
        w = _layer_weights(i, g_pre_mix, w_in, g_q, w_uq, w_uk, g_kv, w_ga, b_ga, w_uv, g_gla, w_out, g_post_mix,
                           g_pre_ffn, w_gate, w_up, w_down, g_post_ffn, w_ple, w_ple_gate)
        z, zs = _inproj(h_p, w["g_pre_mix"], *w["w_in"])
        ckvn, kr, qcat, kt, v = _mla_prep_prompt(z, zs, w["g_q"], w["g_kv"], w["w_uq"],
                                                 w["w_ukt"].reshape(MLA_WIDTH, KV_LORA), w["w_uv"], cos_p, sin_p, seq)
        o_mla_p = _attn_prompt(qcat, kt, v, batch, seq)
        o_gla_p, s_fin = _gla(z, zs, w["w_ga"], w["b_ga"], w["g_gla"], batch, seq, GLA_CHUNK)
        outs[0].append(ckvn.reshape(batch, seq, KV_LORA))
        outs[1].append(kr.reshape(batch, seq, ROPE_DIM))
        outs[2].append(s_fin)
        z, zs = _inproj(h_s, w["g_pre_mix"], *w["w_in"])
        ckvn, kr, q_lat, q_rope = _mla_prep_sample(z, zs, w["g_q"], w["g_kv"], w["w_uq"], w["w_ukt"], cos_s, sin_s)
        o_mla_s = _attn_sample(q_lat, q_rope, cache_ckv[i], jnp.swapaxes(cache_krope[i], 1, 2), ckvn, kr,
                               w["w_uvh"], dseq)
        o_gla_s, s_new = _gla(z, zs, w["w_ga"], w["b_ga"], w["g_gla"], dbatch, dseq, GLA_DEC_ROWS, s0=state_gla[i])
        outs[3].append(ckvn.reshape(dbatch, dseq, KV_LORA))
        outs[4].append(kr.reshape(dbatch, dseq, ROPE_DIM))
        outs[5].append(s_new)
        h_p, h_s = _finish(w, (h_p, h_s),
                           (p_prompt[i].reshape(batch * seq, PLE_DIM), p_sample[i].reshape(dbatch * dseq, PLE_DIM)),
                           (o_mla_p, o_mla_s), (o_gla_p, o_gla_s))
    return (h_p.reshape(batch, seq, D_MODEL), h_s.reshape(dbatch, dseq, D_MODEL),
            jnp.stack(outs[0]), jnp.stack(outs[1]), jnp.stack(outs[2]),
            jnp.stack(outs[3]), jnp.stack(outs[4]), jnp.stack(outs[5]))
```

```python
import functools

import numpy as np
import jax
import jax.numpy as jnp
from jax import lax
from jax.experimental import pallas as pl
from jax.experimental.pallas import tpu as pltpu

F32 = jnp.float32
BF16 = jnp.bfloat16

D_MODEL = 2048
PAST_LEN = 4096
CHUNK = 64
EPS = 1e-6

MLA_HEADS = 8
Q_LORA = 512
KV_LORA = 512
NOPE_DIM = 128
ROPE_DIM = 64
V_DIM = 128
ROPE_THETA = 10000.0
MLA_SCALE = (NOPE_DIM + ROPE_DIM) ** -0.5
QK_PAD = 256

GLA_HEADS = 4
GLA_DK = 128
GLA_DV = 256
GATE_RANK = 16
GATE_TAU = 16.0
GLA_WIDTH = GLA_HEADS * GLA_DV
MLA_WIDTH = MLA_HEADS * V_DIM

IN_SPLITS = (Q_LORA, KV_LORA, ROPE_DIM, GLA_HEADS * GLA_DK, GLA_HEADS * GLA_DK, GLA_WIDTH, GATE_RANK, GLA_WIDTH)
Z_MAIN = 4096
GLR_LO = ROPE_DIM

D_FF = 5632
PLE_DIM = 256

VMEM_LIMIT = 56 * 1024 * 1024
NEG_BIG = -1e30
LOG2E = 1.4426950408889634
QK_SCALE = MLA_SCALE * LOG2E


def _dot(a, b):
    return jnp.dot(a, b, preferred_element_type=F32)


def _dot_nt(a, b):
    return lax.dot_general(a, b, (((1,), (1,)), ((), ())), preferred_element_type=F32)


def _dot_tn(a, b):
    return lax.dot_general(a, b, (((0,), (0,)), ((), ())), preferred_element_type=F32)


def _rms(x, g):
    ms = jnp.mean(x * x, axis=-1, keepdims=True)
    return x * lax.rsqrt(ms + EPS) * g


def _rope(x, cos_t, sin_t):
    lane = lax.broadcasted_iota(jnp.int32, x.shape, 1)
    rot = jnp.where((lane & 32) == 0, pltpu.roll(x, 96, 1), pltpu.roll(x, 32, 1))
    return x * cos_t + rot * sin_t


def _params(*sem):
    return pltpu.CompilerParams(dimension_semantics=sem, vmem_limit_bytes=VMEM_LIMIT)


_IN_OFF = np.concatenate([[0], np.cumsum(IN_SPLITS)]).tolist()


_WIN_TILE = 512


def _win_prep_kernel(w_ref, kr_ref, glr_ref, o_ref, small_ref):
    o_ref[...] = w_ref[...].T.astype(BF16)

    @pl.when(pl.program_id(0) == 0)
    def _():
        pad = jnp.zeros((128 - GLR_LO - GATE_RANK, D_MODEL), F32)
        small_ref[...] = jnp.concatenate([kr_ref[...], glr_ref[...], pad], axis=0).T.astype(BF16)


def _win_prep(w_t):
    c_q, c_kv, k_r, q_g, k_g, v_g, g_lr, r_g, end = _IN_OFF
    n_head, n_mid = k_r // _WIN_TILE, (g_lr - q_g) // _WIN_TILE
    assert k_r % _WIN_TILE == 0 and (g_lr - q_g) % _WIN_TILE == 0 and (end - r_g) % _WIN_TILE == 0
    assert (n_head + n_mid) * _WIN_TILE + end - r_g == Z_MAIN and GLR_LO == ROPE_DIM

    def src_row(j):
        skip_mid, skip_tail = (q_g - k_r) // 8, (r_g - g_lr + q_g - k_r) // 8
        return 8 * (j * (_WIN_TILE // 8) + jnp.where(j < n_head, 0, jnp.where(j < n_head + n_mid, skip_mid, skip_tail)))

    return pl.pallas_call(
        _win_prep_kernel,
        grid=(Z_MAIN // _WIN_TILE,),
        in_specs=[pl.BlockSpec((pl.Element(_WIN_TILE), pl.Element(D_MODEL)), lambda j: (src_row(j), 0)),
                  pl.BlockSpec((pl.Element(ROPE_DIM), pl.Element(D_MODEL)), lambda j: (k_r, 0)),
                  pl.BlockSpec((pl.Element(GATE_RANK), pl.Element(D_MODEL)), lambda j: (g_lr, 0))],
        out_specs=[pl.BlockSpec((D_MODEL, _WIN_TILE), lambda j: (0, j)),
                   pl.BlockSpec((D_MODEL, 128), lambda j: (0, 0))],
        out_shape=[jax.ShapeDtypeStruct((D_MODEL, Z_MAIN), BF16), jax.ShapeDtypeStruct((D_MODEL, 128), BF16)],
        compiler_params=_params("arbitrary"),
        name="win_prep",
    )(w_t, w_t, w_t)


def _inproj_kernel(x_ref, g_ref, w_ref, ws_ref, o_ref, os_ref):
    xn = _rms(x_ref[...], g_ref[...]).astype(BF16)
    o_ref[...] = _dot(xn, w_ref[...])
    os_ref[...] = _dot(xn, ws_ref[...])


def _inproj(x, g, w_main, w_small):
    T = x.shape[0]
    tm = 512
    return pl.pallas_call(
        _inproj_kernel,
        grid=(T // tm,),
        in_specs=[pl.BlockSpec((tm, D_MODEL), lambda i: (i, 0)),
                  pl.BlockSpec((1, D_MODEL), lambda i: (0, 0)),
                  pl.BlockSpec((D_MODEL, Z_MAIN), lambda i: (0, 0), pipeline_mode=pl.Buffered(1)),
                  pl.BlockSpec((D_MODEL, 128), lambda i: (0, 0), pipeline_mode=pl.Buffered(1))],
        out_specs=[pl.BlockSpec((tm, Z_MAIN), lambda i: (i, 0)), pl.BlockSpec((tm, 128), lambda i: (i, 0))],
        out_shape=[jax.ShapeDtypeStruct((T, Z_MAIN), F32), jax.ShapeDtypeStruct((T, 128), F32)],
        compiler_params=_params("parallel"),
        name="inproj",
    )(x, g, w_main, w_small)


def _mla_q(cq_ref, gq_ref, wuq_ref, cos, sin):
    cqn = _rms(cq_ref[...], gq_ref[...]).astype(BF16)
    q = _dot(cqn, wuq_ref[...]) * QK_SCALE
    out = []
    for h in range(MLA_HEADS):
        nope = q[:, h * QK_PAD:h * QK_PAD + NOPE_DIM]
        rp = _rope(q[:, h * QK_PAD + NOPE_DIM:(h + 1) * QK_PAD], cos, sin)
        out.append((nope, rp))
    return out


def _mla_prep_prompt_kernel(cq_ref, ckv_ref, sm_ref, gq_ref, gkv_ref, wuq_ref, wukt_ref, wuv_ref, cos_ref, sin_ref,
                            ckvn_ref, kr_ref, qcat_ref, kt_ref, v_ref):
    cos, sin = cos_ref[...], sin_ref[...]
    for h, (nope, rp) in enumerate(_mla_q(cq_ref, gq_ref, wuq_ref, cos, sin)):
        qcat_ref[:, h * QK_PAD:h * QK_PAD + NOPE_DIM] = nope.astype(BF16)
        qcat_ref[:, h * QK_PAD + NOPE_DIM:(h + 1) * QK_PAD] = rp.astype(BF16)
    ckvn = _rms(ckv_ref[...], gkv_ref[...])
    ckvn_ref[...] = ckvn
    kr = _rope(sm_ref[...], cos, sin)
    kr_ref[...] = kr[:, :ROPE_DIM]
    ckvn_b = ckvn.astype(BF16)
    k_t = _dot_nt(wukt_ref[...], ckvn_b)
    kr_t = kr.T.astype(BF16)
    for h in range(MLA_HEADS):
        kt_ref[h * QK_PAD:h * QK_PAD + NOPE_DIM, :] = k_t[h * NOPE_DIM:(h + 1) * NOPE_DIM].astype(BF16)
        kt_ref[h * QK_PAD + NOPE_DIM:(h + 1) * QK_PAD, :] = kr_t
    v_ref[...] = _dot(ckvn_b, wuv_ref[...]).astype(BF16)


def _mla_prep_prompt(z, zs, g_q, g_kv, w_uq, w_ukt, w_uv, cos_t, sin_t, seq):
    T = z.shape[0]
    tm = 512
    nseq = seq // tm
    row = lambda i: (i, 0)
    const = lambda i: (0, 0)
    return pl.pallas_call(
        _mla_prep_prompt_kernel,
        grid=(T // tm,),
        in_specs=[pl.BlockSpec((tm, Q_LORA), lambda i: (i, 0)),
                  pl.BlockSpec((tm, KV_LORA), lambda i: (i, 1)),
                  pl.BlockSpec((tm, 128), lambda i: (i, 0)),
                  pl.BlockSpec((1, Q_LORA), const),
                  pl.BlockSpec((1, KV_LORA), const),
                  pl.BlockSpec((Q_LORA, MLA_HEADS * QK_PAD), const),
                  pl.BlockSpec((MLA_WIDTH, KV_LORA), const),
                  pl.BlockSpec((KV_LORA, MLA_WIDTH), const),
                  pl.BlockSpec((tm, 128), lambda i: (i % nseq, 0)),
                  pl.BlockSpec((tm, 128), lambda i: (i % nseq, 0))],
        out_specs=[pl.BlockSpec((tm, KV_LORA), row),
                   pl.BlockSpec((tm, ROPE_DIM), row),
                   pl.BlockSpec((tm, MLA_HEADS * QK_PAD), row),
                   pl.BlockSpec((MLA_HEADS * QK_PAD, tm), lambda i: (0, i)),
                   pl.BlockSpec((tm, MLA_WIDTH), row)],
        out_shape=[jax.ShapeDtypeStruct((T, KV_LORA), F32),
                   jax.ShapeDtypeStruct((T, ROPE_DIM), F32),
                   jax.ShapeDtypeStruct((T, MLA_HEADS * QK_PAD), BF16),
                   jax.ShapeDtypeStruct((MLA_HEADS * QK_PAD, T), BF16),
                   jax.ShapeDtypeStruct((T, MLA_WIDTH), BF16)],
        compiler_params=_params("parallel"),
        name="mla_prep_prompt",
    )(z, z, zs, g_q, g_kv, w_uq, w_ukt, w_uv, cos_t, sin_t)


def _mla_prep_sample_kernel(cq_ref, ckv_ref, sm_ref, gq_ref, gkv_ref, wuq_ref, wukt_ref, cos_ref, sin_ref,
                            ckvn_ref, kr_ref, qlat_ref, qr_ref):
    cos, sin = cos_ref[...], sin_ref[...]
    for h, (nope, rp) in enumerate(_mla_q(cq_ref, gq_ref, wuq_ref, cos, sin)):
        qlat_ref[h] = _dot(nope.astype(BF16), wukt_ref[h]).astype(BF16)
        qr_ref[h] = rp[:, :ROPE_DIM].astype(BF16)
    ckvn_ref[...] = _rms(ckv_ref[...], gkv_ref[...])
    kr_ref[...] = _rope(sm_ref[...], cos, sin)[:, :ROPE_DIM]


def _mla_prep_sample(z, zs, g_q, g_kv, w_uq, w_ukt, cos_t, sin_t):
    T = z.shape[0]
    tm = T
    row = lambda i: (i, 0)
    const = lambda i: (0, 0)
    const3 = lambda i: (0, 0, 0)
    return pl.pallas_call(
        _mla_prep_sample_kernel,
        grid=(T // tm,),
        in_specs=[pl.BlockSpec((tm, Q_LORA), lambda i: (i, 0)),
                  pl.BlockSpec((tm, KV_LORA), lambda i: (i, 1)),
                  pl.BlockSpec((tm, 128), lambda i: (i, 0)),
                  pl.BlockSpec((1, Q_LORA), const),
                  pl.BlockSpec((1, KV_LORA), const),
                  pl.BlockSpec((Q_LORA, MLA_HEADS * QK_PAD), const),
                  pl.BlockSpec((MLA_HEADS, NOPE_DIM, KV_LORA), const3),
                  pl.BlockSpec((tm, 128), row),
                  pl.BlockSpec((tm, 128), row)],
        out_specs=[pl.BlockSpec((tm, KV_LORA), row),
                   pl.BlockSpec((tm, ROPE_DIM), row),
                   pl.BlockSpec((MLA_HEADS, tm, KV_LORA), lambda i: (0, i, 0)),
                   pl.BlockSpec((MLA_HEADS, tm, ROPE_DIM), lambda i: (0, i, 0))],
        out_shape=[jax.ShapeDtypeStruct((T, KV_LORA), F32),
                   jax.ShapeDtypeStruct((T, ROPE_DIM), F32),
                   jax.ShapeDtypeStruct((MLA_HEADS, T, KV_LORA), BF16),
                   jax.ShapeDtypeStruct((MLA_HEADS, T, ROPE_DIM), BF16)],
        compiler_params=_params("parallel"),
        name="mla_prep_sample",
    )(z, z, zs, g_q, g_kv, w_uq, w_ukt, cos_t, sin_t)


ATT_TQ = 512
ATT_HEADS = 4


def _attn_prompt_kernel(q_ref, kt_ref, v_ref, o_ref):
    seq = q_ref.shape[0]
    qc = lax.broadcasted_iota(jnp.int32, (ATT_TQ, ATT_TQ), 0) // CHUNK
    kc = lax.broadcasted_iota(jnp.int32, (ATT_TQ, ATT_TQ), 1) // CHUNK
    visible = kc <= qc
    for i in reversed(range(seq // ATT_TQ)):
        lo, hi = i * ATT_TQ, (i + 1) * ATT_TQ
        for h in range(ATT_HEADS):
            qk = slice(h * QK_PAD, (h + 1) * QK_PAD)
            dv = slice(h * V_DIM, (h + 1) * V_DIM)
            s = _dot(q_ref[lo:hi, qk], kt_ref[qk, 0:hi])
            s_diag = jnp.where(visible, s[:, lo:hi], NEG_BIG)
            s = jnp.concatenate([s[:, :lo], s_diag], axis=1) if i else s_diag
            m = jnp.max(s, axis=-1, keepdims=True)
            p = jnp.exp2(s - m).astype(BF16)
            v_one = jnp.concatenate([v_ref[0:hi, dv], jnp.ones((hi, V_DIM), BF16)], axis=1)
            ol = _dot(p, v_one)
            o_ref[lo:hi, dv] = (ol[:, :V_DIM] / ol[:, V_DIM:]).astype(BF16)


def _attn_prompt(qcat, kt, v, batch, seq):
    return pl.pallas_call(
        _attn_prompt_kernel,
        grid=(batch, MLA_HEADS // ATT_HEADS),
        in_specs=[pl.BlockSpec((seq, ATT_HEADS * QK_PAD), lambda b, h: (b, h)),
                  pl.BlockSpec((ATT_HEADS * QK_PAD, seq), lambda b, h: (h, b)),
                  pl.BlockSpec((seq, ATT_HEADS * V_DIM), lambda b, h: (b, h))],
        out_specs=pl.BlockSpec((seq, ATT_HEADS * V_DIM), lambda b, h: (b, h)),
        out_shape=jax.ShapeDtypeStruct((batch * seq, MLA_WIDTH), BF16),
        compiler_params=_params("parallel", "parallel"),
        name="attn_prompt",
    )(qcat, kt, v)


def _attn_sample_kernel(ql_ref, qr_ref, ckv_ref, krt_ref, nckv_ref, nkr_ref, wuv_ref, o_ref):
    dec = ql_ref.shape[1]
    rows = MLA_HEADS * dec
    ql = ql_ref[...].reshape(rows, KV_LORA)
    qr = qr_ref[...].reshape(rows, ROPE_DIM)
    ckv = ckv_ref[...].astype(BF16)
    nckv = nckv_ref[...].astype(BF16)
    s = _dot_nt(ql, ckv) + _dot(qr, krt_ref[...].astype(BF16))
    s_new = _dot_nt(ql, nckv) + _dot_nt(qr, nkr_ref[...].astype(BF16))
    m = jnp.maximum(jnp.max(s, axis=-1, keepdims=True), jnp.max(s_new, axis=-1, keepdims=True))
    p = jnp.exp2(s - m)
    p_new = jnp.exp2(s_new - m)
    l = jnp.sum(p, axis=-1, keepdims=True) + jnp.sum(p_new, axis=-1, keepdims=True)
    o = ((_dot(p.astype(BF16), ckv) + _dot(p_new.astype(BF16), nckv)) / l).astype(BF16)
    for h in range(MLA_HEADS):
        o_ref[:, h * V_DIM:(h + 1) * V_DIM] = _dot(o[h * dec:(h + 1) * dec], wuv_ref[h]).astype(BF16)


def _attn_sample(q_lat, q_rope, cache_ckv, cache_krt, ckvn, kr, w_uvh, dec):
    batch, past, _ = cache_ckv.shape
    return pl.pallas_call(
        _attn_sample_kernel,
        grid=(batch,),
        in_specs=[pl.BlockSpec((MLA_HEADS, dec, KV_LORA), lambda b: (0, b, 0)),
                  pl.BlockSpec((MLA_HEADS, dec, ROPE_DIM), lambda b: (0, b, 0)),
                  pl.BlockSpec((None, past, KV_LORA), lambda b: (b, 0, 0)),
                  pl.BlockSpec((None, ROPE_DIM, past), lambda b: (b, 0, 0)),
                  pl.BlockSpec((dec, KV_LORA), lambda b: (b, 0)),
                  pl.BlockSpec((dec, ROPE_DIM), lambda b: (b, 0)),
                  pl.BlockSpec((MLA_HEADS, KV_LORA, V_DIM), lambda b: (0, 0, 0))],
        out_specs=pl.BlockSpec((dec, MLA_WIDTH), lambda b: (b, 0)),
        out_shape=jax.ShapeDtypeStruct((batch * dec, MLA_WIDTH), BF16),
        compiler_params=_params("parallel"),
        name="attn_sample",
    )(q_lat, q_rope, cache_ckv, cache_krt, ckvn, kr, w_uvh)


GLA_CHUNK = 256
GLA_DEC_ROWS = 128
GLA_SEQS = 2


def _gla_tables(c, sub):
    nlev = int(np.log2(sub))
    assert 1 << nlev == sub and c % sub == 0 and c // sub <= 128
    t = np.arange(c)[:, None]
    u = np.arange(c)[None, :]
    same_sub = (t // sub) == (u // sub)
    blocks = []
    lvl = np.full((c, c), -1, np.int32)
    for l in range(nlev):
        width = sub >> l
        half = width // 2
        m = (t // width) * width + half - 1
        upper = t > m
        blocks.append(np.where(upper, (u > m) & (u <= t), (u > t) & (u <= m)))
        same = (t // width) == (u // width)
        lvl[same & ((t % width) >= half) & ((u % width) < half)] = l
    blocks.append((u > t) & same_sub)
    blocks.append((u <= t) & same_sub)
    lvl[np.arange(c), np.arange(c)] = nlev
    sel = (np.arange(c)[:, None] // sub) == np.arange(128)[None, :]
    return np.concatenate(blocks, 0).astype(np.float32), lvl, sel.astype(np.float32), nlev


def _gla_kernel(*refs, c, sub, nlev, carry):
    if carry:
        (q_ref, k_ref, v_ref, r_ref, sm_ref, wga_ref, bga_ref, gg_ref, p_ref, lvl_ref, sel_ref,
         o_ref, sfin_ref, s_scr) = refs
        assert sub == c
        j = pl.program_id(1)

        @pl.when(j == 0)
        def _():
            s_scr[...] = jnp.zeros(s_scr.shape, F32)
    else:
        (q_ref, k_ref, v_ref, r_ref, sm_ref, wga_ref, bga_ref, gg_ref, p_ref, lvl_ref, sel_ref, s0_ref,
         o_ref, sfin_ref) = refs

    p_mat = p_ref[...]
    sel = sel_ref[...]
    lvl = lvl_ref[...]
    groups = range(q_ref.shape[0])
    split = []
    for g in groups:
        x = _dot(sm_ref[g].astype(BF16), wga_ref[...]) + bga_ref[...]
        la = (jnp.minimum(x, 0.0) - jnp.log1p(jnp.exp(-jnp.abs(x)))) * (1.0 / GATE_TAU)
        hi = la.astype(BF16)
        split.append((hi, (la - hi.astype(F32)).astype(BF16)))
    for g in groups:
        hi, mid = split[g]
        e_all = jnp.exp(_dot(p_mat, hi) + _dot(p_mat, mid))
        d_all = jnp.exp(_dot_tn(hi, sel) + _dot_tn(mid, sel))
        q = q_ref[g] * (GLA_DK ** -0.5)
        k = k_ref[g]
        qe = [(q * e_all[l * c:(l + 1) * c]).astype(BF16) for l in range(nlev)] + [q.astype(BF16)]
        ke = [(k * e_all[l * c:(l + 1) * c]).astype(BF16) for l in range(nlev)] + [k.astype(BF16)]
        k_end = (k * e_all[nlev * c:(nlev + 1) * c]).astype(BF16)
        q_beg = (q * e_all[(nlev + 1) * c:(nlev + 2) * c]).astype(BF16)
        for h in range(GLA_HEADS):
            dk = slice(h * GLA_DK, (h + 1) * GLA_DK)
            dv = slice(h * GLA_DV, (h + 1) * GLA_DV)
            attn = jnp.zeros((c, c), F32)
            for l in range(nlev + 1):
                attn = jnp.where(lvl == l, _dot_nt(qe[l][:, dk], ke[l][:, dk]), attn)
            v = v_ref[g, :, dv].astype(BF16)
            o_intra = _dot(attn.astype(BF16), v)
            for n in range(c // sub):
                rows = slice(n * sub, (n + 1) * sub)
                state = s_scr[g, h] if carry else s0_ref[n, h]
                o = o_intra[rows] + _dot(q_beg[rows, dk], state.astype(BF16))
                new_state = d_all[dk, n:n + 1] * state + _dot_tn(k_end[rows, dk], v[rows])
                if carry:
                    s_scr[g, h] = new_state
                else:
                    sfin_ref[n, h] = new_state
                r = r_ref[g, rows, dv]
                o_ref[g, rows, dv] = (_rms(o, gg_ref[...]) * (r * jax.nn.sigmoid(r))).astype(BF16)

    if carry:
        @pl.when(j == pl.num_programs(1) - 1)
        def _():
            sfin_ref[...] = s_scr[...]


def _gla(z, zs, w_ga, b_ga, g_gla, batch, seq, c, s0=None):
    carry = s0 is None
    sub = c if carry else seq
    p_np, lvl_np, sel_np, nlev = _gla_tables(c, sub)
    if carry:
        nb, lead, grid = GLA_SEQS, batch, (batch // GLA_SEQS, seq // c)
        at = lambda col: (lambda g, j: (g, j, col))
    else:
        nb, lead, grid = 1, 1, (batch * seq // c, 1)
        at = lambda col: (lambda g, j: (0, g, col))
    nstate = nb if carry else c // sub
    z, zs = z.reshape(lead, -1, Z_MAIN), zs.reshape(lead, -1, 128)
    const = lambda g, j: (0, 0)
    in_specs = [pl.BlockSpec((nb, c, 512), at(2)),
                pl.BlockSpec((nb, c, 512), at(3)),
                pl.BlockSpec((nb, c, GLA_WIDTH), at(2)),
                pl.BlockSpec((nb, c, GLA_WIDTH), at(3)),
                pl.BlockSpec((nb, c, 128), at(0)),
                pl.BlockSpec((128, GLA_HEADS * GLA_DK), const),
                pl.BlockSpec((1, GLA_HEADS * GLA_DK), const),
                pl.BlockSpec((1, GLA_DV), const),
                pl.BlockSpec(p_np.shape, const),
                pl.BlockSpec((c, c), const),
                pl.BlockSpec((c, 128), const)]
    args = [z, z, z, z, zs, w_ga, b_ga, g_gla,
            jnp.asarray(p_np, BF16), jnp.asarray(lvl_np), jnp.asarray(sel_np, BF16)]
    state_spec = pl.BlockSpec((nstate, GLA_HEADS, GLA_DK, GLA_DV), lambda g, j: (g, 0, 0, 0))
    if not carry:
        in_specs.append(state_spec)
        args.append(s0)
    o, s_fin = pl.pallas_call(
        functools.partial(_gla_kernel, c=c, sub=sub, nlev=nlev, carry=carry),
        grid=grid,
        in_specs=in_specs,
        out_specs=[pl.BlockSpec((nb, c, GLA_WIDTH), at(0)), state_spec],
        out_shape=[jax.ShapeDtypeStruct((lead, batch * seq // lead, GLA_WIDTH), BF16),
                   jax.ShapeDtypeStruct((batch, GLA_HEADS, GLA_DK, GLA_DV), F32)],
        scratch_shapes=[pltpu.VMEM((nb, GLA_HEADS, GLA_DK, GLA_DV), F32)] if carry else [],
        compiler_params=_params("parallel", "arbitrary"),
        name="gla" if carry else "gla_init",
    )(*args)
    return o.reshape(batch * seq, GLA_WIDTH), s_fin


ROW_SUB = 128


def _mix_kernel(oma_ref, omb_ref, oga_ref, ogb_ref, w1_ref, w2_ref, ha_ref, hb_ref, g1_ref, g2_ref, h1_ref, f_ref,
                *, na):
    def block(om_ref, og_ref, h_ref):
        for r in range(0, h_ref.shape[0], ROW_SUB):
            rows = slice(r, r + ROW_SUB)
            mix = _dot(om_ref[rows, :], w1_ref[...]) + _dot(og_ref[rows, :], w2_ref[...])
            h1 = h_ref[rows, :] + _rms(mix, g1_ref[...])
            h1_ref[rows, :] = h1
            f_ref[rows, :] = _rms(h1, g2_ref[...]).astype(BF16)

    pl.when(pl.program_id(0) < na)(lambda: block(oma_ref, oga_ref, ha_ref))
    pl.when(pl.program_id(0) >= na)(lambda: block(omb_ref, ogb_ref, hb_ref))


def _mix(o_mla, o_gla, h, w_out, g_post_mix, g_pre_ffn):
    tm = 512
    na, nb = h[0].shape[0] // tm, h[1].shape[0] // tm
    T = (na + nb) * tm
    row = lambda i: (i, 0)
    const = lambda i: (0, 0)
    at_a = lambda i: (jnp.minimum(i, na - 1), 0)
    at_b = lambda i: (jnp.maximum(i - na, 0), 0)
    return pl.pallas_call(
        functools.partial(_mix_kernel, na=na),
        grid=(na + nb,),
        in_specs=[pl.BlockSpec((tm, MLA_WIDTH), at_a), pl.BlockSpec((tm, MLA_WIDTH), at_b),
                  pl.BlockSpec((tm, GLA_WIDTH), at_a), pl.BlockSpec((tm, GLA_WIDTH), at_b),
                  pl.BlockSpec((MLA_WIDTH, D_MODEL), lambda i: (0, 0), pipeline_mode=pl.Buffered(1)),
                  pl.BlockSpec((GLA_WIDTH, D_MODEL), lambda i: (1, 0), pipeline_mode=pl.Buffered(1)),
                  pl.BlockSpec((tm, D_MODEL), at_a), pl.BlockSpec((tm, D_MODEL), at_b),
                  pl.BlockSpec((1, D_MODEL), const),
                  pl.BlockSpec((1, D_MODEL), const)],
        out_specs=[pl.BlockSpec((tm, D_MODEL), row), pl.BlockSpec((tm, D_MODEL), row)],
        out_shape=[jax.ShapeDtypeStruct((T, D_MODEL), F32), jax.ShapeDtypeStruct((T, D_MODEL), BF16)],
        compiler_params=_params("arbitrary"),
        name="mix",
    )(o_mla[0], o_mla[1], o_gla[0], o_gla[1], w_out, w_out, h[0], h[1], g_post_mix, g_pre_ffn)


FFN_MAX_ROWS = 1088


def _ffn_kernel(x_ref, wg_ref, wu_ref, wd_ref, o_ref):
    @pl.when(pl.program_id(1) == 0)
    def _():
        o_ref[...] = jnp.zeros(o_ref.shape, F32)

    x = x_ref[...]
    g = _dot(x, wg_ref[...].astype(BF16))
    u = _dot(x, wu_ref[...].astype(BF16))
    o_ref[...] += _dot((g * jax.nn.sigmoid(g) * u).astype(BF16), wd_ref[...].astype(BF16))


def _ffn(x, w_gate, w_up, w_down):
    T = x.shape[0]
    tf = 512
    tm = next(T // n for n in range(1, T + 1) if T % n == 0 and T // n <= FFN_MAX_ROWS and (T // n) % 16 == 0)
    return pl.pallas_call(
        _ffn_kernel,
        grid=(T // tm, D_FF // tf),
        in_specs=[pl.BlockSpec((tm, D_MODEL), lambda i, j: (i, 0)),
                  pl.BlockSpec((D_MODEL, tf), lambda i, j: (0, j)),
                  pl.BlockSpec((D_MODEL, tf), lambda i, j: (0, j)),
                  pl.BlockSpec((tf, D_MODEL), lambda i, j: (j, 0))],
        out_specs=pl.BlockSpec((tm, D_MODEL), lambda i, j: (i, 0)),
        out_shape=jax.ShapeDtypeStruct((T, D_MODEL), F32),
        compiler_params=_params("parallel", "arbitrary"),
        name="ffn",
    )(x, w_gate, w_up, w_down)


def _final_kernel(f_ref, h1_ref, pa_ref, pb_ref, g_ref, wpg_ref, wp_ref, ya_ref, yb_ref, *, na):
    def block(p_ref, y_ref):
        h2 = h1_ref[...] + _rms(f_ref[...], g_ref[...])
        gate = jax.nn.sigmoid(_dot(h2.astype(BF16), wpg_ref[...]))
        y_ref[...] = h2 + gate * _dot(p_ref[...].astype(BF16), wp_ref[...])

    pl.when(pl.program_id(0) < na)(lambda: block(pa_ref, ya_ref))
    pl.when(pl.program_id(0) >= na)(lambda: block(pb_ref, yb_ref))


def _final(f, h1, p, g_post_ffn, w_ple_gate, w_ple):
    tm = 512
    na, nb = p[0].shape[0] // tm, p[1].shape[0] // tm
    row = lambda i: (i, 0)
    const = lambda i: (0, 0)
    at_a = lambda i: (jnp.minimum(i, na - 1), 0)
    at_b = lambda i: (jnp.maximum(i - na, 0), 0)
    return pl.pallas_call(
        functools.partial(_final_kernel, na=na),
        grid=(na + nb,),
        in_specs=[pl.BlockSpec((tm, D_MODEL), row),
                  pl.BlockSpec((tm, D_MODEL), row),
                  pl.BlockSpec((tm, PLE_DIM), at_a), pl.BlockSpec((tm, PLE_DIM), at_b),
                  pl.BlockSpec((1, D_MODEL), const),
                  pl.BlockSpec((D_MODEL, D_MODEL), const, pipeline_mode=pl.Buffered(1)),
                  pl.BlockSpec((PLE_DIM, D_MODEL), const)],
        out_specs=[pl.BlockSpec((tm, D_MODEL), at_a), pl.BlockSpec((tm, D_MODEL), at_b)],
        out_shape=[jax.ShapeDtypeStruct((na * tm, D_MODEL), F32), jax.ShapeDtypeStruct((nb * tm, D_MODEL), F32)],
        compiler_params=_params("arbitrary"),
        name="final",
    )(f, h1, p[0], p[1], g_post_ffn, w_ple_gate, w_ple)


def _rope_tables(pos):
    half = ROPE_DIM // 2
    inv = 1.0 / (ROPE_THETA ** (jnp.arange(half, dtype=F32) / half))
    ang = pos.astype(F32)[:, None] * inv[None, :]
    cos, sin = jnp.cos(ang), jnp.sin(ang)
    zero = jnp.zeros((pos.shape[0], 128 - ROPE_DIM), F32)
    return jnp.concatenate([cos, cos, zero], axis=1), jnp.concatenate([-sin, sin, zero], axis=1)


def _layer_weights(i, g_pre_mix, w_in, g_q, w_uq, w_uk, g_kv, w_ga, b_ga, w_uv, g_gla, w_out, g_post_mix,
                   g_pre_ffn, w_gate, w_up, w_down, g_post_ffn, w_ple, w_ple_gate):
    w_uq_p = jnp.pad(w_uq[i].reshape(Q_LORA, MLA_HEADS, NOPE_DIM + ROPE_DIM),
                     ((0, 0), (0, 0), (0, QK_PAD - NOPE_DIM - ROPE_DIM))).reshape(Q_LORA, MLA_HEADS * QK_PAD)
    w_ga_p = jnp.zeros((128, GLA_HEADS * GLA_DK), F32).at[GLR_LO:GLR_LO + GATE_RANK].set(w_ga[i])
    vec = lambda g: g[i].reshape(1, -1)
    return dict(
        g_pre_mix=vec(g_pre_mix), w_in=_win_prep(jnp.swapaxes(w_in[i], 0, 1)), g_q=vec(g_q), g_kv=vec(g_kv),
        w_uq=w_uq_p.astype(BF16),
        w_uv=w_uv[i].reshape(KV_LORA, MLA_WIDTH).astype(BF16),
        w_ukt=jnp.transpose(w_uk[i], (1, 2, 0)).astype(BF16),
        w_uvh=jnp.transpose(w_uv[i], (1, 0, 2)).astype(BF16),
        w_ga=w_ga_p.astype(BF16), b_ga=vec(b_ga), g_gla=vec(g_gla),
        w_out=w_out[i].astype(BF16), g_post_mix=vec(g_post_mix), g_pre_ffn=vec(g_pre_ffn),
        w_gate=w_gate[i], w_up=w_up[i], w_down=w_down[i],
        g_post_ffn=vec(g_post_ffn), w_ple=w_ple[i].astype(BF16), w_ple_gate=w_ple_gate[i].astype(BF16))


def _finish(w, h, p, o_mla, o_gla):
    h1, f_in = _mix(o_mla, o_gla, h, w["w_out"], w["g_post_mix"], w["g_pre_ffn"])
    f = _ffn(f_in, w["w_gate"], w["w_up"], w["w_down"])
    return _final(f, h1, p, w["g_post_ffn"], w["w_ple_gate"], w["w_ple"])


def kernel(x_prompt, x_sample, cache_ckv, cache_krope, state_gla, p_prompt, p_sample, g_pre_mix, w_in, g_q, w_uq,
           w_uk, g_kv, w_ga, b_ga, w_uv, g_gla, w_out, g_post_mix, g_pre_ffn, w_gate, w_up, w_down, g_post_ffn,
           w_ple, w_ple_gate):
    batch, seq, _ = x_prompt.shape
    dbatch, dseq, _ = x_sample.shape
    depth = w_in.shape[0]
    cos_p, sin_p = _rope_tables(jnp.arange(seq))
    cos_s, sin_s = _rope_tables(PAST_LEN + jnp.arange(dseq))
    cos_s, sin_s = jnp.tile(cos_s, (dbatch, 1)), jnp.tile(sin_s, (dbatch, 1))
    h_p = x_prompt.reshape(batch * seq, D_MODEL)
    h_s = x_sample.reshape(dbatch * dseq, D_MODEL)
    outs = [[] for _ in range(6)]
    for i in range(depth):
        w = _layer_weights(i, g_pre_mix, w_in, g_q, w_uq, w_uk, g_kv, w_ga, b_ga, w_uv, g_gla, w_out, g_post_mix,
                           g_pre_ffn, w_gate, w_up, w_down, g_post_ffn, w_ple, w_ple_gate)
        z, zs = _inproj(h_p, w["g_pre_mix"], *w["w_in"])
        ckvn, kr, qcat, kt, v = _mla_prep_prompt(z, zs, w["g_q"], w["g_kv"], w["w_uq"],
                                                 w["w_ukt"].reshape(MLA_WIDTH, KV_LORA), w["w_uv"], cos_p, sin_p, seq)
        o_mla_p = _attn_prompt(qcat, kt, v, batch, seq)
        o_gla_p, s_fin = _gla(z, zs, w["w_ga"], w["b_ga"], w["g_gla"], batch, seq, GLA_CHUNK)
        outs[0].append(ckvn.reshape(batch, seq, KV_LORA))
        outs[1].append(kr.reshape(batch, seq, ROPE_DIM))
        outs[2].append(s_fin)
        z, zs = _inproj(h_s, w["g_pre_mix"], *w["w_in"])
        ckvn, kr, q_lat, q_rope = _mla_prep_sample(z, zs, w["g_q"], w["g_kv"], w["w_uq"], w["w_ukt"], cos_s, sin_s)
        o_mla_s = _attn_sample(q_lat, q_rope, cache_ckv[i], jnp.swapaxes(cache_krope[i], 1, 2), ckvn, kr,
                               w["w_uvh"], dseq)
        o_gla_s, s_new = _gla(z, zs, w["w_ga"], w["b_ga"], w["g_gla"], dbatch, dseq, GLA_DEC_ROWS, s0=state_gla[i])
        outs[3].append(ckvn.reshape(dbatch, dseq, KV_LORA))
        outs[4].append(kr.reshape(dbatch, dseq, ROPE_DIM))
        outs[5].append(s_new)
        h_p, h_s = _finish(w, (h_p, h_s),
                           (p_prompt[i].reshape(batch * seq, PLE_DIM), p_sample[i].reshape(dbatch * dseq, PLE_DIM)),
                           (o_mla_p, o_mla_s), (o_gla_p, o_gla_s))
    return (h_p.reshape(batch, seq, D_MODEL), h_s.reshape(dbatch, dseq, D_MODEL),
            jnp.stack(outs[0]), jnp.stack(outs[1]), jnp.stack(outs[2]),
            jnp.stack(outs[3]), jnp.stack(outs[4]), jnp.stack(outs[5]))
```

```python
import functools

import numpy as np
import jax
import jax.numpy as jnp
from jax import lax
from jax.experimental import pallas as pl
from jax.experimental.pallas import tpu as pltpu

F32 = jnp.float32
BF16 = jnp.bfloat16

D_MODEL = 2048
PAST_LEN = 4096
CHUNK = 64
EPS = 1e-6

MLA_HEADS = 8
Q_LORA = 512
KV_LORA = 512
NOPE_DIM = 128
ROPE_DIM = 64
V_DIM = 128
ROPE_THETA = 10000.0
MLA_SCALE = (NOPE_DIM + ROPE_DIM) ** -0.5
QK_PAD = 256

GLA_HEADS = 4
GLA_DK = 128
GLA_DV = 256
GATE_RANK = 16
GATE_TAU = 16.0
GLA_WIDTH = GLA_HEADS * GLA_DV
MLA_WIDTH = MLA_HEADS * V_DIM

IN_SPLITS = (Q_LORA, KV_LORA, ROPE_DIM, GLA_HEADS * GLA_DK, GLA_HEADS * GLA_DK, GLA_WIDTH, GATE_RANK, GLA_WIDTH)
Z_MAIN = 4096
GLR_LO = ROPE_DIM

D_FF = 5632
PLE_DIM = 256

VMEM_LIMIT = 56 * 1024 * 1024
VMEM_LIMIT_PROJ = 60 * 1024 * 1024
NEG_BIG = -1e30
LOG2E = 1.4426950408889634
QK_SCALE = MLA_SCALE * LOG2E


def _dot(a, b):
    return jnp.dot(a, b, preferred_element_type=F32)


def _dot_nt(a, b):
    return lax.dot_general(a, b, (((1,), (1,)), ((), ())), preferred_element_type=F32)


def _dot_tn(a, b):
    return lax.dot_general(a, b, (((0,), (0,)), ((), ())), preferred_element_type=F32)


def _rms(x, g):
    ms = jnp.mean(x * x, axis=-1, keepdims=True)
    return x * lax.rsqrt(ms + EPS) * g


def _rope(x, cos_t, sin_t):
    lane = lax.broadcasted_iota(jnp.int32, x.shape, 1)
    rot = jnp.where((lane & 32) == 0, pltpu.roll(x, 96, 1), pltpu.roll(x, 32, 1))
    return x * cos_t + rot * sin_t


def _params(*sem, vmem=VMEM_LIMIT):
    return pltpu.CompilerParams(dimension_semantics=sem, vmem_limit_bytes=vmem)


_IN_OFF = np.concatenate([[0], np.cumsum(IN_SPLITS)]).tolist()


_WIN_TILE = 512


def _win_prep_kernel(w_ref, kr_ref, glr_ref, o_ref, small_ref):
    o_ref[...] = w_ref[...].T.astype(BF16)

    @pl.when(pl.program_id(0) == 0)
    def _():
        pad = jnp.zeros((128 - GLR_LO - GATE_RANK, D_MODEL), F32)
        small_ref[...] = jnp.concatenate([kr_ref[...], glr_ref[...], pad], axis=0).T.astype(BF16)


def _win_prep(w_t):
    c_q, c_kv, k_r, q_g, k_g, v_g, g_lr, r_g, end = _IN_OFF
    n_head, n_mid = k_r // _WIN_TILE, (g_lr - q_g) // _WIN_TILE
    assert k_r % _WIN_TILE == 0 and (g_lr - q_g) % _WIN_TILE == 0 and (end - r_g) % _WIN_TILE == 0
    assert (n_head + n_mid) * _WIN_TILE + end - r_g == Z_MAIN and GLR_LO == ROPE_DIM

    def src_row(j):
        skip_mid, skip_tail = (q_g - k_r) // 8, (r_g - g_lr + q_g - k_r) // 8
        return 8 * (j * (_WIN_TILE // 8) + jnp.where(j < n_head, 0, jnp.where(j < n_head + n_mid, skip_mid, skip_tail)))

    return pl.pallas_call(
        _win_prep_kernel,
        grid=(Z_MAIN // _WIN_TILE,),
        in_specs=[pl.BlockSpec((pl.Element(_WIN_TILE), pl.Element(D_MODEL)), lambda j: (src_row(j), 0)),
                  pl.BlockSpec((pl.Element(ROPE_DIM), pl.Element(D_MODEL)), lambda j: (k_r, 0)),
                  pl.BlockSpec((pl.Element(GATE_RANK), pl.Element(D_MODEL)), lambda j: (g_lr, 0))],
        out_specs=[pl.BlockSpec((D_MODEL, _WIN_TILE), lambda j: (0, j)),
                   pl.BlockSpec((D_MODEL, 128), lambda j: (0, 0))],
        out_shape=[jax.ShapeDtypeStruct((D_MODEL, Z_MAIN), BF16), jax.ShapeDtypeStruct((D_MODEL, 128), BF16)],
        compiler_params=_params("arbitrary"),
        name="win_prep",
    )(w_t, w_t, w_t)


Z_MLA = Q_LORA + KV_LORA
Z_GLA = Z_MAIN - Z_MLA
PROJ_TM = 512


def _inproj(x_ref, g_ref, wm_ref, ws_ref, zg_ref, zs_ref):
    xn = _rms(x_ref[...], g_ref[...]).astype(BF16)
    zg_ref[...] = _dot(xn, wm_ref[:, Z_MLA:])
    sm = _dot(xn, ws_ref[...])
    zs_ref[...] = sm
    z_mla = _dot(xn, wm_ref[:, :Z_MLA])
    return z_mla[:, :Q_LORA], z_mla[:, Q_LORA:], sm


def _inproj_specs(tm):
    row = lambda i: (i, 0)
    const = lambda i: (0, 0)
    in_specs = [pl.BlockSpec((tm, D_MODEL), row),
                pl.BlockSpec((1, D_MODEL), const),
                pl.BlockSpec((D_MODEL, Z_MAIN), const, pipeline_mode=pl.Buffered(1)),
                pl.BlockSpec((D_MODEL, 128), const, pipeline_mode=pl.Buffered(1)),
                pl.BlockSpec((1, Q_LORA), const),
                pl.BlockSpec((1, KV_LORA), const),
                pl.BlockSpec((Q_LORA, MLA_HEADS * QK_PAD), const, pipeline_mode=pl.Buffered(1))]
    out_specs = [pl.BlockSpec((tm, Z_GLA), row), pl.BlockSpec((tm, 128), row),
                 pl.BlockSpec((tm, KV_LORA), row), pl.BlockSpec((tm, ROPE_DIM), row)]
    out_shape = lambda T: [jax.ShapeDtypeStruct((T, Z_GLA), F32), jax.ShapeDtypeStruct((T, 128), F32),
                           jax.ShapeDtypeStruct((T, KV_LORA), F32), jax.ShapeDtypeStruct((T, ROPE_DIM), F32)]
    return in_specs, out_specs, out_shape


def _mla_q(cq, gq_ref, wuq_ref, cos, sin):
    cqn = _rms(cq, gq_ref[...]).astype(BF16)
    q = _dot(cqn, wuq_ref[...]) * QK_SCALE
    out = []
    for h in range(MLA_HEADS):
        nope = q[:, h * QK_PAD:h * QK_PAD + NOPE_DIM]
        rp = _rope(q[:, h * QK_PAD + NOPE_DIM:(h + 1) * QK_PAD], cos, sin)
        out.append((nope, rp))
    return out


def _proj_prompt_kernel(x_ref, g_ref, wm_ref, ws_ref, gq_ref, gkv_ref, wuq_ref, wukt_ref, wuv_ref, cos_ref, sin_ref,
                        zg_ref, zs_ref, ckvn_ref, kr_ref, qcat_ref, kt_ref, v_ref):
    cq, ckv, sm = _inproj(x_ref, g_ref, wm_ref, ws_ref, zg_ref, zs_ref)
    cos, sin = cos_ref[...], sin_ref[...]
    for h, (nope, rp) in enumerate(_mla_q(cq, gq_ref, wuq_ref, cos, sin)):
        qcat_ref[:, h * QK_PAD:h * QK_PAD + NOPE_DIM] = nope.astype(BF16)
        qcat_ref[:, h * QK_PAD + NOPE_DIM:(h + 1) * QK_PAD] = rp.astype(BF16)
    ckvn = _rms(ckv, gkv_ref[...])
    ckvn_ref[...] = ckvn
    kr = _rope(sm, cos, sin)
    kr_ref[...] = kr[:, :ROPE_DIM]
    ckvn_b = ckvn.astype(BF16)
    k_t = _dot_nt(wukt_ref[...], ckvn_b)
    kr_t = kr.T.astype(BF16)
    for h in range(MLA_HEADS):
        kt_ref[h * QK_PAD:h * QK_PAD + NOPE_DIM, :] = k_t[h * NOPE_DIM:(h + 1) * NOPE_DIM].astype(BF16)
        kt_ref[h * QK_PAD + NOPE_DIM:(h + 1) * QK_PAD, :] = kr_t
    v_ref[...] = _dot(ckvn_b, wuv_ref[...]).astype(BF16)


def _proj_prompt(x, w, cos_t, sin_t, seq):
    T = x.shape[0]
    tm = PROJ_TM
    nseq = seq // tm
    row = lambda i: (i, 0)
    const = lambda i: (0, 0)
    in_specs, out_specs, out_shape = _inproj_specs(tm)
    return pl.pallas_call(
        _proj_prompt_kernel,
        grid=(T // tm,),
        in_specs=in_specs + [pl.BlockSpec((MLA_WIDTH, KV_LORA), const, pipeline_mode=pl.Buffered(1)),
                             pl.BlockSpec((KV_LORA, MLA_WIDTH), const, pipeline_mode=pl.Buffered(1)),
                             pl.BlockSpec((tm, 128), lambda i: (i % nseq, 0)),
                             pl.BlockSpec((tm, 128), lambda i: (i % nseq, 0))],
        out_specs=out_specs + [pl.BlockSpec((tm, MLA_HEADS * QK_PAD), row),
                               pl.BlockSpec((MLA_HEADS * QK_PAD, tm), lambda i: (0, i)),
                               pl.BlockSpec((tm, MLA_WIDTH), row)],
        out_shape=out_shape(T) + [jax.ShapeDtypeStruct((T, MLA_HEADS * QK_PAD), BF16),
                                  jax.ShapeDtypeStruct((MLA_HEADS * QK_PAD, T), BF16),
                                  jax.ShapeDtypeStruct((T, MLA_WIDTH), BF16)],
        compiler_params=_params("parallel", vmem=VMEM_LIMIT_PROJ),
        name="proj_prompt",
    )(x, w["g_pre_mix"], *w["w_in"], w["g_q"], w["g_kv"], w["w_uq"],
      w["w_ukt"].reshape(MLA_WIDTH, KV_LORA), w["w_uv"], cos_t, sin_t)


def _proj_sample_kernel(x_ref, g_ref, wm_ref, ws_ref, gq_ref, gkv_ref, wuq_ref, wukt_ref, cos_ref, sin_ref,
                        zg_ref, zs_ref, ckvn_ref, kr_ref, qlat_ref, qr_ref):
    cq, ckv, sm = _inproj(x_ref, g_ref, wm_ref, ws_ref, zg_ref, zs_ref)
    cos, sin = cos_ref[...], sin_ref[...]
    for h, (nope, rp) in enumerate(_mla_q(cq, gq_ref, wuq_ref, cos, sin)):
        qlat_ref[h] = _dot(nope.astype(BF16), wukt_ref[h]).astype(BF16)
        qr_ref[h] = rp[:, :ROPE_DIM].astype(BF16)
    ckvn_ref[...] = _rms(ckv, gkv_ref[...])
    kr_ref[...] = _rope(sm, cos, sin)[:, :ROPE_DIM]


def _proj_sample(x, w, cos_t, sin_t):
    T = x.shape[0]
    tm = PROJ_TM
    row = lambda i: (i, 0)
    in_specs, out_specs, out_shape = _inproj_specs(tm)
    return pl.pallas_call(
        _proj_sample_kernel,
        grid=(T // tm,),
        in_specs=in_specs + [pl.BlockSpec((MLA_HEADS, NOPE_DIM, KV_LORA), lambda i: (0, 0, 0)),
                             pl.BlockSpec((tm, 128), row),
                             pl.BlockSpec((tm, 128), row)],
        out_specs=out_specs + [pl.BlockSpec((MLA_HEADS, tm, KV_LORA), lambda i: (0, i, 0)),
                               pl.BlockSpec((MLA_HEADS, tm, ROPE_DIM), lambda i: (0, i, 0))],
        out_shape=out_shape(T) + [jax.ShapeDtypeStruct((MLA_HEADS, T, KV_LORA), BF16),
                                  jax.ShapeDtypeStruct((MLA_HEADS, T, ROPE_DIM), BF16)],
        compiler_params=_params("parallel"),
        name="proj_sample",
    )(x, w["g_pre_mix"], *w["w_in"], w["g_q"], w["g_kv"], w["w_uq"], w["w_ukt"], cos_t, sin_t)


ATT_TQ = 512
ATT_HEADS = 4


def _attn_prompt_kernel(q_ref, kt_ref, v_ref, o_ref):
    seq = q_ref.shape[0]
    qc = lax.broadcasted_iota(jnp.int32, (ATT_TQ, ATT_TQ), 0) // CHUNK
    kc = lax.broadcasted_iota(jnp.int32, (ATT_TQ, ATT_TQ), 1) // CHUNK
    visible = kc <= qc
    for i in reversed(range(seq // ATT_TQ)):
        lo, hi = i * ATT_TQ, (i + 1) * ATT_TQ
        for h in range(ATT_HEADS):
            qk = slice(h * QK_PAD, (h + 1) * QK_PAD)
            dv = slice(h * V_DIM, (h + 1) * V_DIM)
            s = _dot(q_ref[lo:hi, qk], kt_ref[qk, 0:hi])
            s_diag = jnp.where(visible, s[:, lo:hi], NEG_BIG)
            s = jnp.concatenate([s[:, :lo], s_diag], axis=1) if i else s_diag
            m = jnp.max(s, axis=-1, keepdims=True)
            p = jnp.exp2(s - m).astype(BF16)
            v_one = jnp.concatenate([v_ref[0:hi, dv], jnp.ones((hi, V_DIM), BF16)], axis=1)
            ol = _dot(p, v_one)
            o_ref[lo:hi, dv] = (ol[:, :V_DIM] / ol[:, V_DIM:]).astype(BF16)


def _attn_prompt(qcat, kt, v, batch, seq):
    return pl.pallas_call(
        _attn_prompt_kernel,
        grid=(batch, MLA_HEADS // ATT_HEADS),
        in_specs=[pl.BlockSpec((seq, ATT_HEADS * QK_PAD), lambda b, h: (b, h)),
                  pl.BlockSpec((ATT_HEADS * QK_PAD, seq), lambda b, h: (h, b)),
                  pl.BlockSpec((seq, ATT_HEADS * V_DIM), lambda b, h: (b, h))],
        out_specs=pl.BlockSpec((seq, ATT_HEADS * V_DIM), lambda b, h: (b, h)),
        out_shape=jax.ShapeDtypeStruct((batch * seq, MLA_WIDTH), BF16),
        compiler_params=_params("parallel", "parallel"),
        name="attn_prompt",
    )(qcat, kt, v)


def _attn_sample_kernel(ql_ref, qr_ref, ckv_ref, krt_ref, nckv_ref, nkr_ref, wuv_ref, o_ref):
    dec = ql_ref.shape[1]
    rows = MLA_HEADS * dec
    ql = ql_ref[...].reshape(rows, KV_LORA)
    qr = qr_ref[...].reshape(rows, ROPE_DIM)
    ckv = ckv_ref[...].astype(BF16)
    nckv = nckv_ref[...].astype(BF16)
    s = _dot_nt(ql, ckv) + _dot(qr, krt_ref[...].astype(BF16))
    s_new = _dot_nt(ql, nckv) + _dot_nt(qr, nkr_ref[...].astype(BF16))
    m = jnp.maximum(jnp.max(s, axis=-1, keepdims=True), jnp.max(s_new, axis=-1, keepdims=True))
    p = jnp.exp2(s - m)
    p_new = jnp.exp2(s_new - m)
    l = jnp.sum(p, axis=-1, keepdims=True) + jnp.sum(p_new, axis=-1, keepdims=True)
    o = ((_dot(p.astype(BF16), ckv) + _dot(p_new.astype(BF16), nckv)) / l).astype(BF16)
    for h in range(MLA_HEADS):
        o_ref[:, h * V_DIM:(h + 1) * V_DIM] = _dot(o[h * dec:(h + 1) * dec], wuv_ref[h]).astype(BF16)


def _attn_sample(q_lat, q_rope, cache_ckv, cache_krt, ckvn, kr, w_uvh, dec):
    batch, past, _ = cache_ckv.shape
    return pl.pallas_call(
        _attn_sample_kernel,
        grid=(batch,),
        in_specs=[pl.BlockSpec((MLA_HEADS, dec, KV_LORA), lambda b: (0, b, 0)),
                  pl.BlockSpec((MLA_HEADS, dec, ROPE_DIM), lambda b: (0, b, 0)),
                  pl.BlockSpec((None, past, KV_LORA), lambda b: (b, 0, 0)),
                  pl.BlockSpec((None, ROPE_DIM, past), lambda b: (b, 0, 0)),
                  pl.BlockSpec((dec, KV_LORA), lambda b: (b, 0)),
                  pl.BlockSpec((dec, ROPE_DIM), lambda b: (b, 0)),
                  pl.BlockSpec((MLA_HEADS, KV_LORA, V_DIM), lambda b: (0, 0, 0))],
        out_specs=pl.BlockSpec((dec, MLA_WIDTH), lambda b: (b, 0)),
        out_shape=jax.ShapeDtypeStruct((batch * dec, MLA_WIDTH), BF16),
        compiler_params=_params("parallel"),
        name="attn_sample",
    )(q_lat, q_rope, cache_ckv, cache_krt, ckvn, kr, w_uvh)


GLA_CHUNK = 256
GLA_DEC_ROWS = 128
GLA_SEQS = 2


def _gla_tables(c, sub):
    nlev = int(np.log2(sub))
    assert 1 << nlev == sub and c % sub == 0 and c // sub <= 128
    t = np.arange(c)[:, None]
    u = np.arange(c)[None, :]
    same_sub = (t // sub) == (u // sub)
    blocks = []
    lvl = np.full((c, c), -1, np.int32)
    for l in range(nlev):
        width = sub >> l
        half = width // 2
        m = (t // width) * width + half - 1
        upper = t > m
        blocks.append(np.where(upper, (u > m) & (u <= t), (u > t) & (u <= m)))
        same = (t // width) == (u // width)
        lvl[same & ((t % width) >= half) & ((u % width) < half)] = l
    blocks.append((u > t) & same_sub)
    blocks.append((u <= t) & same_sub)
    lvl[np.arange(c), np.arange(c)] = nlev
    sel = (np.arange(c)[:, None] // sub) == np.arange(128)[None, :]
    return np.concatenate(blocks, 0).astype(np.float32), lvl, sel.astype(np.float32), nlev


def _gla_kernel(*refs, c, sub, nlev, carry):
    if carry:
        (q_ref, k_ref, v_ref, r_ref, sm_ref, wga_ref, bga_ref, gg_ref, p_ref, lvl_ref, sel_ref,
         o_ref, sfin_ref, s_scr) = refs
        assert sub == c
        j = pl.program_id(1)

        @pl.when(j == 0)
        def _():
            s_scr[...] = jnp.zeros(s_scr.shape, F32)
    else:
        (q_ref, k_ref, v_ref, r_ref, sm_ref, wga_ref, bga_ref, gg_ref, p_ref, lvl_ref, sel_ref, s0_ref,
         o_ref, sfin_ref) = refs

    p_mat = p_ref[...]
    sel = sel_ref[...]
    lvl = lvl_ref[...]
    groups = range(q_ref.shape[0])
    split = []
    for g in groups:
        x = _dot(sm_ref[g].astype(BF16), wga_ref[...]) + bga_ref[...]
        la = (jnp.minimum(x, 0.0) - jnp.log1p(jnp.exp(-jnp.abs(x)))) * (1.0 / GATE_TAU)
        hi = la.astype(BF16)
        split.append((hi, (la - hi.astype(F32)).astype(BF16)))
    for g in groups:
        hi, mid = split[g]
        e_all = jnp.exp(_dot(p_mat, hi) + _dot(p_mat, mid))
        d_all = jnp.exp(_dot_tn(hi, sel) + _dot_tn(mid, sel))
        q = q_ref[g] * (GLA_DK ** -0.5)
        k = k_ref[g]
        qe = [(q * e_all[l * c:(l + 1) * c]).astype(BF16) for l in range(nlev)] + [q.astype(BF16)]
        ke = [(k * e_all[l * c:(l + 1) * c]).astype(BF16) for l in range(nlev)] + [k.astype(BF16)]
        k_end = (k * e_all[nlev * c:(nlev + 1) * c]).astype(BF16)
        q_beg = (q * e_all[(nlev + 1) * c:(nlev + 2) * c]).astype(BF16)
        for h in range(GLA_HEADS):
            dk = slice(h * GLA_DK, (h + 1) * GLA_DK)
            dv = slice(h * GLA_DV, (h + 1) * GLA_DV)
            attn = jnp.zeros((c, c), F32)
            for l in range(nlev + 1):
                attn = jnp.where(lvl == l, _dot_nt(qe[l][:, dk], ke[l][:, dk]), attn)
            v = v_ref[g, :, dv].astype(BF16)
            o_intra = _dot(attn.astype(BF16), v)
            for n in range(c // sub):
                rows = slice(n * sub, (n + 1) * sub)
                state = s_scr[g, h] if carry else s0_ref[n, h]
                o = o_intra[rows] + _dot(q_beg[rows, dk], state.astype(BF16))
                new_state = d_all[dk, n:n + 1] * state + _dot_tn(k_end[rows, dk], v[rows])
                if carry:
                    s_scr[g, h] = new_state
                else:
                    sfin_ref[n, h] = new_state
                r = r_ref[g, rows, dv]
                o_ref[g, rows, dv] = (_rms(o, gg_ref[...]) * (r * jax.nn.sigmoid(r))).astype(BF16)

    if carry:
        @pl.when(j == pl.num_programs(1) - 1)
        def _():
            sfin_ref[...] = s_scr[...]


def _gla(z, zs, w_ga, b_ga, g_gla, batch, seq, c, s0=None):
    carry = s0 is None
    sub = c if carry else seq
    p_np, lvl_np, sel_np, nlev = _gla_tables(c, sub)
    if carry:
        nb, lead, grid = GLA_SEQS, batch, (batch // GLA_SEQS, seq // c)
        at = lambda col: (lambda g, j: (g, j, col))
    else:
        nb, lead, grid = 1, 1, (batch * seq // c, 1)
        at = lambda col: (lambda g, j: (0, g, col))
    nstate = nb if carry else c // sub
    z, zs = z.reshape(lead, -1, Z_GLA), zs.reshape(lead, -1, 128)
    const = lambda g, j: (0, 0)
    in_specs = [pl.BlockSpec((nb, c, 512), at(0)),
                pl.BlockSpec((nb, c, 512), at(1)),
                pl.BlockSpec((nb, c, GLA_WIDTH), at(1)),
                pl.BlockSpec((nb, c, GLA_WIDTH), at(2)),
                pl.BlockSpec((nb, c, 128), at(0)),
                pl.BlockSpec((128, GLA_HEADS * GLA_DK), const),
                pl.BlockSpec((1, GLA_HEADS * GLA_DK), const),
                pl.BlockSpec((1, GLA_DV), const),
                pl.BlockSpec(p_np.shape, const),
                pl.BlockSpec((c, c), const),
                pl.BlockSpec((c, 128), const)]
    args = [z, z, z, z, zs, w_ga, b_ga, g_gla,
            jnp.asarray(p_np, BF16), jnp.asarray(lvl_np), jnp.asarray(sel_np, BF16)]
    state_spec = pl.BlockSpec((nstate, GLA_HEADS, GLA_DK, GLA_DV), lambda g, j: (g, 0, 0, 0))
    if not carry:
        in_specs.append(state_spec)
        args.append(s0)
    o, s_fin = pl.pallas_call(
        functools.partial(_gla_kernel, c=c, sub=sub, nlev=nlev, carry=carry),
        grid=grid,
        in_specs=in_specs,
        out_specs=[pl.BlockSpec((nb, c, GLA_WIDTH), at(0)), state_spec],
        out_shape=[jax.ShapeDtypeStruct((lead, batch * seq // lead, GLA_WIDTH), BF16),
                   jax.ShapeDtypeStruct((batch, GLA_HEADS, GLA_DK, GLA_DV), F32)],
        scratch_shapes=[pltpu.VMEM((nb, GLA_HEADS, GLA_DK, GLA_DV), F32)] if carry else [],
        compiler_params=_params("parallel", "arbitrary"),
        name="gla" if carry else "gla_init",
    )(*args)
    return o.reshape(batch * seq, GLA_WIDTH), s_fin


ROW_SUB = 128


def _mix_kernel(oma_ref, omb_ref, oga_ref, ogb_ref, w1_ref, w2_ref, ha_ref, hb_ref, g1_ref, g2_ref, h1_ref, f_ref,
                *, na):
    def block(om_ref, og_ref, h_ref):
        for r in range(0, h_ref.shape[0], ROW_SUB):
            rows = slice(r, r + ROW_SUB)
            mix = _dot(om_ref[rows, :], w1_ref[...]) + _dot(og_ref[rows, :], w2_ref[...])
            h1 = h_ref[rows, :] + _rms(mix, g1_ref[...])
            h1_ref[rows, :] = h1
            f_ref[rows, :] = _rms(h1, g2_ref[...]).astype(BF16)

    pl.when(pl.program_id(0) < na)(lambda: block(oma_ref, oga_ref, ha_ref))
    pl.when(pl.program_id(0) >= na)(lambda: block(omb_ref, ogb_ref, hb_ref))


def _mix(o_mla, o_gla, h, w_out, g_post_mix, g_pre_ffn):
    tm = 512
    na, nb = h[0].shape[0] // tm, h[1].shape[0] // tm
    T = (na + nb) * tm
    row = lambda i: (i, 0)
    const = lambda i: (0, 0)
    at_a = lambda i: (jnp.minimum(i, na - 1), 0)
    at_b = lambda i: (jnp.maximum(i - na, 0), 0)
    return pl.pallas_call(
        functools.partial(_mix_kernel, na=na),
        grid=(na + nb,),
        in_specs=[pl.BlockSpec((tm, MLA_WIDTH), at_a), pl.BlockSpec((tm, MLA_WIDTH), at_b),
                  pl.BlockSpec((tm, GLA_WIDTH), at_a), pl.BlockSpec((tm, GLA_WIDTH), at_b),
                  pl.BlockSpec((MLA_WIDTH, D_MODEL), lambda i: (0, 0), pipeline_mode=pl.Buffered(1)),
                  pl.BlockSpec((GLA_WIDTH, D_MODEL), lambda i: (1, 0), pipeline_mode=pl.Buffered(1)),
                  pl.BlockSpec((tm, D_MODEL), at_a), pl.BlockSpec((tm, D_MODEL), at_b),
                  pl.BlockSpec((1, D_MODEL), const),
                  pl.BlockSpec((1, D_MODEL), const)],
        out_specs=[pl.BlockSpec((tm, D_MODEL), row), pl.BlockSpec((tm, D_MODEL), row)],
        out_shape=[jax.ShapeDtypeStruct((T, D_MODEL), F32), jax.ShapeDtypeStruct((T, D_MODEL), BF16)],
        compiler_params=_params("arbitrary"),
        name="mix",
    )(o_mla[0], o_mla[1], o_gla[0], o_gla[1], w_out, w_out, h[0], h[1], g_post_mix, g_pre_ffn)


FFN_MAX_ROWS = 1088


def _ffn_kernel(x_ref, wg_ref, wu_ref, wd_ref, o_ref):
    @pl.when(pl.program_id(1) == 0)
    def _():
        o_ref[...] = jnp.zeros(o_ref.shape, F32)

    x = x_ref[...]
    g = _dot(x, wg_ref[...].astype(BF16))
    u = _dot(x, wu_ref[...].astype(BF16))
    o_ref[...] += _dot((g * jax.nn.sigmoid(g) * u).astype(BF16), wd_ref[...].astype(BF16))


def _ffn(x, w_gate, w_up, w_down):
    T = x.shape[0]
    tf = 512
    tm = next(T // n for n in range(1, T + 1) if T % n == 0 and T // n <= FFN_MAX_ROWS and (T // n) % 16 == 0)
    return pl.pallas_call(
        _ffn_kernel,
        grid=(T // tm, D_FF // tf),
        in_specs=[pl.BlockSpec((tm, D_MODEL), lambda i, j: (i, 0)),
                  pl.BlockSpec((D_MODEL, tf), lambda i, j: (0, j)),
                  pl.BlockSpec((D_MODEL, tf), lambda i, j: (0, j)),
                  pl.BlockSpec((tf, D_MODEL), lambda i, j: (j, 0))],
        out_specs=pl.BlockSpec((tm, D_MODEL), lambda i, j: (i, 0)),
        out_shape=jax.ShapeDtypeStruct((T, D_MODEL), F32),
        compiler_params=_params("parallel", "arbitrary"),
        name="ffn",
    )(x, w_gate, w_up, w_down)


def _final_kernel(f_ref, h1_ref, pa_ref, pb_ref, g_ref, wpg_ref, wp_ref, ya_ref, yb_ref, *, na):
    def block(p_ref, y_ref):
        h2 = h1_ref[...] + _rms(f_ref[...], g_ref[...])
        gate = jax.nn.sigmoid(_dot(h2.astype(BF16), wpg_ref[...]))
        y_ref[...] = h2 + gate * _dot(p_ref[...].astype(BF16), wp_ref[...])

    pl.when(pl.program_id(0) < na)(lambda: block(pa_ref, ya_ref))
    pl.when(pl.program_id(0) >= na)(lambda: block(pb_ref, yb_ref))


def _final(f, h1, p, g_post_ffn, w_ple_gate, w_ple):
    tm = 512
    na, nb = p[0].shape[0] // tm, p[1].shape[0] // tm
    row = lambda i: (i, 0)
    const = lambda i: (0, 0)
    at_a = lambda i: (jnp.minimum(i, na - 1), 0)
    at_b = lambda i: (jnp.maximum(i - na, 0), 0)
    return pl.pallas_call(
        functools.partial(_final_kernel, na=na),
        grid=(na + nb,),
        in_specs=[pl.BlockSpec((tm, D_MODEL), row),
                  pl.BlockSpec((tm, D_MODEL), row),
                  pl.BlockSpec((tm, PLE_DIM), at_a), pl.BlockSpec((tm, PLE_DIM), at_b),
                  pl.BlockSpec((1, D_MODEL), const),
                  pl.BlockSpec((D_MODEL, D_MODEL), const, pipeline_mode=pl.Buffered(1)),
                  pl.BlockSpec((PLE_DIM, D_MODEL), const)],
        out_specs=[pl.BlockSpec((tm, D_MODEL), at_a), pl.BlockSpec((tm, D_MODEL), at_b)],
        out_shape=[jax.ShapeDtypeStruct((na * tm, D_MODEL), F32), jax.ShapeDtypeStruct((nb * tm, D_MODEL), F32)],
        compiler_params=_params("arbitrary"),
        name="final",
    )(f, h1, p[0], p[1], g_post_ffn, w_ple_gate, w_ple)


def _rope_tables(pos):
    half = ROPE_DIM // 2
    inv = 1.0 / (ROPE_THETA ** (jnp.arange(half, dtype=F32) / half))
    ang = pos.astype(F32)[:, None] * inv[None, :]
    cos, sin = jnp.cos(ang), jnp.sin(ang)
    zero = jnp.zeros((pos.shape[0], 128 - ROPE_DIM), F32)
    return jnp.concatenate([cos, cos, zero], axis=1), jnp.concatenate([-sin, sin, zero], axis=1)


def _layer_weights(i, g_pre_mix, w_in, g_q, w_uq, w_uk, g_kv, w_ga, b_ga, w_uv, g_gla, w_out, g_post_mix,
                   g_pre_ffn, w_gate, w_up, w_down, g_post_ffn, w_ple, w_ple_gate):
    w_uq_p = jnp.pad(w_uq[i].reshape(Q_LORA, MLA_HEADS, NOPE_DIM + ROPE_DIM),
                     ((0, 0), (0, 0), (0, QK_PAD - NOPE_DIM - ROPE_DIM))).reshape(Q_LORA, MLA_HEADS * QK_PAD)
    w_ga_p = jnp.zeros((128, GLA_HEADS * GLA_DK), F32).at[GLR_LO:GLR_LO + GATE_RANK].set(w_ga[i])
    vec = lambda g: g[i].reshape(1, -1)
    return dict(
        g_pre_mix=vec(g_pre_mix), w_in=_win_prep(jnp.swapaxes(w_in[i], 0, 1)), g_q=vec(g_q), g_kv=vec(g_kv),
        w_uq=w_uq_p.astype(BF16),
        w_uv=w_uv[i].reshape(KV_LORA, MLA_WIDTH).astype(BF16),
        w_ukt=jnp.transpose(w_uk[i], (1, 2, 0)).astype(BF16),
        w_uvh=jnp.transpose(w_uv[i], (1, 0, 2)).astype(BF16),
        w_ga=w_ga_p.astype(BF16), b_ga=vec(b_ga), g_gla=vec(g_gla),
        w_out=w_out[i].astype(BF16), g_post_mix=vec(g_post_mix), g_pre_ffn=vec(g_pre_ffn),
        w_gate=w_gate[i], w_up=w_up[i], w_down=w_down[i],
        g_post_ffn=vec(g_post_ffn), w_ple=w_ple[i].astype(BF16), w_ple_gate=w_ple_gate[i].astype(BF16))


def _finish(w, h, p, o_mla, o_gla):
    h1, f_in = _mix(o_mla, o_gla, h, w["w_out"], w["g_post_mix"], w["g_pre_ffn"])
    f = _ffn(f_in, w["w_gate"], w["w_up"], w["w_down"])
    return _final(f, h1, p, w["g_post_ffn"], w["w_ple_gate"], w["w_ple"])


def kernel(x_prompt, x_sample, cache_ckv, cache_krope, state_gla, p_prompt, p_sample, g_pre_mix, w_in, g_q, w_uq,
           w_uk, g_kv, w_ga, b_ga, w_uv, g_gla, w_out, g_post_mix, g_pre_ffn, w_gate, w_up, w_down, g_post_ffn,
           w_ple, w_ple_gate):
    batch, seq, _ = x_prompt.shape
    dbatch, dseq, _ = x_sample.shape
    depth = w_in.shape[0]
    cos_p, sin_p = _rope_tables(jnp.arange(seq))
    cos_s, sin_s = _rope_tables(PAST_LEN + jnp.arange(dseq))
    cos_s, sin_s = jnp.tile(cos_s, (dbatch, 1)), jnp.tile(sin_s, (dbatch, 1))
    h_p = x_prompt.reshape(batch * seq, D_MODEL)
    h_s = x_sample.reshape(dbatch * dseq, D_MODEL)
    outs = [[] for _ in range(6)]
    for i in range(depth):
        w = _layer_weights(i, g_pre_mix, w_in, g_q, w_uq, w_uk, g_kv, w_ga, b_ga, w_uv, g_gla, w_out, g_post_mix,
                           g_pre_ffn, w_gate, w_up, w_down, g_post_ffn, w_ple, w_ple_gate)
        z, zs, ckvn, kr, qcat, kt, v = _proj_prompt(h_p, w, cos_p, sin_p, seq)
        o_mla_p = _attn_prompt(qcat, kt, v, batch, seq)
        o_gla_p, s_fin = _gla(z, zs, w["w_ga"], w["b_ga"], w["g_gla"], batch, seq, GLA_CHUNK)
        outs[0].append(ckvn.reshape(batch, seq, KV_LORA))
        outs[1].append(kr.reshape(batch, seq, ROPE_DIM))
        outs[2].append(s_fin)
        z, zs, ckvn, kr, q_lat, q_rope = _proj_sample(h_s, w, cos_s, sin_s)
        o_mla_s = _attn_sample(q_lat, q_rope, cache_ckv[i], jnp.swapaxes(cache_krope[i], 1, 2), ckvn, kr,
                               w["w_uvh"], dseq)
        o_gla_s, s_new = _gla(z, zs, w["w_ga"], w["b_ga"], w["g_gla"], dbatch, dseq, GLA_DEC_ROWS, s0=state_gla[i])
        outs[3].append(ckvn.reshape(dbatch, dseq, KV_LORA))
        outs[4].append(kr.reshape(dbatch, dseq, ROPE_DIM))
        outs[5].append(s_new)
        h_p, h_s = _finish(w, (h_p, h_s),
                           (p_prompt[i].reshape(batch * seq, PLE_DIM), p_sample[i].reshape(dbatch * dseq, PLE_DIM)),
                           (o_mla_p, o_mla_s), (o_gla_p, o_gla_s))
    return (h_p.reshape(batch, seq, D_MODEL), h_s.reshape(dbatch, dseq, D_MODEL),
            jnp.stack(outs[0]), jnp.stack(outs[1]), jnp.stack(outs[2]),
            jnp.stack(outs[3]), jnp.stack(outs[4]), jnp.stack(outs[5]))
```

```python
import functools

import numpy as np
import jax
import jax.numpy as jnp
from jax import lax
from jax.experimental import pallas as pl
from jax.experimental.pallas import tpu as pltpu

F32 = jnp.float32
BF16 = jnp.bfloat16

D_MODEL = 2048
PAST_LEN = 4096
CHUNK = 64
EPS = 1e-6

MLA_HEADS = 8
Q_LORA = 512
KV_LORA = 512
NOPE_DIM = 128
ROPE_DIM = 64
V_DIM = 128
ROPE_THETA = 10000.0
MLA_SCALE = (NOPE_DIM + ROPE_DIM) ** -0.5
QK_PAD = 256

GLA_HEADS = 4
GLA_DK = 128
GLA_DV = 256
GATE_RANK = 16
GATE_TAU = 16.0
GLA_WIDTH = GLA_HEADS * GLA_DV
MLA_WIDTH = MLA_HEADS * V_DIM

IN_SPLITS = (Q_LORA, KV_LORA, ROPE_DIM, GLA_HEADS * GLA_DK, GLA_HEADS * GLA_DK, GLA_WIDTH, GATE_RANK, GLA_WIDTH)
Z_MAIN = 4096
GLR_LO = ROPE_DIM

D_FF = 5632
PLE_DIM = 256

LANES = 128
VMEM_BYTES = 64 * 1024 * 1024
VMEM_LIMIT = VMEM_BYTES - 8 * 1024 * 1024
VMEM_LIMIT_PROJ = VMEM_BYTES - 4 * 1024 * 1024
NEG_BIG = -1e30
LOG2E = 1.4426950408889634
QK_SCALE = MLA_SCALE * LOG2E


def _dot(a, b):
    return jnp.dot(a, b, preferred_element_type=F32)


def _dot_nt(a, b):
    return lax.dot_general(a, b, (((1,), (1,)), ((), ())), preferred_element_type=F32)


def _dot_tn(a, b):
    return lax.dot_general(a, b, (((0,), (0,)), ((), ())), preferred_element_type=F32)


def _rms(x, g):
    ms = jnp.mean(x * x, axis=-1, keepdims=True)
    return x * lax.rsqrt(ms + EPS) * g


def _rope(x, cos_t, sin_t):
    lane = lax.broadcasted_iota(jnp.int32, x.shape, 1)
    rot = jnp.where((lane & 32) == 0, pltpu.roll(x, 96, 1), pltpu.roll(x, 32, 1))
    return x * cos_t + rot * sin_t


def _params(*sem, vmem=VMEM_LIMIT):
    return pltpu.CompilerParams(dimension_semantics=sem, vmem_limit_bytes=vmem)


_IN_OFF = np.concatenate([[0], np.cumsum(IN_SPLITS)]).tolist()
_WIN_TILE = 512


def _win_prep_kernel(w_ref, kr_ref, glr_ref, o_ref, small_ref):
    o_ref[...] = w_ref[...].T.astype(BF16)

    @pl.when(pl.program_id(0) == 0)
    def _():
        pad = jnp.zeros((LANES - GLR_LO - GATE_RANK, D_MODEL), F32)
        small_ref[...] = jnp.concatenate([kr_ref[...], glr_ref[...], pad], axis=0).T.astype(BF16)


def _win_prep(w_t):
    c_q, c_kv, k_r, q_g, k_g, v_g, g_lr, r_g, end = _IN_OFF
    n_head, n_mid = k_r // _WIN_TILE, (g_lr - q_g) // _WIN_TILE
    assert k_r % _WIN_TILE == 0 and (g_lr - q_g) % _WIN_TILE == 0 and (end - r_g) % _WIN_TILE == 0
    assert (n_head + n_mid) * _WIN_TILE + end - r_g == Z_MAIN and GLR_LO == ROPE_DIM

    def src_row(j):
        skip_mid, skip_tail = (q_g - k_r) // 8, (r_g - g_lr + q_g - k_r) // 8
        return 8 * (j * (_WIN_TILE // 8) + jnp.where(j < n_head, 0, jnp.where(j < n_head + n_mid, skip_mid, skip_tail)))

    return pl.pallas_call(
        _win_prep_kernel,
        grid=(Z_MAIN // _WIN_TILE,),
        in_specs=[pl.BlockSpec((pl.Element(_WIN_TILE), pl.Element(D_MODEL)), lambda j: (src_row(j), 0)),
                  pl.BlockSpec((pl.Element(ROPE_DIM), pl.Element(D_MODEL)), lambda j: (k_r, 0)),
                  pl.BlockSpec((pl.Element(GATE_RANK), pl.Element(D_MODEL)), lambda j: (g_lr, 0))],
        out_specs=[pl.BlockSpec((D_MODEL, _WIN_TILE), lambda j: (0, j)),
                   pl.BlockSpec((D_MODEL, LANES), lambda j: (0, 0))],
        out_shape=[jax.ShapeDtypeStruct((D_MODEL, Z_MAIN), BF16), jax.ShapeDtypeStruct((D_MODEL, LANES), BF16)],
        compiler_params=_params("arbitrary"),
        name="win_prep",
    )(w_t, w_t, w_t)


Z_MLA = Q_LORA + KV_LORA
Z_GLA = Z_MAIN - Z_MLA
PROJ_TM = 512


def _inproj(x_ref, g_ref, wm_ref, ws_ref, zg_ref, zs_ref):
    xn = _rms(x_ref[...], g_ref[...]).astype(BF16)
    zg_ref[...] = _dot(xn, wm_ref[:, Z_MLA:])
    sm = _dot(xn, ws_ref[...])
    zs_ref[...] = sm
    z_mla = _dot(xn, wm_ref[:, :Z_MLA])
    return z_mla[:, :Q_LORA], z_mla[:, Q_LORA:], sm


def _inproj_specs(tm):
    row = lambda i: (i, 0)
    const = lambda i: (0, 0)
    in_specs = [pl.BlockSpec((tm, D_MODEL), row),
                pl.BlockSpec((1, D_MODEL), const),
                pl.BlockSpec((D_MODEL, Z_MAIN), const, pipeline_mode=pl.Buffered(1)),
                pl.BlockSpec((D_MODEL, LANES), const, pipeline_mode=pl.Buffered(1)),
                pl.BlockSpec((1, Q_LORA), const),
                pl.BlockSpec((1, KV_LORA), const),
                pl.BlockSpec((Q_LORA, MLA_HEADS * QK_PAD), const, pipeline_mode=pl.Buffered(1))]
    out_specs = [pl.BlockSpec((tm, Z_GLA), row), pl.BlockSpec((tm, LANES), row),
                 pl.BlockSpec((tm, KV_LORA), row), pl.BlockSpec((tm, ROPE_DIM), row)]
    out_shape = lambda T: [jax.ShapeDtypeStruct((T, Z_GLA), F32), jax.ShapeDtypeStruct((T, LANES), F32),
                           jax.ShapeDtypeStruct((T, KV_LORA), F32), jax.ShapeDtypeStruct((T, ROPE_DIM), F32)]
    return in_specs, out_specs, out_shape


def _mla_q(cq, gq_ref, wuq_ref, cos, sin):
    cqn = _rms(cq, gq_ref[...]).astype(BF16)
    q = _dot(cqn, wuq_ref[...]) * QK_SCALE
    out = []
    for h in range(MLA_HEADS):
        nope = q[:, h * QK_PAD:h * QK_PAD + NOPE_DIM]
        rp = _rope(q[:, h * QK_PAD + NOPE_DIM:(h + 1) * QK_PAD], cos, sin)
        out.append((nope, rp))
    return out


def _proj_prompt_kernel(x_ref, g_ref, wm_ref, ws_ref, gq_ref, gkv_ref, wuq_ref, wukt_ref, wuv_ref, cos_ref, sin_ref,
                        zg_ref, zs_ref, ckvn_ref, kr_ref, qcat_ref, kt_ref, v_ref):
    cq, ckv, sm = _inproj(x_ref, g_ref, wm_ref, ws_ref, zg_ref, zs_ref)
    cos, sin = cos_ref[...], sin_ref[...]
    for h, (nope, rp) in enumerate(_mla_q(cq, gq_ref, wuq_ref, cos, sin)):
        qcat_ref[:, h * QK_PAD:h * QK_PAD + NOPE_DIM] = nope.astype(BF16)
        qcat_ref[:, h * QK_PAD + NOPE_DIM:(h + 1) * QK_PAD] = rp.astype(BF16)
    ckvn = _rms(ckv, gkv_ref[...])
    ckvn_ref[...] = ckvn
    kr = _rope(sm, cos, sin)
    kr_ref[...] = kr[:, :ROPE_DIM]
    ckvn_b = ckvn.astype(BF16)
    k_t = _dot_nt(wukt_ref[...], ckvn_b)
    kr_t = kr.T.astype(BF16)
    for h in range(MLA_HEADS):
        kt_ref[h * QK_PAD:h * QK_PAD + NOPE_DIM, :] = k_t[h * NOPE_DIM:(h + 1) * NOPE_DIM].astype(BF16)
        kt_ref[h * QK_PAD + NOPE_DIM:(h + 1) * QK_PAD, :] = kr_t
    v_ref[...] = _dot(ckvn_b, wuv_ref[...]).astype(BF16)


def _proj_prompt(x, w, cos_t, sin_t, seq):
    T = x.shape[0]
    tm = PROJ_TM
    nseq = seq // tm
    row = lambda i: (i, 0)
    const = lambda i: (0, 0)
    in_specs, out_specs, out_shape = _inproj_specs(tm)
    return pl.pallas_call(
        _proj_prompt_kernel,
        grid=(T // tm,),
        in_specs=in_specs + [pl.BlockSpec((MLA_WIDTH, KV_LORA), const, pipeline_mode=pl.Buffered(1)),
                             pl.BlockSpec((KV_LORA, MLA_WIDTH), const, pipeline_mode=pl.Buffered(1)),
                             pl.BlockSpec((tm, LANES), lambda i: (i % nseq, 0)),
                             pl.BlockSpec((tm, LANES), lambda i: (i % nseq, 0))],
        out_specs=out_specs + [pl.BlockSpec((tm, MLA_HEADS * QK_PAD), row),
                               pl.BlockSpec((MLA_HEADS * QK_PAD, tm), lambda i: (0, i)),
                               pl.BlockSpec((tm, MLA_WIDTH), row)],
        out_shape=out_shape(T) + [jax.ShapeDtypeStruct((T, MLA_HEADS * QK_PAD), BF16),
                                  jax.ShapeDtypeStruct((MLA_HEADS * QK_PAD, T), BF16),
                                  jax.ShapeDtypeStruct((T, MLA_WIDTH), BF16)],
        compiler_params=_params("parallel", vmem=VMEM_LIMIT_PROJ),
        name="proj_prompt",
    )(x, w["g_pre_mix"], *w["w_in"], w["g_q"], w["g_kv"], w["w_uq"],
      w["w_ukt"].reshape(MLA_WIDTH, KV_LORA), w["w_uv"], cos_t, sin_t)


def _proj_sample_kernel(x_ref, g_ref, wm_ref, ws_ref, gq_ref, gkv_ref, wuq_ref, wukt_ref, cos_ref, sin_ref,
                        zg_ref, zs_ref, ckvn_ref, kr_ref, qlat_ref, qr_ref):
    cq, ckv, sm = _inproj(x_ref, g_ref, wm_ref, ws_ref, zg_ref, zs_ref)
    cos, sin = cos_ref[...], sin_ref[...]
    for h, (nope, rp) in enumerate(_mla_q(cq, gq_ref, wuq_ref, cos, sin)):
        qlat_ref[h] = _dot(nope.astype(BF16), wukt_ref[h]).astype(BF16)
        qr_ref[h] = rp[:, :ROPE_DIM].astype(BF16)
    ckvn_ref[...] = _rms(ckv, gkv_ref[...])
    kr_ref[...] = _rope(sm, cos, sin)[:, :ROPE_DIM]


def _proj_sample(x, w, cos_t, sin_t):
    T = x.shape[0]
    tm = PROJ_TM
    row = lambda i: (i, 0)
    in_specs, out_specs, out_shape = _inproj_specs(tm)
    return pl.pallas_call(
        _proj_sample_kernel,
        grid=(T // tm,),
        in_specs=in_specs + [pl.BlockSpec((MLA_HEADS, NOPE_DIM, KV_LORA), lambda i: (0, 0, 0)),
                             pl.BlockSpec((tm, LANES), row),
                             pl.BlockSpec((tm, LANES), row)],
        out_specs=out_specs + [pl.BlockSpec((MLA_HEADS, tm, KV_LORA), lambda i: (0, i, 0)),
                               pl.BlockSpec((MLA_HEADS, tm, ROPE_DIM), lambda i: (0, i, 0))],
        out_shape=out_shape(T) + [jax.ShapeDtypeStruct((MLA_HEADS, T, KV_LORA), BF16),
                                  jax.ShapeDtypeStruct((MLA_HEADS, T, ROPE_DIM), BF16)],
        compiler_params=_params("parallel"),
        name="proj_sample",
    )(x, w["g_pre_mix"], *w["w_in"], w["g_q"], w["g_kv"], w["w_uq"], w["w_ukt"], cos_t, sin_t)


ATT_TQ = 512
ATT_HEADS = 4


def _attn_prompt_kernel(q_ref, kt_ref, v_ref, o_ref):
    seq = q_ref.shape[0]
    qc = lax.broadcasted_iota(jnp.int32, (ATT_TQ, ATT_TQ), 0) // CHUNK
    kc = lax.broadcasted_iota(jnp.int32, (ATT_TQ, ATT_TQ), 1) // CHUNK
    visible = kc <= qc
    for i in reversed(range(seq // ATT_TQ)):
        lo, hi = i * ATT_TQ, (i + 1) * ATT_TQ
        for h in range(ATT_HEADS):
            qk = slice(h * QK_PAD, (h + 1) * QK_PAD)
            dv = slice(h * V_DIM, (h + 1) * V_DIM)
            s = _dot(q_ref[lo:hi, qk], kt_ref[qk, 0:hi])
            s_diag = jnp.where(visible, s[:, lo:hi], NEG_BIG)
            s = jnp.concatenate([s[:, :lo], s_diag], axis=1) if i else s_diag
            m = jnp.max(s, axis=-1, keepdims=True)
            p = jnp.exp2(s - m).astype(BF16)
            v_one = jnp.concatenate([v_ref[0:hi, dv], jnp.ones((hi, V_DIM), BF16)], axis=1)
            ol = _dot(p, v_one)
            o_ref[lo:hi, dv] = (ol[:, :V_DIM] / ol[:, V_DIM:]).astype(BF16)


def _attn_prompt(qcat, kt, v, batch, seq):
    return pl.pallas_call(
        _attn_prompt_kernel,
        grid=(batch, MLA_HEADS // ATT_HEADS),
        in_specs=[pl.BlockSpec((seq, ATT_HEADS * QK_PAD), lambda b, h: (b, h)),
                  pl.BlockSpec((ATT_HEADS * QK_PAD, seq), lambda b, h: (h, b)),
                  pl.BlockSpec((seq, ATT_HEADS * V_DIM), lambda b, h: (b, h))],
        out_specs=pl.BlockSpec((seq, ATT_HEADS * V_DIM), lambda b, h: (b, h)),
        out_shape=jax.ShapeDtypeStruct((batch * seq, MLA_WIDTH), BF16),
        compiler_params=_params("parallel", "parallel"),
        name="attn_prompt",
    )(qcat, kt, v)


DEC_SEQS = 2


def _attn_sample_kernel(ql_ref, qr_ref, ckv_ref, krt_ref, nckv_ref, nkr_ref, wuv_ref, o_ref):
    nb = ckv_ref.shape[0]
    dec = ql_ref.shape[1] // nb
    rows = MLA_HEADS * dec
    for b in range(nb):
        tok = slice(b * dec, (b + 1) * dec)
        ql = ql_ref[:, tok, :].reshape(rows, KV_LORA)
        qr = qr_ref[:, tok, :].reshape(rows, ROPE_DIM)
        ckv = ckv_ref[b].astype(BF16)
        nckv = nckv_ref[tok, :].astype(BF16)
        s = _dot_nt(ql, ckv) + _dot(qr, krt_ref[b].astype(BF16))
        s_new = _dot_nt(ql, nckv) + _dot_nt(qr, nkr_ref[tok, :].astype(BF16))
        m = jnp.maximum(jnp.max(s, axis=-1, keepdims=True), jnp.max(s_new, axis=-1, keepdims=True))
        p = jnp.exp2(s - m)
        p_new = jnp.exp2(s_new - m)
        l = jnp.sum(p, axis=-1, keepdims=True) + jnp.sum(p_new, axis=-1, keepdims=True)
        o = ((_dot(p.astype(BF16), ckv) + _dot(p_new.astype(BF16), nckv)) / l).astype(BF16)
        for h in range(MLA_HEADS):
            o_ref[tok, h * V_DIM:(h + 1) * V_DIM] = _dot(o[h * dec:(h + 1) * dec], wuv_ref[h]).astype(BF16)


def _attn_sample(q_lat, q_rope, cache_ckv, cache_krt, ckvn, kr, w_uvh, dec):
    batch, past, _ = cache_ckv.shape
    nb = DEC_SEQS
    return pl.pallas_call(
        _attn_sample_kernel,
        grid=(batch // nb,),
        in_specs=[pl.BlockSpec((MLA_HEADS, nb * dec, KV_LORA), lambda b: (0, b, 0)),
                  pl.BlockSpec((MLA_HEADS, nb * dec, ROPE_DIM), lambda b: (0, b, 0)),
                  pl.BlockSpec((nb, past, KV_LORA), lambda b: (b, 0, 0)),
                  pl.BlockSpec((nb, ROPE_DIM, past), lambda b: (b, 0, 0)),
                  pl.BlockSpec((nb * dec, KV_LORA), lambda b: (b, 0)),
                  pl.BlockSpec((nb * dec, ROPE_DIM), lambda b: (b, 0)),
                  pl.BlockSpec((MLA_HEADS, KV_LORA, V_DIM), lambda b: (0, 0, 0))],
        out_specs=pl.BlockSpec((nb * dec, MLA_WIDTH), lambda b: (b, 0)),
        out_shape=jax.ShapeDtypeStruct((batch * dec, MLA_WIDTH), BF16),
        compiler_params=_params("parallel"),
        name="attn_sample",
    )(q_lat, q_rope, cache_ckv, cache_krt, ckvn, kr, w_uvh)


GLA_CHUNK = 256
GLA_DEC_ROWS = 128
GLA_SEQS = 2


def _gla_tables(c, sub):
    nlev = int(np.log2(sub))
    assert 1 << nlev == sub and c % sub == 0 and c // sub <= LANES
    t = np.arange(c)[:, None]
    u = np.arange(c)[None, :]
    same_sub = (t // sub) == (u // sub)
    blocks = []
    lvl = np.full((c, c), -1, np.int32)
    for l in range(nlev):
        width = sub >> l
        half = width // 2
        m = (t // width) * width + half - 1
        upper = t > m
        blocks.append(np.where(upper, (u > m) & (u <= t), (u > t) & (u <= m)))
        same = (t // width) == (u // width)
        lvl[same & ((t % width) >= half) & ((u % width) < half)] = l
    blocks.append((u > t) & same_sub)
    blocks.append((u <= t) & same_sub)
    lvl[np.arange(c), np.arange(c)] = nlev
    sel = (np.arange(c)[:, None] // sub) == np.arange(LANES)[None, :]
    return np.concatenate(blocks, 0).astype(np.float32), lvl, sel.astype(np.float32), nlev


def _gla_kernel(*refs, c, sub, nlev, carry):
    if carry:
        (q_ref, k_ref, v_ref, r_ref, sm_ref, wga_ref, bga_ref, gg_ref, p_ref, lvl_ref, sel_ref,
         o_ref, sfin_ref, s_scr) = refs
        assert sub == c
        j = pl.program_id(1)

        @pl.when(j == 0)
        def _():
            s_scr[...] = jnp.zeros(s_scr.shape, F32)
    else:
        (q_ref, k_ref, v_ref, r_ref, sm_ref, wga_ref, bga_ref, gg_ref, p_ref, lvl_ref, sel_ref, s0_ref,
         o_ref, sfin_ref) = refs

    p_mat = p_ref[...]
    sel = sel_ref[...]
    lvl = lvl_ref[...]
    groups = range(q_ref.shape[0])
    split = []
    for g in groups:
        x = _dot(sm_ref[g].astype(BF16), wga_ref[...]) + bga_ref[...]
        la = (jnp.minimum(x, 0.0) - jnp.log1p(jnp.exp(-jnp.abs(x)))) * (1.0 / GATE_TAU)
        hi = la.astype(BF16)
        split.append((hi, (la - hi.astype(F32)).astype(BF16)))
    for g in groups:
        hi, mid = split[g]
        e_all = jnp.exp(_dot(p_mat, hi) + _dot(p_mat, mid))
        d_all = jnp.exp(_dot_tn(hi, sel) + _dot_tn(mid, sel))
        q = q_ref[g] * (GLA_DK ** -0.5)
        k = k_ref[g]
        qe = [(q * e_all[l * c:(l + 1) * c]).astype(BF16) for l in range(nlev)] + [q.astype(BF16)]
        ke = [(k * e_all[l * c:(l + 1) * c]).astype(BF16) for l in range(nlev)] + [k.astype(BF16)]
        k_end = (k * e_all[nlev * c:(nlev + 1) * c]).astype(BF16)
        q_beg = (q * e_all[(nlev + 1) * c:(nlev + 2) * c]).astype(BF16)
        for h in range(GLA_HEADS):
            dk = slice(h * GLA_DK, (h + 1) * GLA_DK)
            dv = slice(h * GLA_DV, (h + 1) * GLA_DV)
            v = v_ref[g, :, dv].astype(BF16)
            if sub == c:
                half = c // 2
                top, bot = slice(0, half), slice(half, c)
                lvl_q = lvl[top, top]
                a_bl = _dot_nt(qe[0][bot, dk], ke[0][top, dk])
                a_t = jnp.zeros((half, half), F32)
                a_b = jnp.zeros((half, half), F32)
                for l in range(1, nlev + 1):
                    a_t = jnp.where(lvl_q == l, _dot_nt(qe[l][top, dk], ke[l][top, dk]), a_t)
                    a_b = jnp.where(lvl_q == l, _dot_nt(qe[l][bot, dk], ke[l][bot, dk]), a_b)
                o_intra = jnp.concatenate(
                    [_dot(a_t.astype(BF16), v[top]),
                     _dot(jnp.concatenate([a_bl, a_b], axis=1).astype(BF16), v)], axis=0)
            else:
                attn = jnp.zeros((c, c), F32)
                for l in range(nlev + 1):
                    attn = jnp.where(lvl == l, _dot_nt(qe[l][:, dk], ke[l][:, dk]), attn)
                o_intra = _dot(attn.astype(BF16), v)
            for n in range(c // sub):
                rows = slice(n * sub, (n + 1) * sub)
                state = s_scr[g, h] if carry else s0_ref[n, h]
                o = o_intra[rows] + _dot(q_beg[rows, dk], state.astype(BF16))
                new_state = d_all[dk, n:n + 1] * state + _dot_tn(k_end[rows, dk], v[rows])
                if carry:
                    s_scr[g, h] = new_state
                else:
                    sfin_ref[n, h] = new_state
                r = r_ref[g, rows, dv]
                o_ref[g, rows, dv] = (_rms(o, gg_ref[...]) * (r * jax.nn.sigmoid(r))).astype(BF16)

    if carry:
        @pl.when(j == pl.num_programs(1) - 1)
        def _():
            sfin_ref[...] = s_scr[...]


def _gla(z, zs, w_ga, b_ga, g_gla, batch, seq, c, s0=None):
    carry = s0 is None
    sub = c if carry else seq
    p_np, lvl_np, sel_np, nlev = _gla_tables(c, sub)
    if carry:
        nb, lead, grid = GLA_SEQS, batch, (batch // GLA_SEQS, seq // c)
        at = lambda col: (lambda g, j: (g, j, col))
    else:
        nb, lead, grid = 1, 1, (batch * seq // c, 1)
        at = lambda col: (lambda g, j: (0, g, col))
    nstate = nb if carry else c // sub
    z, zs = z.reshape(lead, -1, Z_GLA), zs.reshape(lead, -1, LANES)
    const = lambda g, j: (0, 0)
    in_specs = [pl.BlockSpec((nb, c, GLA_HEADS * GLA_DK), at(0)),
                pl.BlockSpec((nb, c, GLA_HEADS * GLA_DK), at(1)),
                pl.BlockSpec((nb, c, GLA_WIDTH), at(1)),
                pl.BlockSpec((nb, c, GLA_WIDTH), at(2)),
                pl.BlockSpec((nb, c, LANES), at(0)),
                pl.BlockSpec((LANES, GLA_HEADS * GLA_DK), const),
                pl.BlockSpec((1, GLA_HEADS * GLA_DK), const),
                pl.BlockSpec((1, GLA_DV), const),
                pl.BlockSpec(p_np.shape, const),
                pl.BlockSpec((c, c), const),
                pl.BlockSpec((c, LANES), const)]
    args = [z, z, z, z, zs, w_ga, b_ga, g_gla,
            jnp.asarray(p_np, BF16), jnp.asarray(lvl_np), jnp.asarray(sel_np, BF16)]
    state_spec = pl.BlockSpec((nstate, GLA_HEADS, GLA_DK, GLA_DV), lambda g, j: (g, 0, 0, 0))
    if not carry:
        in_specs.append(state_spec)
        args.append(s0)
    o, s_fin = pl.pallas_call(
        functools.partial(_gla_kernel, c=c, sub=sub, nlev=nlev, carry=carry),
        grid=grid,
        in_specs=in_specs,
        out_specs=[pl.BlockSpec((nb, c, GLA_WIDTH), at(0)), state_spec],
        out_shape=[jax.ShapeDtypeStruct((lead, batch * seq // lead, GLA_WIDTH), BF16),
                   jax.ShapeDtypeStruct((batch, GLA_HEADS, GLA_DK, GLA_DV), F32)],
        scratch_shapes=[pltpu.VMEM((nb, GLA_HEADS, GLA_DK, GLA_DV), F32)] if carry else [],
        compiler_params=_params("parallel", "arbitrary"),
        name="gla" if carry else "gla_init",
    )(*args)
    return o.reshape(batch * seq, GLA_WIDTH), s_fin


ROW_SUB = 128


def _mix_kernel(oma_ref, omb_ref, oga_ref, ogb_ref, w1_ref, w2_ref, ha_ref, hb_ref, g1_ref, g2_ref, h1_ref, f_ref,
                *, na):
    def block(om_ref, og_ref, h_ref):
        for r in range(0, h_ref.shape[0], ROW_SUB):
            rows = slice(r, r + ROW_SUB)
            mix = _dot(om_ref[rows, :], w1_ref[...]) + _dot(og_ref[rows, :], w2_ref[...])
            h1 = h_ref[rows, :] + _rms(mix, g1_ref[...])
            h1_ref[rows, :] = h1
            f_ref[rows, :] = _rms(h1, g2_ref[...]).astype(BF16)

    pl.when(pl.program_id(0) < na)(lambda: block(oma_ref, oga_ref, ha_ref))
    pl.when(pl.program_id(0) >= na)(lambda: block(omb_ref, ogb_ref, hb_ref))


def _mix(o_mla, o_gla, h, w_out, g_post_mix, g_pre_ffn):
    tm = 512
    na, nb = h[0].shape[0] // tm, h[1].shape[0] // tm
    T = (na + nb) * tm
    row = lambda i: (i, 0)
    const = lambda i: (0, 0)
    at_a = lambda i: (jnp.minimum(i, na - 1), 0)
    at_b = lambda i: (jnp.maximum(i - na, 0), 0)
    return pl.pallas_call(
        functools.partial(_mix_kernel, na=na),
        grid=(na + nb,),
        in_specs=[pl.BlockSpec((tm, MLA_WIDTH), at_a), pl.BlockSpec((tm, MLA_WIDTH), at_b),
                  pl.BlockSpec((tm, GLA_WIDTH), at_a), pl.BlockSpec((tm, GLA_WIDTH), at_b),
                  pl.BlockSpec((MLA_WIDTH, D_MODEL), lambda i: (0, 0), pipeline_mode=pl.Buffered(1)),
                  pl.BlockSpec((GLA_WIDTH, D_MODEL), lambda i: (1, 0), pipeline_mode=pl.Buffered(1)),
                  pl.BlockSpec((tm, D_MODEL), at_a), pl.BlockSpec((tm, D_MODEL), at_b),
                  pl.BlockSpec((1, D_MODEL), const),
                  pl.BlockSpec((1, D_MODEL), const)],
        out_specs=[pl.BlockSpec((tm, D_MODEL), row), pl.BlockSpec((tm, D_MODEL), row)],
        out_shape=[jax.ShapeDtypeStruct((T, D_MODEL), F32), jax.ShapeDtypeStruct((T, D_MODEL), BF16)],
        compiler_params=_params("arbitrary"),
        name="mix",
    )(o_mla[0], o_mla[1], o_gla[0], o_gla[1], w_out, w_out, h[0], h[1], g_post_mix, g_pre_ffn)


FFN_MAX_ROWS = 1088


def _ffn_kernel(x_ref, wg_ref, wu_ref, wd_ref, o_ref):
    def partial_out():
        x = x_ref[...]
        g = _dot(x, wg_ref[...].astype(BF16))
        u = _dot(x, wu_ref[...].astype(BF16))
        return _dot((g * jax.nn.sigmoid(g) * u).astype(BF16), wd_ref[...].astype(BF16))

    @pl.when(pl.program_id(1) == 0)
    def _():
        o_ref[...] = partial_out()

    @pl.when(pl.program_id(1) > 0)
    def _():
        o_ref[...] += partial_out()


def _ffn(x, w_gate, w_up, w_down):
    T = x.shape[0]
    tf = 512
    tm = next(T // n for n in range(1, T + 1) if T % n == 0 and T // n <= FFN_MAX_ROWS and (T // n) % 16 == 0)
    return pl.pallas_call(
        _ffn_kernel,
        grid=(T // tm, D_FF // tf),
        in_specs=[pl.BlockSpec((tm, D_MODEL), lambda i, j: (i, 0)),
                  pl.BlockSpec((D_MODEL, tf), lambda i, j: (0, j)),
                  pl.BlockSpec((D_MODEL, tf), lambda i, j: (0, j)),
                  pl.BlockSpec((tf, D_MODEL), lambda i, j: (j, 0))],
        out_specs=pl.BlockSpec((tm, D_MODEL), lambda i, j: (i, 0)),
        out_shape=jax.ShapeDtypeStruct((T, D_MODEL), F32),
        compiler_params=_params("parallel", "arbitrary"),
        name="ffn",
    )(x, w_gate, w_up, w_down)


def _final_kernel(f_ref, h1_ref, pa_ref, pb_ref, g_ref, wpg_ref, wp_ref, ya_ref, yb_ref, *, na):
    def block(p_ref, y_ref):
        h2 = h1_ref[...] + _rms(f_ref[...], g_ref[...])
        gate = jax.nn.sigmoid(_dot(h2.astype(BF16), wpg_ref[...]))
        y_ref[...] = h2 + gate * _dot(p_ref[...].astype(BF16), wp_ref[...])

    pl.when(pl.program_id(0) < na)(lambda: block(pa_ref, ya_ref))
    pl.when(pl.program_id(0) >= na)(lambda: block(pb_ref, yb_ref))


def _final(f, h1, p, g_post_ffn, w_ple_gate, w_ple):
    tm = 512
    na, nb = p[0].shape[0] // tm, p[1].shape[0] // tm
    row = lambda i: (i, 0)
    const = lambda i: (0, 0)
    at_a = lambda i: (jnp.minimum(i, na - 1), 0)
    at_b = lambda i: (jnp.maximum(i - na, 0), 0)
    return pl.pallas_call(
        functools.partial(_final_kernel, na=na),
        grid=(na + nb,),
        in_specs=[pl.BlockSpec((tm, D_MODEL), row),
                  pl.BlockSpec((tm, D_MODEL), row),
                  pl.BlockSpec((tm, PLE_DIM), at_a), pl.BlockSpec((tm, PLE_DIM), at_b),
                  pl.BlockSpec((1, D_MODEL), const),
                  pl.BlockSpec((D_MODEL, D_MODEL), const, pipeline_mode=pl.Buffered(1)),
                  pl.BlockSpec((PLE_DIM, D_MODEL), const)],
        out_specs=[pl.BlockSpec((tm, D_MODEL), at_a), pl.BlockSpec((tm, D_MODEL), at_b)],
        out_shape=[jax.ShapeDtypeStruct((na * tm, D_MODEL), F32), jax.ShapeDtypeStruct((nb * tm, D_MODEL), F32)],
        compiler_params=_params("arbitrary"),
        name="final",
    )(f, h1, p[0], p[1], g_post_ffn, w_ple_gate, w_ple)


def _rope_tables(pos):
    half = ROPE_DIM // 2
    inv = 1.0 / (ROPE_THETA ** (jnp.arange(half, dtype=F32) / half))
    ang = pos.astype(F32)[:, None] * inv[None, :]
    cos, sin = jnp.cos(ang), jnp.sin(ang)
    zero = jnp.zeros((pos.shape[0], LANES - ROPE_DIM), F32)
    return jnp.concatenate([cos, cos, zero], axis=1), jnp.concatenate([-sin, sin, zero], axis=1)


def _layer_weights(i, g_pre_mix, w_in, g_q, w_uq, w_uk, g_kv, w_ga, b_ga, w_uv, g_gla, w_out, g_post_mix,
                   g_pre_ffn, w_gate, w_up, w_down, g_post_ffn, w_ple, w_ple_gate):
    w_uq_p = jnp.pad(w_uq[i].reshape(Q_LORA, MLA_HEADS, NOPE_DIM + ROPE_DIM),
                     ((0, 0), (0, 0), (0, QK_PAD - NOPE_DIM - ROPE_DIM))).reshape(Q_LORA, MLA_HEADS * QK_PAD)
    w_ga_p = jnp.zeros((LANES, GLA_HEADS * GLA_DK), F32).at[GLR_LO:GLR_LO + GATE_RANK].set(w_ga[i])
    vec = lambda g: g[i].reshape(1, -1)
    return dict(
        g_pre_mix=vec(g_pre_mix), w_in=_win_prep(jnp.swapaxes(w_in[i], 0, 1)), g_q=vec(g_q), g_kv=vec(g_kv),
        w_uq=w_uq_p.astype(BF16),
        w_uv=w_uv[i].reshape(KV_LORA, MLA_WIDTH).astype(BF16),
        w_ukt=jnp.transpose(w_uk[i], (1, 2, 0)).astype(BF16),
        w_uvh=jnp.transpose(w_uv[i], (1, 0, 2)).astype(BF16),
        w_ga=w_ga_p.astype(BF16), b_ga=vec(b_ga), g_gla=vec(g_gla),
        w_out=w_out[i].astype(BF16), g_post_mix=vec(g_post_mix), g_pre_ffn=vec(g_pre_ffn),
        w_gate=w_gate[i], w_up=w_up[i], w_down=w_down[i],
        g_post_ffn=vec(g_post_ffn), w_ple=w_ple[i].astype(BF16), w_ple_gate=w_ple_gate[i].astype(BF16))


def _finish(w, h, p, o_mla, o_gla):
    h1, f_in = _mix(o_mla, o_gla, h, w["w_out"], w["g_post_mix"], w["g_pre_ffn"])
    f = _ffn(f_in, w["w_gate"], w["w_up"], w["w_down"])
    return _final(f, h1, p, w["g_post_ffn"], w["w_ple_gate"], w["w_ple"])


def kernel(x_prompt, x_sample, cache_ckv, cache_krope, state_gla, p_prompt, p_sample, g_pre_mix, w_in, g_q, w_uq,
           w_uk, g_kv, w_ga, b_ga, w_uv, g_gla, w_out, g_post_mix, g_pre_ffn, w_gate, w_up, w_down, g_post_ffn,
           w_ple, w_ple_gate):
    batch, seq, _ = x_prompt.shape
    dbatch, dseq, _ = x_sample.shape
    depth = w_in.shape[0]
    cos_p, sin_p = _rope_tables(jnp.arange(seq))
    cos_s, sin_s = _rope_tables(PAST_LEN + jnp.arange(dseq))
    cos_s, sin_s = jnp.tile(cos_s, (dbatch, 1)), jnp.tile(sin_s, (dbatch, 1))
    h_p = x_prompt.reshape(batch * seq, D_MODEL)
    h_s = x_sample.reshape(dbatch * dseq, D_MODEL)
    outs = [[] for _ in range(6)]
    for i in range(depth):
        w = _layer_weights(i, g_pre_mix, w_in, g_q, w_uq, w_uk, g_kv, w_ga, b_ga, w_uv, g_gla, w_out, g_post_mix,
                           g_pre_ffn, w_gate, w_up, w_down, g_post_ffn, w_ple, w_ple_gate)
        z, zs, ckvn, kr, qcat, kt, v = _proj_prompt(h_p, w, cos_p, sin_p, seq)
        o_mla_p = _attn_prompt(qcat, kt, v, batch, seq)
        o_gla_p, s_fin = _gla(z, zs, w["w_ga"], w["b_ga"], w["g_gla"], batch, seq, GLA_CHUNK)
        outs[0].append(ckvn.reshape(batch, seq, KV_LORA))
        outs[1].append(kr.reshape(batch, seq, ROPE_DIM))
        outs[2].append(s_fin)
        z, zs, ckvn, kr, q_lat, q_rope = _proj_sample(h_s, w, cos_s, sin_s)
        o_mla_s = _attn_sample(q_lat, q_rope, cache_ckv[i], jnp.swapaxes(cache_krope[i], 1, 2), ckvn, kr,
                               w["w_uvh"], dseq)
        o_gla_s, s_new = _gla(z, zs, w["w_ga"], w["b_ga"], w["g_gla"], dbatch, dseq, GLA_DEC_ROWS, s0=state_gla[i])
        outs[3].append(ckvn.reshape(dbatch, dseq, KV_LORA))
        outs[4].append(kr.reshape(dbatch, dseq, ROPE_DIM))
        outs[5].append(s_new)
        h_p, h_s = _finish(w, (h_p, h_s),
                           (p_prompt[i].reshape(batch * seq, PLE_DIM), p_sample[i].reshape(dbatch * dseq, PLE_DIM)),
                           (o_mla_p, o_mla_s), (o_gla_p, o_gla_s))
    return (h_p.reshape(batch, seq, D_MODEL), h_s.reshape(dbatch, dseq, D_MODEL),
            jnp.stack(outs[0]), jnp.stack(outs[1]), jnp.stack(outs[2]),
            jnp.stack(outs[3]), jnp.stack(outs[4]), jnp.stack(outs[5]))
```

```python
import functools

import numpy as np
import jax
import jax.numpy as jnp
from jax import lax
from jax.experimental import pallas as pl
from jax.experimental.pallas import tpu as pltpu

F32 = jnp.float32
BF16 = jnp.bfloat16

D_MODEL = 2048
PAST_LEN = 4096
CHUNK = 64
EPS = 1e-6

MLA_HEADS = 8
Q_LORA = 512
KV_LORA = 512
NOPE_DIM = 128
ROPE_DIM = 64
V_DIM = 128
ROPE_THETA = 10000.0
MLA_SCALE = (NOPE_DIM + ROPE_DIM) ** -0.5
QK_PAD = 256

GLA_HEADS = 4
GLA_DK = 128
GLA_DV = 256
GATE_RANK = 16
GATE_TAU = 16.0
GLA_WIDTH = GLA_HEADS * GLA_DV
MLA_WIDTH = MLA_HEADS * V_DIM

IN_SPLITS = (Q_LORA, KV_LORA, ROPE_DIM, GLA_HEADS * GLA_DK, GLA_HEADS * GLA_DK, GLA_WIDTH, GATE_RANK, GLA_WIDTH)
Z_MAIN = 4096
GLR_LO = ROPE_DIM

D_FF = 5632
PLE_DIM = 256

LANES = 128
VMEM_BYTES = 64 * 1024 * 1024
VMEM_LIMIT = VMEM_BYTES - 8 * 1024 * 1024
VMEM_LIMIT_PROJ = VMEM_BYTES - 4 * 1024 * 1024
NEG_BIG = -1e30
LOG2E = 1.4426950408889634
QK_SCALE = MLA_SCALE * LOG2E


def _dot(a, b):
    return jnp.dot(a, b, preferred_element_type=F32)


def _dot_nt(a, b):
    return lax.dot_general(a, b, (((1,), (1,)), ((), ())), preferred_element_type=F32)


def _dot_tn(a, b):
    return lax.dot_general(a, b, (((0,), (0,)), ((), ())), preferred_element_type=F32)


def _rms(x, g):
    ms = jnp.mean(x * x, axis=-1, keepdims=True)
    return x * lax.rsqrt(ms + EPS) * g


def _rope(x, cos_t, sin_t):
    lane = lax.broadcasted_iota(jnp.int32, x.shape, 1)
    rot = jnp.where((lane & 32) == 0, pltpu.roll(x, 96, 1), pltpu.roll(x, 32, 1))
    return x * cos_t + rot * sin_t


def _params(*sem, vmem=VMEM_LIMIT):
    return pltpu.CompilerParams(dimension_semantics=sem, vmem_limit_bytes=vmem)


_IN_OFF = np.concatenate([[0], np.cumsum(IN_SPLITS)]).tolist()
_WIN_TILE = 512


def _win_prep_kernel(w_ref, kr_ref, glr_ref, o_ref, small_ref):
    o_ref[...] = w_ref[...].T.astype(BF16)

    @pl.when(pl.program_id(0) == 0)
    def _():
        pad = jnp.zeros((LANES - GLR_LO - GATE_RANK, D_MODEL), F32)
        small_ref[...] = jnp.concatenate([kr_ref[...], glr_ref[...], pad], axis=0).T.astype(BF16)


def _win_prep(w_t):
    c_q, c_kv, k_r, q_g, k_g, v_g, g_lr, r_g, end = _IN_OFF
    n_head, n_mid = k_r // _WIN_TILE, (g_lr - q_g) // _WIN_TILE
    assert k_r % _WIN_TILE == 0 and (g_lr - q_g) % _WIN_TILE == 0 and (end - r_g) % _WIN_TILE == 0
    assert (n_head + n_mid) * _WIN_TILE + end - r_g == Z_MAIN and GLR_LO == ROPE_DIM

    def src_row(j):
        skip_mid, skip_tail = (q_g - k_r) // 8, (r_g - g_lr + q_g - k_r) // 8
        return 8 * (j * (_WIN_TILE // 8) + jnp.where(j < n_head, 0, jnp.where(j < n_head + n_mid, skip_mid, skip_tail)))

    return pl.pallas_call(
        _win_prep_kernel,
        grid=(Z_MAIN // _WIN_TILE,),
        in_specs=[pl.BlockSpec((pl.Element(_WIN_TILE), pl.Element(D_MODEL)), lambda j: (src_row(j), 0)),
                  pl.BlockSpec((pl.Element(ROPE_DIM), pl.Element(D_MODEL)), lambda j: (k_r, 0)),
                  pl.BlockSpec((pl.Element(GATE_RANK), pl.Element(D_MODEL)), lambda j: (g_lr, 0))],
        out_specs=[pl.BlockSpec((D_MODEL, _WIN_TILE), lambda j: (0, j)),
                   pl.BlockSpec((D_MODEL, LANES), lambda j: (0, 0))],
        out_shape=[jax.ShapeDtypeStruct((D_MODEL, Z_MAIN), BF16), jax.ShapeDtypeStruct((D_MODEL, LANES), BF16)],
        compiler_params=_params("arbitrary"),
        name="win_prep",
    )(w_t, w_t, w_t)


Z_MLA = Q_LORA + KV_LORA
Z_GLA = Z_MAIN - Z_MLA
PROJ_TM = 512
PROJ_SUB = 256


def _inproj(rows, x_ref, g_ref, wm_ref, ws_ref, zg_ref, zs_ref):
    xn = _rms(x_ref[rows, :], g_ref[...]).astype(BF16)
    zg_ref[rows, :] = _dot(xn, wm_ref[:, Z_MLA:])
    sm = _dot(xn, ws_ref[...])
    zs_ref[rows, :] = sm
    z_mla = _dot(xn, wm_ref[:, :Z_MLA])
    return z_mla[:, :Q_LORA], z_mla[:, Q_LORA:], sm


def _inproj_specs(tm):
    row = lambda i: (i, 0)
    const = lambda i: (0, 0)
    in_specs = [pl.BlockSpec((tm, D_MODEL), row),
                pl.BlockSpec((1, D_MODEL), const),
                pl.BlockSpec((D_MODEL, Z_MAIN), const, pipeline_mode=pl.Buffered(1)),
                pl.BlockSpec((D_MODEL, LANES), const, pipeline_mode=pl.Buffered(1)),
                pl.BlockSpec((1, Q_LORA), const),
                pl.BlockSpec((1, KV_LORA), const),
                pl.BlockSpec((Q_LORA, MLA_HEADS * QK_PAD), const, pipeline_mode=pl.Buffered(1))]
    out_specs = [pl.BlockSpec((tm, Z_GLA), row), pl.BlockSpec((tm, LANES), row),
                 pl.BlockSpec((tm, KV_LORA), row), pl.BlockSpec((tm, ROPE_DIM), row)]
    out_shape = lambda T: [jax.ShapeDtypeStruct((T, Z_GLA), F32), jax.ShapeDtypeStruct((T, LANES), F32),
                           jax.ShapeDtypeStruct((T, KV_LORA), F32), jax.ShapeDtypeStruct((T, ROPE_DIM), F32)]
    return in_specs, out_specs, out_shape


def _mla_q(cq, gq_ref, wuq_ref, cos, sin):
    cqn = _rms(cq, gq_ref[...]).astype(BF16)
    q = _dot(cqn, wuq_ref[...]) * QK_SCALE
    out = []
    for h in range(MLA_HEADS):
        nope = q[:, h * QK_PAD:h * QK_PAD + NOPE_DIM]
        rp = _rope(q[:, h * QK_PAD + NOPE_DIM:(h + 1) * QK_PAD], cos, sin)
        out.append((nope, rp))
    return out


def _proj_prompt_kernel(x_ref, g_ref, wm_ref, ws_ref, gq_ref, gkv_ref, wuq_ref, wukt_ref, wuv_ref, cos_ref, sin_ref,
                        zg_ref, zs_ref, ckvn_ref, kr_ref, qcat_ref, kt_ref, v_ref):
    for r in range(0, x_ref.shape[0], PROJ_SUB):
        rows = slice(r, r + PROJ_SUB)
        cq, ckv, sm = _inproj(rows, x_ref, g_ref, wm_ref, ws_ref, zg_ref, zs_ref)
        cos, sin = cos_ref[rows, :], sin_ref[rows, :]
        for h, (nope, rp) in enumerate(_mla_q(cq, gq_ref, wuq_ref, cos, sin)):
            qcat_ref[rows, h * QK_PAD:h * QK_PAD + NOPE_DIM] = nope.astype(BF16)
            qcat_ref[rows, h * QK_PAD + NOPE_DIM:(h + 1) * QK_PAD] = rp.astype(BF16)
        ckvn = _rms(ckv, gkv_ref[...])
        ckvn_ref[rows, :] = ckvn
        kr = _rope(sm, cos, sin)
        kr_ref[rows, :] = kr[:, :ROPE_DIM]
        ckvn_b = ckvn.astype(BF16)
        k_t = _dot_nt(wukt_ref[...], ckvn_b)
        kr_t = kr.T.astype(BF16)
        for h in range(MLA_HEADS):
            kt_ref[h * QK_PAD:h * QK_PAD + NOPE_DIM, rows] = k_t[h * NOPE_DIM:(h + 1) * NOPE_DIM].astype(BF16)
            kt_ref[h * QK_PAD + NOPE_DIM:(h + 1) * QK_PAD, rows] = kr_t
        v_ref[rows, :] = _dot(ckvn_b, wuv_ref[...]).astype(BF16)


def _proj_prompt(x, w, cos_t, sin_t, seq):
    T = x.shape[0]
    tm = PROJ_TM
    nseq = seq // tm
    row = lambda i: (i, 0)
    const = lambda i: (0, 0)
    in_specs, out_specs, out_shape = _inproj_specs(tm)
    return pl.pallas_call(
        _proj_prompt_kernel,
        grid=(T // tm,),
        in_specs=in_specs + [pl.BlockSpec((MLA_WIDTH, KV_LORA), const, pipeline_mode=pl.Buffered(1)),
                             pl.BlockSpec((KV_LORA, MLA_WIDTH), const, pipeline_mode=pl.Buffered(1)),
                             pl.BlockSpec((tm, LANES), lambda i: (i % nseq, 0)),
                             pl.BlockSpec((tm, LANES), lambda i: (i % nseq, 0))],
        out_specs=out_specs + [pl.BlockSpec((tm, MLA_HEADS * QK_PAD), row),
                               pl.BlockSpec((MLA_HEADS * QK_PAD, tm), lambda i: (0, i)),
                               pl.BlockSpec((tm, MLA_WIDTH), row)],
        out_shape=out_shape(T) + [jax.ShapeDtypeStruct((T, MLA_HEADS * QK_PAD), BF16),
                                  jax.ShapeDtypeStruct((MLA_HEADS * QK_PAD, T), BF16),
                                  jax.ShapeDtypeStruct((T, MLA_WIDTH), BF16)],
        compiler_params=_params("parallel", vmem=VMEM_LIMIT_PROJ),
        name="proj_prompt",
    )(x, w["g_pre_mix"], *w["w_in"], w["g_q"], w["g_kv"], w["w_uq"],
      w["w_ukt"].reshape(MLA_WIDTH, KV_LORA), w["w_uv"], cos_t, sin_t)


def _proj_sample_kernel(x_ref, g_ref, wm_ref, ws_ref, gq_ref, gkv_ref, wuq_ref, wukt_ref, cos_ref, sin_ref,
                        zg_ref, zs_ref, ckvn_ref, kr_ref, qlat_ref, qr_ref):
    cq, ckv, sm = _inproj(slice(None), x_ref, g_ref, wm_ref, ws_ref, zg_ref, zs_ref)
    cos, sin = cos_ref[...], sin_ref[...]
    for h, (nope, rp) in enumerate(_mla_q(cq, gq_ref, wuq_ref, cos, sin)):
        qlat_ref[h] = _dot(nope.astype(BF16), wukt_ref[h]).astype(BF16)
        qr_ref[h] = rp[:, :ROPE_DIM].astype(BF16)
    ckvn_ref[...] = _rms(ckv, gkv_ref[...])
    kr_ref[...] = _rope(sm, cos, sin)[:, :ROPE_DIM]


def _proj_sample(x, w, cos_t, sin_t):
    T = x.shape[0]
    tm = PROJ_TM
    row = lambda i: (i, 0)
    in_specs, out_specs, out_shape = _inproj_specs(tm)
    return pl.pallas_call(
        _proj_sample_kernel,
        grid=(T // tm,),
        in_specs=in_specs + [pl.BlockSpec((MLA_HEADS, NOPE_DIM, KV_LORA), lambda i: (0, 0, 0)),
                             pl.BlockSpec((tm, LANES), row),
                             pl.BlockSpec((tm, LANES), row)],
        out_specs=out_specs + [pl.BlockSpec((MLA_HEADS, tm, KV_LORA), lambda i: (0, i, 0)),
                               pl.BlockSpec((MLA_HEADS, tm, ROPE_DIM), lambda i: (0, i, 0))],
        out_shape=out_shape(T) + [jax.ShapeDtypeStruct((MLA_HEADS, T, KV_LORA), BF16),
                                  jax.ShapeDtypeStruct((MLA_HEADS, T, ROPE_DIM), BF16)],
        compiler_params=_params("parallel"),
        name="proj_sample",
    )(x, w["g_pre_mix"], *w["w_in"], w["g_q"], w["g_kv"], w["w_uq"], w["w_ukt"], cos_t, sin_t)


ATT_TQ = 512
ATT_HEADS = 4


def _attn_prompt_kernel(q_ref, kt_ref, v_ref, o_ref):
    seq = q_ref.shape[0]
    qc = lax.broadcasted_iota(jnp.int32, (ATT_TQ, ATT_TQ), 0) // CHUNK
    kc = lax.broadcasted_iota(jnp.int32, (ATT_TQ, ATT_TQ), 1) // CHUNK
    visible = kc <= qc
    for i in reversed(range(seq // ATT_TQ)):
        lo, hi = i * ATT_TQ, (i + 1) * ATT_TQ
        for h in range(ATT_HEADS):
            qk = slice(h * QK_PAD, (h + 1) * QK_PAD)
            dv = slice(h * V_DIM, (h + 1) * V_DIM)
            s = _dot(q_ref[lo:hi, qk], kt_ref[qk, 0:hi])
            s_diag = jnp.where(visible, s[:, lo:hi], NEG_BIG)
            s = jnp.concatenate([s[:, :lo], s_diag], axis=1) if i else s_diag
            m = jnp.max(s, axis=-1, keepdims=True)
            p = jnp.exp2(s - m).astype(BF16)
            v_one = jnp.concatenate([v_ref[0:hi, dv], jnp.ones((hi, V_DIM), BF16)], axis=1)
            ol = _dot(p, v_one)
            o_ref[lo:hi, dv] = (ol[:, :V_DIM] / ol[:, V_DIM:]).astype(BF16)


def _attn_prompt(qcat, kt, v, batch, seq):
    return pl.pallas_call(
        _attn_prompt_kernel,
        grid=(batch, MLA_HEADS // ATT_HEADS),
        in_specs=[pl.BlockSpec((seq, ATT_HEADS * QK_PAD), lambda b, h: (b, h)),
                  pl.BlockSpec((ATT_HEADS * QK_PAD, seq), lambda b, h: (h, b)),
                  pl.BlockSpec((seq, ATT_HEADS * V_DIM), lambda b, h: (b, h))],
        out_specs=pl.BlockSpec((seq, ATT_HEADS * V_DIM), lambda b, h: (b, h)),
        out_shape=jax.ShapeDtypeStruct((batch * seq, MLA_WIDTH), BF16),
        compiler_params=_params("parallel", "parallel"),
        name="attn_prompt",
    )(qcat, kt, v)


DEC_SEQS = 2


def _attn_sample_kernel(ql_ref, qr_ref, ckv_ref, krt_ref, nckv_ref, nkr_ref, wuv_ref, o_ref):
    nb = ckv_ref.shape[0]
    dec = ql_ref.shape[1] // nb
    rows = MLA_HEADS * dec
    for b in range(nb):
        tok = slice(b * dec, (b + 1) * dec)
        ql = ql_ref[:, tok, :].reshape(rows, KV_LORA)
        qr = qr_ref[:, tok, :].reshape(rows, ROPE_DIM)
        ckv = ckv_ref[b].astype(BF16)
        nckv = nckv_ref[tok, :].astype(BF16)
        s = _dot_nt(ql, ckv) + _dot(qr, krt_ref[b].astype(BF16))
        s_new = _dot_nt(ql, nckv) + _dot_nt(qr, nkr_ref[tok, :].astype(BF16))
        m = jnp.maximum(jnp.max(s, axis=-1, keepdims=True), jnp.max(s_new, axis=-1, keepdims=True))
        p = jnp.exp2(s - m)
        p_new = jnp.exp2(s_new - m)
        l = jnp.sum(p, axis=-1, keepdims=True) + jnp.sum(p_new, axis=-1, keepdims=True)
        o = ((_dot(p.astype(BF16), ckv) + _dot(p_new.astype(BF16), nckv)) / l).astype(BF16)
        for h in range(MLA_HEADS):
            o_ref[tok, h * V_DIM:(h + 1) * V_DIM] = _dot(o[h * dec:(h + 1) * dec], wuv_ref[h]).astype(BF16)


def _attn_sample(q_lat, q_rope, cache_ckv, cache_krt, ckvn, kr, w_uvh, dec):
    batch, past, _ = cache_ckv.shape
    nb = DEC_SEQS
    return pl.pallas_call(
        _attn_sample_kernel,
        grid=(batch // nb,),
        in_specs=[pl.BlockSpec((MLA_HEADS, nb * dec, KV_LORA), lambda b: (0, b, 0)),
                  pl.BlockSpec((MLA_HEADS, nb * dec, ROPE_DIM), lambda b: (0, b, 0)),
                  pl.BlockSpec((nb, past, KV_LORA), lambda b: (b, 0, 0)),
                  pl.BlockSpec((nb, ROPE_DIM, past), lambda b: (b, 0, 0)),
                  pl.BlockSpec((nb * dec, KV_LORA), lambda b: (b, 0)),
                  pl.BlockSpec((nb * dec, ROPE_DIM), lambda b: (b, 0)),
                  pl.BlockSpec((MLA_HEADS, KV_LORA, V_DIM), lambda b: (0, 0, 0))],
        out_specs=pl.BlockSpec((nb * dec, MLA_WIDTH), lambda b: (b, 0)),
        out_shape=jax.ShapeDtypeStruct((batch * dec, MLA_WIDTH), BF16),
        compiler_params=_params("parallel"),
        name="attn_sample",
    )(q_lat, q_rope, cache_ckv, cache_krt, ckvn, kr, w_uvh)


GLA_CHUNK = 256
GLA_DEC_ROWS = 128
GLA_SEQS = 2


def _gla_tables(c, sub):
    nlev = int(np.log2(sub))
    assert 1 << nlev == sub and c % sub == 0 and c // sub <= LANES
    t = np.arange(c)[:, None]
    u = np.arange(c)[None, :]
    same_sub = (t // sub) == (u // sub)
    blocks = []
    lvl = np.full((c, c), -1, np.int32)
    for l in range(nlev):
        width = sub >> l
        half = width // 2
        m = (t // width) * width + half - 1
        upper = t > m
        blocks.append(np.where(upper, (u > m) & (u <= t), (u > t) & (u <= m)))
        same = (t // width) == (u // width)
        lvl[same & ((t % width) >= half) & ((u % width) < half)] = l
    blocks.append((u > t) & same_sub)
    blocks.append((u <= t) & same_sub)
    lvl[np.arange(c), np.arange(c)] = nlev
    sel = (np.arange(c)[:, None] // sub) == np.arange(LANES)[None, :]
    return np.concatenate(blocks, 0).astype(np.float32), lvl, sel.astype(np.float32), nlev


def _gla_kernel(*refs, c, sub, nlev, carry):
    if carry:
        (q_ref, k_ref, v_ref, r_ref, sm_ref, wga_ref, bga_ref, gg_ref, p_ref, lvl_ref, sel_ref,
         o_ref, sfin_ref, s_scr) = refs
        assert sub == c
        j = pl.program_id(1)

        @pl.when(j == 0)
        def _():
            s_scr[...] = jnp.zeros(s_scr.shape, F32)
    else:
        (q_ref, k_ref, v_ref, r_ref, sm_ref, wga_ref, bga_ref, gg_ref, p_ref, lvl_ref, sel_ref, s0_ref,
         o_ref, sfin_ref) = refs

    p_mat = p_ref[...]
    sel = sel_ref[...]
    lvl = lvl_ref[...]
    groups = range(q_ref.shape[0])
    split = []
    for g in groups:
        x = _dot(sm_ref[g].astype(BF16), wga_ref[...]) + bga_ref[...]
        la = (jnp.minimum(x, 0.0) - jnp.log1p(jnp.exp(-jnp.abs(x)))) * (1.0 / GATE_TAU)
        hi = la.astype(BF16)
        split.append((hi, (la - hi.astype(F32)).astype(BF16)))
    for g in groups:
        hi, mid = split[g]
        e_all = jnp.exp(_dot(p_mat, hi) + _dot(p_mat, mid))
        d_all = jnp.exp(_dot_tn(hi, sel) + _dot_tn(mid, sel))
        q = q_ref[g] * (GLA_DK ** -0.5)
        k = k_ref[g]
        qe = [(q * e_all[l * c:(l + 1) * c]).astype(BF16) for l in range(nlev)] + [q.astype(BF16)]
        ke = [(k * e_all[l * c:(l + 1) * c]).astype(BF16) for l in range(nlev)] + [k.astype(BF16)]
        k_end = (k * e_all[nlev * c:(nlev + 1) * c]).astype(BF16)
        q_beg = (q * e_all[(nlev + 1) * c:(nlev + 2) * c]).astype(BF16)
        for h in range(GLA_HEADS):
            dk = slice(h * GLA_DK, (h + 1) * GLA_DK)
            dv = slice(h * GLA_DV, (h + 1) * GLA_DV)
            attn = jnp.zeros((c, c), F32)
            for l in range(nlev + 1):
                attn = jnp.where(lvl == l, _dot_nt(qe[l][:, dk], ke[l][:, dk]), attn)
            v = v_ref[g, :, dv].astype(BF16)
            o_intra = _dot(attn.astype(BF16), v)
            for n in range(c // sub):
                rows = slice(n * sub, (n + 1) * sub)
                state = s_scr[g, h] if carry else s0_ref[n, h]
                o = o_intra[rows] + _dot(q_beg[rows, dk], state.astype(BF16))
                new_state = d_all[dk, n:n + 1] * state + _dot_tn(k_end[rows, dk], v[rows])
                if carry:
                    s_scr[g, h] = new_state
                else:
                    sfin_ref[n, h] = new_state
                r = r_ref[g, rows, dv]
                o_ref[g, rows, dv] = (_rms(o, gg_ref[...]) * (r * jax.nn.sigmoid(r))).astype(BF16)

    if carry:
        @pl.when(j == pl.num_programs(1) - 1)
        def _():
            sfin_ref[...] = s_scr[...]


def _gla(z, zs, w_ga, b_ga, g_gla, batch, seq, c, s0=None):
    carry = s0 is None
    sub = c if carry else seq
    p_np, lvl_np, sel_np, nlev = _gla_tables(c, sub)
    if carry:
        nb, lead, grid = GLA_SEQS, batch, (batch // GLA_SEQS, seq // c)
        at = lambda col: (lambda g, j: (g, j, col))
    else:
        nb, lead, grid = 1, 1, (batch * seq // c, 1)
        at = lambda col: (lambda g, j: (0, g, col))
    nstate = nb if carry else c // sub
    z, zs = z.reshape(lead, -1, Z_GLA), zs.reshape(lead, -1, LANES)
    const = lambda g, j: (0, 0)
    in_specs = [pl.BlockSpec((nb, c, GLA_HEADS * GLA_DK), at(0)),
                pl.BlockSpec((nb, c, GLA_HEADS * GLA_DK), at(1)),
                pl.BlockSpec((nb, c, GLA_WIDTH), at(1)),
                pl.BlockSpec((nb, c, GLA_WIDTH), at(2)),
                pl.BlockSpec((nb, c, LANES), at(0)),
                pl.BlockSpec((LANES, GLA_HEADS * GLA_DK), const),
                pl.BlockSpec((1, GLA_HEADS * GLA_DK), const),
                pl.BlockSpec((1, GLA_DV), const),
                pl.BlockSpec(p_np.shape, const),
                pl.BlockSpec((c, c), const),
                pl.BlockSpec((c, LANES), const)]
    args = [z, z, z, z, zs, w_ga, b_ga, g_gla,
            jnp.asarray(p_np, BF16), jnp.asarray(lvl_np), jnp.asarray(sel_np, BF16)]
    state_spec = pl.BlockSpec((nstate, GLA_HEADS, GLA_DK, GLA_DV), lambda g, j: (g, 0, 0, 0))
    if not carry:
        in_specs.append(state_spec)
        args.append(s0)
    o, s_fin = pl.pallas_call(
        functools.partial(_gla_kernel, c=c, sub=sub, nlev=nlev, carry=carry),
        grid=grid,
        in_specs=in_specs,
        out_specs=[pl.BlockSpec((nb, c, GLA_WIDTH), at(0)), state_spec],
        out_shape=[jax.ShapeDtypeStruct((lead, batch * seq // lead, GLA_WIDTH), BF16),
                   jax.ShapeDtypeStruct((batch, GLA_HEADS, GLA_DK, GLA_DV), F32)],
        scratch_shapes=[pltpu.VMEM((nb, GLA_HEADS, GLA_DK, GLA_DV), F32)] if carry else [],
        compiler_params=_params("parallel", "arbitrary"),
        name="gla" if carry else "gla_init",
    )(*args)
    return o.reshape(batch * seq, GLA_WIDTH), s_fin


ROW_SUB = 128


def _mix_kernel(oma_ref, omb_ref, oga_ref, ogb_ref, w1_ref, w2_ref, ha_ref, hb_ref, g1_ref, g2_ref, h1_ref, f_ref,
                *, na):
    def block(om_ref, og_ref, h_ref):
        for r in range(0, h_ref.shape[0], ROW_SUB):
            rows = slice(r, r + ROW_SUB)
            mix = _dot(om_ref[rows, :], w1_ref[...]) + _dot(og_ref[rows, :], w2_ref[...])
            h1 = h_ref[rows, :] + _rms(mix, g1_ref[...])
            h1_ref[rows, :] = h1
            f_ref[rows, :] = _rms(h1, g2_ref[...]).astype(BF16)

    pl.when(pl.program_id(0) < na)(lambda: block(oma_ref, oga_ref, ha_ref))
    pl.when(pl.program_id(0) >= na)(lambda: block(omb_ref, ogb_ref, hb_ref))


def _mix(o_mla, o_gla, h, w_out, g_post_mix, g_pre_ffn):
    tm = 512
    na, nb = h[0].shape[0] // tm, h[1].shape[0] // tm
    T = (na + nb) * tm
    row = lambda i: (i, 0)
    const = lambda i: (0, 0)
    at_a = lambda i: (jnp.minimum(i, na - 1), 0)
    at_b = lambda i: (jnp.maximum(i - na, 0), 0)
    return pl.pallas_call(
        functools.partial(_mix_kernel, na=na),
        grid=(na + nb,),
        in_specs=[pl.BlockSpec((tm, MLA_WIDTH), at_a), pl.BlockSpec((tm, MLA_WIDTH), at_b),
                  pl.BlockSpec((tm, GLA_WIDTH), at_a), pl.BlockSpec((tm, GLA_WIDTH), at_b),
                  pl.BlockSpec((MLA_WIDTH, D_MODEL), lambda i: (0, 0), pipeline_mode=pl.Buffered(1)),
                  pl.BlockSpec((GLA_WIDTH, D_MODEL), lambda i: (1, 0), pipeline_mode=pl.Buffered(1)),
                  pl.BlockSpec((tm, D_MODEL), at_a), pl.BlockSpec((tm, D_MODEL), at_b),
                  pl.BlockSpec((1, D_MODEL), const),
                  pl.BlockSpec((1, D_MODEL), const)],
        out_specs=[pl.BlockSpec((tm, D_MODEL), row), pl.BlockSpec((tm, D_MODEL), row)],
        out_shape=[jax.ShapeDtypeStruct((T, D_MODEL), F32), jax.ShapeDtypeStruct((T, D_MODEL), BF16)],
        compiler_params=_params("arbitrary"),
        name="mix",
    )(o_mla[0], o_mla[1], o_gla[0], o_gla[1], w_out, w_out, h[0], h[1], g_post_mix, g_pre_ffn)


FFN_MAX_ROWS = 1088


def _ffn_kernel(x_ref, wg_ref, wu_ref, wd_ref, o_ref):
    def partial_out():
        x = x_ref[...]
        g = _dot(x, wg_ref[...].astype(BF16))
        u = _dot(x, wu_ref[...].astype(BF16))
        return _dot((g * jax.nn.sigmoid(g) * u).astype(BF16), wd_ref[...].astype(BF16))

    @pl.when(pl.program_id(1) == 0)
    def _():
        o_ref[...] = partial_out()

    @pl.when(pl.program_id(1) > 0)
    def _():
        o_ref[...] += partial_out()


def _ffn(x, w_gate, w_up, w_down):
    T = x.shape[0]
    tf = 512
    tm = next(T // n for n in range(1, T + 1) if T % n == 0 and T // n <= FFN_MAX_ROWS and (T // n) % 16 == 0)
    return pl.pallas_call(
        _ffn_kernel,
        grid=(T // tm, D_FF // tf),
        in_specs=[pl.BlockSpec((tm, D_MODEL), lambda i, j: (i, 0)),
                  pl.BlockSpec((D_MODEL, tf), lambda i, j: (0, j)),
                  pl.BlockSpec((D_MODEL, tf), lambda i, j: (0, j)),
                  pl.BlockSpec((tf, D_MODEL), lambda i, j: (j, 0))],
        out_specs=pl.BlockSpec((tm, D_MODEL), lambda i, j: (i, 0)),
        out_shape=jax.ShapeDtypeStruct((T, D_MODEL), F32),
        compiler_params=_params("parallel", "arbitrary"),
        name="ffn",
    )(x, w_gate, w_up, w_down)


def _final_kernel(f_ref, h1_ref, pa_ref, pb_ref, g_ref, wpg_ref, wp_ref, ya_ref, yb_ref, *, na):
    def block(p_ref, y_ref):
        h2 = h1_ref[...] + _rms(f_ref[...], g_ref[...])
        gate = jax.nn.sigmoid(_dot(h2.astype(BF16), wpg_ref[...]))
        y_ref[...] = h2 + gate * _dot(p_ref[...].astype(BF16), wp_ref[...])

    pl.when(pl.program_id(0) < na)(lambda: block(pa_ref, ya_ref))
    pl.when(pl.program_id(0) >= na)(lambda: block(pb_ref, yb_ref))


def _final(f, h1, p, g_post_ffn, w_ple_gate, w_ple):
    tm = 512
    na, nb = p[0].shape[0] // tm, p[1].shape[0] // tm
    row = lambda i: (i, 0)
    const = lambda i: (0, 0)
    at_a = lambda i: (jnp.minimum(i, na - 1), 0)
    at_b = lambda i: (jnp.maximum(i - na, 0), 0)
    return pl.pallas_call(
        functools.partial(_final_kernel, na=na),
        grid=(na + nb,),
        in_specs=[pl.BlockSpec((tm, D_MODEL), row),
                  pl.BlockSpec((tm, D_MODEL), row),
                  pl.BlockSpec((tm, PLE_DIM), at_a), pl.BlockSpec((tm, PLE_DIM), at_b),
                  pl.BlockSpec((1, D_MODEL), const),
                  pl.BlockSpec((D_MODEL, D_MODEL), const, pipeline_mode=pl.Buffered(1)),
                  pl.BlockSpec((PLE_DIM, D_MODEL), const)],
        out_specs=[pl.BlockSpec((tm, D_MODEL), at_a), pl.BlockSpec((tm, D_MODEL), at_b)],
        out_shape=[jax.ShapeDtypeStruct((na * tm, D_MODEL), F32), jax.ShapeDtypeStruct((nb * tm, D_MODEL), F32)],
        compiler_params=_params("arbitrary"),
        name="final",
    )(f, h1, p[0], p[1], g_post_ffn, w_ple_gate, w_ple)


def _rope_tables(pos):
    half = ROPE_DIM // 2
    inv = 1.0 / (ROPE_THETA ** (jnp.arange(half, dtype=F32) / half))
    ang = pos.astype(F32)[:, None] * inv[None, :]
    cos, sin = jnp.cos(ang), jnp.sin(ang)
    zero = jnp.zeros((pos.shape[0], LANES - ROPE_DIM), F32)
    return jnp.concatenate([cos, cos, zero], axis=1), jnp.concatenate([-sin, sin, zero], axis=1)


def _layer_weights(i, g_pre_mix, w_in, g_q, w_uq, w_uk, g_kv, w_ga, b_ga, w_uv, g_gla, w_out, g_post_mix,
                   g_pre_ffn, w_gate, w_up, w_down, g_post_ffn, w_ple, w_ple_gate):
    w_uq_p = jnp.pad(w_uq[i].reshape(Q_LORA, MLA_HEADS, NOPE_DIM + ROPE_DIM),
                     ((0, 0), (0, 0), (0, QK_PAD - NOPE_DIM - ROPE_DIM))).reshape(Q_LORA, MLA_HEADS * QK_PAD)
    w_ga_p = jnp.zeros((LANES, GLA_HEADS * GLA_DK), F32).at[GLR_LO:GLR_LO + GATE_RANK].set(w_ga[i])
    vec = lambda g: g[i].reshape(1, -1)
    return dict(
        g_pre_mix=vec(g_pre_mix), w_in=_win_prep(jnp.swapaxes(w_in[i], 0, 1)), g_q=vec(g_q), g_kv=vec(g_kv),
        w_uq=w_uq_p.astype(BF16),
        w_uv=w_uv[i].reshape(KV_LORA, MLA_WIDTH).astype(BF16),
        w_ukt=jnp.transpose(w_uk[i], (1, 2, 0)).astype(BF16),
        w_uvh=jnp.transpose(w_uv[i], (1, 0, 2)).astype(BF16),
        w_ga=w_ga_p.astype(BF16), b_ga=vec(b_ga), g_gla=vec(g_gla),
        w_out=w_out[i].astype(BF16), g_post_mix=vec(g_post_mix), g_pre_ffn=vec(g_pre_ffn),
        w_gate=w_gate[i], w_up=w_up[i], w_down=w_down[i],
        g_post_ffn=vec(g_post_ffn), w_ple=w_ple[i].astype(BF16), w_ple_gate=w_ple_gate[i].astype(BF16))


def _finish(w, h, p, o_mla, o_gla):
    h1, f_in = _mix(o_mla, o_gla, h, w["w_out"], w["g_post_mix"], w["g_pre_ffn"])
    f = _ffn(f_in, w["w_gate"], w["w_up"], w["w_down"])
    return _final(f, h1, p, w["g_post_ffn"], w["w_ple_gate"], w["w_ple"])


def kernel(x_prompt, x_sample, cache_ckv, cache_krope, state_gla, p_prompt, p_sample, g_pre_mix, w_in, g_q, w_uq,
           w_uk, g_kv, w_ga, b_ga, w_uv, g_gla, w_out, g_post_mix, g_pre_ffn, w_gate, w_up, w_down, g_post_ffn,
           w_ple, w_ple_gate):
    batch, seq, _ = x_prompt.shape
    dbatch, dseq, _ = x_sample.shape
    depth = w_in.shape[0]
    cos_p, sin_p = _rope_tables(jnp.arange(seq))
    cos_s, sin_s = _rope_tables(PAST_LEN + jnp.arange(dseq))
    cos_s, sin_s = jnp.tile(cos_s, (dbatch, 1)), jnp.tile(sin_s, (dbatch, 1))
    h_p = x_prompt.reshape(batch * seq, D_MODEL)
    h_s = x_sample.reshape(dbatch * dseq, D_MODEL)
    outs = [[] for _ in range(6)]
    for i in range(depth):
        w = _layer_weights(i, g_pre_mix, w_in, g_q, w_uq, w_uk, g_kv, w_ga, b_ga, w_uv, g_gla, w_out, g_post_mix,
                           g_pre_ffn, w_gate, w_up, w_down, g_post_ffn, w_ple, w_ple_gate)
        z, zs, ckvn, kr, qcat, kt, v = _proj_prompt(h_p, w, cos_p, sin_p, seq)
        o_mla_p = _attn_prompt(qcat, kt, v, batch, seq)
        o_gla_p, s_fin = _gla(z, zs, w["w_ga"], w["b_ga"], w["g_gla"], batch, seq, GLA_CHUNK)
        outs[0].append(ckvn.reshape(batch, seq, KV_LORA))
        outs[1].append(kr.reshape(batch, seq, ROPE_DIM))
        outs[2].append(s_fin)
        z, zs, ckvn, kr, q_lat, q_rope = _proj_sample(h_s, w, cos_s, sin_s)
        o_mla_s = _attn_sample(q_lat, q_rope, cache_ckv[i], jnp.swapaxes(cache_krope[i], 1, 2), ckvn, kr,
                               w["w_uvh"], dseq)
        o_gla_s, s_new = _gla(z, zs, w["w_ga"], w["b_ga"], w["g_gla"], dbatch, dseq, GLA_DEC_ROWS, s0=state_gla[i])
        outs[3].append(ckvn.reshape(dbatch, dseq, KV_LORA))
        outs[4].append(kr.reshape(dbatch, dseq, ROPE_DIM))
        outs[5].append(s_new)
        h_p, h_s = _finish(w, (h_p, h_s),
                           (p_prompt[i].reshape(batch * seq, PLE_DIM), p_sample[i].reshape(dbatch * dseq, PLE_DIM)),
                           (o_mla_p, o_mla_s), (o_gla_p, o_gla_s))
    return (h_p.reshape(batch, seq, D_MODEL), h_s.reshape(dbatch, dseq, D_MODEL),
            jnp.stack(outs[0]), jnp.stack(outs[1]), jnp.stack(outs[2]),
            jnp.stack(outs[3]), jnp.stack(outs[4]), jnp.stack(outs[5]))
```

```python
import functools

import numpy as np
import jax
import jax.numpy as jnp
from jax import lax
from jax.experimental import pallas as pl
from jax.experimental.pallas import tpu as pltpu

F32 = jnp.float32
BF16 = jnp.bfloat16

D_MODEL = 2048
PAST_LEN = 4096
CHUNK = 64
EPS = 1e-6

MLA_HEADS = 8
Q_LORA = 512
KV_LORA = 512
NOPE_DIM = 128
ROPE_DIM = 64
V_DIM = 128
ROPE_THETA = 10000.0
MLA_SCALE = (NOPE_DIM + ROPE_DIM) ** -0.5
QK_PAD = 256

GLA_HEADS = 4
GLA_DK = 128
GLA_DV = 256
GATE_RANK = 16
GATE_TAU = 16.0
GLA_WIDTH = GLA_HEADS * GLA_DV
MLA_WIDTH = MLA_HEADS * V_DIM

IN_SPLITS = (Q_LORA, KV_LORA, ROPE_DIM, GLA_HEADS * GLA_DK, GLA_HEADS * GLA_DK, GLA_WIDTH, GATE_RANK, GLA_WIDTH)
Z_MAIN = 4096
GLR_LO = ROPE_DIM

D_FF = 5632
PLE_DIM = 256

LANES = 128
VMEM_BYTES = 64 * 1024 * 1024
VMEM_LIMIT = VMEM_BYTES - 8 * 1024 * 1024
VMEM_LIMIT_PROJ = VMEM_BYTES - 4 * 1024 * 1024
NEG_BIG = -1e30
LOG2E = 1.4426950408889634
QK_SCALE = MLA_SCALE * LOG2E


def _dot(a, b):
    return jnp.dot(a, b, preferred_element_type=F32)


def _dot_nt(a, b):
    return lax.dot_general(a, b, (((1,), (1,)), ((), ())), preferred_element_type=F32)


def _dot_tn(a, b):
    return lax.dot_general(a, b, (((0,), (0,)), ((), ())), preferred_element_type=F32)


def _rms(x, g):
    ms = jnp.mean(x * x, axis=-1, keepdims=True)
    return x * lax.rsqrt(ms + EPS) * g


def _rope(x, cos_t, sin_t):
    lane = lax.broadcasted_iota(jnp.int32, x.shape, 1)
    rot = jnp.where((lane & 32) == 0, pltpu.roll(x, 96, 1), pltpu.roll(x, 32, 1))
    return x * cos_t + rot * sin_t


def _params(*sem, vmem=VMEM_LIMIT):
    return pltpu.CompilerParams(dimension_semantics=sem, vmem_limit_bytes=vmem)


_IN_OFF = np.concatenate([[0], np.cumsum(IN_SPLITS)]).tolist()
_WIN_TILE = 512


def _win_prep_kernel(w_ref, kr_ref, glr_ref, o_ref, small_ref):
    o_ref[...] = w_ref[...].T.astype(BF16)

    @pl.when(pl.program_id(0) == 0)
    def _():
        pad = jnp.zeros((LANES - GLR_LO - GATE_RANK, D_MODEL), F32)
        small_ref[...] = jnp.concatenate([kr_ref[...], glr_ref[...], pad], axis=0).T.astype(BF16)


def _win_prep(w_t):
    c_q, c_kv, k_r, q_g, k_g, v_g, g_lr, r_g, end = _IN_OFF
    n_head, n_mid = k_r // _WIN_TILE, (g_lr - q_g) // _WIN_TILE
    assert k_r % _WIN_TILE == 0 and (g_lr - q_g) % _WIN_TILE == 0 and (end - r_g) % _WIN_TILE == 0
    assert (n_head + n_mid) * _WIN_TILE + end - r_g == Z_MAIN and GLR_LO == ROPE_DIM

    def src_row(j):
        skip_mid, skip_tail = (q_g - k_r) // 8, (r_g - g_lr + q_g - k_r) // 8
        return 8 * (j * (_WIN_TILE // 8) + jnp.where(j < n_head, 0, jnp.where(j < n_head + n_mid, skip_mid, skip_tail)))

    return pl.pallas_call(
        _win_prep_kernel,
        grid=(Z_MAIN // _WIN_TILE,),
        in_specs=[pl.BlockSpec((pl.Element(_WIN_TILE), pl.Element(D_MODEL)), lambda j: (src_row(j), 0)),
                  pl.BlockSpec((pl.Element(ROPE_DIM), pl.Element(D_MODEL)), lambda j: (k_r, 0)),
                  pl.BlockSpec((pl.Element(GATE_RANK), pl.Element(D_MODEL)), lambda j: (g_lr, 0))],
        out_specs=[pl.BlockSpec((D_MODEL, _WIN_TILE), lambda j: (0, j)),
                   pl.BlockSpec((D_MODEL, LANES), lambda j: (0, 0))],
        out_shape=[jax.ShapeDtypeStruct((D_MODEL, Z_MAIN), BF16), jax.ShapeDtypeStruct((D_MODEL, LANES), BF16)],
        compiler_params=_params("arbitrary"),
        name="win_prep",
    )(w_t, w_t, w_t)


Z_MLA = Q_LORA + KV_LORA
Z_GLA = Z_MAIN - Z_MLA
PROJ_TM = 512


def _inproj(x_ref, g_ref, wm_ref, ws_ref, zg_ref, zs_ref):
    xn = _rms(x_ref[...], g_ref[...]).astype(BF16)
    zg_ref[...] = _dot(xn, wm_ref[:, Z_MLA:])
    sm = _dot(xn, ws_ref[...])
    zs_ref[...] = sm
    z_mla = _dot(xn, wm_ref[:, :Z_MLA])
    return z_mla[:, :Q_LORA], z_mla[:, Q_LORA:], sm


def _inproj_specs(tm):
    row = lambda i: (i, 0)
    const = lambda i: (0, 0)
    in_specs = [pl.BlockSpec((tm, D_MODEL), row),
                pl.BlockSpec((1, D_MODEL), const),
                pl.BlockSpec((D_MODEL, Z_MAIN), const, pipeline_mode=pl.Buffered(1)),
                pl.BlockSpec((D_MODEL, LANES), const, pipeline_mode=pl.Buffered(1)),
                pl.BlockSpec((1, Q_LORA), const),
                pl.BlockSpec((1, KV_LORA), const),
                pl.BlockSpec((Q_LORA, MLA_HEADS * QK_PAD), const, pipeline_mode=pl.Buffered(1))]
    out_specs = [pl.BlockSpec((tm, Z_GLA), row), pl.BlockSpec((tm, LANES), row),
                 pl.BlockSpec((tm, KV_LORA), row), pl.BlockSpec((tm, ROPE_DIM), row)]
    out_shape = lambda T: [jax.ShapeDtypeStruct((T, Z_GLA), F32), jax.ShapeDtypeStruct((T, LANES), F32),
                           jax.ShapeDtypeStruct((T, KV_LORA), F32), jax.ShapeDtypeStruct((T, ROPE_DIM), F32)]
    return in_specs, out_specs, out_shape


def _mla_q(cq, gq_ref, wuq_ref, cos, sin):
    cqn = _rms(cq, gq_ref[...]).astype(BF16)
    q = _dot(cqn, wuq_ref[...]) * QK_SCALE
    out = []
    for h in range(MLA_HEADS):
        nope = q[:, h * QK_PAD:h * QK_PAD + NOPE_DIM]
        rp = _rope(q[:, h * QK_PAD + NOPE_DIM:(h + 1) * QK_PAD], cos, sin)
        out.append((nope, rp))
    return out


def _proj_prompt_kernel(x_ref, g_ref, wm_ref, ws_ref, gq_ref, gkv_ref, wuq_ref, wukt_ref, wuv_ref, cos_ref, sin_ref,
                        zg_ref, zs_ref, ckvn_ref, kr_ref, qcat_ref, kt_ref, v_ref):
    cq, ckv, sm = _inproj(x_ref, g_ref, wm_ref, ws_ref, zg_ref, zs_ref)
    cos, sin = cos_ref[...], sin_ref[...]
    for h, (nope, rp) in enumerate(_mla_q(cq, gq_ref, wuq_ref, cos, sin)):
        qcat_ref[:, h * QK_PAD:h * QK_PAD + NOPE_DIM] = nope.astype(BF16)
        qcat_ref[:, h * QK_PAD + NOPE_DIM:(h + 1) * QK_PAD] = rp.astype(BF16)
    ckvn = _rms(ckv, gkv_ref[...])
    ckvn_ref[...] = ckvn
    kr = _rope(sm, cos, sin)
    kr_ref[...] = kr[:, :ROPE_DIM]
    ckvn_b = ckvn.astype(BF16)
    k_t = _dot_nt(wukt_ref[...], ckvn_b)
    kr_t = kr.T.astype(BF16)
    for h in range(MLA_HEADS):
        kt_ref[h * QK_PAD:h * QK_PAD + NOPE_DIM, :] = k_t[h * NOPE_DIM:(h + 1) * NOPE_DIM].astype(BF16)
        kt_ref[h * QK_PAD + NOPE_DIM:(h + 1) * QK_PAD, :] = kr_t
    v_ref[...] = _dot(ckvn_b, wuv_ref[...]).astype(BF16)


def _proj_prompt(x, w, cos_t, sin_t, seq):
    T = x.shape[0]
    tm = PROJ_TM
    nseq = seq // tm
    row = lambda i: (i, 0)
    const = lambda i: (0, 0)
    in_specs, out_specs, out_shape = _inproj_specs(tm)
    return pl.pallas_call(
        _proj_prompt_kernel,
        grid=(T // tm,),
        in_specs=in_specs + [pl.BlockSpec((MLA_WIDTH, KV_LORA), const, pipeline_mode=pl.Buffered(1)),
                             pl.BlockSpec((KV_LORA, MLA_WIDTH), const, pipeline_mode=pl.Buffered(1)),
                             pl.BlockSpec((tm, LANES), lambda i: (i % nseq, 0)),
                             pl.BlockSpec((tm, LANES), lambda i: (i % nseq, 0))],
        out_specs=out_specs + [pl.BlockSpec((tm, MLA_HEADS * QK_PAD), row),
                               pl.BlockSpec((MLA_HEADS * QK_PAD, tm), lambda i: (0, i)),
                               pl.BlockSpec((tm, MLA_WIDTH), row)],
        out_shape=out_shape(T) + [jax.ShapeDtypeStruct((T, MLA_HEADS * QK_PAD), BF16),
                                  jax.ShapeDtypeStruct((MLA_HEADS * QK_PAD, T), BF16),
                                  jax.ShapeDtypeStruct((T, MLA_WIDTH), BF16)],
        compiler_params=_params("parallel", vmem=VMEM_LIMIT_PROJ),
        name="proj_prompt",
    )(x, w["g_pre_mix"], *w["w_in"], w["g_q"], w["g_kv"], w["w_uq"],
      w["w_ukt"].reshape(MLA_WIDTH, KV_LORA), w["w_uv"], cos_t, sin_t)


def _proj_sample_kernel(x_ref, g_ref, wm_ref, ws_ref, gq_ref, gkv_ref, wuq_ref, wukt_ref, cos_ref, sin_ref,
                        zg_ref, zs_ref, ckvn_ref, kr_ref, qlat_ref, qr_ref):
    cq, ckv, sm = _inproj(x_ref, g_ref, wm_ref, ws_ref, zg_ref, zs_ref)
    cos, sin = cos_ref[...], sin_ref[...]
    for h, (nope, rp) in enumerate(_mla_q(cq, gq_ref, wuq_ref, cos, sin)):
        qlat_ref[h] = _dot(nope.astype(BF16), wukt_ref[h]).astype(BF16)
        qr_ref[h] = rp[:, :ROPE_DIM].astype(BF16)
    ckvn_ref[...] = _rms(ckv, gkv_ref[...])
    kr_ref[...] = _rope(sm, cos, sin)[:, :ROPE_DIM]


def _proj_sample(x, w, cos_t, sin_t):
    T = x.shape[0]
    tm = PROJ_TM
    row = lambda i: (i, 0)
    in_specs, out_specs, out_shape = _inproj_specs(tm)
    return pl.pallas_call(
        _proj_sample_kernel,
        grid=(T // tm,),
        in_specs=in_specs + [pl.BlockSpec((MLA_HEADS, NOPE_DIM, KV_LORA), lambda i: (0, 0, 0)),
                             pl.BlockSpec((tm, LANES), row),
                             pl.BlockSpec((tm, LANES), row)],
        out_specs=out_specs + [pl.BlockSpec((MLA_HEADS, tm, KV_LORA), lambda i: (0, i, 0)),
                               pl.BlockSpec((MLA_HEADS, tm, ROPE_DIM), lambda i: (0, i, 0))],
        out_shape=out_shape(T) + [jax.ShapeDtypeStruct((MLA_HEADS, T, KV_LORA), BF16),
                                  jax.ShapeDtypeStruct((MLA_HEADS, T, ROPE_DIM), BF16)],
        compiler_params=_params("parallel"),
        name="proj_sample",
    )(x, w["g_pre_mix"], *w["w_in"], w["g_q"], w["g_kv"], w["w_uq"], w["w_ukt"], cos_t, sin_t)


ATT_TQ = 512
ATT_HEADS = 4


def _attn_prompt_kernel(q_ref, kt_ref, v_ref, o_ref):
    seq = q_ref.shape[0]
    qc = lax.broadcasted_iota(jnp.int32, (ATT_TQ, ATT_TQ), 0) // CHUNK
    kc = lax.broadcasted_iota(jnp.int32, (ATT_TQ, ATT_TQ), 1) // CHUNK
    visible = kc <= qc
    for i in reversed(range(seq // ATT_TQ)):
        lo, hi = i * ATT_TQ, (i + 1) * ATT_TQ
        for h in range(ATT_HEADS):
            qk = slice(h * QK_PAD, (h + 1) * QK_PAD)
            dv = slice(h * V_DIM, (h + 1) * V_DIM)
            s = _dot(q_ref[lo:hi, qk], kt_ref[qk, 0:hi])
            s_diag = jnp.where(visible, s[:, lo:hi], NEG_BIG)
            s = jnp.concatenate([s[:, :lo], s_diag], axis=1) if i else s_diag
            m = jnp.max(s, axis=-1, keepdims=True)
            p = jnp.exp2(s - m).astype(BF16)
            v_one = jnp.concatenate([v_ref[0:hi, dv], jnp.ones((hi, V_DIM), BF16)], axis=1)
            ol = _dot(p, v_one)
            o_ref[lo:hi, dv] = (ol[:, :V_DIM] / ol[:, V_DIM:]).astype(BF16)


def _attn_prompt(qcat, kt, v, batch, seq):
    return pl.pallas_call(
        _attn_prompt_kernel,
        grid=(batch, MLA_HEADS // ATT_HEADS),
        in_specs=[pl.BlockSpec((seq, ATT_HEADS * QK_PAD), lambda b, h: (b, h)),
                  pl.BlockSpec((ATT_HEADS * QK_PAD, seq), lambda b, h: (h, b)),
                  pl.BlockSpec((seq, ATT_HEADS * V_DIM), lambda b, h: (b, h))],
        out_specs=pl.BlockSpec((seq, ATT_HEADS * V_DIM), lambda b, h: (b, h)),
        out_shape=jax.ShapeDtypeStruct((batch * seq, MLA_WIDTH), BF16),
        compiler_params=_params("parallel", "parallel"),
        name="attn_prompt",
    )(qcat, kt, v)


DEC_SEQS = 2


def _attn_sample_kernel(ql_ref, qr_ref, ckv_ref, krt_ref, nckv_ref, nkr_ref, wuv_ref, o_ref):
    nb = ckv_ref.shape[0]
    dec = ql_ref.shape[1] // nb
    rows = MLA_HEADS * dec
    for b in range(nb):
        tok = slice(b * dec, (b + 1) * dec)
        ql = ql_ref[:, tok, :].reshape(rows, KV_LORA)
        qr = qr_ref[:, tok, :].reshape(rows, ROPE_DIM)
        ckv = ckv_ref[b].astype(BF16)
        nckv = nckv_ref[tok, :].astype(BF16)
        s = _dot_nt(ql, ckv) + _dot(qr, krt_ref[b].astype(BF16))
        s_new = _dot_nt(ql, nckv) + _dot_nt(qr, nkr_ref[tok, :].astype(BF16))
        m = jnp.maximum(jnp.max(s, axis=-1, keepdims=True), jnp.max(s_new, axis=-1, keepdims=True))
        p = jnp.exp2(s - m)
        p_new = jnp.exp2(s_new - m)
        l = jnp.sum(p, axis=-1, keepdims=True) + jnp.sum(p_new, axis=-1, keepdims=True)
        o = ((_dot(p.astype(BF16), ckv) + _dot(p_new.astype(BF16), nckv)) / l).astype(BF16)
        for h in range(MLA_HEADS):
            o_ref[tok, h * V_DIM:(h + 1) * V_DIM] = _dot(o[h * dec:(h + 1) * dec], wuv_ref[h]).astype(BF16)


def _attn_sample(q_lat, q_rope, cache_ckv, cache_krt, ckvn, kr, w_uvh, dec):
    batch, past, _ = cache_ckv.shape
    nb = DEC_SEQS
    return pl.pallas_call(
        _attn_sample_kernel,
        grid=(batch // nb,),
        in_specs=[pl.BlockSpec((MLA_HEADS, nb * dec, KV_LORA), lambda b: (0, b, 0)),
                  pl.BlockSpec((MLA_HEADS, nb * dec, ROPE_DIM), lambda b: (0, b, 0)),
                  pl.BlockSpec((nb, past, KV_LORA), lambda b: (b, 0, 0)),
                  pl.BlockSpec((nb, ROPE_DIM, past), lambda b: (b, 0, 0)),
                  pl.BlockSpec((nb * dec, KV_LORA), lambda b: (b, 0)),
                  pl.BlockSpec((nb * dec, ROPE_DIM), lambda b: (b, 0)),
                  pl.BlockSpec((MLA_HEADS, KV_LORA, V_DIM), lambda b: (0, 0, 0))],
        out_specs=pl.BlockSpec((nb * dec, MLA_WIDTH), lambda b: (b, 0)),
        out_shape=jax.ShapeDtypeStruct((batch * dec, MLA_WIDTH), BF16),
        compiler_params=_params("parallel"),
        name="attn_sample",
    )(q_lat, q_rope, cache_ckv, cache_krt, ckvn, kr, w_uvh)


GLA_CHUNK = 256
GLA_DEC_ROWS = 128
GLA_SEQS = 2


def _gla_tables(c, sub):
    nlev = int(np.log2(sub))
    assert 1 << nlev == sub and c % sub == 0 and c // sub <= LANES
    t = np.arange(c)[:, None]
    u = np.arange(c)[None, :]
    same_sub = (t // sub) == (u // sub)
    blocks = []
    lvl = np.full((c, c), -1, np.int32)
    for l in range(nlev):
        width = sub >> l
        half = width // 2
        m = (t // width) * width + half - 1
        upper = t > m
        blocks.append(np.where(upper, (u > m) & (u <= t), (u > t) & (u <= m)))
        same = (t // width) == (u // width)
        lvl[same & ((t % width) >= half) & ((u % width) < half)] = l
    blocks.append((u > t) & same_sub)
    blocks.append((u <= t) & same_sub)
    lvl[np.arange(c), np.arange(c)] = nlev
    sel = (np.arange(c)[:, None] // sub) == np.arange(LANES)[None, :]
    return np.concatenate(blocks, 0).astype(np.float32), lvl, sel.astype(np.float32), nlev


def _gla_kernel(*refs, c, sub, nlev, carry):
    if carry:
        (q_ref, k_ref, v_ref, r_ref, sm_ref, wga_ref, bga_ref, gg_ref, p_ref, lvl_ref, sel_ref,
         o_ref, sfin_ref, s_scr) = refs
        assert sub == c
        j = pl.program_id(1)

        @pl.when(j == 0)
        def _():
            s_scr[...] = jnp.zeros(s_scr.shape, F32)
    else:
        (q_ref, k_ref, v_ref, r_ref, sm_ref, wga_ref, bga_ref, gg_ref, p_ref, lvl_ref, sel_ref, s0_ref,
         o_ref, sfin_ref) = refs

    p_mat = p_ref[...]
    sel = sel_ref[...]
    lvl = lvl_ref[...]
    groups = range(q_ref.shape[0])
    split = []
    for g in groups:
        x = _dot(sm_ref[g].astype(BF16), wga_ref[...]) + bga_ref[...]
        la = (jnp.minimum(x, 0.0) - jnp.log1p(jnp.exp(-jnp.abs(x)))) * (1.0 / GATE_TAU)
        hi = la.astype(BF16)
        split.append((hi, (la - hi.astype(F32)).astype(BF16)))
    for g in groups:
        hi, mid = split[g]
        e_all = jnp.exp(_dot(p_mat, hi) + _dot(p_mat, mid))
        d_all = jnp.exp(_dot_tn(hi, sel) + _dot_tn(mid, sel))
        q = q_ref[g] * (GLA_DK ** -0.5)
        k = k_ref[g]
        qe = [(q * e_all[l * c:(l + 1) * c]).astype(BF16) for l in range(nlev)] + [q.astype(BF16)]
        ke = [(k * e_all[l * c:(l + 1) * c]).astype(BF16) for l in range(nlev)] + [k.astype(BF16)]
        k_end = (k * e_all[nlev * c:(nlev + 1) * c]).astype(BF16)
        q_beg = (q * e_all[(nlev + 1) * c:(nlev + 2) * c]).astype(BF16)
        for h in range(GLA_HEADS):
            dk = slice(h * GLA_DK, (h + 1) * GLA_DK)
            dv = slice(h * GLA_DV, (h + 1) * GLA_DV)
            attn = jnp.zeros((c, c), F32)
            for l in range(nlev + 1):
                attn = jnp.where(lvl == l, _dot_nt(qe[l][:, dk], ke[l][:, dk]), attn)
            v = v_ref[g, :, dv].astype(BF16)
            o_intra = _dot(attn.astype(BF16), v)
            for n in range(c // sub):
                rows = slice(n * sub, (n + 1) * sub)
                state = s_scr[g, h] if carry else s0_ref[n, h]
                o = o_intra[rows] + _dot(q_beg[rows, dk], state.astype(BF16))
                new_state = d_all[dk, n:n + 1] * state + _dot_tn(k_end[rows, dk], v[rows])
                if carry:
                    s_scr[g, h] = new_state
                else:
                    sfin_ref[n, h] = new_state
                r = r_ref[g, rows, dv]
                o_ref[g, rows, dv] = (_rms(o, gg_ref[...]) * (r * jax.nn.sigmoid(r))).astype(BF16)

    if carry:
        @pl.when(j == pl.num_programs(1) - 1)
        def _():
            sfin_ref[...] = s_scr[...]


def _gla(z, zs, w_ga, b_ga, g_gla, batch, seq, c, s0=None):
    carry = s0 is None
    sub = c if carry else seq
    p_np, lvl_np, sel_np, nlev = _gla_tables(c, sub)
    if carry:
        nb, lead, grid = GLA_SEQS, batch, (batch // GLA_SEQS, seq // c)
        at = lambda col: (lambda g, j: (g, j, col))
    else:
        nb, lead, grid = 1, 1, (batch * seq // c, 1)
        at = lambda col: (lambda g, j: (0, g, col))
    nstate = nb if carry else c // sub
    z, zs = z.reshape(lead, -1, Z_GLA), zs.reshape(lead, -1, LANES)
    const = lambda g, j: (0, 0)
    in_specs = [pl.BlockSpec((nb, c, GLA_HEADS * GLA_DK), at(0)),
                pl.BlockSpec((nb, c, GLA_HEADS * GLA_DK), at(1)),
                pl.BlockSpec((nb, c, GLA_WIDTH), at(1)),
                pl.BlockSpec((nb, c, GLA_WIDTH), at(2)),
                pl.BlockSpec((nb, c, LANES), at(0)),
                pl.BlockSpec((LANES, GLA_HEADS * GLA_DK), const),
                pl.BlockSpec((1, GLA_HEADS * GLA_DK), const),
                pl.BlockSpec((1, GLA_DV), const),
                pl.BlockSpec(p_np.shape, const),
                pl.BlockSpec((c, c), const),
                pl.BlockSpec((c, LANES), const)]
    args = [z, z, z, z, zs, w_ga, b_ga, g_gla,
            jnp.asarray(p_np, BF16), jnp.asarray(lvl_np), jnp.asarray(sel_np, BF16)]
    state_spec = pl.BlockSpec((nstate, GLA_HEADS, GLA_DK, GLA_DV), lambda g, j: (g, 0, 0, 0))
    if not carry:
        in_specs.append(state_spec)
        args.append(s0)
    o, s_fin = pl.pallas_call(
        functools.partial(_gla_kernel, c=c, sub=sub, nlev=nlev, carry=carry),
        grid=grid,
        in_specs=in_specs,
        out_specs=[pl.BlockSpec((nb, c, GLA_WIDTH), at(0)), state_spec],
        out_shape=[jax.ShapeDtypeStruct((lead, batch * seq // lead, GLA_WIDTH), BF16),
                   jax.ShapeDtypeStruct((batch, GLA_HEADS, GLA_DK, GLA_DV), F32)],
        scratch_shapes=[pltpu.VMEM((nb, GLA_HEADS, GLA_DK, GLA_DV), F32)] if carry else [],
        compiler_params=_params("parallel", "arbitrary"),
        name="gla" if carry else "gla_init",
    )(*args)
    return o.reshape(batch * seq, GLA_WIDTH), s_fin


ROW_SUB = 128


def _mix_kernel(oma_ref, omb_ref, oga_ref, ogb_ref, w1_ref, w2_ref, ha_ref, hb_ref, g1_ref, g2_ref, h1_ref, f_ref,
                *, na):
    def block(om_ref, og_ref, h_ref):
        for r in range(0, h_ref.shape[0], ROW_SUB):
            rows = slice(r, r + ROW_SUB)
            mix = _dot(om_ref[rows, :], w1_ref[...]) + _dot(og_ref[rows, :], w2_ref[...])
            h1 = h_ref[rows, :] + _rms(mix, g1_ref[...])
            h1_ref[rows, :] = h1
            f_ref[rows, :] = _rms(h1, g2_ref[...]).astype(BF16)

    pl.when(pl.program_id(0) < na)(lambda: block(oma_ref, oga_ref, ha_ref))
    pl.when(pl.program_id(0) >= na)(lambda: block(omb_ref, ogb_ref, hb_ref))


def _mix(o_mla, o_gla, h, w_out, g_post_mix, g_pre_ffn):
    tm = 512
    na, nb = h[0].shape[0] // tm, h[1].shape[0] // tm
    T = (na + nb) * tm
    row = lambda i: (i, 0)
    const = lambda i: (0, 0)
    at_a = lambda i: (jnp.minimum(i, na - 1), 0)
    at_b = lambda i: (jnp.maximum(i - na, 0), 0)
    return pl.pallas_call(
        functools.partial(_mix_kernel, na=na),
        grid=(na + nb,),
        in_specs=[pl.BlockSpec((tm, MLA_WIDTH), at_a), pl.BlockSpec((tm, MLA_WIDTH), at_b),
                  pl.BlockSpec((tm, GLA_WIDTH), at_a), pl.BlockSpec((tm, GLA_WIDTH), at_b),
                  pl.BlockSpec((MLA_WIDTH, D_MODEL), lambda i: (0, 0), pipeline_mode=pl.Buffered(1)),
                  pl.BlockSpec((GLA_WIDTH, D_MODEL), lambda i: (1, 0), pipeline_mode=pl.Buffered(1)),
                  pl.BlockSpec((tm, D_MODEL), at_a), pl.BlockSpec((tm, D_MODEL), at_b),
                  pl.BlockSpec((1, D_MODEL), const),
                  pl.BlockSpec((1, D_MODEL), const)],
        out_specs=[pl.BlockSpec((tm, D_MODEL), row), pl.BlockSpec((tm, D_MODEL), row)],
        out_shape=[jax.ShapeDtypeStruct((T, D_MODEL), F32), jax.ShapeDtypeStruct((T, D_MODEL), BF16)],
        compiler_params=_params("arbitrary"),
        name="mix",
    )(o_mla[0], o_mla[1], o_gla[0], o_gla[1], w_out, w_out, h[0], h[1], g_post_mix, g_pre_ffn)


FFN_MAX_ROWS = 1088


def _ffn_kernel(x_ref, wg_ref, wu_ref, wd_ref, o_ref):
    def partial_out():
        x = x_ref[...]
        g = _dot(x, wg_ref[...].astype(BF16))
        u = _dot(x, wu_ref[...].astype(BF16))
        return _dot((g * jax.nn.sigmoid(g) * u).astype(BF16), wd_ref[...].astype(BF16))

    @pl.when(pl.program_id(1) == 0)
    def _():
        o_ref[...] = partial_out()

    @pl.when(pl.program_id(1) > 0)
    def _():
        o_ref[...] += partial_out()


def _ffn(x, w_gate, w_up, w_down):
    T = x.shape[0]
    tf = 512
    tm = next(T // n for n in range(1, T + 1) if T % n == 0 and T // n <= FFN_MAX_ROWS and (T // n) % 16 == 0)
    return pl.pallas_call(
        _ffn_kernel,
        grid=(T // tm, D_FF // tf),
        in_specs=[pl.BlockSpec((tm, D_MODEL), lambda i, j: (i, 0)),
                  pl.BlockSpec((D_MODEL, tf), lambda i, j: (0, j)),
                  pl.BlockSpec((D_MODEL, tf), lambda i, j: (0, j)),
                  pl.BlockSpec((tf, D_MODEL), lambda i, j: (j, 0))],
        out_specs=pl.BlockSpec((tm, D_MODEL), lambda i, j: (i, 0)),
        out_shape=jax.ShapeDtypeStruct((T, D_MODEL), F32),
        compiler_params=_params("parallel", "arbitrary"),
        name="ffn",
    )(x, w_gate, w_up, w_down)


def _final_kernel(f_ref, h1_ref, pa_ref, pb_ref, g_ref, wpg_ref, wp_ref, ya_ref, yb_ref, *, na):
    def block(p_ref, y_ref):
        h2 = h1_ref[...] + _rms(f_ref[...], g_ref[...])
        gate = jax.nn.sigmoid(_dot(h2.astype(BF16), wpg_ref[...]))
        y_ref[...] = h2 + gate * _dot(p_ref[...].astype(BF16), wp_ref[...])

    pl.when(pl.program_id(0) < na)(lambda: block(pa_ref, ya_ref))
    pl.when(pl.program_id(0) >= na)(lambda: block(pb_ref, yb_ref))


def _final(f, h1, p, g_post_ffn, w_ple_gate, w_ple):
    tm = 512
    na, nb = p[0].shape[0] // tm, p[1].shape[0] // tm
    row = lambda i: (i, 0)
    const = lambda i: (0, 0)
    at_a = lambda i: (jnp.minimum(i, na - 1), 0)
    at_b = lambda i: (jnp.maximum(i - na, 0), 0)
    return pl.pallas_call(
        functools.partial(_final_kernel, na=na),
        grid=(na + nb,),
        in_specs=[pl.BlockSpec((tm, D_MODEL), row),
                  pl.BlockSpec((tm, D_MODEL), row),
                  pl.BlockSpec((tm, PLE_DIM), at_a), pl.BlockSpec((tm, PLE_DIM), at_b),
                  pl.BlockSpec((1, D_MODEL), const),
                  pl.BlockSpec((D_MODEL, D_MODEL), const, pipeline_mode=pl.Buffered(1)),
                  pl.BlockSpec((PLE_DIM, D_MODEL), const)],
        out_specs=[pl.BlockSpec((tm, D_MODEL), at_a), pl.BlockSpec((tm, D_MODEL), at_b)],
        out_shape=[jax.ShapeDtypeStruct((na * tm, D_MODEL), F32), jax.ShapeDtypeStruct((nb * tm, D_MODEL), F32)],
        compiler_params=_params("arbitrary"),
        name="final",
    )(f, h1, p[0], p[1], g_post_ffn, w_ple_gate, w_ple)


def _rope_tables(pos):
    half = ROPE_DIM // 2
    inv = 1.0 / (ROPE_THETA ** (jnp.arange(half, dtype=F32) / half))
    ang = pos.astype(F32)[:, None] * inv[None, :]
    cos, sin = jnp.cos(ang), jnp.sin(ang)
    zero = jnp.zeros((pos.shape[0], LANES - ROPE_DIM), F32)
    return jnp.concatenate([cos, cos, zero], axis=1), jnp.concatenate([-sin, sin, zero], axis=1)


def _layer_weights(i, g_pre_mix, w_in, g_q, w_uq, w_uk, g_kv, w_ga, b_ga, w_uv, g_gla, w_out, g_post_mix,
                   g_pre_ffn, w_gate, w_up, w_down, g_post_ffn, w_ple, w_ple_gate):
    w_uq_p = jnp.pad(w_uq[i].reshape(Q_LORA, MLA_HEADS, NOPE_DIM + ROPE_DIM),
                     ((0, 0), (0, 0), (0, QK_PAD - NOPE_DIM - ROPE_DIM))).reshape(Q_LORA, MLA_HEADS * QK_PAD)
    w_ga_p = jnp.zeros((LANES, GLA_HEADS * GLA_DK), F32).at[GLR_LO:GLR_LO + GATE_RANK].set(w_ga[i])
    vec = lambda g: g[i].reshape(1, -1)
    return dict(
        g_pre_mix=vec(g_pre_mix), w_in=_win_prep(jnp.swapaxes(w_in[i], 0, 1)), g_q=vec(g_q), g_kv=vec(g_kv),
        w_uq=w_uq_p.astype(BF16),
        w_uv=w_uv[i].reshape(KV_LORA, MLA_WIDTH).astype(BF16),
        w_ukt=jnp.transpose(w_uk[i], (1, 2, 0)).astype(BF16),
        w_uvh=jnp.transpose(w_uv[i], (1, 0, 2)).astype(BF16),
        w_ga=w_ga_p.astype(BF16), b_ga=vec(b_ga), g_gla=vec(g_gla),
        w_out=w_out[i].astype(BF16), g_post_mix=vec(g_post_mix), g_pre_ffn=vec(g_pre_ffn),
        w_gate=w_gate[i], w_up=w_up[i], w_down=w_down[i],
        g_post_ffn=vec(g_post_ffn), w_ple=w_ple[i].astype(BF16), w_ple_gate=w_ple_gate[i].astype(BF16))


def _finish(w, h, p, o_mla, o_gla):
    h1, f_in = _mix(o_mla, o_gla, h, w["w_out"], w["g_post_mix"], w["g_pre_ffn"])
    f = _ffn(f_in, w["w_gate"], w["w_up"], w["w_down"])
    return _final(f, h1, p, w["g_post_ffn"], w["w_ple_gate"], w["w_ple"])


def kernel(x_prompt, x_sample, cache_ckv, cache_krope, state_gla, p_prompt, p_sample, g_pre_mix, w_in, g_q, w_uq,
           w_uk, g_kv, w_ga, b_ga, w_uv, g_gla, w_out, g_post_mix, g_pre_ffn, w_gate, w_up, w_down, g_post_ffn,
           w_ple, w_ple_gate):
    batch, seq, _ = x_prompt.shape
    dbatch, dseq, _ = x_sample.shape
    depth = w_in.shape[0]
    cos_p, sin_p = _rope_tables(jnp.arange(seq))
    cos_s, sin_s = _rope_tables(PAST_LEN + jnp.arange(dseq))
    cos_s, sin_s = jnp.tile(cos_s, (dbatch, 1)), jnp.tile(sin_s, (dbatch, 1))
    h_p = x_prompt.reshape(batch * seq, D_MODEL)
    h_s = x_sample.reshape(dbatch * dseq, D_MODEL)
    outs = [[] for _ in range(6)]
    for i in range(depth):
        w = _layer_weights(i, g_pre_mix, w_in, g_q, w_uq, w_uk, g_kv, w_ga, b_ga, w_uv, g_gla, w_out, g_post_mix,
                           g_pre_ffn, w_gate, w_up, w_down, g_post_ffn, w_ple, w_ple_gate)
        z, zs, ckvn, kr, qcat, kt, v = _proj_prompt(h_p, w, cos_p, sin_p, seq)
        o_mla_p = _attn_prompt(qcat, kt, v, batch, seq)
        o_gla_p, s_fin = _gla(z, zs, w["w_ga"], w["b_ga"], w["g_gla"], batch, seq, GLA_CHUNK)
        outs[0].append(ckvn.reshape(batch, seq, KV_LORA))
        outs[1].append(kr.reshape(batch, seq, ROPE_DIM))
        outs[2].append(s_fin)
        z, zs, ckvn, kr, q_lat, q_rope = _proj_sample(h_s, w, cos_s, sin_s)
        o_mla_s = _attn_sample(q_lat, q_rope, cache_ckv[i], jnp.swapaxes(cache_krope[i], 1, 2), ckvn, kr,
                               w["w_uvh"], dseq)
        o_gla_s, s_new = _gla(z, zs, w["w_ga"], w["b_ga"], w["g_gla"], dbatch, dseq, GLA_DEC_ROWS, s0=state_gla[i])
        outs[3].append(ckvn.reshape(dbatch, dseq, KV_LORA))
        outs[4].append(kr.reshape(dbatch, dseq, ROPE_DIM))
        outs[5].append(s_new)
        h_p, h_s = _finish(w, (h_p, h_s),
                           (p_prompt[i].reshape(batch * seq, PLE_DIM), p_sample[i].reshape(dbatch * dseq, PLE_DIM)),
                           (o_mla_p, o_mla_s), (o_gla_p, o_gla_s))
    return (h_p.reshape(batch, seq, D_MODEL), h_s.reshape(dbatch, dseq, D_MODEL),
            jnp.stack(outs[0]), jnp.stack(outs[1]), jnp.stack(outs[2]),
            jnp.stack(outs[3]), jnp.stack(outs[4]), jnp.stack(outs[5]))
```

```python
import functools

import numpy as np
import jax
import jax.numpy as jnp
from jax import lax
from jax.experimental import pallas as pl
from jax.experimental.pallas import tpu as pltpu

F32 = jnp.float32
BF16 = jnp.bfloat16

D_MODEL = 2048
PAST_LEN = 4096
CHUNK = 64
EPS = 1e-6

MLA_HEADS = 8
Q_LORA = 512
KV_LORA = 512
NOPE_DIM = 128
ROPE_DIM = 64
V_DIM = 128
ROPE_THETA = 10000.0
MLA_SCALE = (NOPE_DIM + ROPE_DIM) ** -0.5
QK_PAD = 256

GLA_HEADS = 4
GLA_DK = 128
GLA_DV = 256
GATE_RANK = 16
GATE_TAU = 16.0
GLA_WIDTH = GLA_HEADS * GLA_DV
MLA_WIDTH = MLA_HEADS * V_DIM

IN_SPLITS = (Q_LORA, KV_LORA, ROPE_DIM, GLA_HEADS * GLA_DK, GLA_HEADS * GLA_DK, GLA_WIDTH, GATE_RANK, GLA_WIDTH)
Z_MAIN = 4096
GLR_LO = ROPE_DIM

D_FF = 5632
PLE_DIM = 256

LANES = 128
VMEM_BYTES = 64 * 1024 * 1024
VMEM_LIMIT = VMEM_BYTES - 8 * 1024 * 1024
VMEM_LIMIT_PROJ = VMEM_BYTES - 4 * 1024 * 1024
NEG_BIG = -1e30
LOG2E = 1.4426950408889634
QK_SCALE = MLA_SCALE * LOG2E


def _dot(a, b):
    return jnp.dot(a, b, preferred_element_type=F32)


def _dot_nt(a, b):
    return lax.dot_general(a, b, (((1,), (1,)), ((), ())), preferred_element_type=F32)


def _dot_tn(a, b):
    return lax.dot_general(a, b, (((0,), (0,)), ((), ())), preferred_element_type=F32)


def _rms(x, g):
    ms = jnp.mean(x * x, axis=-1, keepdims=True)
    return x * lax.rsqrt(ms + EPS) * g


def _rope(x, cos_t, sin_t):
    lane = lax.broadcasted_iota(jnp.int32, x.shape, 1)
    rot = jnp.where((lane & 32) == 0, pltpu.roll(x, 96, 1), pltpu.roll(x, 32, 1))
    return x * cos_t + rot * sin_t


def _params(*sem, vmem=VMEM_LIMIT):
    return pltpu.CompilerParams(dimension_semantics=sem, vmem_limit_bytes=vmem)


_IN_OFF = np.concatenate([[0], np.cumsum(IN_SPLITS)]).tolist()
_WIN_TILE = 512


def _win_prep_kernel(w_ref, kr_ref, glr_ref, o_ref, small_ref):
    o_ref[...] = w_ref[...].T.astype(BF16)

    @pl.when(pl.program_id(0) == 0)
    def _():
        pad = jnp.zeros((LANES - GLR_LO - GATE_RANK, D_MODEL), F32)
        small_ref[...] = jnp.concatenate([kr_ref[...], glr_ref[...], pad], axis=0).T.astype(BF16)


def _win_prep(w_t):
    c_q, c_kv, k_r, q_g, k_g, v_g, g_lr, r_g, end = _IN_OFF
    n_head, n_mid = k_r // _WIN_TILE, (g_lr - q_g) // _WIN_TILE
    assert k_r % _WIN_TILE == 0 and (g_lr - q_g) % _WIN_TILE == 0 and (end - r_g) % _WIN_TILE == 0
    assert (n_head + n_mid) * _WIN_TILE + end - r_g == Z_MAIN and GLR_LO == ROPE_DIM

    def src_row(j):
        skip_mid, skip_tail = (q_g - k_r) // 8, (r_g - g_lr + q_g - k_r) // 8
        return 8 * (j * (_WIN_TILE // 8) + jnp.where(j < n_head, 0, jnp.where(j < n_head + n_mid, skip_mid, skip_tail)))

    return pl.pallas_call(
        _win_prep_kernel,
        grid=(Z_MAIN // _WIN_TILE,),
        in_specs=[pl.BlockSpec((pl.Element(_WIN_TILE), pl.Element(D_MODEL)), lambda j: (src_row(j), 0)),
                  pl.BlockSpec((pl.Element(ROPE_DIM), pl.Element(D_MODEL)), lambda j: (k_r, 0)),
                  pl.BlockSpec((pl.Element(GATE_RANK), pl.Element(D_MODEL)), lambda j: (g_lr, 0))],
        out_specs=[pl.BlockSpec((D_MODEL, _WIN_TILE), lambda j: (0, j)),
                   pl.BlockSpec((D_MODEL, LANES), lambda j: (0, 0))],
        out_shape=[jax.ShapeDtypeStruct((D_MODEL, Z_MAIN), BF16), jax.ShapeDtypeStruct((D_MODEL, LANES), BF16)],
        compiler_params=_params("arbitrary"),
        name="win_prep",
    )(w_t, w_t, w_t)


Z_MLA = Q_LORA + KV_LORA
Z_GLA = Z_MAIN - Z_MLA
PROJ_TM = 512


def _inproj(x_ref, g_ref, wm_ref, ws_ref, zg_ref, zs_ref):
    xn = _rms(x_ref[...], g_ref[...]).astype(BF16)
    zg_ref[...] = _dot(xn, wm_ref[:, Z_MLA:])
    sm = _dot(xn, ws_ref[...])
    zs_ref[...] = sm
    z_mla = _dot(xn, wm_ref[:, :Z_MLA])
    return z_mla[:, :Q_LORA], z_mla[:, Q_LORA:], sm


def _inproj_specs(tm):
    row = lambda i: (i, 0)
    const = lambda i: (0, 0)
    in_specs = [pl.BlockSpec((tm, D_MODEL), row),
                pl.BlockSpec((1, D_MODEL), const),
                pl.BlockSpec((D_MODEL, Z_MAIN), const, pipeline_mode=pl.Buffered(1)),
                pl.BlockSpec((D_MODEL, LANES), const, pipeline_mode=pl.Buffered(1)),
                pl.BlockSpec((1, Q_LORA), const),
                pl.BlockSpec((1, KV_LORA), const),
                pl.BlockSpec((Q_LORA, MLA_HEADS * QK_PAD), const, pipeline_mode=pl.Buffered(1))]
    out_specs = [pl.BlockSpec((tm, Z_GLA), row), pl.BlockSpec((tm, LANES), row),
                 pl.BlockSpec((tm, KV_LORA), row), pl.BlockSpec((tm, ROPE_DIM), row)]
    out_shape = lambda T: [jax.ShapeDtypeStruct((T, Z_GLA), F32), jax.ShapeDtypeStruct((T, LANES), F32),
                           jax.ShapeDtypeStruct((T, KV_LORA), F32), jax.ShapeDtypeStruct((T, ROPE_DIM), F32)]
    return in_specs, out_specs, out_shape


def _mla_q(cq, gq_ref, wuq_ref, cos, sin):
    cqn = _rms(cq, gq_ref[...]).astype(BF16)
    q = _dot(cqn, wuq_ref[...]) * QK_SCALE
    out = []
    for h in range(MLA_HEADS):
        nope = q[:, h * QK_PAD:h * QK_PAD + NOPE_DIM]
        rp = _rope(q[:, h * QK_PAD + NOPE_DIM:(h + 1) * QK_PAD], cos, sin)
        out.append((nope, rp))
    return out


def _proj_prompt_kernel(x_ref, g_ref, wm_ref, ws_ref, gq_ref, gkv_ref, wuq_ref, wukt_ref, wuv_ref, cos_ref, sin_ref,
                        zg_ref, zs_ref, ckvn_ref, kr_ref, qcat_ref, kt_ref, v_ref):
    cq, ckv, sm = _inproj(x_ref, g_ref, wm_ref, ws_ref, zg_ref, zs_ref)
    cos, sin = cos_ref[...], sin_ref[...]
    for h, (nope, rp) in enumerate(_mla_q(cq, gq_ref, wuq_ref, cos, sin)):
        qcat_ref[:, h * QK_PAD:h * QK_PAD + NOPE_DIM] = nope.astype(BF16)
        qcat_ref[:, h * QK_PAD + NOPE_DIM:(h + 1) * QK_PAD] = rp.astype(BF16)
    ckvn = _rms(ckv, gkv_ref[...])
    ckvn_ref[...] = ckvn
    kr = _rope(sm, cos, sin)
    kr_ref[...] = kr[:, :ROPE_DIM]
    ckvn_b = ckvn.astype(BF16)
    k_t = _dot_nt(wukt_ref[...], ckvn_b)
    kr_t = kr.T.astype(BF16)
    for h in range(MLA_HEADS):
        kt_ref[h * QK_PAD:h * QK_PAD + NOPE_DIM, :] = k_t[h * NOPE_DIM:(h + 1) * NOPE_DIM].astype(BF16)
        kt_ref[h * QK_PAD + NOPE_DIM:(h + 1) * QK_PAD, :] = kr_t
    v_ref[...] = _dot(ckvn_b, wuv_ref[...]).astype(BF16)


def _proj_prompt(x, w, cos_t, sin_t, seq):
    T = x.shape[0]
    tm = PROJ_TM
    nseq = seq // tm
    row = lambda i: (i, 0)
    const = lambda i: (0, 0)
    in_specs, out_specs, out_shape = _inproj_specs(tm)
    return pl.pallas_call(
        _proj_prompt_kernel,
        grid=(T // tm,),
        in_specs=in_specs + [pl.BlockSpec((MLA_WIDTH, KV_LORA), const, pipeline_mode=pl.Buffered(1)),
                             pl.BlockSpec((KV_LORA, MLA_WIDTH), const, pipeline_mode=pl.Buffered(1)),
                             pl.BlockSpec((tm, LANES), lambda i: (i % nseq, 0)),
                             pl.BlockSpec((tm, LANES), lambda i: (i % nseq, 0))],
        out_specs=out_specs + [pl.BlockSpec((tm, MLA_HEADS * QK_PAD), row),
                               pl.BlockSpec((MLA_HEADS * QK_PAD, tm), lambda i: (0, i)),
                               pl.BlockSpec((tm, MLA_WIDTH), row)],
        out_shape=out_shape(T) + [jax.ShapeDtypeStruct((T, MLA_HEADS * QK_PAD), BF16),
                                  jax.ShapeDtypeStruct((MLA_HEADS * QK_PAD, T), BF16),
                                  jax.ShapeDtypeStruct((T, MLA_WIDTH), BF16)],
        compiler_params=_params("parallel", vmem=VMEM_LIMIT_PROJ),
        name="proj_prompt",
    )(x, w["g_pre_mix"], *w["w_in"], w["g_q"], w["g_kv"], w["w_uq"],
      w["w_ukt"].reshape(MLA_WIDTH, KV_LORA), w["w_uv"], cos_t, sin_t)


def _proj_sample_kernel(x_ref, g_ref, wm_ref, ws_ref, gq_ref, gkv_ref, wuq_ref, wukt_ref, cos_ref, sin_ref,
                        zg_ref, zs_ref, ckvn_ref, kr_ref, qlat_ref, qr_ref):
    cq, ckv, sm = _inproj(x_ref, g_ref, wm_ref, ws_ref, zg_ref, zs_ref)
    cos, sin = cos_ref[...], sin_ref[...]
    for h, (nope, rp) in enumerate(_mla_q(cq, gq_ref, wuq_ref, cos, sin)):
        qlat_ref[h] = _dot(nope.astype(BF16), wukt_ref[h]).astype(BF16)
        qr_ref[h] = rp[:, :ROPE_DIM].astype(BF16)
    ckvn_ref[...] = _rms(ckv, gkv_ref[...])
    kr_ref[...] = _rope(sm, cos, sin)[:, :ROPE_DIM]


def _proj_sample(x, w, cos_t, sin_t):
    T = x.shape[0]
    tm = PROJ_TM
    row = lambda i: (i, 0)
    in_specs, out_specs, out_shape = _inproj_specs(tm)
    return pl.pallas_call(
        _proj_sample_kernel,
        grid=(T // tm,),
        in_specs=in_specs + [pl.BlockSpec((MLA_HEADS, NOPE_DIM, KV_LORA), lambda i: (0, 0, 0)),
                             pl.BlockSpec((tm, LANES), row),
                             pl.BlockSpec((tm, LANES), row)],
        out_specs=out_specs + [pl.BlockSpec((MLA_HEADS, tm, KV_LORA), lambda i: (0, i, 0)),
                               pl.BlockSpec((MLA_HEADS, tm, ROPE_DIM), lambda i: (0, i, 0))],
        out_shape=out_shape(T) + [jax.ShapeDtypeStruct((MLA_HEADS, T, KV_LORA), BF16),
                                  jax.ShapeDtypeStruct((MLA_HEADS, T, ROPE_DIM), BF16)],
        compiler_params=_params("parallel"),
        name="proj_sample",
    )(x, w["g_pre_mix"], *w["w_in"], w["g_q"], w["g_kv"], w["w_uq"], w["w_ukt"], cos_t, sin_t)


ATT_TQ = 512


ATT_PAIRS = 2


def _prompt_tile(q_ref, kt_ref, v_ref, o_ref, visible, i, h):
    lo, hi = i * ATT_TQ, (i + 1) * ATT_TQ
    qk = slice(h * QK_PAD, (h + 1) * QK_PAD)
    dv = slice(h * V_DIM, (h + 1) * V_DIM)
    s = _dot(q_ref[lo:hi, qk], kt_ref[qk, 0:hi])
    s_diag = jnp.where(visible, s[:, lo:hi], NEG_BIG)
    s = jnp.concatenate([s[:, :lo], s_diag], axis=1) if i else s_diag
    m = jnp.max(s, axis=-1, keepdims=True)
    p = jnp.exp2(s - m).astype(BF16)
    v_one = jnp.concatenate([v_ref[0:hi, dv], jnp.ones((hi, V_DIM), BF16)], axis=1)
    ol = _dot(p, v_one)
    o_ref[lo:hi, dv] = (ol[:, :V_DIM] / ol[:, V_DIM:]).astype(BF16)


def _sample_phases(b, ql_ref, qr_ref, ckv_ref, krt_ref, nckv_ref, nkr_ref, wuv_ref, o_ref):
    dec = ql_ref.shape[1] // ckv_ref.shape[0]
    rows = MLA_HEADS * dec
    tok = slice(b * dec, (b + 1) * dec)
    t = {}

    def scores():
        ql = ql_ref[:, tok, :].reshape(rows, KV_LORA)
        qr = qr_ref[:, tok, :].reshape(rows, ROPE_DIM)
        t["ckv"] = ckv_ref[b].astype(BF16)
        t["nckv"] = nckv_ref[tok, :].astype(BF16)
        t["s"] = _dot_nt(ql, t["ckv"]) + _dot(qr, krt_ref[b].astype(BF16))
        t["s_new"] = _dot_nt(ql, t["nckv"]) + _dot_nt(qr, nkr_ref[tok, :].astype(BF16))

    def softmax():
        s, s_new = t["s"], t["s_new"]
        m = jnp.maximum(jnp.max(s, axis=-1, keepdims=True), jnp.max(s_new, axis=-1, keepdims=True))
        t["p"] = jnp.exp2(s - m)
        t["p_new"] = jnp.exp2(s_new - m)
        t["l"] = jnp.sum(t["p"], axis=-1, keepdims=True) + jnp.sum(t["p_new"], axis=-1, keepdims=True)

    def weighted_sum():
        o = _dot(t["p"].astype(BF16), t["ckv"]) + _dot(t["p_new"].astype(BF16), t["nckv"])
        o = (o / t["l"]).astype(BF16)
        for h in range(MLA_HEADS):
            o_ref[tok, h * V_DIM:(h + 1) * V_DIM] = _dot(o[h * dec:(h + 1) * dec], wuv_ref[h]).astype(BF16)

    return [scores, softmax, weighted_sum]


def _attn_kernel(q_ref, kt_ref, v_ref, ql_ref, qr_ref, ckv_ref, krt_ref, nckv_ref, nkr_ref, wuv_ref,
                 o_ref, os_ref):
    seq = q_ref.shape[0]
    qc = lax.broadcasted_iota(jnp.int32, (ATT_TQ, ATT_TQ), 0) // CHUNK
    kc = lax.broadcasted_iota(jnp.int32, (ATT_TQ, ATT_TQ), 1) // CHUNK
    visible = kc <= qc
    tiles = [(i, h) for i in reversed(range(seq // ATT_TQ)) for h in range(ATT_PAIRS)]
    phases = [ph for b in range(ATT_PAIRS)
              for ph in _sample_phases(b, ql_ref, qr_ref, ckv_ref, krt_ref, nckv_ref, nkr_ref, wuv_ref, os_ref)]
    for i, h in tiles:
        _prompt_tile(q_ref, kt_ref, v_ref, o_ref, visible, i, h)
        if phases:
            phases.pop(0)()
    for phase in phases:
        phase()


def _attn(qcat, kt, v, q_lat, q_rope, cache_ckv, cache_krt, ckvn, kr, w_uvh, batch, seq, dec):
    dbatch, past, _ = cache_ckv.shape
    assert dbatch == batch * MLA_HEADS, "one sample cache element is paired with each prompt (batch, head)"
    n, hsteps = ATT_PAIRS, MLA_HEADS // ATT_PAIRS
    grp = lambda b, h: b * hsteps + h
    return pl.pallas_call(
        _attn_kernel,
        grid=(batch, hsteps),
        in_specs=[pl.BlockSpec((seq, n * QK_PAD), lambda b, h: (b, h)),
                  pl.BlockSpec((n * QK_PAD, seq), lambda b, h: (h, b)),
                  pl.BlockSpec((seq, n * V_DIM), lambda b, h: (b, h)),
                  pl.BlockSpec((MLA_HEADS, n * dec, KV_LORA), lambda b, h: (0, grp(b, h), 0)),
                  pl.BlockSpec((MLA_HEADS, n * dec, ROPE_DIM), lambda b, h: (0, grp(b, h), 0)),
                  pl.BlockSpec((n, past, KV_LORA), lambda b, h: (grp(b, h), 0, 0)),
                  pl.BlockSpec((n, ROPE_DIM, past), lambda b, h: (grp(b, h), 0, 0)),
                  pl.BlockSpec((n * dec, KV_LORA), lambda b, h: (grp(b, h), 0)),
                  pl.BlockSpec((n * dec, ROPE_DIM), lambda b, h: (grp(b, h), 0)),
                  pl.BlockSpec((MLA_HEADS, KV_LORA, V_DIM), lambda b, h: (0, 0, 0))],
        out_specs=[pl.BlockSpec((seq, n * V_DIM), lambda b, h: (b, h)),
                   pl.BlockSpec((n * dec, MLA_WIDTH), lambda b, h: (grp(b, h), 0))],
        out_shape=[jax.ShapeDtypeStruct((batch * seq, MLA_WIDTH), BF16),
                   jax.ShapeDtypeStruct((dbatch * dec, MLA_WIDTH), BF16)],
        compiler_params=_params("parallel", "parallel", vmem=VMEM_LIMIT_PROJ),
        name="attn",
    )(qcat, kt, v, q_lat, q_rope, cache_ckv, cache_krt, ckvn, kr, w_uvh)


GLA_CHUNK = 256
GLA_DEC_ROWS = 128
GLA_SEQS = 2


def _gla_tables(c, sub):
    nlev = int(np.log2(sub))
    assert 1 << nlev == sub and c % sub == 0 and c // sub <= LANES
    t = np.arange(c)[:, None]
    u = np.arange(c)[None, :]
    same_sub = (t // sub) == (u // sub)
    blocks = []
    lvl = np.full((c, c), -1, np.int32)
    for l in range(nlev):
        width = sub >> l
        half = width // 2
        m = (t // width) * width + half - 1
        upper = t > m
        blocks.append(np.where(upper, (u > m) & (u <= t), (u > t) & (u <= m)))
        same = (t // width) == (u // width)
        lvl[same & ((t % width) >= half) & ((u % width) < half)] = l
    blocks.append((u > t) & same_sub)
    blocks.append((u <= t) & same_sub)
    lvl[np.arange(c), np.arange(c)] = nlev
    sel = (np.arange(c)[:, None] // sub) == np.arange(LANES)[None, :]
    return np.concatenate(blocks, 0).astype(np.float32), lvl, sel.astype(np.float32), nlev


def _gla_kernel(*refs, c, sub, nlev, carry):
    if carry:
        (q_ref, k_ref, v_ref, r_ref, sm_ref, wga_ref, bga_ref, gg_ref, p_ref, lvl_ref, sel_ref,
         o_ref, sfin_ref, s_scr) = refs
        assert sub == c
        j = pl.program_id(1)

        @pl.when(j == 0)
        def _():
            s_scr[...] = jnp.zeros(s_scr.shape, F32)
    else:
        (q_ref, k_ref, v_ref, r_ref, sm_ref, wga_ref, bga_ref, gg_ref, p_ref, lvl_ref, sel_ref, s0_ref,
         o_ref, sfin_ref) = refs

    p_mat = p_ref[...]
    sel = sel_ref[...]
    lvl = lvl_ref[...]
    groups = range(q_ref.shape[0])
    split = []
    for g in groups:
        x = _dot(sm_ref[g].astype(BF16), wga_ref[...]) + bga_ref[...]
        la = (jnp.minimum(x, 0.0) - jnp.log1p(jnp.exp(-jnp.abs(x)))) * (1.0 / GATE_TAU)
        hi = la.astype(BF16)
        split.append((hi, (la - hi.astype(F32)).astype(BF16)))
    for g in groups:
        hi, mid = split[g]
        e_all = jnp.exp(_dot(p_mat, hi) + _dot(p_mat, mid))
        d_all = jnp.exp(_dot_tn(hi, sel) + _dot_tn(mid, sel))
        q = q_ref[g] * (GLA_DK ** -0.5)
        k = k_ref[g]
        qe = [(q * e_all[l * c:(l + 1) * c]).astype(BF16) for l in range(nlev)] + [q.astype(BF16)]
        ke = [(k * e_all[l * c:(l + 1) * c]).astype(BF16) for l in range(nlev)] + [k.astype(BF16)]
        k_end = (k * e_all[nlev * c:(nlev + 1) * c]).astype(BF16)
        q_beg = (q * e_all[(nlev + 1) * c:(nlev + 2) * c]).astype(BF16)
        for h in range(GLA_HEADS):
            dk = slice(h * GLA_DK, (h + 1) * GLA_DK)
            dv = slice(h * GLA_DV, (h + 1) * GLA_DV)
            attn = jnp.zeros((c, c), F32)
            for l in range(nlev + 1):
                attn = jnp.where(lvl == l, _dot_nt(qe[l][:, dk], ke[l][:, dk]), attn)
            v = v_ref[g, :, dv].astype(BF16)
            o_intra = _dot(attn.astype(BF16), v)
            for n in range(c // sub):
                rows = slice(n * sub, (n + 1) * sub)
                state = s_scr[g, h] if carry else s0_ref[n, h]
                o = o_intra[rows] + _dot(q_beg[rows, dk], state.astype(BF16))
                new_state = d_all[dk, n:n + 1] * state + _dot_tn(k_end[rows, dk], v[rows])
                if carry:
                    s_scr[g, h] = new_state
                else:
                    sfin_ref[n, h] = new_state
                r = r_ref[g, rows, dv]
                o_ref[g, rows, dv] = (_rms(o, gg_ref[...]) * (r * jax.nn.sigmoid(r))).astype(BF16)

    if carry:
        @pl.when(j == pl.num_programs(1) - 1)
        def _():
            sfin_ref[...] = s_scr[...]


def _gla(z, zs, w_ga, b_ga, g_gla, batch, seq, c, s0=None):
    carry = s0 is None
    sub = c if carry else seq
    p_np, lvl_np, sel_np, nlev = _gla_tables(c, sub)
    if carry:
        nb, lead, grid = GLA_SEQS, batch, (batch // GLA_SEQS, seq // c)
        at = lambda col: (lambda g, j: (g, j, col))
    else:
        nb, lead, grid = 1, 1, (batch * seq // c, 1)
        at = lambda col: (lambda g, j: (0, g, col))
    nstate = nb if carry else c // sub
    z, zs = z.reshape(lead, -1, Z_GLA), zs.reshape(lead, -1, LANES)
    const = lambda g, j: (0, 0)
    in_specs = [pl.BlockSpec((nb, c, GLA_HEADS * GLA_DK), at(0)),
                pl.BlockSpec((nb, c, GLA_HEADS * GLA_DK), at(1)),
                pl.BlockSpec((nb, c, GLA_WIDTH), at(1)),
                pl.BlockSpec((nb, c, GLA_WIDTH), at(2)),
                pl.BlockSpec((nb, c, LANES), at(0)),
                pl.BlockSpec((LANES, GLA_HEADS * GLA_DK), const),
                pl.BlockSpec((1, GLA_HEADS * GLA_DK), const),
                pl.BlockSpec((1, GLA_DV), const),
                pl.BlockSpec(p_np.shape, const),
                pl.BlockSpec((c, c), const),
                pl.BlockSpec((c, LANES), const)]
    args = [z, z, z, z, zs, w_ga, b_ga, g_gla,
            jnp.asarray(p_np, BF16), jnp.asarray(lvl_np), jnp.asarray(sel_np, BF16)]
    state_spec = pl.BlockSpec((nstate, GLA_HEADS, GLA_DK, GLA_DV), lambda g, j: (g, 0, 0, 0))
    if not carry:
        in_specs.append(state_spec)
        args.append(s0)
    o, s_fin = pl.pallas_call(
        functools.partial(_gla_kernel, c=c, sub=sub, nlev=nlev, carry=carry),
        grid=grid,
        in_specs=in_specs,
        out_specs=[pl.BlockSpec((nb, c, GLA_WIDTH), at(0)), state_spec],
        out_shape=[jax.ShapeDtypeStruct((lead, batch * seq // lead, GLA_WIDTH), BF16),
                   jax.ShapeDtypeStruct((batch, GLA_HEADS, GLA_DK, GLA_DV), F32)],
        scratch_shapes=[pltpu.VMEM((nb, GLA_HEADS, GLA_DK, GLA_DV), F32)] if carry else [],
        compiler_params=_params("parallel", "arbitrary"),
        name="gla" if carry else "gla_init",
    )(*args)
    return o.reshape(batch * seq, GLA_WIDTH), s_fin


ROW_SUB = 128


def _mix_kernel(oma_ref, omb_ref, oga_ref, ogb_ref, w1_ref, w2_ref, ha_ref, hb_ref, g1_ref, g2_ref, h1_ref, f_ref,
                *, na):
    def block(om_ref, og_ref, h_ref):
        for r in range(0, h_ref.shape[0], ROW_SUB):
            rows = slice(r, r + ROW_SUB)
            mix = _dot(om_ref[rows, :], w1_ref[...]) + _dot(og_ref[rows, :], w2_ref[...])
            h1 = h_ref[rows, :] + _rms(mix, g1_ref[...])
            h1_ref[rows, :] = h1
            f_ref[rows, :] = _rms(h1, g2_ref[...]).astype(BF16)

    pl.when(pl.program_id(0) < na)(lambda: block(oma_ref, oga_ref, ha_ref))
    pl.when(pl.program_id(0) >= na)(lambda: block(omb_ref, ogb_ref, hb_ref))


def _mix(o_mla, o_gla, h, w_out, g_post_mix, g_pre_ffn):
    tm = 512
    na, nb = h[0].shape[0] // tm, h[1].shape[0] // tm
    T = (na + nb) * tm
    row = lambda i: (i, 0)
    const = lambda i: (0, 0)
    at_a = lambda i: (jnp.minimum(i, na - 1), 0)
    at_b = lambda i: (jnp.maximum(i - na, 0), 0)
    return pl.pallas_call(
        functools.partial(_mix_kernel, na=na),
        grid=(na + nb,),
        in_specs=[pl.BlockSpec((tm, MLA_WIDTH), at_a), pl.BlockSpec((tm, MLA_WIDTH), at_b),
                  pl.BlockSpec((tm, GLA_WIDTH), at_a), pl.BlockSpec((tm, GLA_WIDTH), at_b),
                  pl.BlockSpec((MLA_WIDTH, D_MODEL), lambda i: (0, 0), pipeline_mode=pl.Buffered(1)),
                  pl.BlockSpec((GLA_WIDTH, D_MODEL), lambda i: (1, 0), pipeline_mode=pl.Buffered(1)),
                  pl.BlockSpec((tm, D_MODEL), at_a), pl.BlockSpec((tm, D_MODEL), at_b),
                  pl.BlockSpec((1, D_MODEL), const),
                  pl.BlockSpec((1, D_MODEL), const)],
        out_specs=[pl.BlockSpec((tm, D_MODEL), row), pl.BlockSpec((tm, D_MODEL), row)],
        out_shape=[jax.ShapeDtypeStruct((T, D_MODEL), F32), jax.ShapeDtypeStruct((T, D_MODEL), BF16)],
        compiler_params=_params("arbitrary"),
        name="mix",
    )(o_mla[0], o_mla[1], o_gla[0], o_gla[1], w_out, w_out, h[0], h[1], g_post_mix, g_pre_ffn)


FFN_MAX_ROWS = 1088


def _ffn_kernel(x_ref, wg_ref, wu_ref, wd_ref, o_ref):
    def partial_out():
        x = x_ref[...]
        g = _dot(x, wg_ref[...].astype(BF16))
        u = _dot(x, wu_ref[...].astype(BF16))
        return _dot((g * jax.nn.sigmoid(g) * u).astype(BF16), wd_ref[...].astype(BF16))

    @pl.when(pl.program_id(1) == 0)
    def _():
        o_ref[...] = partial_out()

    @pl.when(pl.program_id(1) > 0)
    def _():
        o_ref[...] += partial_out()


def _ffn(x, w_gate, w_up, w_down):
    T = x.shape[0]
    tf = 512
    tm = next(T // n for n in range(1, T + 1) if T % n == 0 and T // n <= FFN_MAX_ROWS and (T // n) % 16 == 0)
    return pl.pallas_call(
        _ffn_kernel,
        grid=(T // tm, D_FF // tf),
        in_specs=[pl.BlockSpec((tm, D_MODEL), lambda i, j: (i, 0)),
                  pl.BlockSpec((D_MODEL, tf), lambda i, j: (0, j)),
                  pl.BlockSpec((D_MODEL, tf), lambda i, j: (0, j)),
                  pl.BlockSpec((tf, D_MODEL), lambda i, j: (j, 0))],
        out_specs=pl.BlockSpec((tm, D_MODEL), lambda i, j: (i, 0)),
        out_shape=jax.ShapeDtypeStruct((T, D_MODEL), F32),
        compiler_params=_params("parallel", "arbitrary"),
        name="ffn",
    )(x, w_gate, w_up, w_down)


def _final_kernel(f_ref, h1_ref, pa_ref, pb_ref, g_ref, wpg_ref, wp_ref, ya_ref, yb_ref, *, na):
    def block(p_ref, y_ref):
        h2 = h1_ref[...] + _rms(f_ref[...], g_ref[...])
        gate = jax.nn.sigmoid(_dot(h2.astype(BF16), wpg_ref[...]))
        y_ref[...] = h2 + gate * _dot(p_ref[...].astype(BF16), wp_ref[...])

    pl.when(pl.program_id(0) < na)(lambda: block(pa_ref, ya_ref))
    pl.when(pl.program_id(0) >= na)(lambda: block(pb_ref, yb_ref))


def _final(f, h1, p, g_post_ffn, w_ple_gate, w_ple):
    tm = 512
    na, nb = p[0].shape[0] // tm, p[1].shape[0] // tm
    row = lambda i: (i, 0)
    const = lambda i: (0, 0)
    at_a = lambda i: (jnp.minimum(i, na - 1), 0)
    at_b = lambda i: (jnp.maximum(i - na, 0), 0)
    return pl.pallas_call(
        functools.partial(_final_kernel, na=na),
        grid=(na + nb,),
        in_specs=[pl.BlockSpec((tm, D_MODEL), row),
                  pl.BlockSpec((tm, D_MODEL), row),
                  pl.BlockSpec((tm, PLE_DIM), at_a), pl.BlockSpec((tm, PLE_DIM), at_b),
                  pl.BlockSpec((1, D_MODEL), const),
                  pl.BlockSpec((D_MODEL, D_MODEL), const, pipeline_mode=pl.Buffered(1)),
                  pl.BlockSpec((PLE_DIM, D_MODEL), const)],
        out_specs=[pl.BlockSpec((tm, D_MODEL), at_a), pl.BlockSpec((tm, D_MODEL), at_b)],
        out_shape=[jax.ShapeDtypeStruct((na * tm, D_MODEL), F32), jax.ShapeDtypeStruct((nb * tm, D_MODEL), F32)],
        compiler_params=_params("arbitrary"),
        name="final",
    )(f, h1, p[0], p[1], g_post_ffn, w_ple_gate, w_ple)


def _rope_tables(pos):
    half = ROPE_DIM // 2
    inv = 1.0 / (ROPE_THETA ** (jnp.arange(half, dtype=F32) / half))
    ang = pos.astype(F32)[:, None] * inv[None, :]
    cos, sin = jnp.cos(ang), jnp.sin(ang)
    zero = jnp.zeros((pos.shape[0], LANES - ROPE_DIM), F32)
    return jnp.concatenate([cos, cos, zero], axis=1), jnp.concatenate([-sin, sin, zero], axis=1)


def _layer_weights(i, g_pre_mix, w_in, g_q, w_uq, w_uk, g_kv, w_ga, b_ga, w_uv, g_gla, w_out, g_post_mix,
                   g_pre_ffn, w_gate, w_up, w_down, g_post_ffn, w_ple, w_ple_gate):
    w_uq_p = jnp.pad(w_uq[i].reshape(Q_LORA, MLA_HEADS, NOPE_DIM + ROPE_DIM),
                     ((0, 0), (0, 0), (0, QK_PAD - NOPE_DIM - ROPE_DIM))).reshape(Q_LORA, MLA_HEADS * QK_PAD)
    w_ga_p = jnp.zeros((LANES, GLA_HEADS * GLA_DK), F32).at[GLR_LO:GLR_LO + GATE_RANK].set(w_ga[i])
    vec = lambda g: g[i].reshape(1, -1)
    return dict(
        g_pre_mix=vec(g_pre_mix), w_in=_win_prep(jnp.swapaxes(w_in[i], 0, 1)), g_q=vec(g_q), g_kv=vec(g_kv),
        w_uq=w_uq_p.astype(BF16),
        w_uv=w_uv[i].reshape(KV_LORA, MLA_WIDTH).astype(BF16),
        w_ukt=jnp.transpose(w_uk[i], (1, 2, 0)).astype(BF16),
        w_uvh=jnp.transpose(w_uv[i], (1, 0, 2)).astype(BF16),
        w_ga=w_ga_p.astype(BF16), b_ga=vec(b_ga), g_gla=vec(g_gla),
        w_out=w_out[i].astype(BF16), g_post_mix=vec(g_post_mix), g_pre_ffn=vec(g_pre_ffn),
        w_gate=w_gate[i], w_up=w_up[i], w_down=w_down[i],
        g_post_ffn=vec(g_post_ffn), w_ple=w_ple[i].astype(BF16), w_ple_gate=w_ple_gate[i].astype(BF16))


def _finish(w, h, p, o_mla, o_gla):
    h1, f_in = _mix(o_mla, o_gla, h, w["w_out"], w["g_post_mix"], w["g_pre_ffn"])
    f = _ffn(f_in, w["w_gate"], w["w_up"], w["w_down"])
    return _final(f, h1, p, w["g_post_ffn"], w["w_ple_gate"], w["w_ple"])


def kernel(x_prompt, x_sample, cache_ckv, cache_krope, state_gla, p_prompt, p_sample, g_pre_mix, w_in, g_q, w_uq,
           w_uk, g_kv, w_ga, b_ga, w_uv, g_gla, w_out, g_post_mix, g_pre_ffn, w_gate, w_up, w_down, g_post_ffn,
           w_ple, w_ple_gate):
    batch, seq, _ = x_prompt.shape
    dbatch, dseq, _ = x_sample.shape
    depth = w_in.shape[0]
    cos_p, sin_p = _rope_tables(jnp.arange(seq))
    cos_s, sin_s = _rope_tables(PAST_LEN + jnp.arange(dseq))
    cos_s, sin_s = jnp.tile(cos_s, (dbatch, 1)), jnp.tile(sin_s, (dbatch, 1))
    h_p = x_prompt.reshape(batch * seq, D_MODEL)
    h_s = x_sample.reshape(dbatch * dseq, D_MODEL)
    outs = [[] for _ in range(6)]
    for i in range(depth):
        w = _layer_weights(i, g_pre_mix, w_in, g_q, w_uq, w_uk, g_kv, w_ga, b_ga, w_uv, g_gla, w_out, g_post_mix,
                           g_pre_ffn, w_gate, w_up, w_down, g_post_ffn, w_ple, w_ple_gate)
        z_p, zs_p, ckvn_p, kr_p, qcat, kt, v = _proj_prompt(h_p, w, cos_p, sin_p, seq)
        z_s, zs_s, ckvn_s, kr_s, q_lat, q_rope = _proj_sample(h_s, w, cos_s, sin_s)
        o_mla_p, o_mla_s = _attn(qcat, kt, v, q_lat, q_rope, cache_ckv[i], jnp.swapaxes(cache_krope[i], 1, 2),
                                 ckvn_s, kr_s, w["w_uvh"], batch, seq, dseq)
        o_gla_p, s_fin = _gla(z_p, zs_p, w["w_ga"], w["b_ga"], w["g_gla"], batch, seq, GLA_CHUNK)
        o_gla_s, s_new = _gla(z_s, zs_s, w["w_ga"], w["b_ga"], w["g_gla"], dbatch, dseq, GLA_DEC_ROWS,
                              s0=state_gla[i])
        outs[0].append(ckvn_p.reshape(batch, seq, KV_LORA))
        outs[1].append(kr_p.reshape(batch, seq, ROPE_DIM))
        outs[2].append(s_fin)
        outs[3].append(ckvn_s.reshape(dbatch, dseq, KV_LORA))
        outs[4].append(kr_s.reshape(dbatch, dseq, ROPE_DIM))
        outs[5].append(s_new)
        h_p, h_s = _finish(w, (h_p, h_s),
                           (p_prompt[i].reshape(batch * seq, PLE_DIM), p_sample[i].reshape(dbatch * dseq, PLE_DIM)),
                           (o_mla_p, o_mla_s), (o_gla_p, o_gla_s))
    return (h_p.reshape(batch, seq, D_MODEL), h_s.reshape(dbatch, dseq, D_MODEL),
            jnp.stack(outs[0]), jnp.stack(outs[1]), jnp.stack(outs[2]),
            jnp.stack(outs[3]), jnp.stack(outs[4]), jnp.stack(outs[5]))
```

```python
import functools

import numpy as np
import jax
import jax.numpy as jnp
from jax import lax
from jax.experimental import pallas as pl
from jax.experimental.pallas import tpu as pltpu

F32 = jnp.float32
BF16 = jnp.bfloat16

D_MODEL = 2048
PAST_LEN = 4096
CHUNK = 64
EPS = 1e-6

MLA_HEADS = 8
Q_LORA = 512
KV_LORA = 512
NOPE_DIM = 128
ROPE_DIM = 64
V_DIM = 128
ROPE_THETA = 10000.0
MLA_SCALE = (NOPE_DIM + ROPE_DIM) ** -0.5
QK_PAD = 256

GLA_HEADS = 4
GLA_DK = 128
GLA_DV = 256
GATE_RANK = 16
GATE_TAU = 16.0
GLA_WIDTH = GLA_HEADS * GLA_DV
MLA_WIDTH = MLA_HEADS * V_DIM

IN_SPLITS = (Q_LORA, KV_LORA, ROPE_DIM, GLA_HEADS * GLA_DK, GLA_HEADS * GLA_DK, GLA_WIDTH, GATE_RANK, GLA_WIDTH)
Z_MAIN = 4096
GLR_LO = ROPE_DIM

D_FF = 5632
PLE_DIM = 256

LANES = 128
VMEM_BYTES = 64 * 1024 * 1024
VMEM_LIMIT = VMEM_BYTES - 8 * 1024 * 1024
VMEM_LIMIT_PROJ = VMEM_BYTES - 4 * 1024 * 1024
NEG_BIG = -1e30
LOG2E = 1.4426950408889634
QK_SCALE = MLA_SCALE * LOG2E


def _dot(a, b):
    return jnp.dot(a, b, preferred_element_type=F32)


def _dot_nt(a, b):
    return lax.dot_general(a, b, (((1,), (1,)), ((), ())), preferred_element_type=F32)


def _dot_tn(a, b):
    return lax.dot_general(a, b, (((0,), (0,)), ((), ())), preferred_element_type=F32)


def _rms(x, g):
    ms = jnp.mean(x * x, axis=-1, keepdims=True)
    return x * lax.rsqrt(ms + EPS) * g


def _rope(x, cos_t, sin_t):
    lane = lax.broadcasted_iota(jnp.int32, x.shape, 1)
    rot = jnp.where((lane & 32) == 0, pltpu.roll(x, 96, 1), pltpu.roll(x, 32, 1))
    return x * cos_t + rot * sin_t


def _params(*sem, vmem=VMEM_LIMIT):
    return pltpu.CompilerParams(dimension_semantics=sem, vmem_limit_bytes=vmem)


_IN_OFF = np.concatenate([[0], np.cumsum(IN_SPLITS)]).tolist()
_WIN_TILE = 512


def _win_prep_kernel(w_ref, kr_ref, glr_ref, o_ref, small_ref):
    o_ref[...] = w_ref[...].T.astype(BF16)

    @pl.when(pl.program_id(0) == 0)
    def _():
        pad = jnp.zeros((LANES - GLR_LO - GATE_RANK, D_MODEL), F32)
        small_ref[...] = jnp.concatenate([kr_ref[...], glr_ref[...], pad], axis=0).T.astype(BF16)


def _win_prep(w_t):
    c_q, c_kv, k_r, q_g, k_g, v_g, g_lr, r_g, end = _IN_OFF
    n_head, n_mid = k_r // _WIN_TILE, (g_lr - q_g) // _WIN_TILE
    assert k_r % _WIN_TILE == 0 and (g_lr - q_g) % _WIN_TILE == 0 and (end - r_g) % _WIN_TILE == 0
    assert (n_head + n_mid) * _WIN_TILE + end - r_g == Z_MAIN and GLR_LO == ROPE_DIM

    def src_row(j):
        skip_mid, skip_tail = (q_g - k_r) // 8, (r_g - g_lr + q_g - k_r) // 8
        return 8 * (j * (_WIN_TILE // 8) + jnp.where(j < n_head, 0, jnp.where(j < n_head + n_mid, skip_mid, skip_tail)))

    return pl.pallas_call(
        _win_prep_kernel,
        grid=(Z_MAIN // _WIN_TILE,),
        in_specs=[pl.BlockSpec((pl.Element(_WIN_TILE), pl.Element(D_MODEL)), lambda j: (src_row(j), 0)),
                  pl.BlockSpec((pl.Element(ROPE_DIM), pl.Element(D_MODEL)), lambda j: (k_r, 0)),
                  pl.BlockSpec((pl.Element(GATE_RANK), pl.Element(D_MODEL)), lambda j: (g_lr, 0))],
        out_specs=[pl.BlockSpec((D_MODEL, _WIN_TILE), lambda j: (0, j)),
                   pl.BlockSpec((D_MODEL, LANES), lambda j: (0, 0))],
        out_shape=[jax.ShapeDtypeStruct((D_MODEL, Z_MAIN), BF16), jax.ShapeDtypeStruct((D_MODEL, LANES), BF16)],
        compiler_params=_params("arbitrary"),
        name="win_prep",
    )(w_t, w_t, w_t)


Z_MLA = Q_LORA + KV_LORA
Z_GLA = Z_MAIN - Z_MLA
PROJ_TM = 512


def _inproj(x_ref, g_ref, wm_ref, ws_ref, zg_ref, zs_ref):
    xn = _rms(x_ref[...], g_ref[...]).astype(BF16)
    zg_ref[...] = _dot(xn, wm_ref[:, Z_MLA:])
    sm = _dot(xn, ws_ref[...])
    zs_ref[...] = sm
    z_mla = _dot(xn, wm_ref[:, :Z_MLA])
    return z_mla[:, :Q_LORA], z_mla[:, Q_LORA:], sm


def _inproj_specs(tm):
    row = lambda i: (i, 0)
    const = lambda i: (0, 0)
    in_specs = [pl.BlockSpec((tm, D_MODEL), row),
                pl.BlockSpec((1, D_MODEL), const),
                pl.BlockSpec((D_MODEL, Z_MAIN), const, pipeline_mode=pl.Buffered(1)),
                pl.BlockSpec((D_MODEL, LANES), const, pipeline_mode=pl.Buffered(1)),
                pl.BlockSpec((1, Q_LORA), const),
                pl.BlockSpec((1, KV_LORA), const),
                pl.BlockSpec((Q_LORA, MLA_HEADS * (NOPE_DIM + ROPE_DIM)), const, pipeline_mode=pl.Buffered(1))]
    out_specs = [pl.BlockSpec((tm, Z_GLA), row), pl.BlockSpec((tm, LANES), row),
                 pl.BlockSpec((tm, KV_LORA), row), pl.BlockSpec((tm, ROPE_DIM), row)]
    out_shape = lambda T: [jax.ShapeDtypeStruct((T, Z_GLA), F32), jax.ShapeDtypeStruct((T, LANES), F32),
                           jax.ShapeDtypeStruct((T, KV_LORA), F32), jax.ShapeDtypeStruct((T, ROPE_DIM), F32)]
    return in_specs, out_specs, out_shape


def _mla_q(cq, gq_ref, wuq_ref, cos, sin):
    cqn = _rms(cq, gq_ref[...]).astype(BF16)
    q = _dot(cqn, wuq_ref[...]) * QK_SCALE
    cos2 = cos + pltpu.roll(cos, ROPE_DIM, 1)
    sin2 = sin + pltpu.roll(sin, ROPE_DIM, 1)
    low = lax.broadcasted_iota(jnp.int32, cos.shape, 1) < ROPE_DIM
    out = []
    for j in range(MLA_HEADS // 2):
        first = MLA_HEADS * NOPE_DIM + j * LANES
        rot = _rope(q[:, first:first + LANES], cos2, sin2)
        for h, rp in ((2 * j, rot), (2 * j + 1, pltpu.roll(rot, ROPE_DIM, 1))):
            out.append((q[:, h * NOPE_DIM:(h + 1) * NOPE_DIM], jnp.where(low, rp, 0.0)))
    return out


def _proj_prompt_kernel(x_ref, g_ref, wm_ref, ws_ref, gq_ref, gkv_ref, wuq_ref, wukt_ref, wuv_ref, cos_ref, sin_ref,
                        zg_ref, zs_ref, ckvn_ref, kr_ref, qcat_ref, kt_ref, v_ref):
    cq, ckv, sm = _inproj(x_ref, g_ref, wm_ref, ws_ref, zg_ref, zs_ref)
    cos, sin = cos_ref[...], sin_ref[...]
    for h, (nope, rp) in enumerate(_mla_q(cq, gq_ref, wuq_ref, cos, sin)):
        qcat_ref[:, h * QK_PAD:h * QK_PAD + NOPE_DIM] = nope.astype(BF16)
        qcat_ref[:, h * QK_PAD + NOPE_DIM:(h + 1) * QK_PAD] = rp.astype(BF16)
    ckvn = _rms(ckv, gkv_ref[...])
    ckvn_ref[...] = ckvn
    kr = _rope(sm, cos, sin)
    kr_ref[...] = kr[:, :ROPE_DIM]
    ckvn_b = ckvn.astype(BF16)
    k_t = _dot_nt(wukt_ref[...], ckvn_b)
    kr_t = kr.T.astype(BF16)
    for h in range(MLA_HEADS):
        kt_ref[h * QK_PAD:h * QK_PAD + NOPE_DIM, :] = k_t[h * NOPE_DIM:(h + 1) * NOPE_DIM].astype(BF16)
        kt_ref[h * QK_PAD + NOPE_DIM:(h + 1) * QK_PAD, :] = kr_t
    v_ref[...] = _dot(ckvn_b, wuv_ref[...]).astype(BF16)


def _proj_prompt(x, w, cos_t, sin_t, seq):
    T = x.shape[0]
    tm = PROJ_TM
    nseq = seq // tm
    row = lambda i: (i, 0)
    const = lambda i: (0, 0)
    in_specs, out_specs, out_shape = _inproj_specs(tm)
    return pl.pallas_call(
        _proj_prompt_kernel,
        grid=(T // tm,),
        in_specs=in_specs + [pl.BlockSpec((MLA_WIDTH, KV_LORA), const, pipeline_mode=pl.Buffered(1)),
                             pl.BlockSpec((KV_LORA, MLA_WIDTH), const, pipeline_mode=pl.Buffered(1)),
                             pl.BlockSpec((tm, LANES), lambda i: (i % nseq, 0)),
                             pl.BlockSpec((tm, LANES), lambda i: (i % nseq, 0))],
        out_specs=out_specs + [pl.BlockSpec((tm, MLA_HEADS * QK_PAD), row),
                               pl.BlockSpec((MLA_HEADS * QK_PAD, tm), lambda i: (0, i)),
                               pl.BlockSpec((tm, MLA_WIDTH), row)],
        out_shape=out_shape(T) + [jax.ShapeDtypeStruct((T, MLA_HEADS * QK_PAD), BF16),
                                  jax.ShapeDtypeStruct((MLA_HEADS * QK_PAD, T), BF16),
                                  jax.ShapeDtypeStruct((T, MLA_WIDTH), BF16)],
        compiler_params=_params("parallel", vmem=VMEM_LIMIT_PROJ),
        name="proj_prompt",
    )(x, w["g_pre_mix"], *w["w_in"], w["g_q"], w["g_kv"], w["w_uq"],
      w["w_ukt"].reshape(MLA_WIDTH, KV_LORA), w["w_uv"], cos_t, sin_t)


def _proj_sample_kernel(x_ref, g_ref, wm_ref, ws_ref, gq_ref, gkv_ref, wuq_ref, wukt_ref, cos_ref, sin_ref,
                        zg_ref, zs_ref, ckvn_ref, kr_ref, qlat_ref, qr_ref):
    cq, ckv, sm = _inproj(x_ref, g_ref, wm_ref, ws_ref, zg_ref, zs_ref)
    cos, sin = cos_ref[...], sin_ref[...]
    for h, (nope, rp) in enumerate(_mla_q(cq, gq_ref, wuq_ref, cos, sin)):
        qlat_ref[h] = _dot(nope.astype(BF16), wukt_ref[h]).astype(BF16)
        qr_ref[h] = rp[:, :ROPE_DIM].astype(BF16)
    ckvn_ref[...] = _rms(ckv, gkv_ref[...])
    kr_ref[...] = _rope(sm, cos, sin)[:, :ROPE_DIM]


def _proj_sample(x, w, cos_t, sin_t):
    T = x.shape[0]
    tm = PROJ_TM
    row = lambda i: (i, 0)
    in_specs, out_specs, out_shape = _inproj_specs(tm)
    return pl.pallas_call(
        _proj_sample_kernel,
        grid=(T // tm,),
        in_specs=in_specs + [pl.BlockSpec((MLA_HEADS, NOPE_DIM, KV_LORA), lambda i: (0, 0, 0)),
                             pl.BlockSpec((tm, LANES), row),
                             pl.BlockSpec((tm, LANES), row)],
        out_specs=out_specs + [pl.BlockSpec((MLA_HEADS, tm, KV_LORA), lambda i: (0, i, 0)),
                               pl.BlockSpec((MLA_HEADS, tm, ROPE_DIM), lambda i: (0, i, 0))],
        out_shape=out_shape(T) + [jax.ShapeDtypeStruct((MLA_HEADS, T, KV_LORA), BF16),
                                  jax.ShapeDtypeStruct((MLA_HEADS, T, ROPE_DIM), BF16)],
        compiler_params=_params("parallel"),
        name="proj_sample",
    )(x, w["g_pre_mix"], *w["w_in"], w["g_q"], w["g_kv"], w["w_uq"], w["w_ukt"], cos_t, sin_t)


ATT_TQ = 512
ATT_HEADS = 4


def _attn_prompt_kernel(q_ref, kt_ref, v_ref, o_ref):
    seq = q_ref.shape[0]
    qc = lax.broadcasted_iota(jnp.int32, (ATT_TQ, ATT_TQ), 0) // CHUNK
    kc = lax.broadcasted_iota(jnp.int32, (ATT_TQ, ATT_TQ), 1) // CHUNK
    visible = kc <= qc
    for i in reversed(range(seq // ATT_TQ)):
        lo, hi = i * ATT_TQ, (i + 1) * ATT_TQ
        for h in range(ATT_HEADS):
            qk = slice(h * QK_PAD, (h + 1) * QK_PAD)
            dv = slice(h * V_DIM, (h + 1) * V_DIM)
            s = _dot(q_ref[lo:hi, qk], kt_ref[qk, 0:hi])
            s_diag = jnp.where(visible, s[:, lo:hi], NEG_BIG)
            s = jnp.concatenate([s[:, :lo], s_diag], axis=1) if i else s_diag
            m = jnp.max(s, axis=-1, keepdims=True)
            p = jnp.exp2(s - m).astype(BF16)
            v_one = jnp.concatenate([v_ref[0:hi, dv], jnp.ones((hi, V_DIM), BF16)], axis=1)
            ol = _dot(p, v_one)
            o_ref[lo:hi, dv] = (ol[:, :V_DIM] / ol[:, V_DIM:]).astype(BF16)


def _attn_prompt(qcat, kt, v, batch, seq):
    return pl.pallas_call(
        _attn_prompt_kernel,
        grid=(batch, MLA_HEADS // ATT_HEADS),
        in_specs=[pl.BlockSpec((seq, ATT_HEADS * QK_PAD), lambda b, h: (b, h)),
                  pl.BlockSpec((ATT_HEADS * QK_PAD, seq), lambda b, h: (h, b)),
                  pl.BlockSpec((seq, ATT_HEADS * V_DIM), lambda b, h: (b, h))],
        out_specs=pl.BlockSpec((seq, ATT_HEADS * V_DIM), lambda b, h: (b, h)),
        out_shape=jax.ShapeDtypeStruct((batch * seq, MLA_WIDTH), BF16),
        compiler_params=_params("parallel", "parallel"),
        name="attn_prompt",
    )(qcat, kt, v)


DEC_SEQS = 2


def _attn_sample_kernel(ql_ref, qr_ref, ckv_ref, krt_ref, nckv_ref, nkr_ref, wuv_ref, o_ref):
    nb = ckv_ref.shape[0]
    dec = ql_ref.shape[1] // nb
    rows = MLA_HEADS * dec
    for b in range(nb):
        tok = slice(b * dec, (b + 1) * dec)
        ql = ql_ref[:, tok, :].reshape(rows, KV_LORA)
        qr = qr_ref[:, tok, :].reshape(rows, ROPE_DIM)
        ckv = ckv_ref[b].astype(BF16)
        nckv = nckv_ref[tok, :].astype(BF16)
        s = _dot_nt(ql, ckv) + _dot(qr, krt_ref[b].astype(BF16))
        s_new = _dot_nt(ql, nckv) + _dot_nt(qr, nkr_ref[tok, :].astype(BF16))
        m = jnp.maximum(jnp.max(s, axis=-1, keepdims=True), jnp.max(s_new, axis=-1, keepdims=True))
        p = jnp.exp2(s - m)
        p_new = jnp.exp2(s_new - m)
        l = jnp.sum(p, axis=-1, keepdims=True) + jnp.sum(p_new, axis=-1, keepdims=True)
        o = ((_dot(p.astype(BF16), ckv) + _dot(p_new.astype(BF16), nckv)) / l).astype(BF16)
        for h in range(MLA_HEADS):
            o_ref[tok, h * V_DIM:(h + 1) * V_DIM] = _dot(o[h * dec:(h + 1) * dec], wuv_ref[h]).astype(BF16)


def _attn_sample(q_lat, q_rope, cache_ckv, cache_krt, ckvn, kr, w_uvh, dec):
    batch, past, _ = cache_ckv.shape
    nb = DEC_SEQS
    return pl.pallas_call(
        _attn_sample_kernel,
        grid=(batch // nb,),
        in_specs=[pl.BlockSpec((MLA_HEADS, nb * dec, KV_LORA), lambda b: (0, b, 0)),
                  pl.BlockSpec((MLA_HEADS, nb * dec, ROPE_DIM), lambda b: (0, b, 0)),
                  pl.BlockSpec((nb, past, KV_LORA), lambda b: (b, 0, 0)),
                  pl.BlockSpec((nb, ROPE_DIM, past), lambda b: (b, 0, 0)),
                  pl.BlockSpec((nb * dec, KV_LORA), lambda b: (b, 0)),
                  pl.BlockSpec((nb * dec, ROPE_DIM), lambda b: (b, 0)),
                  pl.BlockSpec((MLA_HEADS, KV_LORA, V_DIM), lambda b: (0, 0, 0))],
        out_specs=pl.BlockSpec((nb * dec, MLA_WIDTH), lambda b: (b, 0)),
        out_shape=jax.ShapeDtypeStruct((batch * dec, MLA_WIDTH), BF16),
        compiler_params=_params("parallel"),
        name="attn_sample",
    )(q_lat, q_rope, cache_ckv, cache_krt, ckvn, kr, w_uvh)


GLA_CHUNK = 256
GLA_DEC_ROWS = 128
GLA_SEQS = 2


def _gla_tables(c, sub):
    nlev = int(np.log2(sub))
    assert 1 << nlev == sub and c % sub == 0 and c // sub <= LANES
    t = np.arange(c)[:, None]
    u = np.arange(c)[None, :]
    same_sub = (t // sub) == (u // sub)
    blocks = []
    lvl = np.full((c, c), -1, np.int32)
    for l in range(nlev):
        width = sub >> l
        half = width // 2
        m = (t // width) * width + half - 1
        upper = t > m
        blocks.append(np.where(upper, (u > m) & (u <= t), (u > t) & (u <= m)))
        same = (t // width) == (u // width)
        lvl[same & ((t % width) >= half) & ((u % width) < half)] = l
    blocks.append((u > t) & same_sub)
    blocks.append((u <= t) & same_sub)
    lvl[np.arange(c), np.arange(c)] = nlev
    sel = (np.arange(c)[:, None] // sub) == np.arange(LANES)[None, :]
    return np.concatenate(blocks, 0).astype(np.float32), lvl, sel.astype(np.float32), nlev


def _gla_kernel(*refs, c, sub, nlev, carry):
    if carry:
        (q_ref, k_ref, v_ref, r_ref, sm_ref, wga_ref, bga_ref, gg_ref, p_ref, lvl_ref, sel_ref,
         o_ref, sfin_ref, s_scr) = refs
        assert sub == c
        j = pl.program_id(1)

        @pl.when(j == 0)
        def _():
            s_scr[...] = jnp.zeros(s_scr.shape, F32)
    else:
        (q_ref, k_ref, v_ref, r_ref, sm_ref, wga_ref, bga_ref, gg_ref, p_ref, lvl_ref, sel_ref, s0_ref,
         o_ref, sfin_ref) = refs

    p_mat = p_ref[...]
    sel = sel_ref[...]
    lvl = lvl_ref[...]
    groups = range(q_ref.shape[0])
    split = []
    for g in groups:
        x = _dot(sm_ref[g].astype(BF16), wga_ref[...]) + bga_ref[...]
        la = (jnp.minimum(x, 0.0) - jnp.log1p(jnp.exp(-jnp.abs(x)))) * (1.0 / GATE_TAU)
        hi = la.astype(BF16)
        split.append((hi, (la - hi.astype(F32)).astype(BF16)))
    for g in groups:
        hi, mid = split[g]
        e_all = jnp.exp(_dot(p_mat, hi) + _dot(p_mat, mid))
        d_all = jnp.exp(_dot_tn(hi, sel) + _dot_tn(mid, sel))
        q = q_ref[g] * (GLA_DK ** -0.5)
        k = k_ref[g]
        qe = [(q * e_all[l * c:(l + 1) * c]).astype(BF16) for l in range(nlev)] + [q.astype(BF16)]
        ke = [(k * e_all[l * c:(l + 1) * c]).astype(BF16) for l in range(nlev)] + [k.astype(BF16)]
        k_end = (k * e_all[nlev * c:(nlev + 1) * c]).astype(BF16)
        q_beg = (q * e_all[(nlev + 1) * c:(nlev + 2) * c]).astype(BF16)
        for h in range(GLA_HEADS):
            dk = slice(h * GLA_DK, (h + 1) * GLA_DK)
            dv = slice(h * GLA_DV, (h + 1) * GLA_DV)
            attn = jnp.zeros((c, c), F32)
            for l in range(nlev + 1):
                attn = jnp.where(lvl == l, _dot_nt(qe[l][:, dk], ke[l][:, dk]), attn)
            v = v_ref[g, :, dv].astype(BF16)
            o_intra = _dot(attn.astype(BF16), v)
            for n in range(c // sub):
                rows = slice(n * sub, (n + 1) * sub)
                state = s_scr[g, h] if carry else s0_ref[n, h]
                o = o_intra[rows] + _dot(q_beg[rows, dk], state.astype(BF16))
                new_state = d_all[dk, n:n + 1] * state + _dot_tn(k_end[rows, dk], v[rows])
                if carry:
                    s_scr[g, h] = new_state
                else:
                    sfin_ref[n, h] = new_state
                r = r_ref[g, rows, dv]
                o_ref[g, rows, dv] = (_rms(o, gg_ref[...]) * (r * jax.nn.sigmoid(r))).astype(BF16)

    if carry:
        @pl.when(j == pl.num_programs(1) - 1)
        def _():
            sfin_ref[...] = s_scr[...]


def _gla(z, zs, w_ga, b_ga, g_gla, batch, seq, c, s0=None):
    carry = s0 is None
    sub = c if carry else seq
    p_np, lvl_np, sel_np, nlev = _gla_tables(c, sub)
    if carry:
        nb, lead, grid = GLA_SEQS, batch, (batch // GLA_SEQS, seq // c)
        at = lambda col: (lambda g, j: (g, j, col))
    else:
        nb, lead, grid = 1, 1, (batch * seq // c, 1)
        at = lambda col: (lambda g, j: (0, g, col))
    nstate = nb if carry else c // sub
    z, zs = z.reshape(lead, -1, Z_GLA), zs.reshape(lead, -1, LANES)
    const = lambda g, j: (0, 0)
    in_specs = [pl.BlockSpec((nb, c, GLA_HEADS * GLA_DK), at(0)),
                pl.BlockSpec((nb, c, GLA_HEADS * GLA_DK), at(1)),
                pl.BlockSpec((nb, c, GLA_WIDTH), at(1)),
                pl.BlockSpec((nb, c, GLA_WIDTH), at(2)),
                pl.BlockSpec((nb, c, LANES), at(0)),
                pl.BlockSpec((LANES, GLA_HEADS * GLA_DK), const),
                pl.BlockSpec((1, GLA_HEADS * GLA_DK), const),
                pl.BlockSpec((1, GLA_DV), const),
                pl.BlockSpec(p_np.shape, const),
                pl.BlockSpec((c, c), const),
                pl.BlockSpec((c, LANES), const)]
    args = [z, z, z, z, zs, w_ga, b_ga, g_gla,
            jnp.asarray(p_np, BF16), jnp.asarray(lvl_np), jnp.asarray(sel_np, BF16)]
    state_spec = pl.BlockSpec((nstate, GLA_HEADS, GLA_DK, GLA_DV), lambda g, j: (g, 0, 0, 0))
    if not carry:
        in_specs.append(state_spec)
        args.append(s0)
    o, s_fin = pl.pallas_call(
        functools.partial(_gla_kernel, c=c, sub=sub, nlev=nlev, carry=carry),
        grid=grid,
        in_specs=in_specs,
        out_specs=[pl.BlockSpec((nb, c, GLA_WIDTH), at(0)), state_spec],
        out_shape=[jax.ShapeDtypeStruct((lead, batch * seq // lead, GLA_WIDTH), BF16),
                   jax.ShapeDtypeStruct((batch, GLA_HEADS, GLA_DK, GLA_DV), F32)],
        scratch_shapes=[pltpu.VMEM((nb, GLA_HEADS, GLA_DK, GLA_DV), F32)] if carry else [],
        compiler_params=_params("parallel", "arbitrary"),
        name="gla" if carry else "gla_init",
    )(*args)
    return o.reshape(batch * seq, GLA_WIDTH), s_fin


ROW_SUB = 128


def _mix_kernel(oma_ref, omb_ref, oga_ref, ogb_ref, w1_ref, w2_ref, ha_ref, hb_ref, g1_ref, g2_ref, h1_ref, f_ref,
                *, na):
    def block(om_ref, og_ref, h_ref):
        for r in range(0, h_ref.shape[0], ROW_SUB):
            rows = slice(r, r + ROW_SUB)
            mix = _dot(om_ref[rows, :], w1_ref[...]) + _dot(og_ref[rows, :], w2_ref[...])
            h1 = h_ref[rows, :] + _rms(mix, g1_ref[...])
            h1_ref[rows, :] = h1
            f_ref[rows, :] = _rms(h1, g2_ref[...]).astype(BF16)

    pl.when(pl.program_id(0) < na)(lambda: block(oma_ref, oga_ref, ha_ref))
    pl.when(pl.program_id(0) >= na)(lambda: block(omb_ref, ogb_ref, hb_ref))


def _mix(o_mla, o_gla, h, w_out, g_post_mix, g_pre_ffn):
    tm = 512
    na, nb = h[0].shape[0] // tm, h[1].shape[0] // tm
    T = (na + nb) * tm
    row = lambda i: (i, 0)
    const = lambda i: (0, 0)
    at_a = lambda i: (jnp.minimum(i, na - 1), 0)
    at_b = lambda i: (jnp.maximum(i - na, 0), 0)
    return pl.pallas_call(
        functools.partial(_mix_kernel, na=na),
        grid=(na + nb,),
        in_specs=[pl.BlockSpec((tm, MLA_WIDTH), at_a), pl.BlockSpec((tm, MLA_WIDTH), at_b),
                  pl.BlockSpec((tm, GLA_WIDTH), at_a), pl.BlockSpec((tm, GLA_WIDTH), at_b),
                  pl.BlockSpec((MLA_WIDTH, D_MODEL), lambda i: (0, 0), pipeline_mode=pl.Buffered(1)),
                  pl.BlockSpec((GLA_WIDTH, D_MODEL), lambda i: (1, 0), pipeline_mode=pl.Buffered(1)),
                  pl.BlockSpec((tm, D_MODEL), at_a), pl.BlockSpec((tm, D_MODEL), at_b),
                  pl.BlockSpec((1, D_MODEL), const),
                  pl.BlockSpec((1, D_MODEL), const)],
        out_specs=[pl.BlockSpec((tm, D_MODEL), row), pl.BlockSpec((tm, D_MODEL), row)],
        out_shape=[jax.ShapeDtypeStruct((T, D_MODEL), F32), jax.ShapeDtypeStruct((T, D_MODEL), BF16)],
        compiler_params=_params("arbitrary"),
        name="mix",
    )(o_mla[0], o_mla[1], o_gla[0], o_gla[1], w_out, w_out, h[0], h[1], g_post_mix, g_pre_ffn)


FFN_MAX_ROWS = 1088


def _ffn_kernel(x_ref, wg_ref, wu_ref, wd_ref, o_ref):
    def partial_out():
        x = x_ref[...]
        g = _dot(x, wg_ref[...].astype(BF16))
        u = _dot(x, wu_ref[...].astype(BF16))
        return _dot((g * jax.nn.sigmoid(g) * u).astype(BF16), wd_ref[...].astype(BF16))

    @pl.when(pl.program_id(1) == 0)
    def _():
        o_ref[...] = partial_out()

    @pl.when(pl.program_id(1) > 0)
    def _():
        o_ref[...] += partial_out()


def _ffn(x, w_gate, w_up, w_down):
    T = x.shape[0]
    tf = 512
    tm = next(T // n for n in range(1, T + 1) if T % n == 0 and T // n <= FFN_MAX_ROWS and (T // n) % 16 == 0)
    return pl.pallas_call(
        _ffn_kernel,
        grid=(T // tm, D_FF // tf),
        in_specs=[pl.BlockSpec((tm, D_MODEL), lambda i, j: (i, 0)),
                  pl.BlockSpec((D_MODEL, tf), lambda i, j: (0, j)),
                  pl.BlockSpec((D_MODEL, tf), lambda i, j: (0, j)),
                  pl.BlockSpec((tf, D_MODEL), lambda i, j: (j, 0))],
        out_specs=pl.BlockSpec((tm, D_MODEL), lambda i, j: (i, 0)),
        out_shape=jax.ShapeDtypeStruct((T, D_MODEL), F32),
        compiler_params=_params("parallel", "arbitrary"),
        name="ffn",
    )(x, w_gate, w_up, w_down)


def _final_kernel(f_ref, h1_ref, pa_ref, pb_ref, g_ref, wpg_ref, wp_ref, ya_ref, yb_ref, *, na):
    def block(p_ref, y_ref):
        h2 = h1_ref[...] + _rms(f_ref[...], g_ref[...])
        gate = jax.nn.sigmoid(_dot(h2.astype(BF16), wpg_ref[...]))
        y_ref[...] = h2 + gate * _dot(p_ref[...].astype(BF16), wp_ref[...])

    pl.when(pl.program_id(0) < na)(lambda: block(pa_ref, ya_ref))
    pl.when(pl.program_id(0) >= na)(lambda: block(pb_ref, yb_ref))


def _final(f, h1, p, g_post_ffn, w_ple_gate, w_ple):
    tm = 512
    na, nb = p[0].shape[0] // tm, p[1].shape[0] // tm
    row = lambda i: (i, 0)
    const = lambda i: (0, 0)
    at_a = lambda i: (jnp.minimum(i, na - 1), 0)
    at_b = lambda i: (jnp.maximum(i - na, 0), 0)
    return pl.pallas_call(
        functools.partial(_final_kernel, na=na),
        grid=(na + nb,),
        in_specs=[pl.BlockSpec((tm, D_MODEL), row),
                  pl.BlockSpec((tm, D_MODEL), row),
                  pl.BlockSpec((tm, PLE_DIM), at_a), pl.BlockSpec((tm, PLE_DIM), at_b),
                  pl.BlockSpec((1, D_MODEL), const),
                  pl.BlockSpec((D_MODEL, D_MODEL), const, pipeline_mode=pl.Buffered(1)),
                  pl.BlockSpec((PLE_DIM, D_MODEL), const)],
        out_specs=[pl.BlockSpec((tm, D_MODEL), at_a), pl.BlockSpec((tm, D_MODEL), at_b)],
        out_shape=[jax.ShapeDtypeStruct((na * tm, D_MODEL), F32), jax.ShapeDtypeStruct((nb * tm, D_MODEL), F32)],
        compiler_params=_params("arbitrary"),
        name="final",
    )(f, h1, p[0], p[1], g_post_ffn, w_ple_gate, w_ple)


def _rope_tables(pos):
    half = ROPE_DIM // 2
    inv = 1.0 / (ROPE_THETA ** (jnp.arange(half, dtype=F32) / half))
    ang = pos.astype(F32)[:, None] * inv[None, :]
    cos, sin = jnp.cos(ang), jnp.sin(ang)
    zero = jnp.zeros((pos.shape[0], LANES - ROPE_DIM), F32)
    return jnp.concatenate([cos, cos, zero], axis=1), jnp.concatenate([-sin, sin, zero], axis=1)


def _layer_weights(i, g_pre_mix, w_in, g_q, w_uq, w_uk, g_kv, w_ga, b_ga, w_uv, g_gla, w_out, g_post_mix,
                   g_pre_ffn, w_gate, w_up, w_down, g_post_ffn, w_ple, w_ple_gate):
    w_uq_h = w_uq[i].reshape(Q_LORA, MLA_HEADS, NOPE_DIM + ROPE_DIM)
    w_uq_p = jnp.concatenate([w_uq_h[:, :, :NOPE_DIM].reshape(Q_LORA, MLA_HEADS * NOPE_DIM),
                              w_uq_h[:, :, NOPE_DIM:].reshape(Q_LORA, MLA_HEADS * ROPE_DIM)], axis=1)
    w_ga_p = jnp.zeros((LANES, GLA_HEADS * GLA_DK), F32).at[GLR_LO:GLR_LO + GATE_RANK].set(w_ga[i])
    vec = lambda g: g[i].reshape(1, -1)
    return dict(
        g_pre_mix=vec(g_pre_mix), w_in=_win_prep(jnp.swapaxes(w_in[i], 0, 1)), g_q=vec(g_q), g_kv=vec(g_kv),
        w_uq=w_uq_p.astype(BF16),
        w_uv=w_uv[i].reshape(KV_LORA, MLA_WIDTH).astype(BF16),
        w_ukt=jnp.transpose(w_uk[i], (1, 2, 0)).astype(BF16),
        w_uvh=jnp.transpose(w_uv[i], (1, 0, 2)).astype(BF16),
        w_ga=w_ga_p.astype(BF16), b_ga=vec(b_ga), g_gla=vec(g_gla),
        w_out=w_out[i].astype(BF16), g_post_mix=vec(g_post_mix), g_pre_ffn=vec(g_pre_ffn),
        w_gate=w_gate[i], w_up=w_up[i], w_down=w_down[i],
        g_post_ffn=vec(g_post_ffn), w_ple=w_ple[i].astype(BF16), w_ple_gate=w_ple_gate[i].astype(BF16))


def _finish(w, h, p, o_mla, o_gla):
    h1, f_in = _mix(o_mla, o_gla, h, w["w_out"], w["g_post_mix"], w["g_pre_ffn"])
    f = _ffn(f_in, w["w_gate"], w["w_up"], w["w_down"])
    return _final(f, h1, p, w["g_post_ffn"], w["w_ple_gate"], w["w_ple"])


def kernel(x_prompt, x_sample, cache_ckv, cache_krope, state_gla, p_prompt, p_sample, g_pre_mix, w_in, g_q, w_uq,
           w_uk, g_kv, w_ga, b_ga, w_uv, g_gla, w_out, g_post_mix, g_pre_ffn, w_gate, w_up, w_down, g_post_ffn,
           w_ple, w_ple_gate):
    batch, seq, _ = x_prompt.shape
    dbatch, dseq, _ = x_sample.shape
    depth = w_in.shape[0]
    cos_p, sin_p = _rope_tables(jnp.arange(seq))
    cos_s, sin_s = _rope_tables(PAST_LEN + jnp.arange(dseq))
    cos_s, sin_s = jnp.tile(cos_s, (dbatch, 1)), jnp.tile(sin_s, (dbatch, 1))
    h_p = x_prompt.reshape(batch * seq, D_MODEL)
    h_s = x_sample.reshape(dbatch * dseq, D_MODEL)
    outs = [[] for _ in range(6)]
    for i in range(depth):
        w = _layer_weights(i, g_pre_mix, w_in, g_q, w_uq, w_uk, g_kv, w_ga, b_ga, w_uv, g_gla, w_out, g_post_mix,
                           g_pre_ffn, w_gate, w_up, w_down, g_post_ffn, w_ple, w_ple_gate)
        z, zs, ckvn, kr, qcat, kt, v = _proj_prompt(h_p, w, cos_p, sin_p, seq)
        o_mla_p = _attn_prompt(qcat, kt, v, batch, seq)
        o_gla_p, s_fin = _gla(z, zs, w["w_ga"], w["b_ga"], w["g_gla"], batch, seq, GLA_CHUNK)
        outs[0].append(ckvn.reshape(batch, seq, KV_LORA))
        outs[1].append(kr.reshape(batch, seq, ROPE_DIM))
        outs[2].append(s_fin)
        z, zs, ckvn, kr, q_lat, q_rope = _proj_sample(h_s, w, cos_s, sin_s)
        o_mla_s = _attn_sample(q_lat, q_rope, cache_ckv[i], jnp.swapaxes(cache_krope[i], 1, 2), ckvn, kr,
                               w["w_uvh"], dseq)
        o_gla_s, s_new = _gla(z, zs, w["w_ga"], w["b_ga"], w["g_gla"], dbatch, dseq, GLA_DEC_ROWS, s0=state_gla[i])
        outs[3].append(ckvn.reshape(dbatch, dseq, KV_LORA))
        outs[4].append(kr.reshape(dbatch, dseq, ROPE_DIM))
        outs[5].append(s_new)
        h_p, h_s = _finish(w, (h_p, h_s),
                           (p_prompt[i].reshape(batch * seq, PLE_DIM), p_sample[i].reshape(dbatch * dseq, PLE_DIM)),
                           (o_mla_p, o_mla_s), (o_gla_p, o_gla_s))
    return (h_p.reshape(batch, seq, D_MODEL), h_s.reshape(dbatch, dseq, D_MODEL),
            jnp.stack(outs[0]), jnp.stack(outs[1]), jnp.stack(outs[2]),
            jnp.stack(outs[3]), jnp.stack(outs[4]), jnp.stack(outs[5]))
```

```python
import functools

import numpy as np
import jax
import jax.numpy as jnp
from jax import lax
from jax.experimental import pallas as pl
from jax.experimental.pallas import tpu as pltpu

F32 = jnp.float32
BF16 = jnp.bfloat16

D_MODEL = 2048
PAST_LEN = 4096
CHUNK = 64
EPS = 1e-6

MLA_HEADS = 8
Q_LORA = 512
KV_LORA = 512
NOPE_DIM = 128
ROPE_DIM = 64
V_DIM = 128
ROPE_THETA = 10000.0
MLA_SCALE = (NOPE_DIM + ROPE_DIM) ** -0.5
QK_PAD = 256

GLA_HEADS = 4
GLA_DK = 128
GLA_DV = 256
GATE_RANK = 16
GATE_TAU = 16.0
GLA_WIDTH = GLA_HEADS * GLA_DV
MLA_WIDTH = MLA_HEADS * V_DIM

IN_SPLITS = (Q_LORA, KV_LORA, ROPE_DIM, GLA_HEADS * GLA_DK, GLA_HEADS * GLA_DK, GLA_WIDTH, GATE_RANK, GLA_WIDTH)
Z_MAIN = 4096
GLR_LO = ROPE_DIM

D_FF = 5632
PLE_DIM = 256

LANES = 128
VMEM_BYTES = 64 * 1024 * 1024
VMEM_LIMIT = VMEM_BYTES - 8 * 1024 * 1024
VMEM_LIMIT_PROJ = VMEM_BYTES - 4 * 1024 * 1024
NEG_BIG = -1e30
LOG2E = 1.4426950408889634
QK_SCALE = MLA_SCALE * LOG2E


def _dot(a, b):
    return jnp.dot(a, b, preferred_element_type=F32)


def _dot_nt(a, b):
    return lax.dot_general(a, b, (((1,), (1,)), ((), ())), preferred_element_type=F32)


def _dot_tn(a, b):
    return lax.dot_general(a, b, (((0,), (0,)), ((), ())), preferred_element_type=F32)


def _rms(x, g):
    ms = jnp.mean(x * x, axis=-1, keepdims=True)
    return x * lax.rsqrt(ms + EPS) * g


def _rope(x, cos_t, sin_t):
    lane = lax.broadcasted_iota(jnp.int32, x.shape, 1)
    rot = jnp.where((lane & 32) == 0, pltpu.roll(x, 96, 1), pltpu.roll(x, 32, 1))
    return x * cos_t + rot * sin_t


def _params(*sem, vmem=VMEM_LIMIT):
    return pltpu.CompilerParams(dimension_semantics=sem, vmem_limit_bytes=vmem)


_IN_OFF = np.concatenate([[0], np.cumsum(IN_SPLITS)]).tolist()
_WIN_TILE = 512


def _win_prep_kernel(w_ref, kr_ref, glr_ref, o_ref, small_ref):
    o_ref[...] = w_ref[...].T.astype(BF16)

    @pl.when(pl.program_id(0) == 0)
    def _():
        pad = jnp.zeros((LANES - GLR_LO - GATE_RANK, D_MODEL), F32)
        small_ref[...] = jnp.concatenate([kr_ref[...], glr_ref[...], pad], axis=0).T.astype(BF16)


def _win_prep(w_t):
    c_q, c_kv, k_r, q_g, k_g, v_g, g_lr, r_g, end = _IN_OFF
    n_head, n_mid = k_r // _WIN_TILE, (g_lr - q_g) // _WIN_TILE
    assert k_r % _WIN_TILE == 0 and (g_lr - q_g) % _WIN_TILE == 0 and (end - r_g) % _WIN_TILE == 0
    assert (n_head + n_mid) * _WIN_TILE + end - r_g == Z_MAIN and GLR_LO == ROPE_DIM

    def src_row(j):
        skip_mid, skip_tail = (q_g - k_r) // 8, (r_g - g_lr + q_g - k_r) // 8
        return 8 * (j * (_WIN_TILE // 8) + jnp.where(j < n_head, 0, jnp.where(j < n_head + n_mid, skip_mid, skip_tail)))

    return pl.pallas_call(
        _win_prep_kernel,
        grid=(Z_MAIN // _WIN_TILE,),
        in_specs=[pl.BlockSpec((pl.Element(_WIN_TILE), pl.Element(D_MODEL)), lambda j: (src_row(j), 0)),
                  pl.BlockSpec((pl.Element(ROPE_DIM), pl.Element(D_MODEL)), lambda j: (k_r, 0)),
                  pl.BlockSpec((pl.Element(GATE_RANK), pl.Element(D_MODEL)), lambda j: (g_lr, 0))],
        out_specs=[pl.BlockSpec((D_MODEL, _WIN_TILE), lambda j: (0, j)),
                   pl.BlockSpec((D_MODEL, LANES), lambda j: (0, 0))],
        out_shape=[jax.ShapeDtypeStruct((D_MODEL, Z_MAIN), BF16), jax.ShapeDtypeStruct((D_MODEL, LANES), BF16)],
        compiler_params=_params("arbitrary"),
        name="win_prep",
    )(w_t, w_t, w_t)


Z_MLA = Q_LORA + KV_LORA
Z_GLA = Z_MAIN - Z_MLA
PROJ_TM = 512


def _inproj(x_ref, g_ref, wm_ref, ws_ref, zg_ref, zs_ref):
    xn = _rms(x_ref[...], g_ref[...]).astype(BF16)
    zg_ref[...] = _dot(xn, wm_ref[:, Z_MLA:])
    sm = _dot(xn, ws_ref[...])
    zs_ref[...] = sm
    z_mla = _dot(xn, wm_ref[:, :Z_MLA])
    return z_mla[:, :Q_LORA], z_mla[:, Q_LORA:], sm


def _inproj_specs(tm):
    row = lambda i: (i, 0)
    const = lambda i: (0, 0)
    in_specs = [pl.BlockSpec((tm, D_MODEL), row),
                pl.BlockSpec((1, D_MODEL), const),
                pl.BlockSpec((D_MODEL, Z_MAIN), const, pipeline_mode=pl.Buffered(1)),
                pl.BlockSpec((D_MODEL, LANES), const, pipeline_mode=pl.Buffered(1)),
                pl.BlockSpec((1, Q_LORA), const),
                pl.BlockSpec((1, KV_LORA), const),
                pl.BlockSpec((Q_LORA, MLA_HEADS * (NOPE_DIM + ROPE_DIM)), const, pipeline_mode=pl.Buffered(1))]
    out_specs = [pl.BlockSpec((tm, Z_GLA), row), pl.BlockSpec((tm, LANES), row),
                 pl.BlockSpec((tm, KV_LORA), row), pl.BlockSpec((tm, ROPE_DIM), row)]
    out_shape = lambda T: [jax.ShapeDtypeStruct((T, Z_GLA), F32), jax.ShapeDtypeStruct((T, LANES), F32),
                           jax.ShapeDtypeStruct((T, KV_LORA), F32), jax.ShapeDtypeStruct((T, ROPE_DIM), F32)]
    return in_specs, out_specs, out_shape


def _mla_q(cq, gq_ref, wuq_ref, cos, sin):
    cqn = _rms(cq, gq_ref[...]).astype(BF16)
    q = _dot(cqn, wuq_ref[...]) * QK_SCALE
    cos2 = cos + pltpu.roll(cos, ROPE_DIM, 1)
    sin2 = sin + pltpu.roll(sin, ROPE_DIM, 1)
    low = lax.broadcasted_iota(jnp.int32, cos.shape, 1) < ROPE_DIM
    out = []
    for j in range(MLA_HEADS // 2):
        first = MLA_HEADS * NOPE_DIM + j * LANES
        rot = _rope(q[:, first:first + LANES], cos2, sin2)
        for h, rp in ((2 * j, rot), (2 * j + 1, pltpu.roll(rot, ROPE_DIM, 1))):
            out.append((q[:, h * NOPE_DIM:(h + 1) * NOPE_DIM], jnp.where(low, rp, 0.0)))
    return out


def _proj_prompt_kernel(x_ref, g_ref, wm_ref, ws_ref, gq_ref, gkv_ref, wuq_ref, wukt_ref, wuv_ref, cos_ref, sin_ref,
                        zg_ref, zs_ref, ckvn_ref, kr_ref, qcat_ref, kt_ref, v_ref):
    cq, ckv, sm = _inproj(x_ref, g_ref, wm_ref, ws_ref, zg_ref, zs_ref)
    cos, sin = cos_ref[...], sin_ref[...]
    for h, (nope, rp) in enumerate(_mla_q(cq, gq_ref, wuq_ref, cos, sin)):
        qcat_ref[:, h * QK_PAD:h * QK_PAD + NOPE_DIM] = nope.astype(BF16)
        qcat_ref[:, h * QK_PAD + NOPE_DIM:(h + 1) * QK_PAD] = rp.astype(BF16)
    ckvn = _rms(ckv, gkv_ref[...])
    ckvn_ref[...] = ckvn
    kr = _rope(sm, cos, sin)
    kr_ref[...] = kr[:, :ROPE_DIM]
    ckvn_b = ckvn.astype(BF16)
    k_t = _dot_nt(wukt_ref[...], ckvn_b)
    kr_t = kr.T.astype(BF16)
    for h in range(MLA_HEADS):
        kt_ref[h * QK_PAD:h * QK_PAD + NOPE_DIM, :] = k_t[h * NOPE_DIM:(h + 1) * NOPE_DIM].astype(BF16)
        kt_ref[h * QK_PAD + NOPE_DIM:(h + 1) * QK_PAD, :] = kr_t
    v_ref[...] = _dot(ckvn_b, wuv_ref[...]).astype(BF16)


def _proj_prompt(x, w, cos_t, sin_t, seq):
    T = x.shape[0]
    tm = PROJ_TM
    nseq = seq // tm
    row = lambda i: (i, 0)
    const = lambda i: (0, 0)
    in_specs, out_specs, out_shape = _inproj_specs(tm)
    return pl.pallas_call(
        _proj_prompt_kernel,
        grid=(T // tm,),
        in_specs=in_specs + [pl.BlockSpec((MLA_WIDTH, KV_LORA), const, pipeline_mode=pl.Buffered(1)),
                             pl.BlockSpec((KV_LORA, MLA_WIDTH), const, pipeline_mode=pl.Buffered(1)),
                             pl.BlockSpec((tm, LANES), lambda i: (i % nseq, 0)),
                             pl.BlockSpec((tm, LANES), lambda i: (i % nseq, 0))],
        out_specs=out_specs + [pl.BlockSpec((tm, MLA_HEADS * QK_PAD), row),
                               pl.BlockSpec((MLA_HEADS * QK_PAD, tm), lambda i: (0, i)),
                               pl.BlockSpec((tm, MLA_WIDTH), row)],
        out_shape=out_shape(T) + [jax.ShapeDtypeStruct((T, MLA_HEADS * QK_PAD), BF16),
                                  jax.ShapeDtypeStruct((MLA_HEADS * QK_PAD, T), BF16),
                                  jax.ShapeDtypeStruct((T, MLA_WIDTH), BF16)],
        compiler_params=_params("parallel", vmem=VMEM_LIMIT_PROJ),
        name="proj_prompt",
    )(x, w["g_pre_mix"], *w["w_in"], w["g_q"], w["g_kv"], w["w_uq"],
      w["w_ukt"].reshape(MLA_WIDTH, KV_LORA), w["w_uv"], cos_t, sin_t)


def _proj_sample_kernel(x_ref, g_ref, wm_ref, ws_ref, gq_ref, gkv_ref, wuq_ref, wukt_ref, cos_ref, sin_ref,
                        zg_ref, zs_ref, ckvn_ref, kr_ref, qlat_ref, qr_ref):
    cq, ckv, sm = _inproj(x_ref, g_ref, wm_ref, ws_ref, zg_ref, zs_ref)
    cos, sin = cos_ref[...], sin_ref[...]
    for h, (nope, rp) in enumerate(_mla_q(cq, gq_ref, wuq_ref, cos, sin)):
        qlat_ref[h] = _dot(nope.astype(BF16), wukt_ref[h]).astype(BF16)
        qr_ref[h] = rp[:, :ROPE_DIM].astype(BF16)
    ckvn_ref[...] = _rms(ckv, gkv_ref[...])
    kr_ref[...] = _rope(sm, cos, sin)[:, :ROPE_DIM]


def _proj_sample(x, w, cos_t, sin_t):
    T = x.shape[0]
    tm = PROJ_TM
    row = lambda i: (i, 0)
    in_specs, out_specs, out_shape = _inproj_specs(tm)
    return pl.pallas_call(
        _proj_sample_kernel,
        grid=(T // tm,),
        in_specs=in_specs + [pl.BlockSpec((MLA_HEADS, NOPE_DIM, KV_LORA), lambda i: (0, 0, 0)),
                             pl.BlockSpec((tm, LANES), row),
                             pl.BlockSpec((tm, LANES), row)],
        out_specs=out_specs + [pl.BlockSpec((MLA_HEADS, tm, KV_LORA), lambda i: (0, i, 0)),
                               pl.BlockSpec((MLA_HEADS, tm, ROPE_DIM), lambda i: (0, i, 0))],
        out_shape=out_shape(T) + [jax.ShapeDtypeStruct((MLA_HEADS, T, KV_LORA), BF16),
                                  jax.ShapeDtypeStruct((MLA_HEADS, T, ROPE_DIM), BF16)],
        compiler_params=_params("parallel"),
        name="proj_sample",
    )(x, w["g_pre_mix"], *w["w_in"], w["g_q"], w["g_kv"], w["w_uq"], w["w_ukt"], cos_t, sin_t)


ATT_TQ = 512
ATT_HEADS = 4


def _attn_prompt_kernel(q_ref, kt_ref, v_ref, o_ref):
    seq = q_ref.shape[0]
    qc = lax.broadcasted_iota(jnp.int32, (ATT_TQ, ATT_TQ), 0) // CHUNK
    kc = lax.broadcasted_iota(jnp.int32, (ATT_TQ, ATT_TQ), 1) // CHUNK
    visible = kc <= qc
    for i in reversed(range(seq // ATT_TQ)):
        lo, hi = i * ATT_TQ, (i + 1) * ATT_TQ
        for h in range(ATT_HEADS):
            qk = slice(h * QK_PAD, (h + 1) * QK_PAD)
            dv = slice(h * V_DIM, (h + 1) * V_DIM)
            s = _dot(q_ref[lo:hi, qk], kt_ref[qk, 0:hi])
            s_diag = jnp.where(visible, s[:, lo:hi], NEG_BIG)
            s = jnp.concatenate([s[:, :lo], s_diag], axis=1) if i else s_diag
            m = jnp.max(s, axis=-1, keepdims=True)
            p = jnp.exp2(s - m).astype(BF16)
            v_one = jnp.concatenate([v_ref[0:hi, dv], jnp.ones((hi, V_DIM), BF16)], axis=1)
            ol = _dot(p, v_one)
            o_ref[lo:hi, dv] = (ol[:, :V_DIM] / ol[:, V_DIM:]).astype(BF16)


def _attn_prompt(qcat, kt, v, batch, seq):
    return pl.pallas_call(
        _attn_prompt_kernel,
        grid=(batch, MLA_HEADS // ATT_HEADS),
        in_specs=[pl.BlockSpec((seq, ATT_HEADS * QK_PAD), lambda b, h: (b, h)),
                  pl.BlockSpec((ATT_HEADS * QK_PAD, seq), lambda b, h: (h, b)),
                  pl.BlockSpec((seq, ATT_HEADS * V_DIM), lambda b, h: (b, h))],
        out_specs=pl.BlockSpec((seq, ATT_HEADS * V_DIM), lambda b, h: (b, h)),
        out_shape=jax.ShapeDtypeStruct((batch * seq, MLA_WIDTH), BF16),
        compiler_params=_params("parallel", "parallel"),
        name="attn_prompt",
    )(qcat, kt, v)


DEC_SEQS = 2


def _attn_sample_kernel(ql_ref, qr_ref, ckv_ref, krt_ref, nckv_ref, nkr_ref, wuv_ref, o_ref):
    nb = ckv_ref.shape[0]
    dec = ql_ref.shape[1] // nb
    rows = MLA_HEADS * dec
    for b in range(nb):
        tok = slice(b * dec, (b + 1) * dec)
        ql = ql_ref[:, tok, :].reshape(rows, KV_LORA)
        qr = qr_ref[:, tok, :].reshape(rows, ROPE_DIM)
        ckv = ckv_ref[b].astype(BF16)
        nckv = nckv_ref[tok, :].astype(BF16)
        s = _dot_nt(ql, ckv) + _dot(qr, krt_ref[b].astype(BF16))
        s_new = _dot_nt(ql, nckv) + _dot_nt(qr, nkr_ref[tok, :].astype(BF16))
        m = jnp.maximum(jnp.max(s, axis=-1, keepdims=True), jnp.max(s_new, axis=-1, keepdims=True))
        p = jnp.exp2(s - m)
        p_new = jnp.exp2(s_new - m)
        l = jnp.sum(p, axis=-1, keepdims=True) + jnp.sum(p_new, axis=-1, keepdims=True)
        o = ((_dot(p.astype(BF16), ckv) + _dot(p_new.astype(BF16), nckv)) / l).astype(BF16)
        for h in range(MLA_HEADS):
            o_ref[tok, h * V_DIM:(h + 1) * V_DIM] = _dot(o[h * dec:(h + 1) * dec], wuv_ref[h]).astype(BF16)


def _attn_sample(q_lat, q_rope, cache_ckv, cache_krt, ckvn, kr, w_uvh, dec):
    batch, past, _ = cache_ckv.shape
    nb = DEC_SEQS
    return pl.pallas_call(
        _attn_sample_kernel,
        grid=(batch // nb,),
        in_specs=[pl.BlockSpec((MLA_HEADS, nb * dec, KV_LORA), lambda b: (0, b, 0)),
                  pl.BlockSpec((MLA_HEADS, nb * dec, ROPE_DIM), lambda b: (0, b, 0)),
                  pl.BlockSpec((nb, past, KV_LORA), lambda b: (b, 0, 0)),
                  pl.BlockSpec((nb, ROPE_DIM, past), lambda b: (b, 0, 0)),
                  pl.BlockSpec((nb * dec, KV_LORA), lambda b: (b, 0)),
                  pl.BlockSpec((nb * dec, ROPE_DIM), lambda b: (b, 0)),
                  pl.BlockSpec((MLA_HEADS, KV_LORA, V_DIM), lambda b: (0, 0, 0))],
        out_specs=pl.BlockSpec((nb * dec, MLA_WIDTH), lambda b: (b, 0)),
        out_shape=jax.ShapeDtypeStruct((batch * dec, MLA_WIDTH), BF16),
        compiler_params=_params("parallel"),
        name="attn_sample",
    )(q_lat, q_rope, cache_ckv, cache_krt, ckvn, kr, w_uvh)


GLA_CHUNK = 256
GLA_DEC_ROWS = 128
GLA_SEQS = 2


def _gla_tables(c, sub):
    nlev = int(np.log2(sub))
    assert 1 << nlev == sub and c % sub == 0 and c // sub <= LANES
    t = np.arange(c)[:, None]
    u = np.arange(c)[None, :]
    same_sub = (t // sub) == (u // sub)
    blocks = []
    lvl = np.full((c, c), -1, np.int32)
    for l in range(nlev):
        width = sub >> l
        half = width // 2
        m = (t // width) * width + half - 1
        upper = t > m
        blocks.append(np.where(upper, (u > m) & (u <= t), (u > t) & (u <= m)))
        same = (t // width) == (u // width)
        lvl[same & ((t % width) >= half) & ((u % width) < half)] = l
    blocks.append((u > t) & same_sub)
    blocks.append((u <= t) & same_sub)
    lvl[np.arange(c), np.arange(c)] = nlev
    sel = (np.arange(c)[:, None] // sub) == np.arange(LANES)[None, :]
    return np.concatenate(blocks, 0).astype(np.float32), lvl, sel.astype(np.float32), nlev


def _gla_kernel(*refs, c, sub, nlev, carry):
    if carry:
        (q_ref, k_ref, v_ref, r_ref, sm_ref, wga_ref, bga_ref, gg_ref, p_ref, lvl_ref, sel_ref,
         o_ref, sfin_ref, s_scr) = refs
        assert sub == c
        j = pl.program_id(1)

        @pl.when(j == 0)
        def _():
            s_scr[...] = jnp.zeros(s_scr.shape, F32)
    else:
        (q_ref, k_ref, v_ref, r_ref, sm_ref, wga_ref, bga_ref, gg_ref, p_ref, lvl_ref, sel_ref, s0_ref,
         o_ref, sfin_ref) = refs

    p_mat = p_ref[...]
    sel = sel_ref[...]
    lvl = lvl_ref[...]
    groups = range(q_ref.shape[0])
    split = []
    for g in groups:
        x = _dot(sm_ref[g].astype(BF16), wga_ref[...]) + bga_ref[...]
        la = (jnp.minimum(x, 0.0) - jnp.log1p(jnp.exp(-jnp.abs(x)))) * (1.0 / GATE_TAU)
        hi = la.astype(BF16)
        split.append((hi, (la - hi.astype(F32)).astype(BF16)))
    for g in groups:
        hi, mid = split[g]
        x_all = _dot(p_mat, hi) + _dot(p_mat, mid)
        if carry:
            b = x_all[(nlev - 1) * c:]
            d = b - b[c // 2 - 1:c // 2, :]
            x_lv = [jnp.concatenate([-d[:c // 2], d[c // 2:]], axis=0)]
            x_lv += [x_all[l * c:(l + 1) * c] for l in range(nlev - 1)]
            x_end = b[c - 1:c, :] - b
        else:
            x_lv = [x_all[l * c:(l + 1) * c] for l in range(nlev)]
            x_end = x_all[nlev * c:(nlev + 1) * c]
            b = x_all[(nlev + 1) * c:]
        d_all = jnp.exp(_dot_tn(hi, sel) + _dot_tn(mid, sel))
        q = q_ref[g] * (GLA_DK ** -0.5)
        k = k_ref[g]
        e_lv = [jnp.exp(x) for x in x_lv]
        qe = [(q * e).astype(BF16) for e in e_lv] + [q.astype(BF16)]
        ke = [(k * e).astype(BF16) for e in e_lv] + [k.astype(BF16)]
        k_end = (k * jnp.exp(x_end)).astype(BF16)
        q_beg = (q * jnp.exp(b)).astype(BF16)
        for h in range(GLA_HEADS):
            dk = slice(h * GLA_DK, (h + 1) * GLA_DK)
            dv = slice(h * GLA_DV, (h + 1) * GLA_DV)
            attn = jnp.zeros((c, c), F32)
            for l in range(nlev + 1):
                attn = jnp.where(lvl == l, _dot_nt(qe[l][:, dk], ke[l][:, dk]), attn)
            v = v_ref[g, :, dv].astype(BF16)
            o_intra = _dot(attn.astype(BF16), v)
            for n in range(c // sub):
                rows = slice(n * sub, (n + 1) * sub)
                state = s_scr[g, h] if carry else s0_ref[n, h]
                o = o_intra[rows] + _dot(q_beg[rows, dk], state.astype(BF16))
                new_state = d_all[dk, n:n + 1] * state + _dot_tn(k_end[rows, dk], v[rows])
                if carry:
                    s_scr[g, h] = new_state
                else:
                    sfin_ref[n, h] = new_state
                r = r_ref[g, rows, dv]
                o_ref[g, rows, dv] = (_rms(o, gg_ref[...]) * (r * jax.nn.sigmoid(r))).astype(BF16)

    if carry:
        @pl.when(j == pl.num_programs(1) - 1)
        def _():
            sfin_ref[...] = s_scr[...]


def _gla(z, zs, w_ga, b_ga, g_gla, batch, seq, c, s0=None):
    carry = s0 is None
    sub = c if carry else seq
    p_np, lvl_np, sel_np, nlev = _gla_tables(c, sub)
    if carry:
        p_np = np.concatenate([p_np[c:nlev * c], p_np[(nlev + 1) * c:]], axis=0)
    if carry:
        nb, lead, grid = GLA_SEQS, batch, (batch // GLA_SEQS, seq // c)
        at = lambda col: (lambda g, j: (g, j, col))
    else:
        nb, lead, grid = 1, 1, (batch * seq // c, 1)
        at = lambda col: (lambda g, j: (0, g, col))
    nstate = nb if carry else c // sub
    z, zs = z.reshape(lead, -1, Z_GLA), zs.reshape(lead, -1, LANES)
    const = lambda g, j: (0, 0)
    in_specs = [pl.BlockSpec((nb, c, GLA_HEADS * GLA_DK), at(0)),
                pl.BlockSpec((nb, c, GLA_HEADS * GLA_DK), at(1)),
                pl.BlockSpec((nb, c, GLA_WIDTH), at(1)),
                pl.BlockSpec((nb, c, GLA_WIDTH), at(2)),
                pl.BlockSpec((nb, c, LANES), at(0)),
                pl.BlockSpec((LANES, GLA_HEADS * GLA_DK), const),
                pl.BlockSpec((1, GLA_HEADS * GLA_DK), const),
                pl.BlockSpec((1, GLA_DV), const),
                pl.BlockSpec(p_np.shape, const),
                pl.BlockSpec((c, c), const),
                pl.BlockSpec((c, LANES), const)]
    args = [z, z, z, z, zs, w_ga, b_ga, g_gla,
            jnp.asarray(p_np, BF16), jnp.asarray(lvl_np), jnp.asarray(sel_np, BF16)]
    state_spec = pl.BlockSpec((nstate, GLA_HEADS, GLA_DK, GLA_DV), lambda g, j: (g, 0, 0, 0))
    if not carry:
        in_specs.append(state_spec)
        args.append(s0)
    o, s_fin = pl.pallas_call(
        functools.partial(_gla_kernel, c=c, sub=sub, nlev=nlev, carry=carry),
        grid=grid,
        in_specs=in_specs,
        out_specs=[pl.BlockSpec((nb, c, GLA_WIDTH), at(0)), state_spec],
        out_shape=[jax.ShapeDtypeStruct((lead, batch * seq // lead, GLA_WIDTH), BF16),
                   jax.ShapeDtypeStruct((batch, GLA_HEADS, GLA_DK, GLA_DV), F32)],
        scratch_shapes=[pltpu.VMEM((nb, GLA_HEADS, GLA_DK, GLA_DV), F32)] if carry else [],
        compiler_params=_params("parallel", "arbitrary"),
        name="gla" if carry else "gla_init",
    )(*args)
    return o.reshape(batch * seq, GLA_WIDTH), s_fin


ROW_SUB = 128


def _mix_kernel(oma_ref, omb_ref, oga_ref, ogb_ref, w1_ref, w2_ref, ha_ref, hb_ref, g1_ref, g2_ref, h1_ref, f_ref,
                *, na):
    def block(om_ref, og_ref, h_ref):
        for r in range(0, h_ref.shape[0], ROW_SUB):
            rows = slice(r, r + ROW_SUB)
            mix = _dot(om_ref[rows, :], w1_ref[...]) + _dot(og_ref[rows, :], w2_ref[...])
            h1 = h_ref[rows, :] + _rms(mix, g1_ref[...])
            h1_ref[rows, :] = h1
            f_ref[rows, :] = _rms(h1, g2_ref[...]).astype(BF16)

    pl.when(pl.program_id(0) < na)(lambda: block(oma_ref, oga_ref, ha_ref))
    pl.when(pl.program_id(0) >= na)(lambda: block(omb_ref, ogb_ref, hb_ref))


def _mix(o_mla, o_gla, h, w_out, g_post_mix, g_pre_ffn):
    tm = 512
    na, nb = h[0].shape[0] // tm, h[1].shape[0] // tm
    T = (na + nb) * tm
    row = lambda i: (i, 0)
    const = lambda i: (0, 0)
    at_a = lambda i: (jnp.minimum(i, na - 1), 0)
    at_b = lambda i: (jnp.maximum(i - na, 0), 0)
    return pl.pallas_call(
        functools.partial(_mix_kernel, na=na),
        grid=(na + nb,),
        in_specs=[pl.BlockSpec((tm, MLA_WIDTH), at_a), pl.BlockSpec((tm, MLA_WIDTH), at_b),
                  pl.BlockSpec((tm, GLA_WIDTH), at_a), pl.BlockSpec((tm, GLA_WIDTH), at_b),
                  pl.BlockSpec((MLA_WIDTH, D_MODEL), lambda i: (0, 0), pipeline_mode=pl.Buffered(1)),
                  pl.BlockSpec((GLA_WIDTH, D_MODEL), lambda i: (1, 0), pipeline_mode=pl.Buffered(1)),
                  pl.BlockSpec((tm, D_MODEL), at_a), pl.BlockSpec((tm, D_MODEL), at_b),
                  pl.BlockSpec((1, D_MODEL), const),
                  pl.BlockSpec((1, D_MODEL), const)],
        out_specs=[pl.BlockSpec((tm, D_MODEL), row), pl.BlockSpec((tm, D_MODEL), row)],
        out_shape=[jax.ShapeDtypeStruct((T, D_MODEL), F32), jax.ShapeDtypeStruct((T, D_MODEL), BF16)],
        compiler_params=_params("arbitrary"),
        name="mix",
    )(o_mla[0], o_mla[1], o_gla[0], o_gla[1], w_out, w_out, h[0], h[1], g_post_mix, g_pre_ffn)


FFN_MAX_ROWS = 1088


def _ffn_kernel(x_ref, wg_ref, wu_ref, wd_ref, o_ref):
    def partial_out():
        x = x_ref[...]
        g = _dot(x, wg_ref[...].astype(BF16))
        u = _dot(x, wu_ref[...].astype(BF16))
        return _dot((g * jax.nn.sigmoid(g) * u).astype(BF16), wd_ref[...].astype(BF16))

    @pl.when(pl.program_id(1) == 0)
    def _():
        o_ref[...] = partial_out()

    @pl.when(pl.program_id(1) > 0)
    def _():
        o_ref[...] += partial_out()


def _ffn(x, w_gate, w_up, w_down):
    T = x.shape[0]
    tf = 512
    tm = next(T // n for n in range(1, T + 1) if T % n == 0 and T // n <= FFN_MAX_ROWS and (T // n) % 16 == 0)
    return pl.pallas_call(
        _ffn_kernel,
        grid=(T // tm, D_FF // tf),
        in_specs=[pl.BlockSpec((tm, D_MODEL), lambda i, j: (i, 0)),
                  pl.BlockSpec((D_MODEL, tf), lambda i, j: (0, j)),
                  pl.BlockSpec((D_MODEL, tf), lambda i, j: (0, j)),
                  pl.BlockSpec((tf, D_MODEL), lambda i, j: (j, 0))],
        out_specs=pl.BlockSpec((tm, D_MODEL), lambda i, j: (i, 0)),
        out_shape=jax.ShapeDtypeStruct((T, D_MODEL), F32),
        compiler_params=_params("parallel", "arbitrary"),
        name="ffn",
    )(x, w_gate, w_up, w_down)


def _final_kernel(f_ref, h1_ref, pa_ref, pb_ref, g_ref, wpg_ref, wp_ref, ya_ref, yb_ref, *, na):
    def block(p_ref, y_ref):
        h2 = h1_ref[...] + _rms(f_ref[...], g_ref[...])
        gate = jax.nn.sigmoid(_dot(h2.astype(BF16), wpg_ref[...]))
        y_ref[...] = h2 + gate * _dot(p_ref[...].astype(BF16), wp_ref[...])

    pl.when(pl.program_id(0) < na)(lambda: block(pa_ref, ya_ref))
    pl.when(pl.program_id(0) >= na)(lambda: block(pb_ref, yb_ref))


def _final(f, h1, p, g_post_ffn, w_ple_gate, w_ple):
    tm = 512
    na, nb = p[0].shape[0] // tm, p[1].shape[0] // tm
    row = lambda i: (i, 0)
    const = lambda i: (0, 0)
    at_a = lambda i: (jnp.minimum(i, na - 1), 0)
    at_b = lambda i: (jnp.maximum(i - na, 0), 0)
    return pl.pallas_call(
        functools.partial(_final_kernel, na=na),
        grid=(na + nb,),
        in_specs=[pl.BlockSpec((tm, D_MODEL), row),
                  pl.BlockSpec((tm, D_MODEL), row),
                  pl.BlockSpec((tm, PLE_DIM), at_a), pl.BlockSpec((tm, PLE_DIM), at_b),
                  pl.BlockSpec((1, D_MODEL), const),
                  pl.BlockSpec((D_MODEL, D_MODEL), const, pipeline_mode=pl.Buffered(1)),
                  pl.BlockSpec((PLE_DIM, D_MODEL), const)],
        out_specs=[pl.BlockSpec((tm, D_MODEL), at_a), pl.BlockSpec((tm, D_MODEL), at_b)],
        out_shape=[jax.ShapeDtypeStruct((na * tm, D_MODEL), F32), jax.ShapeDtypeStruct((nb * tm, D_MODEL), F32)],
        compiler_params=_params("arbitrary"),
        name="final",
    )(f, h1, p[0], p[1], g_post_ffn, w_ple_gate, w_ple)


def _rope_tables(pos):
    half = ROPE_DIM // 2
    inv = 1.0 / (ROPE_THETA ** (jnp.arange(half, dtype=F32) / half))
    ang = pos.astype(F32)[:, None] * inv[None, :]
    cos, sin = jnp.cos(ang), jnp.sin(ang)
    zero = jnp.zeros((pos.shape[0], LANES - ROPE_DIM), F32)
    return jnp.concatenate([cos, cos, zero], axis=1), jnp.concatenate([-sin, sin, zero], axis=1)


def _layer_weights(i, g_pre_mix, w_in, g_q, w_uq, w_uk, g_kv, w_ga, b_ga, w_uv, g_gla, w_out, g_post_mix,
                   g_pre_ffn, w_gate, w_up, w_down, g_post_ffn, w_ple, w_ple_gate):
    w_uq_h = w_uq[i].reshape(Q_LORA, MLA_HEADS, NOPE_DIM + ROPE_DIM)
    w_uq_p = jnp.concatenate([w_uq_h[:, :, :NOPE_DIM].reshape(Q_LORA, MLA_HEADS * NOPE_DIM),
                              w_uq_h[:, :, NOPE_DIM:].reshape(Q_LORA, MLA_HEADS * ROPE_DIM)], axis=1)
    w_ga_p = jnp.zeros((LANES, GLA_HEADS * GLA_DK), F32).at[GLR_LO:GLR_LO + GATE_RANK].set(w_ga[i])
    vec = lambda g: g[i].reshape(1, -1)
    return dict(
        g_pre_mix=vec(g_pre_mix), w_in=_win_prep(jnp.swapaxes(w_in[i], 0, 1)), g_q=vec(g_q), g_kv=vec(g_kv),
        w_uq=w_uq_p.astype(BF16),
        w_uv=w_uv[i].reshape(KV_LORA, MLA_WIDTH).astype(BF16),
        w_ukt=jnp.transpose(w_uk[i], (1, 2, 0)).astype(BF16),
        w_uvh=jnp.transpose(w_uv[i], (1, 0, 2)).astype(BF16),
        w_ga=w_ga_p.astype(BF16), b_ga=vec(b_ga), g_gla=vec(g_gla),
        w_out=w_out[i].astype(BF16), g_post_mix=vec(g_post_mix), g_pre_ffn=vec(g_pre_ffn),
        w_gate=w_gate[i], w_up=w_up[i], w_down=w_down[i],
        g_post_ffn=vec(g_post_ffn), w_ple=w_ple[i].astype(BF16), w_ple_gate=w_ple_gate[i].astype(BF16))


def _finish(w, h, p, o_mla, o_gla):
    h1, f_in = _mix(o_mla, o_gla, h, w["w_out"], w["g_post_mix"], w["g_pre_ffn"])
    f = _ffn(f_in, w["w_gate"], w["w_up"], w["w_down"])
    return _final(f, h1, p, w["g_post_ffn"], w["w_ple_gate"], w["w_ple"])


def kernel(x_prompt, x_sample, cache_ckv, cache_krope, state_gla, p_prompt, p_sample, g_pre_mix, w_in, g_q, w_uq,
           w_uk, g_kv, w_ga, b_ga, w_uv, g_gla, w_out, g_post_mix, g_pre_ffn, w_gate, w_up, w_down, g_post_ffn,
           w_ple, w_ple_gate):
    batch, seq, _ = x_prompt.shape
    dbatch, dseq, _ = x_sample.shape
    depth = w_in.shape[0]
    cos_p, sin_p = _rope_tables(jnp.arange(seq))
    cos_s, sin_s = _rope_tables(PAST_LEN + jnp.arange(dseq))
    cos_s, sin_s = jnp.tile(cos_s, (dbatch, 1)), jnp.tile(sin_s, (dbatch, 1))
    h_p = x_prompt.reshape(batch * seq, D_MODEL)
    h_s = x_sample.reshape(dbatch * dseq, D_MODEL)
    outs = [[] for _ in range(6)]
    for i in range(depth):
        w = _layer_weights(i, g_pre_mix, w_in, g_q, w_uq, w_uk, g_kv, w_ga, b_ga, w_uv, g_gla, w_out, g_post_mix,
                           g_pre_ffn, w_gate, w_up, w_down, g_post_ffn, w_ple, w_ple_gate)
        z, zs, ckvn, kr, qcat, kt, v = _proj_prompt(h_p, w, cos_p, sin_p, seq)
        o_mla_p = _attn_prompt(qcat, kt, v, batch, seq)
        o_gla_p, s_fin = _gla(z, zs, w["w_ga"], w["b_ga"], w["g_gla"], batch, seq, GLA_CHUNK)
        outs[0].append(ckvn.reshape(batch, seq, KV_LORA))
        outs[1].append(kr.reshape(batch, seq, ROPE_DIM))
        outs[2].append(s_fin)
        z, zs, ckvn, kr, q_lat, q_rope = _proj_sample(h_s, w, cos_s, sin_s)
        o_mla_s = _attn_sample(q_lat, q_rope, cache_ckv[i], jnp.swapaxes(cache_krope[i], 1, 2), ckvn, kr,
                               w["w_uvh"], dseq)
        o_gla_s, s_new = _gla(z, zs, w["w_ga"], w["b_ga"], w["g_gla"], dbatch, dseq, GLA_DEC_ROWS, s0=state_gla[i])
        outs[3].append(ckvn.reshape(dbatch, dseq, KV_LORA))
        outs[4].append(kr.reshape(dbatch, dseq, ROPE_DIM))
        outs[5].append(s_new)
        h_p, h_s = _finish(w, (h_p, h_s),
                           (p_prompt[i].reshape(batch * seq, PLE_DIM), p_sample[i].reshape(dbatch * dseq, PLE_DIM)),
                           (o_mla_p, o_mla_s), (o_gla_p, o_gla_s))
    return (h_p.reshape(batch, seq, D_MODEL), h_s.reshape(dbatch, dseq, D_MODEL),
            jnp.stack(outs[0]), jnp.stack(outs[1]), jnp.stack(outs[2]),
            jnp.stack(outs[3]), jnp.stack(outs[4]), jnp.stack(outs[5]))
```

```python
import functools

import numpy as np
import jax
import jax.numpy as jnp
from jax import lax
from jax.experimental import pallas as pl
from jax.experimental.pallas import tpu as pltpu

F32 = jnp.float32
BF16 = jnp.bfloat16

D_MODEL = 2048
PAST_LEN = 4096
CHUNK = 64
EPS = 1e-6

MLA_HEADS = 8
Q_LORA = 512
KV_LORA = 512
NOPE_DIM = 128
ROPE_DIM = 64
V_DIM = 128
ROPE_THETA = 10000.0
MLA_SCALE = (NOPE_DIM + ROPE_DIM) ** -0.5
QK_PAD = 256

GLA_HEADS = 4
GLA_DK = 128
GLA_DV = 256
GATE_RANK = 16
GATE_TAU = 16.0
GLA_WIDTH = GLA_HEADS * GLA_DV
MLA_WIDTH = MLA_HEADS * V_DIM

IN_SPLITS = (Q_LORA, KV_LORA, ROPE_DIM, GLA_HEADS * GLA_DK, GLA_HEADS * GLA_DK, GLA_WIDTH, GATE_RANK, GLA_WIDTH)
Z_MAIN = 4096
GLR_LO = ROPE_DIM

D_FF = 5632
PLE_DIM = 256

LANES = 128
VMEM_BYTES = 64 * 1024 * 1024
VMEM_LIMIT = VMEM_BYTES - 8 * 1024 * 1024
VMEM_LIMIT_PROJ = VMEM_BYTES - 4 * 1024 * 1024
NEG_BIG = -1e30
LOG2E = 1.4426950408889634
QK_SCALE = MLA_SCALE * LOG2E


def _dot(a, b):
    return jnp.dot(a, b, preferred_element_type=F32)


def _dot_nt(a, b):
    return lax.dot_general(a, b, (((1,), (1,)), ((), ())), preferred_element_type=F32)


def _dot_tn(a, b):
    return lax.dot_general(a, b, (((0,), (0,)), ((), ())), preferred_element_type=F32)


def _rms(x, g):
    ms = jnp.mean(x * x, axis=-1, keepdims=True)
    return x * lax.rsqrt(ms + EPS) * g


def _rope(x, cos_t, sin_t):
    lane = lax.broadcasted_iota(jnp.int32, x.shape, 1)
    rot = jnp.where((lane & 32) == 0, pltpu.roll(x, 96, 1), pltpu.roll(x, 32, 1))
    return x * cos_t + rot * sin_t


def _params(*sem, vmem=VMEM_LIMIT):
    return pltpu.CompilerParams(dimension_semantics=sem, vmem_limit_bytes=vmem)


_IN_OFF = np.concatenate([[0], np.cumsum(IN_SPLITS)]).tolist()
_WIN_TILE = 512


def _win_prep_kernel(w_ref, kr_ref, glr_ref, o_ref, small_ref):
    o_ref[...] = w_ref[...].T.astype(BF16)

    @pl.when(pl.program_id(0) == 0)
    def _():
        pad = jnp.zeros((LANES - GLR_LO - GATE_RANK, D_MODEL), F32)
        small_ref[...] = jnp.concatenate([kr_ref[...], glr_ref[...], pad], axis=0).T.astype(BF16)


def _win_prep(w_t):
    c_q, c_kv, k_r, q_g, k_g, v_g, g_lr, r_g, end = _IN_OFF
    n_head, n_mid = k_r // _WIN_TILE, (g_lr - q_g) // _WIN_TILE
    assert k_r % _WIN_TILE == 0 and (g_lr - q_g) % _WIN_TILE == 0 and (end - r_g) % _WIN_TILE == 0
    assert (n_head + n_mid) * _WIN_TILE + end - r_g == Z_MAIN and GLR_LO == ROPE_DIM

    def src_row(j):
        skip_mid, skip_tail = (q_g - k_r) // 8, (r_g - g_lr + q_g - k_r) // 8
        return 8 * (j * (_WIN_TILE // 8) + jnp.where(j < n_head, 0, jnp.where(j < n_head + n_mid, skip_mid, skip_tail)))

    return pl.pallas_call(
        _win_prep_kernel,
        grid=(Z_MAIN // _WIN_TILE,),
        in_specs=[pl.BlockSpec((pl.Element(_WIN_TILE), pl.Element(D_MODEL)), lambda j: (src_row(j), 0)),
                  pl.BlockSpec((pl.Element(ROPE_DIM), pl.Element(D_MODEL)), lambda j: (k_r, 0)),
                  pl.BlockSpec((pl.Element(GATE_RANK), pl.Element(D_MODEL)), lambda j: (g_lr, 0))],
        out_specs=[pl.BlockSpec((D_MODEL, _WIN_TILE), lambda j: (0, j)),
                   pl.BlockSpec((D_MODEL, LANES), lambda j: (0, 0))],
        out_shape=[jax.ShapeDtypeStruct((D_MODEL, Z_MAIN), BF16), jax.ShapeDtypeStruct((D_MODEL, LANES), BF16)],
        compiler_params=_params("arbitrary"),
        name="win_prep",
    )(w_t, w_t, w_t)


Z_MLA = Q_LORA + KV_LORA
Z_GLA = Z_MAIN - Z_MLA
PROJ_TM = 512


def _inproj(x_ref, g_ref, wm_ref, ws_ref, zg_ref, zs_ref):
    xn = _rms(x_ref[...], g_ref[...]).astype(BF16)
    zg_ref[...] = _dot(xn, wm_ref[:, Z_MLA:])
    sm = _dot(xn, ws_ref[...])
    zs_ref[...] = sm
    z_mla = _dot(xn, wm_ref[:, :Z_MLA])
    return z_mla[:, :Q_LORA], z_mla[:, Q_LORA:], sm


def _inproj_specs(tm):
    row = lambda i: (i, 0)
    const = lambda i: (0, 0)
    in_specs = [pl.BlockSpec((tm, D_MODEL), row),
                pl.BlockSpec((1, D_MODEL), const),
                pl.BlockSpec((D_MODEL, Z_MAIN), const, pipeline_mode=pl.Buffered(1)),
                pl.BlockSpec((D_MODEL, LANES), const, pipeline_mode=pl.Buffered(1)),
                pl.BlockSpec((1, Q_LORA), const),
                pl.BlockSpec((1, KV_LORA), const),
                pl.BlockSpec((Q_LORA, MLA_HEADS * (NOPE_DIM + ROPE_DIM)), const, pipeline_mode=pl.Buffered(1))]
    out_specs = [pl.BlockSpec((tm, Z_GLA), row), pl.BlockSpec((tm, LANES), row),
                 pl.BlockSpec((tm, KV_LORA), row), pl.BlockSpec((tm, ROPE_DIM), row)]
    out_shape = lambda T: [jax.ShapeDtypeStruct((T, Z_GLA), F32), jax.ShapeDtypeStruct((T, LANES), F32),
                           jax.ShapeDtypeStruct((T, KV_LORA), F32), jax.ShapeDtypeStruct((T, ROPE_DIM), F32)]
    return in_specs, out_specs, out_shape


def _mla_q(cq, gq_ref, wuq_ref, cos, sin):
    cqn = _rms(cq, gq_ref[...]).astype(BF16)
    q = _dot(cqn, wuq_ref[...]) * QK_SCALE
    cos2 = cos + pltpu.roll(cos, ROPE_DIM, 1)
    sin2 = sin + pltpu.roll(sin, ROPE_DIM, 1)
    low = lax.broadcasted_iota(jnp.int32, cos.shape, 1) < ROPE_DIM
    out = []
    for j in range(MLA_HEADS // 2):
        first = MLA_HEADS * NOPE_DIM + j * LANES
        rot = _rope(q[:, first:first + LANES], cos2, sin2)
        for h, rp in ((2 * j, rot), (2 * j + 1, pltpu.roll(rot, ROPE_DIM, 1))):
            out.append((q[:, h * NOPE_DIM:(h + 1) * NOPE_DIM], jnp.where(low, rp, 0.0)))
    return out


def _proj_prompt_kernel(x_ref, g_ref, wm_ref, ws_ref, gq_ref, gkv_ref, wuq_ref, wukt_ref, wuv_ref, cos_ref, sin_ref,
                        zg_ref, zs_ref, ckvn_ref, kr_ref, qcat_ref, kt_ref, v_ref):
    cq, ckv, sm = _inproj(x_ref, g_ref, wm_ref, ws_ref, zg_ref, zs_ref)
    cos, sin = cos_ref[...], sin_ref[...]
    for h, (nope, rp) in enumerate(_mla_q(cq, gq_ref, wuq_ref, cos, sin)):
        qcat_ref[:, h * QK_PAD:h * QK_PAD + NOPE_DIM] = nope.astype(BF16)
        qcat_ref[:, h * QK_PAD + NOPE_DIM:(h + 1) * QK_PAD] = rp.astype(BF16)
    ckvn = _rms(ckv, gkv_ref[...])
    ckvn_ref[...] = ckvn
    kr = _rope(sm, cos, sin)
    kr_ref[...] = kr[:, :ROPE_DIM]
    ckvn_b = ckvn.astype(BF16)
    k_t = _dot_nt(wukt_ref[...], ckvn_b)
    kr_t = kr.T.astype(BF16)
    for h in range(MLA_HEADS):
        kt_ref[h * QK_PAD:h * QK_PAD + NOPE_DIM, :] = k_t[h * NOPE_DIM:(h + 1) * NOPE_DIM].astype(BF16)
        kt_ref[h * QK_PAD + NOPE_DIM:(h + 1) * QK_PAD, :] = kr_t
    v_ref[...] = _dot(ckvn_b, wuv_ref[...]).astype(BF16)


def _proj_prompt(x, w, cos_t, sin_t, seq):
    T = x.shape[0]
    tm = PROJ_TM
    nseq = seq // tm
    row = lambda i: (i, 0)
    const = lambda i: (0, 0)
    in_specs, out_specs, out_shape = _inproj_specs(tm)
    return pl.pallas_call(
        _proj_prompt_kernel,
        grid=(T // tm,),
        in_specs=in_specs + [pl.BlockSpec((MLA_WIDTH, KV_LORA), const, pipeline_mode=pl.Buffered(1)),
                             pl.BlockSpec((KV_LORA, MLA_WIDTH), const, pipeline_mode=pl.Buffered(1)),
                             pl.BlockSpec((tm, LANES), lambda i: (i % nseq, 0)),
                             pl.BlockSpec((tm, LANES), lambda i: (i % nseq, 0))],
        out_specs=out_specs + [pl.BlockSpec((tm, MLA_HEADS * QK_PAD), row),
                               pl.BlockSpec((MLA_HEADS * QK_PAD, tm), lambda i: (0, i)),
                               pl.BlockSpec((tm, MLA_WIDTH), row)],
        out_shape=out_shape(T) + [jax.ShapeDtypeStruct((T, MLA_HEADS * QK_PAD), BF16),
                                  jax.ShapeDtypeStruct((MLA_HEADS * QK_PAD, T), BF16),
                                  jax.ShapeDtypeStruct((T, MLA_WIDTH), BF16)],
        compiler_params=_params("parallel", vmem=VMEM_LIMIT_PROJ),
        name="proj_prompt",
    )(x, w["g_pre_mix"], *w["w_in"], w["g_q"], w["g_kv"], w["w_uq"],
      w["w_ukt"].reshape(MLA_WIDTH, KV_LORA), w["w_uv"], cos_t, sin_t)


def _proj_sample_kernel(x_ref, g_ref, wm_ref, ws_ref, gq_ref, gkv_ref, wuq_ref, wukt_ref, cos_ref, sin_ref,
                        zg_ref, zs_ref, ckvn_ref, kr_ref, qlat_ref, qr_ref):
    cq, ckv, sm = _inproj(x_ref, g_ref, wm_ref, ws_ref, zg_ref, zs_ref)
    cos, sin = cos_ref[...], sin_ref[...]
    for h, (nope, rp) in enumerate(_mla_q(cq, gq_ref, wuq_ref, cos, sin)):
        qlat_ref[h] = _dot(nope.astype(BF16), wukt_ref[h]).astype(BF16)
        qr_ref[h] = rp[:, :ROPE_DIM].astype(BF16)
    ckvn_ref[...] = _rms(ckv, gkv_ref[...])
    kr_ref[...] = _rope(sm, cos, sin)[:, :ROPE_DIM]


def _proj_sample(x, w, cos_t, sin_t):
    T = x.shape[0]
    tm = PROJ_TM
    row = lambda i: (i, 0)
    in_specs, out_specs, out_shape = _inproj_specs(tm)
    return pl.pallas_call(
        _proj_sample_kernel,
        grid=(T // tm,),
        in_specs=in_specs + [pl.BlockSpec((MLA_HEADS, NOPE_DIM, KV_LORA), lambda i: (0, 0, 0)),
                             pl.BlockSpec((tm, LANES), row),
                             pl.BlockSpec((tm, LANES), row)],
        out_specs=out_specs + [pl.BlockSpec((MLA_HEADS, tm, KV_LORA), lambda i: (0, i, 0)),
                               pl.BlockSpec((MLA_HEADS, tm, ROPE_DIM), lambda i: (0, i, 0))],
        out_shape=out_shape(T) + [jax.ShapeDtypeStruct((MLA_HEADS, T, KV_LORA), BF16),
                                  jax.ShapeDtypeStruct((MLA_HEADS, T, ROPE_DIM), BF16)],
        compiler_params=_params("parallel"),
        name="proj_sample",
    )(x, w["g_pre_mix"], *w["w_in"], w["g_q"], w["g_kv"], w["w_uq"], w["w_ukt"], cos_t, sin_t)


ATT_TQ = 512
ATT_HEADS = 4


def _attn_prompt_kernel(q_ref, kt_ref, v_ref, o_ref):
    seq = q_ref.shape[0]
    qc = lax.broadcasted_iota(jnp.int32, (ATT_TQ, ATT_TQ), 0) // CHUNK
    kc = lax.broadcasted_iota(jnp.int32, (ATT_TQ, ATT_TQ), 1) // CHUNK
    visible = kc <= qc
    for i in reversed(range(seq // ATT_TQ)):
        lo, hi = i * ATT_TQ, (i + 1) * ATT_TQ
        for h in range(ATT_HEADS):
            qk = slice(h * QK_PAD, (h + 1) * QK_PAD)
            dv = slice(h * V_DIM, (h + 1) * V_DIM)
            s = _dot(q_ref[lo:hi, qk], kt_ref[qk, 0:hi])
            s_diag = jnp.where(visible, s[:, lo:hi], NEG_BIG)
            s = jnp.concatenate([s[:, :lo], s_diag], axis=1) if i else s_diag
            m = jnp.max(s, axis=-1, keepdims=True)
            p = jnp.exp2(s - m).astype(BF16)
            v_one = jnp.concatenate([v_ref[0:hi, dv], jnp.ones((hi, V_DIM), BF16)], axis=1)
            ol = _dot(p, v_one)
            o_ref[lo:hi, dv] = (ol[:, :V_DIM] / ol[:, V_DIM:]).astype(BF16)


def _attn_prompt(qcat, kt, v, batch, seq):
    return pl.pallas_call(
        _attn_prompt_kernel,
        grid=(batch, MLA_HEADS // ATT_HEADS),
        in_specs=[pl.BlockSpec((seq, ATT_HEADS * QK_PAD), lambda b, h: (b, h)),
                  pl.BlockSpec((ATT_HEADS * QK_PAD, seq), lambda b, h: (h, b)),
                  pl.BlockSpec((seq, ATT_HEADS * V_DIM), lambda b, h: (b, h))],
        out_specs=pl.BlockSpec((seq, ATT_HEADS * V_DIM), lambda b, h: (b, h)),
        out_shape=jax.ShapeDtypeStruct((batch * seq, MLA_WIDTH), BF16),
        compiler_params=_params("parallel", "parallel"),
        name="attn_prompt",
    )(qcat, kt, v)


DEC_SEQS = 2


def _attn_sample_kernel(ql_ref, qr_ref, ckv_ref, krt_ref, nckv_ref, nkr_ref, wuv_ref, o_ref):
    nb = ckv_ref.shape[0]
    dec = ql_ref.shape[1] // nb
    rows = MLA_HEADS * dec
    for b in range(nb):
        tok = slice(b * dec, (b + 1) * dec)
        ql = ql_ref[:, tok, :].reshape(rows, KV_LORA)
        qr = qr_ref[:, tok, :].reshape(rows, ROPE_DIM)
        ckv = ckv_ref[b].astype(BF16)
        nckv = nckv_ref[tok, :].astype(BF16)
        s = _dot_nt(ql, ckv) + _dot(qr, krt_ref[b].astype(BF16))
        s_new = _dot_nt(ql, nckv) + _dot_nt(qr, nkr_ref[tok, :].astype(BF16))
        m = jnp.maximum(jnp.max(s, axis=-1, keepdims=True), jnp.max(s_new, axis=-1, keepdims=True))
        p = jnp.exp2(s - m)
        p_new = jnp.exp2(s_new - m)
        l = jnp.sum(p, axis=-1, keepdims=True) + jnp.sum(p_new, axis=-1, keepdims=True)
        o = ((_dot(p.astype(BF16), ckv) + _dot(p_new.astype(BF16), nckv)) / l).astype(BF16)
        for h in range(MLA_HEADS):
            o_ref[tok, h * V_DIM:(h + 1) * V_DIM] = _dot(o[h * dec:(h + 1) * dec], wuv_ref[h]).astype(BF16)


def _attn_sample(q_lat, q_rope, cache_ckv, cache_krt, ckvn, kr, w_uvh, dec):
    batch, past, _ = cache_ckv.shape
    nb = DEC_SEQS
    return pl.pallas_call(
        _attn_sample_kernel,
        grid=(batch // nb,),
        in_specs=[pl.BlockSpec((MLA_HEADS, nb * dec, KV_LORA), lambda b: (0, b, 0)),
                  pl.BlockSpec((MLA_HEADS, nb * dec, ROPE_DIM), lambda b: (0, b, 0)),
                  pl.BlockSpec((nb, past, KV_LORA), lambda b: (b, 0, 0)),
                  pl.BlockSpec((nb, ROPE_DIM, past), lambda b: (b, 0, 0)),
                  pl.BlockSpec((nb * dec, KV_LORA), lambda b: (b, 0)),
                  pl.BlockSpec((nb * dec, ROPE_DIM), lambda b: (b, 0)),
                  pl.BlockSpec((MLA_HEADS, KV_LORA, V_DIM), lambda b: (0, 0, 0))],
        out_specs=pl.BlockSpec((nb * dec, MLA_WIDTH), lambda b: (b, 0)),
        out_shape=jax.ShapeDtypeStruct((batch * dec, MLA_WIDTH), BF16),
        compiler_params=_params("parallel"),
        name="attn_sample",
    )(q_lat, q_rope, cache_ckv, cache_krt, ckvn, kr, w_uvh)


GLA_CHUNK = 256
GLA_DEC_ROWS = 128
GLA_SEQS = 2
GLA_VPU_LEVELS = 3


def _gla_tables(c, sub):
    nlev = int(np.log2(sub))
    assert 1 << nlev == sub and c % sub == 0 and c // sub <= LANES
    t = np.arange(c)[:, None]
    u = np.arange(c)[None, :]
    same_sub = (t // sub) == (u // sub)
    blocks = []
    lvl = np.full((c, c), -1, np.int32)
    for l in range(nlev):
        width = sub >> l
        half = width // 2
        m = (t // width) * width + half - 1
        upper = t > m
        blocks.append(np.where(upper, (u > m) & (u <= t), (u > t) & (u <= m)))
        same = (t // width) == (u // width)
        lvl[same & ((t % width) >= half) & ((u % width) < half)] = l
    blocks.append((u > t) & same_sub)
    blocks.append((u <= t) & same_sub)
    lvl[np.arange(c), np.arange(c)] = nlev
    sel = (np.arange(c)[:, None] // sub) == np.arange(LANES)[None, :]
    return np.concatenate(blocks, 0).astype(np.float32), lvl, sel.astype(np.float32), nlev


def _gla_kernel(*refs, c, sub, nlev, carry):
    if carry:
        (q_ref, k_ref, v_ref, r_ref, sm_ref, wga_ref, bga_ref, gg_ref, p_ref, lvl_ref, sel_ref,
         o_ref, sfin_ref, s_scr) = refs
        assert sub == c
        j = pl.program_id(1)

        @pl.when(j == 0)
        def _():
            s_scr[...] = jnp.zeros(s_scr.shape, F32)
    else:
        (q_ref, k_ref, v_ref, r_ref, sm_ref, wga_ref, bga_ref, gg_ref, p_ref, lvl_ref, sel_ref, s0_ref,
         o_ref, sfin_ref) = refs

    p_mat = p_ref[...]
    sel = sel_ref[...]
    lvl = lvl_ref[...]
    groups = range(q_ref.shape[0])
    split = []
    for g in groups:
        x = _dot(sm_ref[g].astype(BF16), wga_ref[...]) + bga_ref[...]
        la = (jnp.minimum(x, 0.0) - jnp.log1p(jnp.exp(-jnp.abs(x)))) * (1.0 / GATE_TAU)
        hi = la.astype(BF16)
        split.append((hi, (la - hi.astype(F32)).astype(BF16)))
    for g in groups:
        hi, mid = split[g]
        x_all = _dot(p_mat, hi) + _dot(p_mat, mid)
        if carry:
            b = x_all[(nlev - GLA_VPU_LEVELS) * c:]
            x_lv = []
            for l in range(GLA_VPU_LEVELS):
                width = c >> l
                half = width // 2
                parts = []
                for r0 in range(0, c, width):
                    d = b[r0:r0 + width] - b[r0 + half - 1:r0 + half, :]
                    parts += [-d[:half], d[half:]]
                x_lv.append(jnp.concatenate(parts, axis=0))
            x_lv += [x_all[l * c:(l + 1) * c] for l in range(nlev - GLA_VPU_LEVELS)]
            x_end = b[c - 1:c, :] - b
        else:
            x_lv = [x_all[l * c:(l + 1) * c] for l in range(nlev)]
            x_end = x_all[nlev * c:(nlev + 1) * c]
            b = x_all[(nlev + 1) * c:]
        d_all = jnp.exp(_dot_tn(hi, sel) + _dot_tn(mid, sel))
        q = q_ref[g] * (GLA_DK ** -0.5)
        k = k_ref[g]
        e_lv = [jnp.exp(x) for x in x_lv]
        qe = [(q * e).astype(BF16) for e in e_lv] + [q.astype(BF16)]
        ke = [(k * e).astype(BF16) for e in e_lv] + [k.astype(BF16)]
        k_end = (k * jnp.exp(x_end)).astype(BF16)
        q_beg = (q * jnp.exp(b)).astype(BF16)
        for h in range(GLA_HEADS):
            dk = slice(h * GLA_DK, (h + 1) * GLA_DK)
            dv = slice(h * GLA_DV, (h + 1) * GLA_DV)
            attn = jnp.zeros((c, c), F32)
            for l in range(nlev + 1):
                attn = jnp.where(lvl == l, _dot_nt(qe[l][:, dk], ke[l][:, dk]), attn)
            v = v_ref[g, :, dv].astype(BF16)
            o_intra = _dot(attn.astype(BF16), v)
            for n in range(c // sub):
                rows = slice(n * sub, (n + 1) * sub)
                state = s_scr[g, h] if carry else s0_ref[n, h]
                o = o_intra[rows] + _dot(q_beg[rows, dk], state.astype(BF16))
                new_state = d_all[dk, n:n + 1] * state + _dot_tn(k_end[rows, dk], v[rows])
                if carry:
                    s_scr[g, h] = new_state
                else:
                    sfin_ref[n, h] = new_state
                r = r_ref[g, rows, dv]
                o_ref[g, rows, dv] = (_rms(o, gg_ref[...]) * (r * jax.nn.sigmoid(r))).astype(BF16)

    if carry:
        @pl.when(j == pl.num_programs(1) - 1)
        def _():
            sfin_ref[...] = s_scr[...]


def _gla(z, zs, w_ga, b_ga, g_gla, batch, seq, c, s0=None):
    carry = s0 is None
    sub = c if carry else seq
    p_np, lvl_np, sel_np, nlev = _gla_tables(c, sub)
    if carry:
        p_np = np.concatenate([p_np[GLA_VPU_LEVELS * c:nlev * c], p_np[(nlev + 1) * c:]], axis=0)
    if carry:
        nb, lead, grid = GLA_SEQS, batch, (batch // GLA_SEQS, seq // c)
        at = lambda col: (lambda g, j: (g, j, col))
    else:
        nb, lead, grid = 1, 1, (batch * seq // c, 1)
        at = lambda col: (lambda g, j: (0, g, col))
    nstate = nb if carry else c // sub
    z, zs = z.reshape(lead, -1, Z_GLA), zs.reshape(lead, -1, LANES)
    const = lambda g, j: (0, 0)
    in_specs = [pl.BlockSpec((nb, c, GLA_HEADS * GLA_DK), at(0)),
                pl.BlockSpec((nb, c, GLA_HEADS * GLA_DK), at(1)),
                pl.BlockSpec((nb, c, GLA_WIDTH), at(1)),
                pl.BlockSpec((nb, c, GLA_WIDTH), at(2)),
                pl.BlockSpec((nb, c, LANES), at(0)),
                pl.BlockSpec((LANES, GLA_HEADS * GLA_DK), const),
                pl.BlockSpec((1, GLA_HEADS * GLA_DK), const),
                pl.BlockSpec((1, GLA_DV), const),
                pl.BlockSpec(p_np.shape, const),
                pl.BlockSpec((c, c), const),
                pl.BlockSpec((c, LANES), const)]
    args = [z, z, z, z, zs, w_ga, b_ga, g_gla,
            jnp.asarray(p_np, BF16), jnp.asarray(lvl_np), jnp.asarray(sel_np, BF16)]
    state_spec = pl.BlockSpec((nstate, GLA_HEADS, GLA_DK, GLA_DV), lambda g, j: (g, 0, 0, 0))
    if not carry:
        in_specs.append(state_spec)
        args.append(s0)
    o, s_fin = pl.pallas_call(
        functools.partial(_gla_kernel, c=c, sub=sub, nlev=nlev, carry=carry),
        grid=grid,
        in_specs=in_specs,
        out_specs=[pl.BlockSpec((nb, c, GLA_WIDTH), at(0)), state_spec],
        out_shape=[jax.ShapeDtypeStruct((lead, batch * seq // lead, GLA_WIDTH), BF16),
                   jax.ShapeDtypeStruct((batch, GLA_HEADS, GLA_DK, GLA_DV), F32)],
        scratch_shapes=[pltpu.VMEM((nb, GLA_HEADS, GLA_DK, GLA_DV), F32)] if carry else [],
        compiler_params=_params("parallel", "arbitrary"),
        name="gla" if carry else "gla_init",
    )(*args)
    return o.reshape(batch * seq, GLA_WIDTH), s_fin


ROW_SUB = 128


def _mix_kernel(oma_ref, omb_ref, oga_ref, ogb_ref, w1_ref, w2_ref, ha_ref, hb_ref, g1_ref, g2_ref, h1_ref, f_ref,
                *, na):
    def block(om_ref, og_ref, h_ref):
        for r in range(0, h_ref.shape[0], ROW_SUB):
            rows = slice(r, r + ROW_SUB)
            mix = _dot(om_ref[rows, :], w1_ref[...]) + _dot(og_ref[rows, :], w2_ref[...])
            h1 = h_ref[rows, :] + _rms(mix, g1_ref[...])
            h1_ref[rows, :] = h1
            f_ref[rows, :] = _rms(h1, g2_ref[...]).astype(BF16)

    pl.when(pl.program_id(0) < na)(lambda: block(oma_ref, oga_ref, ha_ref))
    pl.when(pl.program_id(0) >= na)(lambda: block(omb_ref, ogb_ref, hb_ref))


def _mix(o_mla, o_gla, h, w_out, g_post_mix, g_pre_ffn):
    tm = 512
    na, nb = h[0].shape[0] // tm, h[1].shape[0] // tm
    T = (na + nb) * tm
    row = lambda i: (i, 0)
    const = lambda i: (0, 0)
    at_a = lambda i: (jnp.minimum(i, na - 1), 0)
    at_b = lambda i: (jnp.maximum(i - na, 0), 0)
    return pl.pallas_call(
        functools.partial(_mix_kernel, na=na),
        grid=(na + nb,),
        in_specs=[pl.BlockSpec((tm, MLA_WIDTH), at_a), pl.BlockSpec((tm, MLA_WIDTH), at_b),
                  pl.BlockSpec((tm, GLA_WIDTH), at_a), pl.BlockSpec((tm, GLA_WIDTH), at_b),
                  pl.BlockSpec((MLA_WIDTH, D_MODEL), lambda i: (0, 0), pipeline_mode=pl.Buffered(1)),
                  pl.BlockSpec((GLA_WIDTH, D_MODEL), lambda i: (1, 0), pipeline_mode=pl.Buffered(1)),
                  pl.BlockSpec((tm, D_MODEL), at_a), pl.BlockSpec((tm, D_MODEL), at_b),
                  pl.BlockSpec((1, D_MODEL), const),
                  pl.BlockSpec((1, D_MODEL), const)],
        out_specs=[pl.BlockSpec((tm, D_MODEL), row), pl.BlockSpec((tm, D_MODEL), row)],
        out_shape=[jax.ShapeDtypeStruct((T, D_MODEL), F32), jax.ShapeDtypeStruct((T, D_MODEL), BF16)],
        compiler_params=_params("arbitrary"),
        name="mix",
    )(o_mla[0], o_mla[1], o_gla[0], o_gla[1], w_out, w_out, h[0], h[1], g_post_mix, g_pre_ffn)


FFN_MAX_ROWS = 1088


def _ffn_kernel(x_ref, wg_ref, wu_ref, wd_ref, o_ref):
    def partial_out():
        x = x_ref[...]
        g = _dot(x, wg_ref[...].astype(BF16))
        u = _dot(x, wu_ref[...].astype(BF16))
        return _dot((g * jax.nn.sigmoid(g) * u).astype(BF16), wd_ref[...].astype(BF16))

    @pl.when(pl.program_id(1) == 0)
    def _():
        o_ref[...] = partial_out()

    @pl.when(pl.program_id(1) > 0)
    def _():
        o_ref[...] += partial_out()


def _ffn(x, w_gate, w_up, w_down):
    T = x.shape[0]
    tf = 512
    tm = next(T // n for n in range(1, T + 1) if T % n == 0 and T // n <= FFN_MAX_ROWS and (T // n) % 16 == 0)
    return pl.pallas_call(
        _ffn_kernel,
        grid=(T // tm, D_FF // tf),
        in_specs=[pl.BlockSpec((tm, D_MODEL), lambda i, j: (i, 0)),
                  pl.BlockSpec((D_MODEL, tf), lambda i, j: (0, j)),
                  pl.BlockSpec((D_MODEL, tf), lambda i, j: (0, j)),
                  pl.BlockSpec((tf, D_MODEL), lambda i, j: (j, 0))],
        out_specs=pl.BlockSpec((tm, D_MODEL), lambda i, j: (i, 0)),
        out_shape=jax.ShapeDtypeStruct((T, D_MODEL), F32),
        compiler_params=_params("parallel", "arbitrary"),
        name="ffn",
    )(x, w_gate, w_up, w_down)


def _final_kernel(f_ref, h1_ref, pa_ref, pb_ref, g_ref, wpg_ref, wp_ref, ya_ref, yb_ref, *, na):
    def block(p_ref, y_ref):
        h2 = h1_ref[...] + _rms(f_ref[...], g_ref[...])
        gate = jax.nn.sigmoid(_dot(h2.astype(BF16), wpg_ref[...]))
        y_ref[...] = h2 + gate * _dot(p_ref[...].astype(BF16), wp_ref[...])

    pl.when(pl.program_id(0) < na)(lambda: block(pa_ref, ya_ref))
    pl.when(pl.program_id(0) >= na)(lambda: block(pb_ref, yb_ref))


def _final(f, h1, p, g_post_ffn, w_ple_gate, w_ple):
    tm = 512
    na, nb = p[0].shape[0] // tm, p[1].shape[0] // tm
    row = lambda i: (i, 0)
    const = lambda i: (0, 0)
    at_a = lambda i: (jnp.minimum(i, na - 1), 0)
    at_b = lambda i: (jnp.maximum(i - na, 0), 0)
    return pl.pallas_call(
        functools.partial(_final_kernel, na=na),
        grid=(na + nb,),
        in_specs=[pl.BlockSpec((tm, D_MODEL), row),
                  pl.BlockSpec((tm, D_MODEL), row),
                  pl.BlockSpec((tm, PLE_DIM), at_a), pl.BlockSpec((tm, PLE_DIM), at_b),
                  pl.BlockSpec((1, D_MODEL), const),
                  pl.BlockSpec((D_MODEL, D_MODEL), const, pipeline_mode=pl.Buffered(1)),
                  pl.BlockSpec((PLE_DIM, D_MODEL), const)],
        out_specs=[pl.BlockSpec((tm, D_MODEL), at_a), pl.BlockSpec((tm, D_MODEL), at_b)],
        out_shape=[jax.ShapeDtypeStruct((na * tm, D_MODEL), F32), jax.ShapeDtypeStruct((nb * tm, D_MODEL), F32)],
        compiler_params=_params("arbitrary"),
        name="final",
    )(f, h1, p[0], p[1], g_post_ffn, w_ple_gate, w_ple)


def _rope_tables(pos):
    half = ROPE_DIM // 2
    inv = 1.0 / (ROPE_THETA ** (jnp.arange(half, dtype=F32) / half))
    ang = pos.astype(F32)[:, None] * inv[None, :]
    cos, sin = jnp.cos(ang), jnp.sin(ang)
    zero = jnp.zeros((pos.shape[0], LANES - ROPE_DIM), F32)
    return jnp.concatenate([cos, cos, zero], axis=1), jnp.concatenate([-sin, sin, zero], axis=1)


def _layer_weights(i, g_pre_mix, w_in, g_q, w_uq, w_uk, g_kv, w_ga, b_ga, w_uv, g_gla, w_out, g_post_mix,
                   g_pre_ffn, w_gate, w_up, w_down, g_post_ffn, w_ple, w_ple_gate):
    w_uq_h = w_uq[i].reshape(Q_LORA, MLA_HEADS, NOPE_DIM + ROPE_DIM)
    w_uq_p = jnp.concatenate([w_uq_h[:, :, :NOPE_DIM].reshape(Q_LORA, MLA_HEADS * NOPE_DIM),
                              w_uq_h[:, :, NOPE_DIM:].reshape(Q_LORA, MLA_HEADS * ROPE_DIM)], axis=1)
    w_ga_p = jnp.zeros((LANES, GLA_HEADS * GLA_DK), F32).at[GLR_LO:GLR_LO + GATE_RANK].set(w_ga[i])
    vec = lambda g: g[i].reshape(1, -1)
    return dict(
        g_pre_mix=vec(g_pre_mix), w_in=_win_prep(jnp.swapaxes(w_in[i], 0, 1)), g_q=vec(g_q), g_kv=vec(g_kv),
        w_uq=w_uq_p.astype(BF16),
        w_uv=w_uv[i].reshape(KV_LORA, MLA_WIDTH).astype(BF16),
        w_ukt=jnp.transpose(w_uk[i], (1, 2, 0)).astype(BF16),
        w_uvh=jnp.transpose(w_uv[i], (1, 0, 2)).astype(BF16),
        w_ga=w_ga_p.astype(BF16), b_ga=vec(b_ga), g_gla=vec(g_gla),
        w_out=w_out[i].astype(BF16), g_post_mix=vec(g_post_mix), g_pre_ffn=vec(g_pre_ffn),
        w_gate=w_gate[i], w_up=w_up[i], w_down=w_down[i],
        g_post_ffn=vec(g_post_ffn), w_ple=w_ple[i].astype(BF16), w_ple_gate=w_ple_gate[i].astype(BF16))


def _finish(w, h, p, o_mla, o_gla):
    h1, f_in = _mix(o_mla, o_gla, h, w["w_out"], w["g_post_mix"], w["g_pre_ffn"])
    f = _ffn(f_in, w["w_gate"], w["w_up"], w["w_down"])
    return _final(f, h1, p, w["g_post_ffn"], w["w_ple_gate"], w["w_ple"])


def kernel(x_prompt, x_sample, cache_ckv, cache_krope, state_gla, p_prompt, p_sample, g_pre_mix, w_in, g_q, w_uq,
           w_uk, g_kv, w_ga, b_ga, w_uv, g_gla, w_out, g_post_mix, g_pre_ffn, w_gate, w_up, w_down, g_post_ffn,
           w_ple, w_ple_gate):
    batch, seq, _ = x_prompt.shape
    dbatch, dseq, _ = x_sample.shape
    depth = w_in.shape[0]
    cos_p, sin_p = _rope_tables(jnp.arange(seq))
    cos_s, sin_s = _rope_tables(PAST_LEN + jnp.arange(dseq))
    cos_s, sin_s = jnp.tile(cos_s, (dbatch, 1)), jnp.tile(sin_s, (dbatch, 1))
    h_p = x_prompt.reshape(batch * seq, D_MODEL)
    h_s = x_sample.reshape(dbatch * dseq, D_MODEL)
    outs = [[] for _ in range(6)]
    for i in range(depth):
        w = _layer_weights(i, g_pre_mix, w_in, g_q, w_uq, w_uk, g_kv, w_ga, b_ga, w_uv, g_gla, w_out, g_post_mix,
                           g_pre_ffn, w_gate, w_up, w_down, g_post_ffn, w_ple, w_ple_gate)
        z, zs, ckvn, kr, qcat, kt, v = _proj_prompt(h_p, w, cos_p, sin_p, seq)
        o_mla_p = _attn_prompt(qcat, kt, v, batch, seq)
        o_gla_p, s_fin = _gla(z, zs, w["w_ga"], w["b_ga"], w["g_gla"], batch, seq, GLA_CHUNK)
        outs[0].append(ckvn.reshape(batch, seq, KV_LORA))
        outs[1].append(kr.reshape(batch, seq, ROPE_DIM))
        outs[2].append(s_fin)
        z, zs, ckvn, kr, q_lat, q_rope = _proj_sample(h_s, w, cos_s, sin_s)
        o_mla_s = _attn_sample(q_lat, q_rope, cache_ckv[i], jnp.swapaxes(cache_krope[i], 1, 2), ckvn, kr,
                               w["w_uvh"], dseq)
        o_gla_s, s_new = _gla(z, zs, w["w_ga"], w["b_ga"], w["g_gla"], dbatch, dseq, GLA_DEC_ROWS, s0=state_gla[i])
        outs[3].append(ckvn.reshape(dbatch, dseq, KV_LORA))
        outs[4].append(kr.reshape(dbatch, dseq, ROPE_DIM))
        outs[5].append(s_new)
        h_p, h_s = _finish(w, (h_p, h_s),
                           (p_prompt[i].reshape(batch * seq, PLE_DIM), p_sample[i].reshape(dbatch * dseq, PLE_DIM)),
                           (o_mla_p, o_mla_s), (o_gla_p, o_gla_s))
    return (h_p.reshape(batch, seq, D_MODEL), h_s.reshape(dbatch, dseq, D_MODEL),
            jnp.stack(outs[0]), jnp.stack(outs[1]), jnp.stack(outs[2]),
            jnp.stack(outs[3]), jnp.stack(outs[4]), jnp.stack(outs[5]))
```
